```python
import numpy as np
import jax
import jax.numpy as jnp
from jax import lax

D_MODEL = 1024
BATCH = 4
SEQ = 4096
DEPTH = 2

HEAD_DIM = 64
N_HEADS_NSA = 8
N_KV_NSA = 2
G_NSA = N_HEADS_NSA // N_KV_NSA
N_HEADS_SWA = 8
N_KV_SWA = 2
G_SWA = N_HEADS_SWA // N_KV_SWA
D_NSA = N_HEADS_NSA * HEAD_DIM
D_SWA = N_HEADS_SWA * HEAD_DIM
D_MIX = D_NSA + D_SWA
N_HEADS_TOTAL = N_HEADS_NSA + N_HEADS_SWA
KV_NSA = N_KV_NSA * HEAD_DIM
KV_SWA = N_KV_SWA * HEAD_DIM
N_NSA_BRANCHES = 3
IN_SPLITS = (D_NSA,) + (KV_NSA,) * (2 * N_NSA_BRANCHES) + (N_HEADS_NSA * N_NSA_BRANCHES, D_SWA, KV_SWA, KV_SWA)
N_IN = sum(IN_SPLITS)
CMP_LEN = 32
CMP_STRIDE = 16
CMP_HIDDEN = 256
SEL_LEN = 64
SEL_TOPK = 16
SEL_CHUNK = 64
NSA_WINDOW = 512
SWA_WINDOW = 128
BLOCK_Q = 128
N_EXPERTS = 16
N_GROUPS = 4
EXPERTS_PER_GROUP = N_EXPERTS // N_GROUPS
TOP_K = 2
D_EXPERT = 512
EPS = 1e-6
NEG_INF = -1e30
ATTN_SCALE = HEAD_DIM ** -0.5

kernel_name = 'hybrid_nsa_swa_sink_grouped_moe_adaln'


def rms_norm(x, gain):
    xf = x.astype(jnp.float32)
    y = xf * lax.rsqrt(jnp.mean(xf * xf, axis=-1, keepdims=True) + EPS)
    return (y * gain.astype(jnp.float32)).astype(x.dtype)


def alibi_slopes(first, count):
    h = jnp.arange(first + 1, first + count + 1, dtype=jnp.float32)
    return jnp.exp2(-8.0 * h / N_HEADS_TOTAL)


def banded_attention(q, k, v, slopes, window, sinks=None):
    B, T, Hkv, G, Dh = q.shape
    n_blk = T // BLOCK_Q
    span = window + BLOCK_Q
    kp = jnp.pad(k, ((0, 0), (window, 0), (0, 0), (0, 0)))
    vp = jnp.pad(v, ((0, 0), (window, 0), (0, 0), (0, 0)))
    q_b = q.reshape(B, n_blk, BLOCK_Q, Hkv, G, Dh).transpose(1, 0, 2, 3, 4, 5)
    dist = np.arange(BLOCK_Q)[:, None] + window - np.arange(span)[None, :]
    in_band = (dist >= 0) & (dist < window)
    dist_f = jnp.asarray(dist, jnp.float32)

    def one_block(args):
        i, qi = args
        ki = lax.dynamic_slice_in_dim(kp, i * BLOCK_Q, span, axis=1)
        vi = lax.dynamic_slice_in_dim(vp, i * BLOCK_Q, span, axis=1)
        s_abs = i * BLOCK_Q - window + jnp.arange(span)
        mask = in_band & (s_abs >= 0)[None, :]
        s = jnp.einsum('bqhgd,bkhd->bhgqk', qi, ki).astype(jnp.float32)
        s = s - slopes[:, :, None, None] * dist_f
        s = jnp.where(mask, s, NEG_INF)
        if sinks is None:
            p = jax.nn.softmax(s, axis=-1)
        else:
            sink = sinks.astype(jnp.float32)[:, :, None, None]
            m = jnp.maximum(jnp.max(s, axis=-1, keepdims=True), sink)
            e = jnp.exp(s - m)
            p = e / (jnp.sum(e, axis=-1, keepdims=True) + jnp.exp(sink - m))
        return jnp.einsum('bhgqk,bkhd->bqhgd', p.astype(vi.dtype), vi)

    o = lax.map(one_block, (jnp.arange(n_blk), q_b))
    return o.transpose(1, 0, 2, 3, 4, 5).reshape(B, T, Hkv, G, Dh)


def nsa_compressed(q, k, v, pos_k, pos_v, w1k, w2k, w1v, w2v, k_gain, slopes):
    B, T, Hkv, G, Dh = q.shape
    n_cmp = (T - CMP_LEN) // CMP_STRIDE + 1
    idx = np.arange(n_cmp)[:, None] * CMP_STRIDE + np.arange(CMP_LEN)[None, :]

    def compress(z, pos, w1, w2):
        blk = z[:, idx] + pos[None, None, :, None, :]
        blk = blk.transpose(0, 1, 3, 2, 4).reshape(B, n_cmp, Hkv, CMP_LEN * Dh)
        return jax.nn.gelu(blk @ w1) @ w2

    kc = rms_norm(compress(k, pos_k, w1k, w2k), k_gain)
    vc = compress(v, pos_v, w1v, w2v)
    dist = np.arange(T)[:, None] - idx[:, -1][None, :]
    valid = dist >= 0
    s = jnp.einsum('bthgd,bnhd->bthgn', q, kc).astype(jnp.float32)
    s = s - slopes[None, None, :, :, None] * jnp.asarray(dist, jnp.float32)[None, :, None, None, :]
    s = jnp.where(valid[None, :, None, None, :], s, NEG_INF)
    row_valid = jnp.asarray(valid.any(-1), jnp.float32)[None, :, None, None, None]
    p = jax.nn.softmax(s, axis=-1) * row_valid
    o = jnp.einsum('bthgn,bnhd->bthgd', p.astype(vc.dtype), vc)
    return o, p


def nsa_selected(q, k, v, p_cmp, slopes):
    B, T, Hkv, G, Dh = q.shape
    n_cmp = p_cmp.shape[-1]
    n_sel = T // SEL_LEN
    top = min(SEL_TOPK, n_sel)
    cs = np.arange(n_cmp) * CMP_STRIDE
    ss = np.arange(n_sel) * SEL_LEN
    overlap = np.clip(np.minimum(cs[:, None] + CMP_LEN, ss[None, :] + SEL_LEN)
                      - np.maximum(cs[:, None], ss[None, :]), 0, None) / CMP_LEN
    imp = jnp.einsum('bthgn,nj->bthj', p_cmp, jnp.asarray(overlap, jnp.float32))
    cur = np.arange(T) // SEL_LEN
    j = np.arange(n_sel)
    forced = (j[None, :] == 0) | (j[None, :] == cur[:, None]) | (j[None, :] == cur[:, None] - 1)
    future = j[None, :] > cur[:, None]
    imp = jnp.where(forced[None, :, None, :], 1e9, jnp.where(future[None, :, None, :], NEG_INF, imp))
    _, sel = lax.top_k(imp, top)

    k_blocks = k.reshape(B, n_sel, SEL_LEN, Hkv, Dh).transpose(0, 3, 1, 2, 4)
    v_blocks = v.reshape(B, n_sel, SEL_LEN, Hkv, Dh).transpose(0, 3, 1, 2, 4)
    n_chunk = T // SEL_CHUNK
    q_c = q.reshape(B, n_chunk, SEL_CHUNK, Hkv, G, Dh).transpose(1, 0, 3, 2, 4, 5)
    sel_c = sel.reshape(B, n_chunk, SEL_CHUNK, Hkv, top).transpose(1, 0, 3, 2, 4)
    b_ix = jnp.arange(B)[:, None, None, None]
    h_ix = jnp.arange(Hkv)[None, :, None, None]
    within = jnp.arange(SEL_LEN)

    def one_chunk(args):
        ci, qi, si = args
        kg = k_blocks[b_ix, h_ix, si]
        vg = v_blocks[b_ix, h_ix, si]
        t_q = ci * SEL_CHUNK + jnp.arange(SEL_CHUNK)
        s_pos = si[..., None] * SEL_LEN + within
        dist = (t_q[None, None, :, None, None] - s_pos).astype(jnp.float32)[:, :, :, None]
        s = jnp.einsum('bhcgd,bhcnkd->bhcgnk', qi, kg).astype(jnp.float32)
        s = s - slopes[None, :, None, :, None, None] * dist
        s = jnp.where(dist >= 0, s, NEG_INF).reshape(B, Hkv, SEL_CHUNK, G, top * SEL_LEN)
        p = jax.nn.softmax(s, axis=-1)
        vflat = vg.reshape(B, Hkv, SEL_CHUNK, top * SEL_LEN, Dh)
        return jnp.einsum('bhcgm,bhcmd->bhcgd', p.astype(vflat.dtype), vflat)

    o = lax.map(one_chunk, (jnp.arange(n_chunk), q_c, sel_c))
    return o.transpose(1, 0, 3, 2, 4, 5).reshape(B, T, Hkv, G, Dh)


def hybrid_mixer(u, w_in, w_out, q_gain_n, k_gain_n, pos_k, pos_v, w1k, w2k, w1v, w2v,
                 q_gain_s, k_gain_s, sinks, beta_n, beta_s):
    B, T, _ = u.shape
    parts = jnp.split(u @ w_in, np.cumsum(IN_SPLITS)[:-1].tolist(), axis=-1)
    q_n, k_c, v_c, k_sl, v_sl, k_w, v_w, gate_logits, q_s, k_s, v_s = parts

    def kv_n(z):
        return z.reshape(B, T, N_KV_NSA, HEAD_DIM)

    def kv_s(z):
        return z.reshape(B, T, N_KV_SWA, HEAD_DIM)

    slopes_swa = alibi_slopes(0, N_HEADS_SWA).reshape(N_KV_SWA, G_SWA)
    slopes_nsa = alibi_slopes(N_HEADS_SWA, N_HEADS_NSA).reshape(N_KV_NSA, G_NSA)

    q_n = rms_norm(q_n.reshape(B, T, N_KV_NSA, G_NSA, HEAD_DIM), q_gain_n) * ATTN_SCALE
    o_cmp, p_cmp = nsa_compressed(q_n, kv_n(k_c), kv_n(v_c), pos_k, pos_v, w1k, w2k, w1v, w2v,
                                  k_gain_n[0], slopes_nsa)
    o_slc = nsa_selected(q_n, rms_norm(kv_n(k_sl), k_gain_n[1]), kv_n(v_sl), p_cmp, slopes_nsa)
    o_win = banded_attention(q_n, rms_norm(kv_n(k_w), k_gain_n[2]), kv_n(v_w), slopes_nsa, NSA_WINDOW)
    g = jax.nn.sigmoid(gate_logits.reshape(B, T, N_KV_NSA, G_NSA, N_NSA_BRANCHES))
    o_nsa = (g[..., 0:1] * o_cmp + g[..., 1:2] * o_slc + g[..., 2:3] * o_win).reshape(B, T, D_NSA)

    q_s = rms_norm(q_s.reshape(B, T, N_KV_SWA, G_SWA, HEAD_DIM), q_gain_s) * ATTN_SCALE
    o_swa = banded_attention(q_s, rms_norm(kv_s(k_s), k_gain_s), kv_s(v_s), slopes_swa, SWA_WINDOW,
                             sinks.reshape(N_KV_SWA, G_SWA)).reshape(B, T, D_SWA)

    merged = jnp.concatenate([rms_norm(o_nsa, beta_n), rms_norm(o_swa, beta_s)], axis=-1)
    return merged @ w_out


def grouped_moe(u, w_router, w1, w3, w2):
    B, T, D = u.shape
    xt = u.reshape(B * T, D)
    aff = jax.nn.softmax((xt @ w_router).astype(jnp.float32), axis=-1)
    group_score = lax.top_k(aff.reshape(-1, N_GROUPS, EXPERTS_PER_GROUP), TOP_K)[0].sum(-1)
    g_sel = jnp.argmax(group_score, axis=-1)
    in_group = (jnp.arange(N_EXPERTS) // EXPERTS_PER_GROUP)[None, :] == g_sel[:, None]
    top_w, top_i = lax.top_k(jnp.where(in_group, aff, -1.0), TOP_K)
    top_w = top_w / jnp.sum(top_w, axis=-1, keepdims=True)
    gate = jnp.sum(jax.nn.one_hot(top_i, N_EXPERTS, dtype=jnp.float32) * top_w[..., None], axis=1)
    gate = gate.astype(xt.dtype)
    y = jnp.zeros_like(xt)
    for e in range(N_EXPERTS):
        h = jax.nn.silu(xt @ w1[e]) * (xt @ w3[e])
        y = y + gate[:, e:e + 1] * (h @ w2[e])
    return y.reshape(B, T, D)


def setup_inputs(seed: int = 0) -> dict:
    key = jax.random.key(seed)
    ks = jax.random.split(key, 28)
    L = DEPTH

    def nrm(k, shape, scale):
        return jax.random.normal(k, shape, jnp.float32) * scale

    def gain(k, shape):
        return 1.0 + 0.05 * jax.random.normal(k, shape, jnp.float32)

    return {
        'x': nrm(ks[0], (BATCH, SEQ, D_MODEL), 1.0),
        'c': nrm(ks[1], (BATCH, D_MODEL), 1.0),
        'w_router': nrm(ks[2], (D_MODEL, N_EXPERTS), D_MODEL ** -0.5),
        'ada_w': nrm(ks[3], (L, D_MODEL, 6 * D_MODEL), 0.5 * D_MODEL ** -0.5),
        'ada_b': nrm(ks[4], (L, 6 * D_MODEL), 0.01),
        'norm1_g': gain(ks[5], (L, D_MODEL)),
        'norm2_g': gain(ks[6], (L, D_MODEL)),
        'w_in': nrm(ks[7], (L, D_MODEL, N_IN), D_MODEL ** -0.5),
        'w_out': nrm(ks[8], (L, D_MIX, D_MODEL), D_MIX ** -0.5),
        'nsa_q_gain': gain(ks[9], (L, HEAD_DIM)),
        'nsa_k_gain': gain(ks[10], (L, N_NSA_BRANCHES, HEAD_DIM)),
        'cmp_pos_k': nrm(ks[11], (L, CMP_LEN, HEAD_DIM), 0.1),
        'cmp_pos_v': nrm(ks[12], (L, CMP_LEN, HEAD_DIM), 0.1),
        'cmp_w1_k': nrm(ks[13], (L, CMP_LEN * HEAD_DIM, CMP_HIDDEN), (CMP_LEN * HEAD_DIM) ** -0.5),
        'cmp_w2_k': nrm(ks[14], (L, CMP_HIDDEN, HEAD_DIM), CMP_HIDDEN ** -0.5),
        'cmp_w1_v': nrm(ks[15], (L, CMP_LEN * HEAD_DIM, CMP_HIDDEN), (CMP_LEN * HEAD_DIM) ** -0.5),
        'cmp_w2_v': nrm(ks[16], (L, CMP_HIDDEN, HEAD_DIM), CMP_HIDDEN ** -0.5),
        'swa_q_gain': gain(ks[17], (L, HEAD_DIM)),
        'swa_k_gain': gain(ks[18], (L, HEAD_DIM)),
        'swa_sinks': nrm(ks[19], (L, N_HEADS_SWA), 0.5),
        'beta_nsa': gain(ks[20], (L, D_NSA)),
        'beta_swa': gain(ks[21], (L, D_SWA)),
        'moe_w1': nrm(ks[22], (L, N_EXPERTS, D_MODEL, D_EXPERT), D_MODEL ** -0.5),
        'moe_w3': nrm(ks[23], (L, N_EXPERTS, D_MODEL, D_EXPERT), D_MODEL ** -0.5),
        'moe_w2': nrm(ks[24], (L, N_EXPERTS, D_EXPERT, D_MODEL), D_EXPERT ** -0.5),
    }


def reference(x, c, w_router, ada_w, ada_b, norm1_g, norm2_g, w_in, w_out, nsa_q_gain, nsa_k_gain,
              cmp_pos_k, cmp_pos_v, cmp_w1_k, cmp_w2_k, cmp_w1_v, cmp_w2_v, swa_q_gain, swa_k_gain,
              swa_sinks, beta_nsa, beta_swa, moe_w1, moe_w3, moe_w2):
    cond = jax.nn.silu(c)
    for l in range(DEPTH):
        mod = (cond @ ada_w[l] + ada_b[l])[:, None, :]
        sh1, sc1, g1, sh2, sc2, g2 = jnp.split(mod, 6, axis=-1)
        u = rms_norm(x, norm1_g[l]) * (1.0 + sc1) + sh1
        x = x + g1 * hybrid_mixer(u, w_in[l], w_out[l], nsa_q_gain[l], nsa_k_gain[l],
                                  cmp_pos_k[l], cmp_pos_v[l], cmp_w1_k[l], cmp_w2_k[l],
                                  cmp_w1_v[l], cmp_w2_v[l], swa_q_gain[l], swa_k_gain[l],
                                  swa_sinks[l], beta_nsa[l], beta_swa[l])
        u = rms_norm(x, norm2_g[l]) * (1.0 + sc2) + sh2
        x = x + g2 * grouped_moe(u, w_router, moe_w1[l], moe_w3[l], moe_w2[l])
    return x
```

```python
import functools

import numpy as np
import jax
import jax.numpy as jnp
from jax import lax
from jax.experimental import pallas as pl
from jax.experimental.pallas import tpu as pltpu

D_MODEL = 1024
DEPTH = 2
HEAD_DIM = 64
N_HEADS_NSA = 8
N_KV_NSA = 2
G_NSA = N_HEADS_NSA // N_KV_NSA
N_HEADS_SWA = 8
N_KV_SWA = 2
G_SWA = N_HEADS_SWA // N_KV_SWA
D_NSA = N_HEADS_NSA * HEAD_DIM
D_SWA = N_HEADS_SWA * HEAD_DIM
D_MIX = D_NSA + D_SWA
N_HEADS_TOTAL = N_HEADS_NSA + N_HEADS_SWA
KV_NSA = N_KV_NSA * HEAD_DIM
KV_SWA = N_KV_SWA * HEAD_DIM
N_NSA_BRANCHES = 3
CMP_LEN = 32
CMP_STRIDE = 16
CMP_HIDDEN = 256
SEL_LEN = 64
SEL_TOPK = 16
NSA_WINDOW = 512
SWA_WINDOW = 128
N_EXPERTS = 16
N_GROUPS = 4
EXPERTS_PER_GROUP = N_EXPERTS // N_GROUPS
TOP_K = 2
D_EXPERT = 512
EPS = 1e-6
NEG_INF = -1e30
ATTN_SCALE = HEAD_DIM ** -0.5

F32 = jnp.float32
BF16 = jnp.bfloat16

LANES = 128
VMEM_LIMIT = 48 * 1024 * 1024

SLOT_QN = 0
SLOT_QS = 8
SLOT_KSL = 16
SLOT_KW = 18
SLOT_KS = 20
SLOT_VSL = 22
SLOT_VW = 24
SLOT_VS = 26
SLOT_KC = 28
SLOT_VC = 29
SLOT_GATE = 30
N_SLOTS = 31
N_CHUNKS = 17
N_NORM_CHUNKS = 11

_C_QN = (0, 512)
_C_KC = (512, 640)
_C_VC = (640, 768)
_C_KSL = (768, 896)
_C_VSL = (896, 1024)
_C_KW = (1024, 1152)
_C_VW = (1152, 1280)
_C_GATE = (1280, 1304)
_C_QS = (1304, 1816)
_C_KS = (1816, 1944)
_C_VS = (1944, 2072)


def _cparams(sem):
    return pltpu.CompilerParams(dimension_semantics=sem, vmem_limit_bytes=VMEM_LIMIT)


def _dot(a, b, **kw):
    return jnp.dot(a, b, preferred_element_type=F32, **kw)


def _dot_nt(a, b):
    return lax.dot_general(a, b, (((1,), (1,)), ((), ())), preferred_element_type=F32)


def _sigmoid(x):
    return 1.0 / (1.0 + jnp.exp(-x))


def _ada_kernel(c_ref, w_ref, b_ref, o_ref):
    c = c_ref[...]
    cond = c * _sigmoid(c)
    o_ref[0] = _dot(cond.astype(BF16), w_ref[0].astype(BF16)) + b_ref[0]


def _ada(c_pad, ada_w, ada_b):
    L, D, N6 = ada_w.shape
    tn = 1536
    return pl.pallas_call(
        _ada_kernel,
        grid=(L, N6 // tn),
        in_specs=[
            pl.BlockSpec((8, D), lambda l, j: (0, 0)),
            pl.BlockSpec((1, D, tn), lambda l, j: (l, 0, j)),
            pl.BlockSpec((1, 1, tn), lambda l, j: (l, 0, j)),
        ],
        out_specs=pl.BlockSpec((1, 8, tn), lambda l, j: (l, 0, j)),
        out_shape=jax.ShapeDtypeStruct((L, 8, N6), F32),
        compiler_params=_cparams(("arbitrary", "arbitrary")),
        name="ada",
    )(c_pad, ada_w, ada_b.reshape(L, 1, N6))


def _inproj_kernel(x_ref, mod_ref, g_ref, w_ref, gain_ref, qc_ref, bd_ref, o_ref, *, tm, seq):
    i = pl.program_id(0)
    x = x_ref[...]
    ms = jnp.mean(x * x, axis=-1, keepdims=True)
    u = x * lax.rsqrt(ms + EPS) * g_ref[...]
    u = u * (1.0 + mod_ref[0, 1:2, :]) + mod_ref[0, 0:1, :]
    y = _dot(u.astype(BF16), w_ref[...])

    lane = lax.broadcasted_iota(jnp.int32, (tm, LANES), 1)
    low = lane < HEAD_DIM
    t = lax.broadcasted_iota(jnp.int32, (tm, LANES), 0) + lax.rem(i * tm, seq)
    hi_part = (t >> 6).astype(F32)
    lo_part = (t & 63).astype(F32)
    poscols = jnp.where((lane == 64) | (lane == 65), hi_part,
                        jnp.where((lane == 66) | (lane == 67), lo_part, 0.0))
    bd = bd_ref[...]

    def put(slot, val):
        o_ref[:, slot * LANES:(slot + 1) * LANES] = val.astype(o_ref.dtype)

    for c in range(N_CHUNKS):
        blk = y[:, c * LANES:(c + 1) * LANES]
        if c < N_NORM_CHUNKS:
            msb = _dot((blk * blk).astype(BF16), bd) * (1.0 / HEAD_DIM)
            blk = blk * lax.rsqrt(msb + EPS) * gain_ref[:, c * LANES:(c + 1) * LANES]
        if c < 14:
            rolled = pltpu.roll(blk, HEAD_DIM, axis=1)
            if c < 8:
                f0 = qc_ref[2 * c:2 * c + 1, :]
                f1 = qc_ref[2 * c + 1:2 * c + 2, :]
                s0 = 2 * c
            elif c < N_NORM_CHUNKS:
                f0 = f1 = poscols
                s0 = SLOT_KSL + 2 * (c - 8)
            else:
                f0 = f1 = 0.0
                s0 = SLOT_VSL + 2 * (c - N_NORM_CHUNKS)
            put(s0, jnp.where(low, blk, f0))
            put(s0 + 1, jnp.where(low, rolled, f1))
        else:
            put(SLOT_KC + (c - 14), blk)


def _inproj(x2, mod, norm_g, w_c, gain_c, qconst, bd, *, seq):
    N, D = x2.shape
    tm = 512
    tiles_per_seq = seq // tm
    kern = functools.partial(_inproj_kernel, tm=tm, seq=seq)
    return pl.pallas_call(
        kern,
        grid=(N // tm,),
        in_specs=[
            pl.BlockSpec((tm, D), lambda i: (i, 0)),
            pl.BlockSpec((1, 6, D), lambda i: (i // tiles_per_seq, 0, 0)),
            pl.BlockSpec((1, D), lambda i: (0, 0)),
            pl.BlockSpec((D, N_CHUNKS * LANES), lambda i: (0, 0)),
            pl.BlockSpec((1, N_NORM_CHUNKS * LANES), lambda i: (0, 0)),
            pl.BlockSpec((16, LANES), lambda i: (0, 0)),
            pl.BlockSpec((LANES, LANES), lambda i: (0, 0)),
        ],
        out_specs=pl.BlockSpec((tm, N_SLOTS * LANES), lambda i: (i, 0)),
        out_shape=jax.ShapeDtypeStruct((N, N_SLOTS * LANES), BF16),
        compiler_params=_cparams(("arbitrary",)),
        name="inproj",
    )(x2, mod, norm_g, w_c, gain_c, qconst, bd)


def _gelu_tanh(x):
    return 0.5 * x * (1.0 + jnp.tanh(np.sqrt(2.0 / np.pi).astype(np.float32) * (x + 0.044715 * (x * x * x))))


def _cmpmlp_kernel(x_ref, w1_ref, w2_ref, pos_ref, gain_ref, o_ref, *, nc):
    kv = pl.program_id(0)
    xc = x_ref[0, 0, 0]
    w1 = w1_ref[0]
    half = (CMP_LEN // 2) * HEAD_DIM
    first = _dot(xc, w1[:half])
    second = _dot(xc, w1[half:])
    bias = _dot(pos_ref[0], w1)[0:1]
    h = first + pltpu.roll(second, nc - 1, axis=0) + bias
    h = _gelu_tanh(h)
    z = _dot(h.astype(BF16), w2_ref[0])
    zn = z * lax.rsqrt(jnp.sum(z * z, axis=-1, keepdims=True) * (1.0 / HEAD_DIM) + EPS) * gain_ref[...]
    o_ref[0, 0, 0] = jnp.where(kv == 0, zn, z)


def _cmpmlp(xc, w1s, w2s, pos, gain):
    _, B, H, nc, K = xc.shape
    kern = functools.partial(_cmpmlp_kernel, nc=nc)
    return pl.pallas_call(
        kern,
        grid=(2, B, H),
        in_specs=[
            pl.BlockSpec((1, 1, 1, nc, K), lambda kv, b, h: (kv, b, h, 0, 0)),
            pl.BlockSpec((1, 2 * K, CMP_HIDDEN), lambda kv, b, h: (kv, 0, 0)),
            pl.BlockSpec((1, CMP_HIDDEN, LANES), lambda kv, b, h: (kv, 0, 0)),
            pl.BlockSpec((1, 8, 2 * K), lambda kv, b, h: (kv, 0, 0)),
            pl.BlockSpec((1, LANES), lambda kv, b, h: (0, 0)),
        ],
        out_specs=pl.BlockSpec((1, 1, 1, nc, LANES), lambda kv, b, h: (kv, b, h, 0, 0)),
        out_shape=jax.ShapeDtypeStruct((2, B, H, nc, LANES), F32),
        compiler_params=_cparams(("arbitrary", "arbitrary", "arbitrary")),
        name="cmpmlp",
    )(xc, w1s, w2s, pos, gain)


def _pack_heads(outs):
    lane = lax.broadcasted_iota(jnp.int32, outs[0].shape, 1)
    low = lane < HEAD_DIM
    pairs = [jnp.where(low, outs[2 * p], pltpu.roll(outs[2 * p + 1], HEAD_DIM, axis=1)) for p in range(2)]
    return jnp.concatenate(pairs, axis=-1)


def _cmpattn_kernel(sl_ref, q_ref, kc_ref, vc_ref, ov_ref, o_ref, ns_ref, *, tq, nc, n_cmp, n_sel, top):
    h = pl.program_id(1)
    q0 = pl.program_id(2) * tq
    kc = kc_ref[0, 0, 0].astype(BF16)
    vc = vc_ref[0, 0, 0].astype(BF16)

    t_col = q0 + lax.broadcasted_iota(jnp.int32, (tq, 1), 0)
    n_row = lax.broadcasted_iota(jnp.int32, (1, nc), 1)
    dist = t_col - (n_row * CMP_STRIDE + (CMP_LEN - 1))
    valid = (dist >= 0) & (n_row < n_cmp)
    distf = dist.astype(F32)
    row_valid = (t_col >= CMP_LEN - 1).astype(F32)
    t_row = q0 + lax.broadcasted_iota(jnp.int32, (1, tq), 1)
    n_col = lax.broadcasted_iota(jnp.int32, (nc, 1), 0)
    dist_t = t_row - (n_col * CMP_STRIDE + (CMP_LEN - 1))
    valid_t = (dist_t >= 0) & (n_col < n_cmp)
    dist_tf = dist_t.astype(F32)
    row_valid_t = (t_row >= CMP_LEN - 1).astype(F32)

    outs = []
    psum_t = jnp.zeros((nc, tq), F32)
    for g in range(G_NSA):
        slope = sl_ref[h * G_NSA + g]
        qg = q_ref[:, g * LANES:(g + 1) * LANES]
        s = _dot_nt(qg, kc) - slope * distf
        s = jnp.where(valid, s, NEG_INF)
        e = jnp.exp(s - jnp.max(s, axis=-1, keepdims=True))
        p = e / jnp.sum(e, axis=-1, keepdims=True) * row_valid
        outs.append(_dot(p.astype(BF16), vc))
        st = _dot_nt(kc, qg) - slope * dist_tf
        st = jnp.where(valid_t, st, NEG_INF)
        et = jnp.exp(st - jnp.max(st, axis=0, keepdims=True))
        psum_t = psum_t + et / jnp.sum(et, axis=0, keepdims=True)
    o_ref[...] = _pack_heads(outs).astype(o_ref.dtype)

    psum_t = psum_t * row_valid_t
    imp_t = _dot(ov_ref[...], psum_t, precision=lax.Precision.HIGHEST)
    j = lax.broadcasted_iota(jnp.int32, (n_sel, 1), 0)
    cur = t_row >> 6
    forced = (j == 0) | (j == cur) | (j == cur - 1)
    v = jnp.where(forced, 1e9, jnp.where(j > cur, NEG_INF, imp_t))
    cnt = jnp.zeros((n_sel, tq), F32)
    for jp in range(n_sel):
        vj = v[jp:jp + 1, :]
        cnt = cnt + jnp.where(j > jp, jnp.where(vj >= v, 1.0, 0.0), jnp.where(vj > v, 1.0, 0.0))
    notsel_t = jnp.where(cnt < top, 0.0, 1.0)
    if n_sel < LANES:
        notsel_t = jnp.concatenate([notsel_t, jnp.zeros((LANES - n_sel, tq), F32)], axis=0)
    ns_ref[0, 0] = notsel_t.T.astype(ns_ref.dtype)


def _cmpattn(slopes_nsa, proj, kvc, ov, *, B, T):
    tq = 256
    nt = T // tq
    nc = T // CMP_STRIDE
    n_cmp = (T - CMP_LEN) // CMP_STRIDE + 1
    n_sel = T // SEL_LEN
    top = min(SEL_TOPK, n_sel)
    assert n_sel <= LANES and n_sel % 8 == 0
    kern = functools.partial(_cmpattn_kernel, tq=tq, nc=nc, n_cmp=n_cmp, n_sel=n_sel, top=top)
    return pl.pallas_call(
        kern,
        grid=(B, N_KV_NSA, nt),
        in_specs=[
            pl.BlockSpec(memory_space=pltpu.SMEM),
            pl.BlockSpec((tq, G_NSA * LANES), lambda b, h, i: (b * nt + i, h)),
            pl.BlockSpec((1, 1, 1, nc, LANES), lambda b, h, i: (0, b, h, 0, 0)),
            pl.BlockSpec((1, 1, 1, nc, LANES), lambda b, h, i: (1, b, h, 0, 0)),
            pl.BlockSpec((n_sel, nc), lambda b, h, i: (0, 0)),
        ],
        out_specs=[
            pl.BlockSpec((tq, G_NSA * HEAD_DIM), lambda b, h, i: (b * nt + i, h)),
            pl.BlockSpec((1, 1, tq, LANES), lambda b, h, i: (b, h, i, 0)),
        ],
        out_shape=[
            jax.ShapeDtypeStruct((B * T, D_NSA), BF16),
            jax.ShapeDtypeStruct((B, N_KV_NSA, T, LANES), BF16),
        ],
        compiler_params=_cparams(("arbitrary", "arbitrary", "arbitrary")),
        name="cmpattn",
    )(slopes_nsa, proj, kvc, kvc, ov)


def _selattn_kernel(q_ref, ns_ref, k_ref, v_ref, bc_ref, o_ref, *, tq, tk):
    q0 = pl.program_id(2) * tq
    ns = ns_ref[0, 0]
    qx = jnp.concatenate(
        [jnp.concatenate([q_ref[:, g * LANES:(g + 1) * LANES], ns], axis=-1) for g in range(G_NSA)], axis=0)
    m_rows = G_NSA * tq
    diag = q0 // tk

    def step(kt, carry, masked):
        m, l, acc = carry
        start = pl.multiple_of(kt * tk, tk)
        kx = jnp.concatenate([k_ref[pl.ds(start, tk), :], bc_ref[pl.ds(start, tk), :]], axis=-1)
        s = _dot_nt(qx, kx)
        if masked:
            r = lax.broadcasted_iota(jnp.int32, (m_rows, tk), 0)
            col = lax.broadcasted_iota(jnp.int32, (m_rows, tk), 1)
            s = jnp.where(start + col <= q0 + (r & (tq - 1)), s, NEG_INF)
        m_new = jnp.maximum(m, jnp.max(s, axis=-1, keepdims=True))
        p = jnp.exp(s - m_new)
        alpha = jnp.exp(m - m_new)
        l = alpha * l + jnp.sum(p, axis=-1, keepdims=True)
        acc = alpha * acc + _dot(p.astype(BF16), v_ref[pl.ds(start, tk), :])
        return m_new, l, acc

    init = (jnp.full((m_rows, 1), NEG_INF, F32), jnp.zeros((m_rows, 1), F32), jnp.zeros((m_rows, LANES), F32))
    carry = lax.fori_loop(0, diag, functools.partial(step, masked=False), init)
    _, l, acc = step(diag, carry, True)
    o = acc / l
    o_ref[...] = _pack_heads([o[g * tq:(g + 1) * tq] for g in range(G_NSA)]).astype(o_ref.dtype)


def _selattn(proj, notsel, blkcols, *, B, T):
    tq = 128
    tk = 512
    nt = T // tq
    assert T % tk == 0 and tk % tq == 0
    kern = functools.partial(_selattn_kernel, tq=tq, tk=tk)
    return pl.pallas_call(
        kern,
        grid=(B, N_KV_NSA, nt),
        in_specs=[
            pl.BlockSpec((tq, G_NSA * LANES), lambda b, h, i: (b * nt + i, h)),
            pl.BlockSpec((1, 1, tq, LANES), lambda b, h, i: (b, h, i, 0)),
            pl.BlockSpec((T, LANES), lambda b, h, i: (b, SLOT_KSL + h)),
            pl.BlockSpec((T, LANES), lambda b, h, i: (b, SLOT_VSL + h)),
            pl.BlockSpec((T, LANES), lambda b, h, i: (0, 0)),
        ],
        out_specs=pl.BlockSpec((tq, G_NSA * HEAD_DIM), lambda b, h, i: (b * nt + i, h)),
        out_shape=jax.ShapeDtypeStruct((B * T, D_NSA), BF16),
        compiler_params=_cparams(("arbitrary", "arbitrary", "arbitrary")),
        name="selattn",
    )(proj, notsel, proj, proj, blkcols)


def _band_kernel(sl_ref, sink_ref, q_ref, k_ref, v_ref, o_ref, *, tq, window, seq, use_sinks):
    h = pl.program_id(1)
    q0 = pl.program_id(2) * tq
    span = window + tq
    start = pl.multiple_of(jnp.clip(q0 - window, 0, seq - span), tq)
    groups = q_ref.shape[1] // LANES
    m_rows = groups * tq
    qx = jnp.concatenate([q_ref[:, g * LANES:(g + 1) * LANES] for g in range(groups)], axis=0)
    kx = k_ref[pl.ds(start, span), :]
    s = _dot_nt(qx, kx)
    r = lax.broadcasted_iota(jnp.int32, (m_rows, span), 0)
    col = lax.broadcasted_iota(jnp.int32, (m_rows, span), 1)
    d = (q0 + (r & (tq - 1))) - (start + col)
    s = jnp.where((d >= 0) & (d < window), s, NEG_INF)
    m = jnp.max(s, axis=-1, keepdims=True)
    if use_sinks:
        t_col = (q0 + lax.broadcasted_iota(jnp.int32, (tq, 1), 0)).astype(F32)
        sink = jnp.concatenate(
            [sink_ref[h * groups + g] + sl_ref[h * groups + g] * t_col for g in range(groups)], axis=0)
        m = jnp.maximum(m, sink)
    e = jnp.exp(s - m)
    l = jnp.sum(e, axis=-1, keepdims=True)
    if use_sinks:
        l = l + jnp.exp(sink - m)
    o = _dot(e.astype(BF16), v_ref[pl.ds(start, span), :]) / l
    o_ref[...] = _pack_heads([o[g * tq:(g + 1) * tq] for g in range(groups)]).astype(o_ref.dtype)


def _band(slopes, sinks, proj, *, B, T, window, q_slot, k_slot, v_slot, use_sinks, name):
    tq = 128
    nt = T // tq
    groups = 4
    assert window % tq == 0 and window + tq <= T
    qb = q_slot // groups
    kern = functools.partial(_band_kernel, tq=tq, window=window, seq=T, use_sinks=use_sinks)
    return pl.pallas_call(
        kern,
        grid=(B, 2, nt),
        in_specs=[
            pl.BlockSpec(memory_space=pltpu.SMEM),
            pl.BlockSpec(memory_space=pltpu.SMEM),
            pl.BlockSpec((tq, groups * LANES), lambda b, h, i: (b * nt + i, qb + h)),
            pl.BlockSpec((T, LANES), lambda b, h, i: (b, k_slot + h)),
            pl.BlockSpec((T, LANES), lambda b, h, i: (b, v_slot + h)),
        ],
        out_specs=pl.BlockSpec((tq, groups * HEAD_DIM), lambda b, h, i: (b * nt + i, h)),
        out_shape=jax.ShapeDtypeStruct((B * T, 2 * groups * HEAD_DIM), BF16),
        compiler_params=_cparams(("arbitrary", "arbitrary", "arbitrary")),
        name=name,
    )(slopes, sinks, proj, proj, proj)


def _outproj_kernel(ocmp_ref, oslc_ref, owin_ref, oswa_ref, gate_ref, x_ref, mod_ref, ex_ref, bn_ref, bs_ref,
                    w_ref, g2_ref, wr_ref, x1_ref, u2_ref, lt_ref):
    sg = _sigmoid(gate_ref[...].astype(F32)).astype(BF16)
    o_nsa = (_dot(sg, ex_ref[0]) * ocmp_ref[...].astype(F32)
             + _dot(sg, ex_ref[1]) * oslc_ref[...].astype(F32)
             + _dot(sg, ex_ref[2]) * owin_ref[...].astype(F32))
    o_swa = oswa_ref[...].astype(F32)
    n1 = o_nsa * lax.rsqrt(jnp.mean(o_nsa * o_nsa, axis=-1, keepdims=True) + EPS) * bn_ref[...]
    n2 = o_swa * lax.rsqrt(jnp.mean(o_swa * o_swa, axis=-1, keepdims=True) + EPS) * bs_ref[...]
    merged = jnp.concatenate([n1, n2], axis=-1).astype(BF16)
    y = _dot(merged, w_ref[...])
    x1 = x_ref[...] + mod_ref[0, 2:3, :] * y
    x1_ref[...] = x1
    u2 = x1 * lax.rsqrt(jnp.mean(x1 * x1, axis=-1, keepdims=True) + EPS) * g2_ref[...]
    u2 = u2 * (1.0 + mod_ref[0, 4:5, :]) + mod_ref[0, 3:4, :]
    u2_ref[...] = u2.astype(u2_ref.dtype)
    lt_ref[...] = lax.dot_general(wr_ref[...], u2, (((1,), (1,)), ((), ())), preferred_element_type=F32,
                                  precision=lax.Precision.HIGHEST)


def _outproj(ocmp, oslc, owin, oswa, proj, x2, mod, expand, beta_n, beta_s, w_out, norm2_g, wr_t, *, seq):
    N, D = x2.shape
    tm = 512
    tiles_per_seq = seq // tm
    row = lambda i: (i, 0)
    const2 = lambda i: (0, 0)
    return pl.pallas_call(
        _outproj_kernel,
        grid=(N // tm,),
        in_specs=[
            pl.BlockSpec((tm, D_NSA), row),
            pl.BlockSpec((tm, D_NSA), row),
            pl.BlockSpec((tm, D_NSA), row),
            pl.BlockSpec((tm, D_SWA), row),
            pl.BlockSpec((tm, LANES), lambda i: (i, SLOT_GATE)),
            pl.BlockSpec((tm, D), row),
            pl.BlockSpec((1, 6, D), lambda i: (i // tiles_per_seq, 0, 0)),
            pl.BlockSpec((N_NSA_BRANCHES, LANES, D_NSA), lambda i: (0, 0, 0)),
            pl.BlockSpec((1, D_NSA), const2),
            pl.BlockSpec((1, D_SWA), const2),
            pl.BlockSpec((D_MIX, D), const2),
            pl.BlockSpec((1, D), const2),
            pl.BlockSpec((N_EXPERTS, D), const2),
        ],
        out_specs=[
            pl.BlockSpec((tm, D), row),
            pl.BlockSpec((tm, D), row),
            pl.BlockSpec((N_EXPERTS, tm), lambda i: (0, i)),
        ],
        out_shape=[
            jax.ShapeDtypeStruct((N, D), F32),
            jax.ShapeDtypeStruct((N, D), BF16),
            jax.ShapeDtypeStruct((N_EXPERTS, N), F32),
        ],
        compiler_params=_cparams(("arbitrary",)),
        name="outproj",
    )(ocmp, oslc, owin, oswa, proj, x2, mod, expand, beta_n, beta_s, w_out, norm2_g, wr_t)


def _route_kernel(lt_ref, g_ref, *, tm):
    lt = lt_ref[...]
    e = jnp.exp(lt - jnp.max(lt, axis=0, keepdims=True))
    aff = e / jnp.sum(e, axis=0, keepdims=True)
    rows = [aff[k:k + 1, :] for k in range(N_EXPERTS)]
    epg = EXPERTS_PER_GROUP
    scores = []
    for gr in range(N_GROUPS):
        xs = rows[gr * epg:(gr + 1) * epg]
        top1 = functools.reduce(jnp.maximum, xs)
        second = None
        for a in range(epg):
            for b in range(a + 1, epg):
                mn = jnp.minimum(xs[a], xs[b])
                second = mn if second is None else jnp.maximum(second, mn)
        scores.append(top1 + second)
    taken = None
    in_group = []
    for gr in range(N_GROUPS):
        best = None
        for o in range(gr + 1, N_GROUPS):
            c = scores[gr] >= scores[o]
            best = c if best is None else (best & c)
        if best is None:
            best = jnp.full(scores[gr].shape, True)
        sel = best if taken is None else (best & ~taken)
        taken = sel if taken is None else (taken | sel)
        in_group.append(sel)
    ys = []
    for k in range(epg):
        yk = rows[k]
        for gr in range(1, N_GROUPS):
            yk = jnp.where(in_group[gr], rows[gr * epg + k], yk)
        ys.append(yk)
    chosen = []
    for k in range(epg):
        rk = jnp.zeros_like(ys[k])
        for o in range(epg):
            if o == k:
                continue
            before = (ys[o] >= ys[k]) if o < k else (ys[o] > ys[k])
            rk = rk + jnp.where(before, 1.0, 0.0)
        chosen.append(rk < TOP_K)
    total = functools.reduce(lambda a, b: a + b, [jnp.where(chosen[k], ys[k], 0.0) for k in range(epg)])
    gates = []
    for ex in range(N_EXPERTS):
        gr, k = divmod(ex, epg)
        gates.append(jnp.where(in_group[gr] & chosen[k], ys[k] / total, 0.0))
    gt = jnp.concatenate(gates + [jnp.zeros((LANES - N_EXPERTS, tm), F32)], axis=0)
    g_ref[...] = gt.T


def _route(lt):
    E, N = lt.shape
    tm = 512
    return pl.pallas_call(
        functools.partial(_route_kernel, tm=tm),
        grid=(N // tm,),
        in_specs=[pl.BlockSpec((E, tm), lambda i: (0, i))],
        out_specs=pl.BlockSpec((tm, LANES), lambda i: (i, 0)),
        out_shape=jax.ShapeDtypeStruct((N, LANES), F32),
        compiler_params=_cparams(("arbitrary",)),
        name="route",
    )(lt)


def _moe_kernel(u_ref, gate_ref, w1_ref, w3_ref, w2_ref, x1_ref, mod_ref, o_ref, acc_ref):
    ex = pl.program_id(1)

    @pl.when(ex == 0)
    def _():
        acc_ref[...] = jnp.zeros_like(acc_ref)

    u = u_ref[...]
    a = _dot(u, w1_ref[0])
    hmid = a * _sigmoid(a) * _dot(u, w3_ref[0])
    lane = lax.broadcasted_iota(jnp.int32, gate_ref.shape, 1)
    ge = jnp.sum(jnp.where(lane == ex, gate_ref[...], 0.0), axis=-1, keepdims=True)
    acc_ref[...] += _dot((hmid * ge).astype(BF16), w2_ref[0])

    @pl.when(ex == pl.num_programs(1) - 1)
    def _():
        o_ref[...] = x1_ref[...] + mod_ref[0, 5:6, :] * acc_ref[...]


def _moe(u2, gate, w1, w3, w2, x1, mod, *, seq):
    N, D = u2.shape
    E = w1.shape[0]
    tm = 1024
    tiles_per_seq = seq // tm
    return pl.pallas_call(
        _moe_kernel,
        grid=(N // tm, E),
        in_specs=[
            pl.BlockSpec((tm, D), lambda i, e: (i, 0)),
            pl.BlockSpec((tm, LANES), lambda i, e: (i, 0)),
            pl.BlockSpec((1, D, D_EXPERT), lambda i, e: (e, 0, 0)),
            pl.BlockSpec((1, D, D_EXPERT), lambda i, e: (e, 0, 0)),
            pl.BlockSpec((1, D_EXPERT, D), lambda i, e: (e, 0, 0)),
            pl.BlockSpec((tm, D), lambda i, e: (i, 0)),
            pl.BlockSpec((1, 6, D), lambda i, e: (i // tiles_per_seq, 0, 0)),
        ],
        out_specs=pl.BlockSpec((tm, D), lambda i, e: (i, 0)),
        out_shape=jax.ShapeDtypeStruct((N, D), F32),
        scratch_shapes=[pltpu.VMEM((tm, D), F32)],
        compiler_params=_cparams(("arbitrary", "arbitrary")),
        name="moe",
    )(u2, gate, w1, w3, w2, x1, mod)


def _alibi_slopes(first, count):
    hh = jnp.arange(first + 1, first + count + 1, dtype=F32)
    return jnp.exp2(-8.0 * hh / N_HEADS_TOTAL)


def _bd_const():
    idx = np.arange(LANES) // HEAD_DIM
    return jnp.asarray((idx[:, None] == idx[None, :]).astype(np.float32), BF16)


def _blkcols_const(T):
    blk = np.arange(T) // SEL_LEN
    m = np.where(np.arange(LANES)[None, :] == blk[:, None], NEG_INF, 0.0).astype(np.float32)
    return jnp.asarray(m, BF16)


def _overlap_const(T):
    nc = T // CMP_STRIDE
    n_sel = T // SEL_LEN
    cs = np.arange(nc) * CMP_STRIDE
    ss = np.arange(n_sel) * SEL_LEN
    ov = np.clip(np.minimum(cs[:, None] + CMP_LEN, ss[None, :] + SEL_LEN)
                 - np.maximum(cs[:, None], ss[None, :]), 0, None) / CMP_LEN
    return jnp.asarray(ov.T.astype(np.float32))


def _expand_const():
    ex = np.zeros((N_NSA_BRANCHES, LANES, D_NSA), np.float32)
    for br in range(N_NSA_BRANCHES):
        for hd in range(N_HEADS_NSA):
            ex[br, hd * N_NSA_BRANCHES + br, hd * HEAD_DIM:(hd + 1) * HEAD_DIM] = 1.0
    return jnp.asarray(ex, BF16)


def _compact_w_in(w):
    parts = [w[:, a:b] for a, b in (_C_QN, _C_QS, _C_KSL, _C_KW, _C_KS, _C_VSL, _C_VW, _C_VS, _C_KC, _C_VC, _C_GATE)]
    pad = jnp.zeros((w.shape[0], LANES - (_C_GATE[1] - _C_GATE[0])), w.dtype)
    return jnp.concatenate(parts + [pad], axis=1).astype(BF16)


def _qconst(slopes_q):
    hi = slopes_q.astype(BF16).astype(F32)
    lo = (slopes_q - hi).astype(BF16).astype(F32)
    cols = jnp.stack([64.0 * hi, 64.0 * lo, hi, lo], axis=1)
    return jnp.zeros((slopes_q.shape[0], LANES), F32).at[:, HEAD_DIM:HEAD_DIM + 4].set(cols)


def kernel(x, c, w_router, ada_w, ada_b, norm1_g, norm2_g, w_in, w_out, nsa_q_gain, nsa_k_gain, cmp_pos_k,
           cmp_pos_v, cmp_w1_k, cmp_w2_k, cmp_w1_v, cmp_w2_v, swa_q_gain, swa_k_gain, swa_sinks, beta_nsa,
           beta_swa, moe_w1, moe_w3, moe_w2):
    B, T, D = x.shape
    L = ada_w.shape[0]
    N = B * T
    nc = T // CMP_STRIDE
    assert D == D_MODEL and T % 512 == 0 and T <= 64 * 128 and B <= 8

    slopes_nsa = _alibi_slopes(N_HEADS_SWA, N_HEADS_NSA)
    slopes_swa = _alibi_slopes(0, N_HEADS_SWA)
    qconst = _qconst(jnp.concatenate([slopes_nsa, slopes_swa]))
    bd = _bd_const()
    blkcols = _blkcols_const(T)
    ov = _overlap_const(T)
    expand = _expand_const()
    wr_t = w_router.T
    zero_sinks = jnp.zeros((N_HEADS_NSA,), F32)

    c_pad = jnp.zeros((8, D), F32).at[:B].set(c)
    mod_all = _ada(c_pad, ada_w, ada_b)[:, :B].reshape(L, B, 6, D)

    x2 = x.reshape(N, D)
    for l in range(L):
        mod = mod_all[l]
        tile2 = lambda g: jnp.tile(g, 2)
        gain_c = jnp.concatenate([
            jnp.tile(nsa_q_gain[l], N_HEADS_NSA) * ATTN_SCALE,
            jnp.tile(swa_q_gain[l], N_HEADS_SWA) * ATTN_SCALE,
            tile2(nsa_k_gain[l, 1]), tile2(nsa_k_gain[l, 2]), tile2(swa_k_gain[l])]).reshape(1, -1)
        proj = _inproj(x2, mod, norm1_g[l].reshape(1, D), _compact_w_in(w_in[l]), gain_c, qconst, bd, seq=T)

        kvc_raw = proj[:, SLOT_KC * LANES:(SLOT_VC + 1) * LANES]
        xc = kvc_raw.reshape(B, nc, CMP_STRIDE, 2, N_KV_NSA, HEAD_DIM).transpose(3, 0, 4, 1, 2, 5)
        xc = xc.reshape(2, B, N_KV_NSA, nc, CMP_STRIDE * HEAD_DIM)
        w1s = jnp.stack([cmp_w1_k[l], cmp_w1_v[l]]).astype(BF16)
        w2s = jnp.pad(jnp.stack([cmp_w2_k[l], cmp_w2_v[l]]), ((0, 0), (0, 0), (0, LANES - HEAD_DIM))).astype(BF16)
        pos = jnp.stack([cmp_pos_k[l].reshape(-1), cmp_pos_v[l].reshape(-1)])
        pos = jnp.broadcast_to(pos[:, None, :], (2, 8, CMP_LEN * HEAD_DIM)).astype(BF16)
        kgain = jnp.pad(nsa_k_gain[l, 0], (0, LANES - HEAD_DIM)).reshape(1, LANES)
        kvc = _cmpmlp(xc, w1s, w2s, pos, kgain)

        o_cmp, notsel = _cmpattn(slopes_nsa, proj, kvc, ov, B=B, T=T)
        o_slc = _selattn(proj, notsel, blkcols, B=B, T=T)
        o_win = _band(slopes_nsa, zero_sinks, proj, B=B, T=T, window=NSA_WINDOW, q_slot=SLOT_QN,
                      k_slot=SLOT_KW, v_slot=SLOT_VW, use_sinks=False, name="winattn")
        o_swa = _band(slopes_swa, swa_sinks[l], proj, B=B, T=T, window=SWA_WINDOW, q_slot=SLOT_QS,
                      k_slot=SLOT_KS, v_slot=SLOT_VS, use_sinks=True, name="swaattn")

        x1, u2, lt = _outproj(o_cmp, o_slc, o_win, o_swa, proj, x2, mod, expand,
                              beta_nsa[l].reshape(1, -1), beta_swa[l].reshape(1, -1), w_out[l].astype(BF16),
                              norm2_g[l].reshape(1, D), wr_t, seq=T)
        gate = _route(lt)
        x2 = _moe(u2, gate, moe_w1[l].astype(BF16), moe_w3[l].astype(BF16), moe_w2[l].astype(BF16), x1, mod, seq=T)
    return x2.reshape(B, T, D)
```

```python
import functools

import numpy as np
import jax
import jax.numpy as jnp
from jax import lax
from jax.experimental import pallas as pl
from jax.experimental.pallas import tpu as pltpu

D_MODEL = 1024
DEPTH = 2
HEAD_DIM = 64
N_HEADS_NSA = 8
N_KV_NSA = 2
G_NSA = N_HEADS_NSA // N_KV_NSA
N_HEADS_SWA = 8
N_KV_SWA = 2
G_SWA = N_HEADS_SWA // N_KV_SWA
D_NSA = N_HEADS_NSA * HEAD_DIM
D_SWA = N_HEADS_SWA * HEAD_DIM
D_MIX = D_NSA + D_SWA
N_HEADS_TOTAL = N_HEADS_NSA + N_HEADS_SWA
KV_NSA = N_KV_NSA * HEAD_DIM
KV_SWA = N_KV_SWA * HEAD_DIM
N_NSA_BRANCHES = 3
CMP_LEN = 32
CMP_STRIDE = 16
CMP_HIDDEN = 256
SEL_LEN = 64
SEL_TOPK = 16
NSA_WINDOW = 512
SWA_WINDOW = 128
N_EXPERTS = 16
N_GROUPS = 4
EXPERTS_PER_GROUP = N_EXPERTS // N_GROUPS
TOP_K = 2
D_EXPERT = 512
EPS = 1e-6
NEG_INF = -1e30
ATTN_SCALE = HEAD_DIM ** -0.5

F32 = jnp.float32
BF16 = jnp.bfloat16

LANES = 128
VMEM_LIMIT = 48 * 1024 * 1024

SLOT_QN = 0
SLOT_QS = 8
SLOT_KSL = 16
SLOT_KW = 18
SLOT_KS = 20
SLOT_VSL = 22
SLOT_VW = 24
SLOT_VS = 26
SLOT_KC = 28
SLOT_VC = 29
SLOT_GATE = 30
N_SLOTS = 31
N_CHUNKS = 17
N_NORM_CHUNKS = 11

_C_QN = (0, 512)
_C_KC = (512, 640)
_C_VC = (640, 768)
_C_KSL = (768, 896)
_C_VSL = (896, 1024)
_C_KW = (1024, 1152)
_C_VW = (1152, 1280)
_C_GATE = (1280, 1304)
_C_QS = (1304, 1816)
_C_KS = (1816, 1944)
_C_VS = (1944, 2072)


def _cparams(sem):
    return pltpu.CompilerParams(dimension_semantics=sem, vmem_limit_bytes=VMEM_LIMIT)


def _dot(a, b, **kw):
    return jnp.dot(a, b, preferred_element_type=F32, **kw)


def _dot_nt(a, b):
    return lax.dot_general(a, b, (((1,), (1,)), ((), ())), preferred_element_type=F32)


def _sigmoid(x):
    return 1.0 / (1.0 + jnp.exp(-x))


def _ada_kernel(c_ref, w_ref, b_ref, o_ref):
    c = c_ref[...]
    cond = c * _sigmoid(c)
    o_ref[0] = _dot(cond.astype(BF16), w_ref[0].astype(BF16)) + b_ref[0]


def _ada(c_pad, ada_w, ada_b):
    L, D, N6 = ada_w.shape
    tn = 1536
    return pl.pallas_call(
        _ada_kernel,
        grid=(L, N6 // tn),
        in_specs=[
            pl.BlockSpec((8, D), lambda l, j: (0, 0)),
            pl.BlockSpec((1, D, tn), lambda l, j: (l, 0, j)),
            pl.BlockSpec((1, 1, tn), lambda l, j: (l, 0, j)),
        ],
        out_specs=pl.BlockSpec((1, 8, tn), lambda l, j: (l, 0, j)),
        out_shape=jax.ShapeDtypeStruct((L, 8, N6), F32),
        compiler_params=_cparams(("arbitrary", "arbitrary")),
        name="ada",
    )(c_pad, ada_w, ada_b.reshape(L, 1, N6))


def _inproj_kernel(x_ref, mod_ref, g_ref, w_ref, gain_ref, qc_ref, bd_ref, o_ref, *, tm, seq):
    i = pl.program_id(0)
    x = x_ref[...]
    ms = jnp.mean(x * x, axis=-1, keepdims=True)
    u = x * lax.rsqrt(ms + EPS) * g_ref[...]
    u = u * (1.0 + mod_ref[0, 1:2, :]) + mod_ref[0, 0:1, :]
    y = _dot(u.astype(BF16), w_ref[...])

    lane = lax.broadcasted_iota(jnp.int32, (tm, LANES), 1)
    low = lane < HEAD_DIM
    t = lax.broadcasted_iota(jnp.int32, (tm, LANES), 0) + lax.rem(i * tm, seq)
    hi_part = (t >> 6).astype(F32)
    lo_part = (t & 63).astype(F32)
    poscols = jnp.where((lane == 64) | (lane == 65), hi_part,
                        jnp.where((lane == 66) | (lane == 67), lo_part, 0.0))
    bd = bd_ref[...]

    def put(slot, val):
        o_ref[:, slot * LANES:(slot + 1) * LANES] = val.astype(o_ref.dtype)

    for c in range(N_CHUNKS):
        blk = y[:, c * LANES:(c + 1) * LANES]
        if c < N_NORM_CHUNKS:
            msb = _dot((blk * blk).astype(BF16), bd) * (1.0 / HEAD_DIM)
            blk = blk * lax.rsqrt(msb + EPS) * gain_ref[:, c * LANES:(c + 1) * LANES]
        if c < 14:
            rolled = pltpu.roll(blk, HEAD_DIM, axis=1)
            if c < 8:
                f0 = qc_ref[2 * c:2 * c + 1, :]
                f1 = qc_ref[2 * c + 1:2 * c + 2, :]
                s0 = 2 * c
            elif c < N_NORM_CHUNKS:
                f0 = f1 = poscols
                s0 = SLOT_KSL + 2 * (c - 8)
            else:
                f0 = f1 = 1.0
                s0 = SLOT_VSL + 2 * (c - N_NORM_CHUNKS)
            put(s0, jnp.where(low, blk, f0))
            put(s0 + 1, jnp.where(low, rolled, f1))
        else:
            put(SLOT_KC + (c - 14), blk)


def _inproj(x2, mod, norm_g, w_c, gain_c, qconst, bd, *, seq):
    N, D = x2.shape
    tm = 512
    tiles_per_seq = seq // tm
    kern = functools.partial(_inproj_kernel, tm=tm, seq=seq)
    return pl.pallas_call(
        kern,
        grid=(N // tm,),
        in_specs=[
            pl.BlockSpec((tm, D), lambda i: (i, 0)),
            pl.BlockSpec((1, 6, D), lambda i: (i // tiles_per_seq, 0, 0)),
            pl.BlockSpec((1, D), lambda i: (0, 0)),
            pl.BlockSpec((D, N_CHUNKS * LANES), lambda i: (0, 0)),
            pl.BlockSpec((1, N_NORM_CHUNKS * LANES), lambda i: (0, 0)),
            pl.BlockSpec((16, LANES), lambda i: (0, 0)),
            pl.BlockSpec((LANES, LANES), lambda i: (0, 0)),
        ],
        out_specs=pl.BlockSpec((tm, N_SLOTS * LANES), lambda i: (i, 0)),
        out_shape=jax.ShapeDtypeStruct((N, N_SLOTS * LANES), BF16),
        compiler_params=_cparams(("arbitrary",)),
        name="inproj",
    )(x2, mod, norm_g, w_c, gain_c, qconst, bd)


def _gelu_tanh(x):
    return 0.5 * x * (1.0 + jnp.tanh(np.sqrt(2.0 / np.pi).astype(np.float32) * (x + 0.044715 * (x * x * x))))


def _cmpmlp_kernel(x_ref, w1_ref, w2_ref, pos_ref, gain_ref, o_ref, *, nc):
    kv = pl.program_id(0)
    xc = x_ref[0, 0, 0]
    w1 = w1_ref[0]
    half = (CMP_LEN // 2) * HEAD_DIM
    first = _dot(xc, w1[:half])
    second = _dot(xc, w1[half:])
    bias = _dot(pos_ref[0], w1)[0:1]
    h = first + pltpu.roll(second, nc - 1, axis=0) + bias
    h = _gelu_tanh(h)
    z = _dot(h.astype(BF16), w2_ref[0])
    zn = z * lax.rsqrt(jnp.sum(z * z, axis=-1, keepdims=True) * (1.0 / HEAD_DIM) + EPS) * gain_ref[...]
    lane = lax.broadcasted_iota(jnp.int32, (nc, LANES), 1)
    last = lax.broadcasted_iota(jnp.int32, (nc, LANES), 0) * CMP_STRIDE + (CMP_LEN - 1)
    poscols = jnp.where((lane == 64) | (lane == 65), (last >> 6).astype(F32),
                        jnp.where((lane == 66) | (lane == 67), (last & 63).astype(F32), 0.0))
    o_ref[0, 0, 0] = jnp.where(kv == 0, zn + poscols, z + jnp.where(lane >= HEAD_DIM, 1.0, 0.0))


def _cmpmlp(xc, w1s, w2s, pos, gain):
    _, B, H, nc, K = xc.shape
    kern = functools.partial(_cmpmlp_kernel, nc=nc)
    return pl.pallas_call(
        kern,
        grid=(2, B, H),
        in_specs=[
            pl.BlockSpec((1, 1, 1, nc, K), lambda kv, b, h: (kv, b, h, 0, 0)),
            pl.BlockSpec((1, 2 * K, CMP_HIDDEN), lambda kv, b, h: (kv, 0, 0)),
            pl.BlockSpec((1, CMP_HIDDEN, LANES), lambda kv, b, h: (kv, 0, 0)),
            pl.BlockSpec((1, 8, 2 * K), lambda kv, b, h: (kv, 0, 0)),
            pl.BlockSpec((1, LANES), lambda kv, b, h: (0, 0)),
        ],
        out_specs=pl.BlockSpec((1, 1, 1, nc, LANES), lambda kv, b, h: (kv, b, h, 0, 0)),
        out_shape=jax.ShapeDtypeStruct((2, B, H, nc, LANES), F32),
        compiler_params=_cparams(("arbitrary", "arbitrary", "arbitrary")),
        name="cmpmlp",
    )(xc, w1s, w2s, pos, gain)


def _normalize_pack(pvs, extra=None):
    lane = lax.broadcasted_iota(jnp.int32, pvs[0].shape, 1)
    low = lane < HEAD_DIM
    pairs = []
    for p in range(len(pvs) // 2):
        even, odd = pvs[2 * p], pvs[2 * p + 1]
        den_e = even if extra is None else even + extra[2 * p]
        den_o = odd if extra is None else odd + extra[2 * p + 1]
        o_e = even * (1.0 / pltpu.roll(den_e, HEAD_DIM, axis=1))
        o_o = pltpu.roll(odd, HEAD_DIM, axis=1) * (1.0 / den_o)
        pairs.append(jnp.where(low, o_e, o_o))
    return jnp.concatenate(pairs, axis=-1)


def _cmpattn_kernel(q_ref, kc_ref, vc_ref, ov_ref, o_ref, ns_ref, *, tq, nc, n_cmp, n_sel, top):
    q0 = pl.program_id(2) * tq
    kc = kc_ref[0, 0, 0].astype(BF16)
    vc = vc_ref[0, 0, 0].astype(BF16)

    t_col = q0 + lax.broadcasted_iota(jnp.int32, (tq, 1), 0)
    n_row = lax.broadcasted_iota(jnp.int32, (1, nc), 1)
    valid = (t_col - (n_row * CMP_STRIDE + (CMP_LEN - 1)) >= 0) & (n_row < n_cmp)
    bias = jnp.where(valid, 0.0, NEG_INF)
    row_valid = (t_col >= CMP_LEN - 1).astype(F32)
    t_row = q0 + lax.broadcasted_iota(jnp.int32, (1, tq), 1)
    n_col = lax.broadcasted_iota(jnp.int32, (nc, 1), 0)
    valid_t = (t_row - (n_col * CMP_STRIDE + (CMP_LEN - 1)) >= 0) & (n_col < n_cmp)
    bias_t = jnp.where(valid_t, 0.0, NEG_INF)
    row_valid_t = (t_row >= CMP_LEN - 1).astype(F32)

    outs = []
    psum_t = jnp.zeros((nc, tq), F32)
    for g in range(G_NSA):
        qg = q_ref[:, g * LANES:(g + 1) * LANES]
        s = _dot_nt(qg, kc) + bias
        e = jnp.exp(s - jnp.max(s, axis=-1, keepdims=True))
        outs.append(_dot(e.astype(BF16), vc))
        st = _dot_nt(kc, qg) + bias_t
        et = jnp.exp(st - jnp.max(st, axis=0, keepdims=True))
        psum_t = psum_t + et * (1.0 / jnp.sum(et, axis=0, keepdims=True))
    o_ref[...] = (_normalize_pack(outs) * row_valid).astype(o_ref.dtype)

    psum_t = psum_t * row_valid_t
    imp_t = _dot(ov_ref[...], psum_t, precision=lax.Precision.HIGHEST)
    j = lax.broadcasted_iota(jnp.int32, (n_sel, 1), 0)
    cur = t_row >> 6
    forced = (j == 0) | (j == cur) | (j == cur - 1)
    v = jnp.where(forced, 1e9, jnp.where(j > cur, NEG_INF, imp_t))
    n_grp = n_sel // 8
    vg = [v[8 * r:8 * r + 8, :] for r in range(n_grp)]
    jg = lax.broadcasted_iota(jnp.int32, (8, 1), 0)
    cnt = [jnp.zeros((8, tq), F32) for _ in range(n_grp)]
    for jp in range(n_sel):
        vj = v[jp:jp + 1, :]
        for r in range(n_grp):
            ge = jnp.where(vj >= vg[r], 1.0, 0.0)
            gt = jnp.where(vj > vg[r], 1.0, 0.0)
            if 8 * r > jp:
                inc = ge
            elif 8 * r + 7 < jp:
                inc = gt
            else:
                inc = jnp.where(jg + 8 * r > jp, ge, gt)
            cnt[r] = cnt[r] + inc
    notsel_t = jnp.concatenate([jnp.where(cn < top, 0.0, 1.0) for cn in cnt], axis=0)
    if n_sel < LANES:
        notsel_t = jnp.concatenate([notsel_t, jnp.zeros((LANES - n_sel, tq), F32)], axis=0)
    ns_ref[0, 0] = notsel_t.T.astype(ns_ref.dtype)


def _cmpattn(proj, kvc, ov, *, B, T):
    tq = 256
    nt = T // tq
    nc = T // CMP_STRIDE
    n_cmp = (T - CMP_LEN) // CMP_STRIDE + 1
    n_sel = T // SEL_LEN
    top = min(SEL_TOPK, n_sel)
    assert n_sel <= LANES and n_sel % 8 == 0
    kern = functools.partial(_cmpattn_kernel, tq=tq, nc=nc, n_cmp=n_cmp, n_sel=n_sel, top=top)
    return pl.pallas_call(
        kern,
        grid=(B, N_KV_NSA, nt),
        in_specs=[
            pl.BlockSpec((tq, G_NSA * LANES), lambda b, h, i: (b * nt + i, h)),
            pl.BlockSpec((1, 1, 1, nc, LANES), lambda b, h, i: (0, b, h, 0, 0)),
            pl.BlockSpec((1, 1, 1, nc, LANES), lambda b, h, i: (1, b, h, 0, 0)),
            pl.BlockSpec((n_sel, nc), lambda b, h, i: (0, 0)),
        ],
        out_specs=[
            pl.BlockSpec((tq, G_NSA * HEAD_DIM), lambda b, h, i: (b * nt + i, h)),
            pl.BlockSpec((1, 1, tq, LANES), lambda b, h, i: (b, h, i, 0)),
        ],
        out_shape=[
            jax.ShapeDtypeStruct((B * T, D_NSA), BF16),
            jax.ShapeDtypeStruct((B, N_KV_NSA, T, LANES), BF16),
        ],
        compiler_params=_cparams(("arbitrary", "arbitrary", "arbitrary")),
        name="cmpattn",
    )(proj, kvc, kvc, ov)


def _selattn_kernel(q_ref, ns_ref, k_ref, v_ref, bc_ref, o_ref, *, tq, tk):
    q0 = pl.program_id(2) * tq
    ns = ns_ref[0, 0]
    qx = jnp.concatenate(
        [jnp.concatenate([q_ref[:, g * LANES:(g + 1) * LANES], ns], axis=-1) for g in range(G_NSA)], axis=0)
    m_rows = G_NSA * tq
    diag = q0 // tk

    def step(kt, carry, masked):
        m, acc = carry
        start = pl.multiple_of(kt * tk, tk)
        kx = jnp.concatenate([k_ref[pl.ds(start, tk), :], bc_ref[pl.ds(start, tk), :]], axis=-1)
        s = _dot_nt(qx, kx)
        if masked:
            r = lax.broadcasted_iota(jnp.int32, (tq, tk), 0)
            col = lax.broadcasted_iota(jnp.int32, (tq, tk), 1)
            s = s + jnp.concatenate([jnp.where(col <= r, 0.0, NEG_INF)] * G_NSA, axis=0)
        m_new = jnp.maximum(m, jnp.max(s, axis=-1, keepdims=True))
        p = jnp.exp(s - m_new).astype(BF16)
        acc = jnp.exp(m - m_new) * acc + _dot(p, v_ref[pl.ds(start, tk), :])
        return m_new, acc

    init = (jnp.full((m_rows, 1), NEG_INF, F32), jnp.zeros((m_rows, LANES), F32))
    carry = lax.fori_loop(0, diag, functools.partial(step, masked=False), init)
    _, acc = step(diag, carry, True)
    o_ref[...] = _normalize_pack([acc[g * tq:(g + 1) * tq] for g in range(G_NSA)]).astype(o_ref.dtype)


def _selattn(proj, notsel, blkcols, *, B, T):
    tq = 512
    tk = 512
    nt = T // tq
    assert T % tk == 0 and tk == tq
    kern = functools.partial(_selattn_kernel, tq=tq, tk=tk)
    return pl.pallas_call(
        kern,
        grid=(B, N_KV_NSA, nt),
        in_specs=[
            pl.BlockSpec((tq, G_NSA * LANES), lambda b, h, i: (b * nt + i, h)),
            pl.BlockSpec((1, 1, tq, LANES), lambda b, h, i: (b, h, i, 0)),
            pl.BlockSpec((T, LANES), lambda b, h, i: (b, SLOT_KSL + h)),
            pl.BlockSpec((T, LANES), lambda b, h, i: (b, SLOT_VSL + h)),
            pl.BlockSpec((T, LANES), lambda b, h, i: (0, 0)),
        ],
        out_specs=pl.BlockSpec((tq, G_NSA * HEAD_DIM), lambda b, h, i: (b * nt + i, h)),
        out_shape=jax.ShapeDtypeStruct((B * T, D_NSA), BF16),
        compiler_params=_cparams(("arbitrary", "arbitrary", "arbitrary")),
        name="selattn",
    )(proj, notsel, proj, proj, blkcols)


def _band_kernel(sl_ref, sink_ref, q_ref, k_ref, v_ref, o_ref, *, tq, n_sub, window, seq, use_sinks):
    h = pl.program_id(1)
    span = window + tq
    groups = q_ref.shape[1] // LANES
    r = lax.broadcasted_iota(jnp.int32, (tq, span), 0)
    col = lax.broadcasted_iota(jnp.int32, (tq, span), 1)
    for sub in range(n_sub):
        q0 = (pl.program_id(2) * n_sub + sub) * tq
        start = pl.multiple_of(jnp.clip(q0 - window, 0, seq - span), tq)
        rows = pl.ds(sub * tq, tq)
        qx = jnp.concatenate([q_ref[rows, g * LANES:(g + 1) * LANES] for g in range(groups)], axis=0)
        d = (q0 - start) + r - col
        bias = jnp.where((d >= 0) & (d < window), 0.0, NEG_INF)
        s = _dot_nt(qx, k_ref[pl.ds(start, span), :]) + jnp.concatenate([bias] * groups, axis=0)
        m = jnp.broadcast_to(jnp.max(s, axis=-1, keepdims=True), (groups * tq, LANES))
        if use_sinks:
            t_rep = (q0 + lax.broadcasted_iota(jnp.int32, (tq, LANES), 0)).astype(F32)
            sink = jnp.concatenate(
                [sink_ref[h * groups + g] + sl_ref[h * groups + g] * t_rep for g in range(groups)], axis=0)
            m = jnp.maximum(m, sink)
        e = jnp.exp(s - jnp.concatenate([m] * (span // LANES), axis=1)).astype(BF16)
        pv = _dot(e, v_ref[pl.ds(start, span), :])
        extra = None
        if use_sinks:
            esink = jnp.exp(sink - m)
            extra = [esink[g * tq:(g + 1) * tq] for g in range(groups)]
        o_ref[rows, :] = _normalize_pack([pv[g * tq:(g + 1) * tq] for g in range(groups)], extra).astype(o_ref.dtype)


def _band(slopes, sinks, proj, *, B, T, window, q_slot, k_slot, v_slot, use_sinks, name):
    tq = 128
    n_sub = 4
    tb = tq * n_sub
    nt = T // tb
    groups = 4
    assert window % tq == 0 and window + tq <= T and T % tb == 0
    qb = q_slot // groups
    kern = functools.partial(_band_kernel, tq=tq, n_sub=n_sub, window=window, seq=T, use_sinks=use_sinks)
    return pl.pallas_call(
        kern,
        grid=(B, 2, nt),
        in_specs=[
            pl.BlockSpec(memory_space=pltpu.SMEM),
            pl.BlockSpec(memory_space=pltpu.SMEM),
            pl.BlockSpec((tb, groups * LANES), lambda b, h, i: (b * nt + i, qb + h)),
            pl.BlockSpec((T, LANES), lambda b, h, i: (b, k_slot + h)),
            pl.BlockSpec((T, LANES), lambda b, h, i: (b, v_slot + h)),
        ],
        out_specs=pl.BlockSpec((tb, groups * HEAD_DIM), lambda b, h, i: (b * nt + i, h)),
        out_shape=jax.ShapeDtypeStruct((B * T, 2 * groups * HEAD_DIM), BF16),
        compiler_params=_cparams(("arbitrary", "arbitrary", "arbitrary")),
        name=name,
    )(slopes, sinks, proj, proj, proj)


def _outproj_kernel(ocmp_ref, oslc_ref, owin_ref, oswa_ref, gate_ref, x_ref, mod_ref, ex_ref, bn_ref, bs_ref,
                    w_ref, g2_ref, wr_ref, x1_ref, u2_ref, lt_ref):
    sg = _sigmoid(gate_ref[...].astype(F32)).astype(BF16)
    o_nsa = (_dot(sg, ex_ref[0]) * ocmp_ref[...].astype(F32)
             + _dot(sg, ex_ref[1]) * oslc_ref[...].astype(F32)
             + _dot(sg, ex_ref[2]) * owin_ref[...].astype(F32))
    o_swa = oswa_ref[...].astype(F32)
    n1 = o_nsa * lax.rsqrt(jnp.mean(o_nsa * o_nsa, axis=-1, keepdims=True) + EPS) * bn_ref[...]
    n2 = o_swa * lax.rsqrt(jnp.mean(o_swa * o_swa, axis=-1, keepdims=True) + EPS) * bs_ref[...]
    merged = jnp.concatenate([n1, n2], axis=-1).astype(BF16)
    y = _dot(merged, w_ref[...])
    x1 = x_ref[...] + mod_ref[0, 2:3, :] * y
    x1_ref[...] = x1
    u2 = x1 * lax.rsqrt(jnp.mean(x1 * x1, axis=-1, keepdims=True) + EPS) * g2_ref[...]
    u2 = u2 * (1.0 + mod_ref[0, 4:5, :]) + mod_ref[0, 3:4, :]
    u2_ref[...] = u2.astype(u2_ref.dtype)
    lt_ref[...] = lax.dot_general(wr_ref[...], u2, (((1,), (1,)), ((), ())), preferred_element_type=F32,
                                  precision=lax.Precision.HIGHEST)


def _outproj(ocmp, oslc, owin, oswa, proj, x2, mod, expand, beta_n, beta_s, w_out, norm2_g, wr_t, *, seq):
    N, D = x2.shape
    tm = 512
    tiles_per_seq = seq // tm
    row = lambda i: (i, 0)
    const2 = lambda i: (0, 0)
    return pl.pallas_call(
        _outproj_kernel,
        grid=(N // tm,),
        in_specs=[
            pl.BlockSpec((tm, D_NSA), row),
            pl.BlockSpec((tm, D_NSA), row),
            pl.BlockSpec((tm, D_NSA), row),
            pl.BlockSpec((tm, D_SWA), row),
            pl.BlockSpec((tm, LANES), lambda i: (i, SLOT_GATE)),
            pl.BlockSpec((tm, D), row),
            pl.BlockSpec((1, 6, D), lambda i: (i // tiles_per_seq, 0, 0)),
            pl.BlockSpec((N_NSA_BRANCHES, LANES, D_NSA), lambda i: (0, 0, 0)),
            pl.BlockSpec((1, D_NSA), const2),
            pl.BlockSpec((1, D_SWA), const2),
            pl.BlockSpec((D_MIX, D), const2),
            pl.BlockSpec((1, D), const2),
            pl.BlockSpec((N_EXPERTS, D), const2),
        ],
        out_specs=[
            pl.BlockSpec((tm, D), row),
            pl.BlockSpec((tm, D), row),
            pl.BlockSpec((N_EXPERTS, tm), lambda i: (0, i)),
        ],
        out_shape=[
            jax.ShapeDtypeStruct((N, D), F32),
            jax.ShapeDtypeStruct((N, D), BF16),
            jax.ShapeDtypeStruct((N_EXPERTS, N), F32),
        ],
        compiler_params=_cparams(("arbitrary",)),
        name="outproj",
    )(ocmp, oslc, owin, oswa, proj, x2, mod, expand, beta_n, beta_s, w_out, norm2_g, wr_t)


def _route_kernel(lt_ref, g_ref, *, tm):
    lt = lt_ref[...]
    e = jnp.exp(lt - jnp.max(lt, axis=0, keepdims=True))
    aff = e / jnp.sum(e, axis=0, keepdims=True)
    rows = [aff[k:k + 1, :] for k in range(N_EXPERTS)]
    epg = EXPERTS_PER_GROUP
    scores = []
    for gr in range(N_GROUPS):
        xs = rows[gr * epg:(gr + 1) * epg]
        top1 = functools.reduce(jnp.maximum, xs)
        second = None
        for a in range(epg):
            for b in range(a + 1, epg):
                mn = jnp.minimum(xs[a], xs[b])
                second = mn if second is None else jnp.maximum(second, mn)
        scores.append(top1 + second)
    taken = None
    in_group = []
    for gr in range(N_GROUPS):
        best = None
        for o in range(gr + 1, N_GROUPS):
            c = scores[gr] >= scores[o]
            best = c if best is None else (best & c)
        if best is None:
            best = jnp.full(scores[gr].shape, True)
        sel = best if taken is None else (best & ~taken)
        taken = sel if taken is None else (taken | sel)
        in_group.append(sel)
    ys = []
    for k in range(epg):
        yk = rows[k]
        for gr in range(1, N_GROUPS):
            yk = jnp.where(in_group[gr], rows[gr * epg + k], yk)
        ys.append(yk)
    chosen = []
    for k in range(epg):
        rk = jnp.zeros_like(ys[k])
        for o in range(epg):
            if o == k:
                continue
            before = (ys[o] >= ys[k]) if o < k else (ys[o] > ys[k])
            rk = rk + jnp.where(before, 1.0, 0.0)
        chosen.append(rk < TOP_K)
    total = functools.reduce(lambda a, b: a + b, [jnp.where(chosen[k], ys[k], 0.0) for k in range(epg)])
    gates = []
    for ex in range(N_EXPERTS):
        gr, k = divmod(ex, epg)
        gates.append(jnp.where(in_group[gr] & chosen[k], ys[k] / total, 0.0))
    gt = jnp.concatenate(gates + [jnp.zeros((LANES - N_EXPERTS, tm), F32)], axis=0)
    g_ref[...] = gt.T


def _route(lt):
    E, N = lt.shape
    tm = 512
    return pl.pallas_call(
        functools.partial(_route_kernel, tm=tm),
        grid=(N // tm,),
        in_specs=[pl.BlockSpec((E, tm), lambda i: (0, i))],
        out_specs=pl.BlockSpec((tm, LANES), lambda i: (i, 0)),
        out_shape=jax.ShapeDtypeStruct((N, LANES), F32),
        compiler_params=_cparams(("arbitrary",)),
        name="route",
    )(lt)


def _moe_kernel(u_ref, gate_ref, w1_ref, w3_ref, w2_ref, x1_ref, mod_ref, o_ref, acc_ref):
    ex = pl.program_id(1)

    @pl.when(ex == 0)
    def _():
        acc_ref[...] = jnp.zeros_like(acc_ref)

    u = u_ref[...]
    a = _dot(u, w1_ref[0])
    hmid = a * _sigmoid(a) * _dot(u, w3_ref[0])
    lane = lax.broadcasted_iota(jnp.int32, gate_ref.shape, 1)
    ge = jnp.sum(jnp.where(lane == ex, gate_ref[...], 0.0), axis=-1, keepdims=True)
    acc_ref[...] += _dot((hmid * ge).astype(BF16), w2_ref[0])

    @pl.when(ex == pl.num_programs(1) - 1)
    def _():
        o_ref[...] = x1_ref[...] + mod_ref[0, 5:6, :] * acc_ref[...]


def _moe(u2, gate, w1, w3, w2, x1, mod, *, seq):
    N, D = u2.shape
    E = w1.shape[0]
    tm = 1024
    tiles_per_seq = seq // tm
    return pl.pallas_call(
        _moe_kernel,
        grid=(N // tm, E),
        in_specs=[
            pl.BlockSpec((tm, D), lambda i, e: (i, 0)),
            pl.BlockSpec((tm, LANES), lambda i, e: (i, 0)),
            pl.BlockSpec((1, D, D_EXPERT), lambda i, e: (e, 0, 0)),
            pl.BlockSpec((1, D, D_EXPERT), lambda i, e: (e, 0, 0)),
            pl.BlockSpec((1, D_EXPERT, D), lambda i, e: (e, 0, 0)),
            pl.BlockSpec((tm, D), lambda i, e: (i, 0)),
            pl.BlockSpec((1, 6, D), lambda i, e: (i // tiles_per_seq, 0, 0)),
        ],
        out_specs=pl.BlockSpec((tm, D), lambda i, e: (i, 0)),
        out_shape=jax.ShapeDtypeStruct((N, D), F32),
        scratch_shapes=[pltpu.VMEM((tm, D), F32)],
        compiler_params=_cparams(("arbitrary", "arbitrary")),
        name="moe",
    )(u2, gate, w1, w3, w2, x1, mod)


def _alibi_slopes(first, count):
    hh = jnp.arange(first + 1, first + count + 1, dtype=F32)
    return jnp.exp2(-8.0 * hh / N_HEADS_TOTAL)


def _bd_const():
    idx = np.arange(LANES) // HEAD_DIM
    return jnp.asarray((idx[:, None] == idx[None, :]).astype(np.float32), BF16)


def _blkcols_const(T):
    blk = np.arange(T) // SEL_LEN
    m = np.where(np.arange(LANES)[None, :] == blk[:, None], NEG_INF, 0.0).astype(np.float32)
    return jnp.asarray(m, BF16)


def _overlap_const(T):
    nc = T // CMP_STRIDE
    n_sel = T // SEL_LEN
    cs = np.arange(nc) * CMP_STRIDE
    ss = np.arange(n_sel) * SEL_LEN
    ov = np.clip(np.minimum(cs[:, None] + CMP_LEN, ss[None, :] + SEL_LEN)
                 - np.maximum(cs[:, None], ss[None, :]), 0, None) / CMP_LEN
    return jnp.asarray(ov.T.astype(np.float32))


def _expand_const():
    ex = np.zeros((N_NSA_BRANCHES, LANES, D_NSA), np.float32)
    for br in range(N_NSA_BRANCHES):
        for hd in range(N_HEADS_NSA):
            ex[br, hd * N_NSA_BRANCHES + br, hd * HEAD_DIM:(hd + 1) * HEAD_DIM] = 1.0
    return jnp.asarray(ex, BF16)


def _compact_w_in(w):
    parts = [w[:, a:b] for a, b in (_C_QN, _C_QS, _C_KSL, _C_KW, _C_KS, _C_VSL, _C_VW, _C_VS, _C_KC, _C_VC, _C_GATE)]
    pad = jnp.zeros((w.shape[0], LANES - (_C_GATE[1] - _C_GATE[0])), w.dtype)
    return jnp.concatenate(parts + [pad], axis=1).astype(BF16)


def _qconst(slopes_q):
    hi = slopes_q.astype(BF16).astype(F32)
    lo = (slopes_q - hi).astype(BF16).astype(F32)
    cols = jnp.stack([64.0 * hi, 64.0 * lo, hi, lo], axis=1)
    return jnp.zeros((slopes_q.shape[0], LANES), F32).at[:, HEAD_DIM:HEAD_DIM + 4].set(cols)


def kernel(x, c, w_router, ada_w, ada_b, norm1_g, norm2_g, w_in, w_out, nsa_q_gain, nsa_k_gain, cmp_pos_k,
           cmp_pos_v, cmp_w1_k, cmp_w2_k, cmp_w1_v, cmp_w2_v, swa_q_gain, swa_k_gain, swa_sinks, beta_nsa,
           beta_swa, moe_w1, moe_w3, moe_w2):
    B, T, D = x.shape
    L = ada_w.shape[0]
    N = B * T
    nc = T // CMP_STRIDE
    assert D == D_MODEL and T % 512 == 0 and T <= 64 * 128 and B <= 8

    slopes_nsa = _alibi_slopes(N_HEADS_SWA, N_HEADS_NSA)
    slopes_swa = _alibi_slopes(0, N_HEADS_SWA)
    qconst = _qconst(jnp.concatenate([slopes_nsa, slopes_swa]))
    bd = _bd_const()
    blkcols = _blkcols_const(T)
    ov = _overlap_const(T)
    expand = _expand_const()
    wr_t = w_router.T
    zero_sinks = jnp.zeros((N_HEADS_NSA,), F32)

    c_pad = jnp.zeros((8, D), F32).at[:B].set(c)
    mod_all = _ada(c_pad, ada_w, ada_b)[:, :B].reshape(L, B, 6, D)

    x2 = x.reshape(N, D)
    for l in range(L):
        mod = mod_all[l]
        tile2 = lambda g: jnp.tile(g, 2)
        gain_c = jnp.concatenate([
            jnp.tile(nsa_q_gain[l], N_HEADS_NSA) * ATTN_SCALE,
            jnp.tile(swa_q_gain[l], N_HEADS_SWA) * ATTN_SCALE,
            tile2(nsa_k_gain[l, 1]), tile2(nsa_k_gain[l, 2]), tile2(swa_k_gain[l])]).reshape(1, -1)
        proj = _inproj(x2, mod, norm1_g[l].reshape(1, D), _compact_w_in(w_in[l]), gain_c, qconst, bd, seq=T)

        kvc_raw = proj[:, SLOT_KC * LANES:(SLOT_VC + 1) * LANES]
        xc = kvc_raw.reshape(B, nc, CMP_STRIDE, 2, N_KV_NSA, HEAD_DIM).transpose(3, 0, 4, 1, 2, 5)
        xc = xc.reshape(2, B, N_KV_NSA, nc, CMP_STRIDE * HEAD_DIM)
        w1s = jnp.stack([cmp_w1_k[l], cmp_w1_v[l]]).astype(BF16)
        w2s = jnp.pad(jnp.stack([cmp_w2_k[l], cmp_w2_v[l]]), ((0, 0), (0, 0), (0, LANES - HEAD_DIM))).astype(BF16)
        pos = jnp.stack([cmp_pos_k[l].reshape(-1), cmp_pos_v[l].reshape(-1)])
        pos = jnp.broadcast_to(pos[:, None, :], (2, 8, CMP_LEN * HEAD_DIM)).astype(BF16)
        kgain = jnp.pad(nsa_k_gain[l, 0], (0, LANES - HEAD_DIM)).reshape(1, LANES)
        kvc = _cmpmlp(xc, w1s, w2s, pos, kgain)

        o_cmp, notsel = _cmpattn(proj, kvc, ov, B=B, T=T)
        o_slc = _selattn(proj, notsel, blkcols, B=B, T=T)
        o_win = _band(slopes_nsa, zero_sinks, proj, B=B, T=T, window=NSA_WINDOW, q_slot=SLOT_QN,
                      k_slot=SLOT_KW, v_slot=SLOT_VW, use_sinks=False, name="winattn")
        o_swa = _band(slopes_swa, swa_sinks[l], proj, B=B, T=T, window=SWA_WINDOW, q_slot=SLOT_QS,
                      k_slot=SLOT_KS, v_slot=SLOT_VS, use_sinks=True, name="swaattn")

        x1, u2, lt = _outproj(o_cmp, o_slc, o_win, o_swa, proj, x2, mod, expand,
                              beta_nsa[l].reshape(1, -1), beta_swa[l].reshape(1, -1), w_out[l].astype(BF16),
                              norm2_g[l].reshape(1, D), wr_t, seq=T)
        gate = _route(lt)
        x2 = _moe(u2, gate, moe_w1[l].astype(BF16), moe_w3[l].astype(BF16), moe_w2[l].astype(BF16), x1, mod, seq=T)
    return x2.reshape(B, T, D)
```

```python
import functools

import numpy as np
import jax
import jax.numpy as jnp
from jax import lax
from jax.experimental import pallas as pl
from jax.experimental.pallas import tpu as pltpu

D_MODEL = 1024
DEPTH = 2
HEAD_DIM = 64
N_HEADS_NSA = 8
N_KV_NSA = 2
G_NSA = N_HEADS_NSA // N_KV_NSA
N_HEADS_SWA = 8
N_KV_SWA = 2
G_SWA = N_HEADS_SWA // N_KV_SWA
D_NSA = N_HEADS_NSA * HEAD_DIM
D_SWA = N_HEADS_SWA * HEAD_DIM
D_MIX = D_NSA + D_SWA
N_HEADS_TOTAL = N_HEADS_NSA + N_HEADS_SWA
KV_NSA = N_KV_NSA * HEAD_DIM
KV_SWA = N_KV_SWA * HEAD_DIM
N_NSA_BRANCHES = 3
CMP_LEN = 32
CMP_STRIDE = 16
CMP_HIDDEN = 256
SEL_LEN = 64
SEL_TOPK = 16
NSA_WINDOW = 512
SWA_WINDOW = 128
N_EXPERTS = 16
N_GROUPS = 4
EXPERTS_PER_GROUP = N_EXPERTS // N_GROUPS
TOP_K = 2
D_EXPERT = 512
EPS = 1e-6
NEG_INF = -1e30
ATTN_SCALE = HEAD_DIM ** -0.5

F32 = jnp.float32
BF16 = jnp.bfloat16

LANES = 128
VMEM_LIMIT = 48 * 1024 * 1024

MOE_TILE = 2048
MOE_CHUNK = 256
SEG_ALIGN = 16
TRI = 256
MOE_ROWS = TOP_K * MOE_TILE + N_EXPERTS * SEG_ALIGN + MOE_CHUNK

SLOT_QN = 0
SLOT_QS = 8
SLOT_KSL = 16
SLOT_KW = 18
SLOT_KS = 20
SLOT_VSL = 22
SLOT_VW = 24
SLOT_VS = 26
SLOT_KC = 28
SLOT_VC = 29
SLOT_GATE = 30
N_SLOTS = 31
N_CHUNKS = 17
N_NORM_CHUNKS = 11

_C_QN = (0, 512)
_C_KC = (512, 640)
_C_VC = (640, 768)
_C_KSL = (768, 896)
_C_VSL = (896, 1024)
_C_KW = (1024, 1152)
_C_VW = (1152, 1280)
_C_GATE = (1280, 1304)
_C_QS = (1304, 1816)
_C_KS = (1816, 1944)
_C_VS = (1944, 2072)


def _cparams(sem):
    return pltpu.CompilerParams(dimension_semantics=sem, vmem_limit_bytes=VMEM_LIMIT)


def _dot(a, b, **kw):
    return jnp.dot(a, b, preferred_element_type=F32, **kw)


def _dot_nt(a, b):
    return lax.dot_general(a, b, (((1,), (1,)), ((), ())), preferred_element_type=F32)


def _sigmoid(x):
    return 1.0 / (1.0 + jnp.exp(-x))


def _pack_bf16_pairs(x):
    n = x.shape[1] // 2
    return pltpu.pack_elementwise([x[:, :n], x[:, n:]], packed_dtype=BF16)


def _unpack_bf16_pairs(w):
    lo = pltpu.unpack_elementwise(w, index=0, packed_dtype=BF16, unpacked_dtype=F32)
    hi = pltpu.unpack_elementwise(w, index=1, packed_dtype=BF16, unpacked_dtype=F32)
    return lo, hi


def _ada_kernel(c_ref, w_ref, b_ref, o_ref):
    c = c_ref[...]
    cond = c * _sigmoid(c)
    o_ref[0] = _dot(cond.astype(BF16), w_ref[0].astype(BF16)) + b_ref[0]


def _ada(c_pad, ada_w, ada_b):
    L, D, N6 = ada_w.shape
    tn = 1536
    return pl.pallas_call(
        _ada_kernel,
        grid=(L, N6 // tn),
        in_specs=[
            pl.BlockSpec((8, D), lambda l, j: (0, 0)),
            pl.BlockSpec((1, D, tn), lambda l, j: (l, 0, j)),
            pl.BlockSpec((1, 1, tn), lambda l, j: (l, 0, j)),
        ],
        out_specs=pl.BlockSpec((1, 8, tn), lambda l, j: (l, 0, j)),
        out_shape=jax.ShapeDtypeStruct((L, 8, N6), F32),
        compiler_params=_cparams(("arbitrary", "arbitrary")),
        name="ada",
    )(c_pad, ada_w, ada_b.reshape(L, 1, N6))


def _inproj_kernel(x_ref, mod_ref, g_ref, w_ref, gain_ref, qc_ref, bd_ref, o_ref, *, tm, seq):
    i = pl.program_id(0)
    x = x_ref[...]
    ms = jnp.mean(x * x, axis=-1, keepdims=True)
    u = x * lax.rsqrt(ms + EPS) * g_ref[...]
    u = u * (1.0 + mod_ref[0, 1:2, :]) + mod_ref[0, 0:1, :]
    y = _dot(u.astype(BF16), w_ref[...])

    lane = lax.broadcasted_iota(jnp.int32, (tm, LANES), 1)
    low = lane < HEAD_DIM
    t = lax.broadcasted_iota(jnp.int32, (tm, LANES), 0) + lax.rem(i * tm, seq)
    hi_part = (t >> 6).astype(F32)
    lo_part = (t & 63).astype(F32)
    poscols = jnp.where((lane == 64) | (lane == 65), hi_part,
                        jnp.where((lane == 66) | (lane == 67), lo_part, 0.0))
    bd = bd_ref[...]

    def put(slot, val):
        o_ref[:, slot * LANES:(slot + 1) * LANES] = val.astype(o_ref.dtype)

    for c in range(N_CHUNKS):
        blk = y[:, c * LANES:(c + 1) * LANES]
        if c < N_NORM_CHUNKS:
            msb = _dot((blk * blk).astype(BF16), bd) * (1.0 / HEAD_DIM)
            blk = blk * lax.rsqrt(msb + EPS) * gain_ref[:, c * LANES:(c + 1) * LANES]
        if c < 14:
            rolled = pltpu.roll(blk, HEAD_DIM, axis=1)
            if c < 8:
                f0 = qc_ref[2 * c:2 * c + 1, :]
                f1 = qc_ref[2 * c + 1:2 * c + 2, :]
                s0 = 2 * c
            elif c < N_NORM_CHUNKS:
                f0 = f1 = poscols
                s0 = SLOT_KSL + 2 * (c - 8)
            else:
                f0 = f1 = 1.0
                s0 = SLOT_VSL + 2 * (c - N_NORM_CHUNKS)
            put(s0, jnp.where(low, blk, f0))
            put(s0 + 1, jnp.where(low, rolled, f1))
        else:
            put(SLOT_KC + (c - 14), blk)


def _inproj(x2, mod, norm_g, w_c, gain_c, qconst, bd, *, seq):
    N, D = x2.shape
    tm = 512
    tiles_per_seq = seq // tm
    kern = functools.partial(_inproj_kernel, tm=tm, seq=seq)
    return pl.pallas_call(
        kern,
        grid=(N // tm,),
        in_specs=[
            pl.BlockSpec((tm, D), lambda i: (i, 0)),
            pl.BlockSpec((1, 6, D), lambda i: (i // tiles_per_seq, 0, 0)),
            pl.BlockSpec((1, D), lambda i: (0, 0)),
            pl.BlockSpec((D, N_CHUNKS * LANES), lambda i: (0, 0)),
            pl.BlockSpec((1, N_NORM_CHUNKS * LANES), lambda i: (0, 0)),
            pl.BlockSpec((16, LANES), lambda i: (0, 0)),
            pl.BlockSpec((LANES, LANES), lambda i: (0, 0)),
        ],
        out_specs=pl.BlockSpec((tm, N_SLOTS * LANES), lambda i: (i, 0)),
        out_shape=jax.ShapeDtypeStruct((N, N_SLOTS * LANES), BF16),
        compiler_params=_cparams(("arbitrary",)),
        name="inproj",
    )(x2, mod, norm_g, w_c, gain_c, qconst, bd)


def _gelu_tanh(x):
    return 0.5 * x * (1.0 + jnp.tanh(np.sqrt(2.0 / np.pi).astype(np.float32) * (x + 0.044715 * (x * x * x))))


def _cmpmlp_kernel(x_ref, w1_ref, w2_ref, pos_ref, gain_ref, o_ref, *, nc):
    kv = pl.program_id(0)
    xc = x_ref[0, 0, 0]
    w1 = w1_ref[0]
    half = (CMP_LEN // 2) * HEAD_DIM
    first = _dot(xc, w1[:half])
    second = _dot(xc, w1[half:])
    bias = _dot(pos_ref[0], w1)[0:1]
    h = first + pltpu.roll(second, nc - 1, axis=0) + bias
    h = _gelu_tanh(h)
    z = _dot(h.astype(BF16), w2_ref[0])
    zn = z * lax.rsqrt(jnp.sum(z * z, axis=-1, keepdims=True) * (1.0 / HEAD_DIM) + EPS) * gain_ref[...]
    lane = lax.broadcasted_iota(jnp.int32, (nc, LANES), 1)
    last = lax.broadcasted_iota(jnp.int32, (nc, LANES), 0) * CMP_STRIDE + (CMP_LEN - 1)
    poscols = jnp.where((lane == 64) | (lane == 65), (last >> 6).astype(F32),
                        jnp.where((lane == 66) | (lane == 67), (last & 63).astype(F32), 0.0))
    o_ref[0, 0, 0] = jnp.where(kv == 0, zn + poscols, z + jnp.where(lane >= HEAD_DIM, 1.0, 0.0))


def _cmpmlp(xc, w1s, w2s, pos, gain):
    _, B, H, nc, K = xc.shape
    kern = functools.partial(_cmpmlp_kernel, nc=nc)
    return pl.pallas_call(
        kern,
        grid=(2, B, H),
        in_specs=[
            pl.BlockSpec((1, 1, 1, nc, K), lambda kv, b, h: (kv, b, h, 0, 0)),
            pl.BlockSpec((1, 2 * K, CMP_HIDDEN), lambda kv, b, h: (kv, 0, 0)),
            pl.BlockSpec((1, CMP_HIDDEN, LANES), lambda kv, b, h: (kv, 0, 0)),
            pl.BlockSpec((1, 8, 2 * K), lambda kv, b, h: (kv, 0, 0)),
            pl.BlockSpec((1, LANES), lambda kv, b, h: (0, 0)),
        ],
        out_specs=pl.BlockSpec((1, 1, 1, nc, LANES), lambda kv, b, h: (kv, b, h, 0, 0)),
        out_shape=jax.ShapeDtypeStruct((2, B, H, nc, LANES), F32),
        compiler_params=_cparams(("arbitrary", "arbitrary", "arbitrary")),
        name="cmpmlp",
    )(xc, w1s, w2s, pos, gain)


def _normalize_pack(pvs, extra=None):
    lane = lax.broadcasted_iota(jnp.int32, pvs[0].shape, 1)
    low = lane < HEAD_DIM
    pairs = []
    for p in range(len(pvs) // 2):
        even, odd = pvs[2 * p], pvs[2 * p + 1]
        den_e = even if extra is None else even + extra[2 * p]
        den_o = odd if extra is None else odd + extra[2 * p + 1]
        o_e = even * (1.0 / pltpu.roll(den_e, HEAD_DIM, axis=1))
        o_o = pltpu.roll(odd, HEAD_DIM, axis=1) * (1.0 / den_o)
        pairs.append(jnp.where(low, o_e, o_o))
    return jnp.concatenate(pairs, axis=-1)


def _cmpattn_kernel(q_ref, kc_ref, vc_ref, ov_ref, o_ref, ns_ref, *, tq, nc, n_cmp, n_sel, top):
    q0 = pl.program_id(2) * tq
    kc = kc_ref[0, 0, 0].astype(BF16)
    vc = vc_ref[0, 0, 0].astype(BF16)

    t_col = q0 + lax.broadcasted_iota(jnp.int32, (tq, 1), 0)
    n_row = lax.broadcasted_iota(jnp.int32, (1, nc), 1)
    valid = (t_col - (n_row * CMP_STRIDE + (CMP_LEN - 1)) >= 0) & (n_row < n_cmp)
    bias = jnp.where(valid, 0.0, NEG_INF)
    row_valid = (t_col >= CMP_LEN - 1).astype(F32)
    t_row = q0 + lax.broadcasted_iota(jnp.int32, (1, tq), 1)
    n_col = lax.broadcasted_iota(jnp.int32, (nc, 1), 0)
    valid_t = (t_row - (n_col * CMP_STRIDE + (CMP_LEN - 1)) >= 0) & (n_col < n_cmp)
    bias_t = jnp.where(valid_t, 0.0, NEG_INF)
    row_valid_t = (t_row >= CMP_LEN - 1).astype(F32)

    outs = []
    psum_t = jnp.zeros((nc, tq), F32)
    for g in range(G_NSA):
        qg = q_ref[:, g * LANES:(g + 1) * LANES]
        s = _dot_nt(qg, kc) + bias
        e = jnp.exp(s - jnp.max(s, axis=-1, keepdims=True))
        outs.append(_dot(e.astype(BF16), vc))
        st = _dot_nt(kc, qg) + bias_t
        et = jnp.exp(st - jnp.max(st, axis=0, keepdims=True))
        psum_t = psum_t + et * (1.0 / jnp.sum(et, axis=0, keepdims=True))
    o_ref[...] = (_normalize_pack(outs) * row_valid).astype(o_ref.dtype)

    psum_t = psum_t * row_valid_t
    imp_t = _dot(ov_ref[...], psum_t, precision=lax.Precision.HIGHEST)
    j = lax.broadcasted_iota(jnp.int32, (n_sel, 1), 0)
    cur = t_row >> 6
    forced = (j == 0) | (j == cur) | (j == cur - 1)
    v = jnp.where(forced, 1e9, jnp.where(j > cur, NEG_INF, imp_t))
    n_grp = n_sel // 8
    vg = [v[8 * r:8 * r + 8, :] for r in range(n_grp)]
    jg = lax.broadcasted_iota(jnp.int32, (8, 1), 0)
    cnt = [jnp.zeros((8, tq), F32) for _ in range(n_grp)]
    for jp in range(n_sel):
        vj = v[jp:jp + 1, :]
        for r in range(n_grp):
            ge = jnp.where(vj >= vg[r], 1.0, 0.0)
            gt = jnp.where(vj > vg[r], 1.0, 0.0)
            if 8 * r > jp:
                inc = ge
            elif 8 * r + 7 < jp:
                inc = gt
            else:
                inc = jnp.where(jg + 8 * r > jp, ge, gt)
            cnt[r] = cnt[r] + inc
    notsel_t = jnp.concatenate([jnp.where(cn < top, 0.0, 1.0) for cn in cnt], axis=0)
    if n_sel < LANES:
        notsel_t = jnp.concatenate([notsel_t, jnp.zeros((LANES - n_sel, tq), F32)], axis=0)
    ns_ref[0, 0] = notsel_t.T.astype(ns_ref.dtype)


def _cmpattn(proj, kvc, ov, *, B, T):
    tq = 256
    nt = T // tq
    nc = T // CMP_STRIDE
    n_cmp = (T - CMP_LEN) // CMP_STRIDE + 1
    n_sel = T // SEL_LEN
    top = min(SEL_TOPK, n_sel)
    assert n_sel <= LANES and n_sel % 8 == 0
    kern = functools.partial(_cmpattn_kernel, tq=tq, nc=nc, n_cmp=n_cmp, n_sel=n_sel, top=top)
    return pl.pallas_call(
        kern,
        grid=(B, N_KV_NSA, nt),
        in_specs=[
            pl.BlockSpec((tq, G_NSA * LANES), lambda b, h, i: (b * nt + i, h)),
            pl.BlockSpec((1, 1, 1, nc, LANES), lambda b, h, i: (0, b, h, 0, 0)),
            pl.BlockSpec((1, 1, 1, nc, LANES), lambda b, h, i: (1, b, h, 0, 0)),
            pl.BlockSpec((n_sel, nc), lambda b, h, i: (0, 0)),
        ],
        out_specs=[
            pl.BlockSpec((tq, G_NSA * HEAD_DIM), lambda b, h, i: (b * nt + i, h)),
            pl.BlockSpec((1, 1, tq, LANES), lambda b, h, i: (b, h, i, 0)),
        ],
        out_shape=[
            jax.ShapeDtypeStruct((B * T, D_NSA), BF16),
            jax.ShapeDtypeStruct((B, N_KV_NSA, T, LANES), BF16),
        ],
        compiler_params=_cparams(("arbitrary", "arbitrary", "arbitrary")),
        name="cmpattn",
    )(proj, kvc, kvc, ov)


def _selattn_kernel(q_ref, ns_ref, k_ref, v_ref, bc_ref, o_ref, *, tq, tk):
    q0 = pl.program_id(2) * tq
    ns = ns_ref[0, 0]
    qx = jnp.concatenate(
        [jnp.concatenate([q_ref[:, g * LANES:(g + 1) * LANES], ns], axis=-1) for g in range(G_NSA)], axis=0)
    m_rows = G_NSA * tq
    diag = q0 // tk

    def step(kt, carry, masked):
        m, acc = carry
        start = pl.multiple_of(kt * tk, tk)
        kx = jnp.concatenate([k_ref[pl.ds(start, tk), :], bc_ref[pl.ds(start, tk), :]], axis=-1)
        s = _dot_nt(qx, kx)
        if masked:
            r = lax.broadcasted_iota(jnp.int32, (tq, tk), 0)
            col = lax.broadcasted_iota(jnp.int32, (tq, tk), 1)
            s = s + jnp.concatenate([jnp.where(col <= r, 0.0, NEG_INF)] * G_NSA, axis=0)
        m_new = jnp.maximum(m, jnp.max(s, axis=-1, keepdims=True))
        p = jnp.exp(s - m_new).astype(BF16)
        acc = jnp.exp(m - m_new) * acc + _dot(p, v_ref[pl.ds(start, tk), :])
        return m_new, acc

    init = (jnp.full((m_rows, 1), NEG_INF, F32), jnp.zeros((m_rows, LANES), F32))
    carry = lax.fori_loop(0, diag, functools.partial(step, masked=False), init)
    _, acc = step(diag, carry, True)
    o_ref[...] = _normalize_pack([acc[g * tq:(g + 1) * tq] for g in range(G_NSA)]).astype(o_ref.dtype)


def _selattn(proj, notsel, blkcols, *, B, T):
    tq = 512
    tk = 512
    nt = T // tq
    assert T % tk == 0 and tk == tq
    kern = functools.partial(_selattn_kernel, tq=tq, tk=tk)
    return pl.pallas_call(
        kern,
        grid=(B, N_KV_NSA, nt),
        in_specs=[
            pl.BlockSpec((tq, G_NSA * LANES), lambda b, h, i: (b * nt + i, h)),
            pl.BlockSpec((1, 1, tq, LANES), lambda b, h, i: (b, h, i, 0)),
            pl.BlockSpec((T, LANES), lambda b, h, i: (b, SLOT_KSL + h)),
            pl.BlockSpec((T, LANES), lambda b, h, i: (b, SLOT_VSL + h)),
            pl.BlockSpec((T, LANES), lambda b, h, i: (0, 0)),
        ],
        out_specs=pl.BlockSpec((tq, G_NSA * HEAD_DIM), lambda b, h, i: (b * nt + i, h)),
        out_shape=jax.ShapeDtypeStruct((B * T, D_NSA), BF16),
        compiler_params=_cparams(("arbitrary", "arbitrary", "arbitrary")),
        name="selattn",
    )(proj, notsel, proj, proj, blkcols)


def _band_kernel(sl_ref, sink_ref, q_ref, k_ref, v_ref, o_ref, *, tq, n_sub, window, seq, use_sinks):
    h = pl.program_id(1)
    span = window + tq
    groups = q_ref.shape[1] // LANES
    r = lax.broadcasted_iota(jnp.int32, (tq, span), 0)
    col = lax.broadcasted_iota(jnp.int32, (tq, span), 1)
    for sub in range(n_sub):
        q0 = (pl.program_id(2) * n_sub + sub) * tq
        start = pl.multiple_of(jnp.clip(q0 - window, 0, seq - span), tq)
        rows = pl.ds(sub * tq, tq)
        qx = jnp.concatenate([q_ref[rows, g * LANES:(g + 1) * LANES] for g in range(groups)], axis=0)
        d = (q0 - start) + r - col
        bias = jnp.where((d >= 0) & (d < window), 0.0, NEG_INF)
        s = _dot_nt(qx, k_ref[pl.ds(start, span), :]) + jnp.concatenate([bias] * groups, axis=0)
        m = jnp.broadcast_to(jnp.max(s, axis=-1, keepdims=True), (groups * tq, LANES))
        if use_sinks:
            t_rep = (q0 + lax.broadcasted_iota(jnp.int32, (tq, LANES), 0)).astype(F32)
            sink = jnp.concatenate(
                [sink_ref[h * groups + g] + sl_ref[h * groups + g] * t_rep for g in range(groups)], axis=0)
            m = jnp.maximum(m, sink)
        e = jnp.exp(s - jnp.concatenate([m] * (span // LANES), axis=1)).astype(BF16)
        pv = _dot(e, v_ref[pl.ds(start, span), :])
        extra = None
        if use_sinks:
            esink = jnp.exp(sink - m)
            extra = [esink[g * tq:(g + 1) * tq] for g in range(groups)]
        o_ref[rows, :] = _normalize_pack([pv[g * tq:(g + 1) * tq] for g in range(groups)], extra).astype(o_ref.dtype)


def _band(slopes, sinks, proj, *, B, T, window, q_slot, k_slot, v_slot, use_sinks, name):
    tq = 128
    n_sub = 4
    tb = tq * n_sub
    nt = T // tb
    groups = 4
    assert window % tq == 0 and window + tq <= T and T % tb == 0
    qb = q_slot // groups
    kern = functools.partial(_band_kernel, tq=tq, n_sub=n_sub, window=window, seq=T, use_sinks=use_sinks)
    return pl.pallas_call(
        kern,
        grid=(B, 2, nt),
        in_specs=[
            pl.BlockSpec(memory_space=pltpu.SMEM),
            pl.BlockSpec(memory_space=pltpu.SMEM),
            pl.BlockSpec((tb, groups * LANES), lambda b, h, i: (b * nt + i, qb + h)),
            pl.BlockSpec((T, LANES), lambda b, h, i: (b, k_slot + h)),
            pl.BlockSpec((T, LANES), lambda b, h, i: (b, v_slot + h)),
        ],
        out_specs=pl.BlockSpec((tb, groups * HEAD_DIM), lambda b, h, i: (b * nt + i, h)),
        out_shape=jax.ShapeDtypeStruct((B * T, 2 * groups * HEAD_DIM), BF16),
        compiler_params=_cparams(("arbitrary", "arbitrary", "arbitrary")),
        name=name,
    )(slopes, sinks, proj, proj, proj)


def _outproj_kernel(ocmp_ref, oslc_ref, owin_ref, oswa_ref, gate_ref, x_ref, mod_ref, ex_ref, bn_ref, bs_ref,
                    w_ref, g2_ref, wr_ref, x1_ref, u2_ref, lt_ref):
    sg = _sigmoid(gate_ref[...].astype(F32)).astype(BF16)
    o_nsa = (_dot(sg, ex_ref[0]) * ocmp_ref[...].astype(F32)
             + _dot(sg, ex_ref[1]) * oslc_ref[...].astype(F32)
             + _dot(sg, ex_ref[2]) * owin_ref[...].astype(F32))
    o_swa = oswa_ref[...].astype(F32)
    n1 = o_nsa * lax.rsqrt(jnp.mean(o_nsa * o_nsa, axis=-1, keepdims=True) + EPS) * bn_ref[...]
    n2 = o_swa * lax.rsqrt(jnp.mean(o_swa * o_swa, axis=-1, keepdims=True) + EPS) * bs_ref[...]
    merged = jnp.concatenate([n1, n2], axis=-1).astype(BF16)
    y = _dot(merged, w_ref[...])
    x1 = x_ref[...] + mod_ref[0, 2:3, :] * y
    x1_ref[...] = x1
    u2 = x1 * lax.rsqrt(jnp.mean(x1 * x1, axis=-1, keepdims=True) + EPS) * g2_ref[...]
    u2 = u2 * (1.0 + mod_ref[0, 4:5, :]) + mod_ref[0, 3:4, :]
    u2_ref[...] = _pack_bf16_pairs(u2)
    lt_ref[...] = lax.dot_general(wr_ref[...], u2, (((1,), (1,)), ((), ())), preferred_element_type=F32,
                                  precision=lax.Precision.HIGHEST)


def _outproj(ocmp, oslc, owin, oswa, proj, x2, mod, expand, beta_n, beta_s, w_out, norm2_g, wr_t, *, seq):
    N, D = x2.shape
    tm = 512
    tiles_per_seq = seq // tm
    row = lambda i: (i, 0)
    const2 = lambda i: (0, 0)
    return pl.pallas_call(
        _outproj_kernel,
        grid=(N // tm,),
        in_specs=[
            pl.BlockSpec((tm, D_NSA), row),
            pl.BlockSpec((tm, D_NSA), row),
            pl.BlockSpec((tm, D_NSA), row),
            pl.BlockSpec((tm, D_SWA), row),
            pl.BlockSpec((tm, LANES), lambda i: (i, SLOT_GATE)),
            pl.BlockSpec((tm, D), row),
            pl.BlockSpec((1, 6, D), lambda i: (i // tiles_per_seq, 0, 0)),
            pl.BlockSpec((N_NSA_BRANCHES, LANES, D_NSA), lambda i: (0, 0, 0)),
            pl.BlockSpec((1, D_NSA), const2),
            pl.BlockSpec((1, D_SWA), const2),
            pl.BlockSpec((D_MIX, D), const2),
            pl.BlockSpec((1, D), const2),
            pl.BlockSpec((N_EXPERTS, D), const2),
        ],
        out_specs=[
            pl.BlockSpec((tm, D), row),
            pl.BlockSpec((tm, D // 2), row),
            pl.BlockSpec((N_EXPERTS, tm), lambda i: (0, i)),
        ],
        out_shape=[
            jax.ShapeDtypeStruct((N, D), F32),
            jax.ShapeDtypeStruct((N, D // 2), jnp.uint32),
            jax.ShapeDtypeStruct((N_EXPERTS, N), F32),
        ],
        compiler_params=_cparams(("arbitrary",)),
        name="outproj",
    )(ocmp, oslc, owin, oswa, proj, x2, mod, expand, beta_n, beta_s, w_out, norm2_g, wr_t)


def _route_kernel(lt_ref, tri_ref, d0_ref, d1_ref, w0_ref, w1_ref, seg_ref, nch_ref, *, tm):
    lt = lt_ref[...]
    e = jnp.exp(lt - jnp.max(lt, axis=0, keepdims=True))
    aff = e / jnp.sum(e, axis=0, keepdims=True)
    rows = [aff[k:k + 1, :] for k in range(N_EXPERTS)]
    epg = EXPERTS_PER_GROUP
    scores = []
    for gr in range(N_GROUPS):
        xs = rows[gr * epg:(gr + 1) * epg]
        top1 = functools.reduce(jnp.maximum, xs)
        second = None
        for a in range(epg):
            for b in range(a + 1, epg):
                mn = jnp.minimum(xs[a], xs[b])
                second = mn if second is None else jnp.maximum(second, mn)
        scores.append(top1 + second)
    taken = None
    in_group = []
    for gr in range(N_GROUPS):
        best = None
        for o in range(gr + 1, N_GROUPS):
            c = scores[gr] >= scores[o]
            best = c if best is None else (best & c)
        if best is None:
            best = jnp.full(scores[gr].shape, True)
        sel = best if taken is None else (best & ~taken)
        taken = sel if taken is None else (taken | sel)
        in_group.append(sel)
    ys = []
    for k in range(epg):
        yk = rows[k]
        for gr in range(1, N_GROUPS):
            yk = jnp.where(in_group[gr], rows[gr * epg + k], yk)
        ys.append(yk)
    chosen = []
    for k in range(epg):
        rk = jnp.zeros_like(ys[k])
        for o in range(epg):
            if o == k:
                continue
            before = (ys[o] >= ys[k]) if o < k else (ys[o] > ys[k])
            rk = rk + jnp.where(before, 1.0, 0.0)
        chosen.append(rk < TOP_K)
    total = functools.reduce(lambda a, b: a + b, [jnp.where(chosen[k], ys[k], 0.0) for k in range(epg)])
    first, seen = [], None
    for k in range(epg):
        first.append(chosen[k] if seen is None else (chosen[k] & ~seen))
        seen = chosen[k] if seen is None else (seen | chosen[k])
    second = [chosen[k] & ~first[k] for k in range(epg)]
    inv_total = 1.0 / total
    w_first = functools.reduce(lambda a, b: a + b, [jnp.where(first[k], ys[k], 0.0) for k in range(epg)]) * inv_total
    w_second = functools.reduce(lambda a, b: a + b, [jnp.where(second[k], ys[k], 0.0) for k in range(epg)]) * inv_total
    f_rows, s_rows = [], []
    for ex in range(N_EXPERTS):
        gr, k = divmod(ex, epg)
        f_rows.append(jnp.where(in_group[gr] & first[k], 1.0, 0.0))
        s_rows.append(jnp.where(in_group[gr] & second[k], 1.0, 0.0))
    fmat = jnp.concatenate(f_rows, axis=0)
    smat = jnp.concatenate(s_rows, axis=0)
    cmat = fmat + smat
    carry = jnp.zeros((N_EXPERTS, 1), F32)
    ranks = []
    for blk in range(tm // TRI):
        cb = cmat[:, blk * TRI:(blk + 1) * TRI]
        ranks.append(_dot(cb.astype(BF16), tri_ref[...]) + carry)
        carry = carry + jnp.sum(cb, axis=1, keepdims=True)
    rank = jnp.concatenate(ranks, axis=1)
    padded = jnp.floor((carry + (SEG_ALIGN - 1)) * (1.0 / SEG_ALIGN)) * SEG_ALIGN
    seg_rows, run = [], jnp.zeros((1, 1), F32)
    for ex in range(N_EXPERTS):
        seg_rows.append(run)
        run = run + padded[ex:ex + 1]
    seg = jnp.concatenate(seg_rows, axis=0)
    dest = rank + seg
    d0_ref[0] = jnp.sum(fmat * dest, axis=0, keepdims=True).astype(jnp.int32)
    d1_ref[0] = jnp.sum(smat * dest, axis=0, keepdims=True).astype(jnp.int32)
    w0_ref[...] = jnp.broadcast_to(w_first, (LANES, tm)).T
    w1_ref[...] = jnp.broadcast_to(w_second, (LANES, tm)).T
    seg_ref[0] = jnp.broadcast_to(seg, (N_EXPERTS, LANES)).astype(jnp.int32)
    n_chunks = jnp.floor((carry + (MOE_CHUNK - 1)) * (1.0 / MOE_CHUNK))
    nch_ref[0] = jnp.broadcast_to(n_chunks, (N_EXPERTS, LANES)).astype(jnp.int32)


def _route(lt, tri):
    E, N = lt.shape
    tm = MOE_TILE
    nt = N // tm
    return pl.pallas_call(
        functools.partial(_route_kernel, tm=tm),
        grid=(nt,),
        in_specs=[pl.BlockSpec((E, tm), lambda i: (0, i)), pl.BlockSpec((TRI, TRI), lambda i: (0, 0))],
        out_specs=[
            pl.BlockSpec((1, 1, tm), lambda i: (i, 0, 0)),
            pl.BlockSpec((1, 1, tm), lambda i: (i, 0, 0)),
            pl.BlockSpec((tm, LANES), lambda i: (i, 0)),
            pl.BlockSpec((tm, LANES), lambda i: (i, 0)),
            pl.BlockSpec((1, E, LANES), lambda i: (i, 0, 0)),
            pl.BlockSpec((1, E, LANES), lambda i: (i, 0, 0)),
        ],
        out_shape=[
            jax.ShapeDtypeStruct((nt, 1, tm), jnp.int32),
            jax.ShapeDtypeStruct((nt, 1, tm), jnp.int32),
            jax.ShapeDtypeStruct((N, LANES), F32),
            jax.ShapeDtypeStruct((N, LANES), F32),
            jax.ShapeDtypeStruct((nt, E, LANES), jnp.int32),
            jax.ShapeDtypeStruct((nt, E, LANES), jnp.int32),
        ],
        compiler_params=_cparams(("arbitrary",)),
        name="route",
    )(lt, tri)


def _experts_kernel(seg_ref, nch_ref, d0_ref, d1_ref, u_ref, w1_ref, w3_ref, w2_ref, o_ref, xs_ref, *, tm, chunk):
    i = pl.program_id(0)
    ex = pl.program_id(1)

    @pl.when(ex == 0)
    def _():
        xs_ref[...] = jnp.zeros_like(xs_ref)
        o_ref[...] = jnp.zeros_like(o_ref)

        def move(t, carry):
            row = u_ref[pl.ds(t, 1), :]
            xs_ref[pl.ds(d0_ref[0, 0, t], 1), :] = row
            xs_ref[pl.ds(d1_ref[0, 0, t], 1), :] = row
            return carry

        lax.fori_loop(0, tm, move, 0, unroll=8)

    start = seg_ref[i * N_EXPERTS + ex]

    def do_chunk(j, carry):
        r0 = pl.multiple_of(start + j * chunk, SEG_ALIGN)
        lo, hi = _unpack_bf16_pairs(xs_ref[pl.ds(r0, chunk), :])
        xb = jnp.concatenate([lo, hi], axis=-1).astype(BF16)
        a = _dot(xb, w1_ref[0])
        hmid = a * _sigmoid(a) * _dot(xb, w3_ref[0])
        o_ref[pl.ds(r0, chunk), :] = _pack_bf16_pairs(_dot(hmid.astype(BF16), w2_ref[0]))
        return carry

    lax.fori_loop(0, nch_ref[i * N_EXPERTS + ex], do_chunk, 0)


def _experts(seg, nch, d0, d1, u2p, w1, w3, w2):
    N, half = u2p.shape
    D = 2 * half
    E = w1.shape[0]
    tm = MOE_TILE
    nt = N // tm
    kern = functools.partial(_experts_kernel, tm=tm, chunk=MOE_CHUNK)
    grid_spec = pltpu.PrefetchScalarGridSpec(
        num_scalar_prefetch=2,
        grid=(nt, E),
        in_specs=[
            pl.BlockSpec((1, 1, tm), lambda i, e, *_: (i, 0, 0), memory_space=pltpu.SMEM),
            pl.BlockSpec((1, 1, tm), lambda i, e, *_: (i, 0, 0), memory_space=pltpu.SMEM),
            pl.BlockSpec((tm, half), lambda i, e, *_: (i, 0)),
            pl.BlockSpec((1, D, D_EXPERT), lambda i, e, *_: (e, 0, 0)),
            pl.BlockSpec((1, D, D_EXPERT), lambda i, e, *_: (e, 0, 0)),
            pl.BlockSpec((1, D_EXPERT, D), lambda i, e, *_: (e, 0, 0)),
        ],
        out_specs=pl.BlockSpec((MOE_ROWS, half), lambda i, e, *_: (i, 0)),
        scratch_shapes=[pltpu.VMEM((MOE_ROWS, half), jnp.uint32)],
    )
    return pl.pallas_call(
        kern,
        grid_spec=grid_spec,
        out_shape=jax.ShapeDtypeStruct((nt * MOE_ROWS, half), jnp.uint32),
        compiler_params=_cparams(("arbitrary", "arbitrary")),
        name="experts",
    )(seg, nch, d0, d1, u2p, w1, w3, w2)


def _combine_kernel(d0_ref, d1_ref, ys_ref, w0_ref, w1_ref, x1_ref, mod_ref, o_ref, g0_ref, g1_ref, *, tsub):
    base = pl.program_id(1) * tsub

    def move(t, carry):
        g0_ref[pl.ds(t, 1), :] = ys_ref[pl.ds(d0_ref[0, 0, base + t], 1), :]
        g1_ref[pl.ds(t, 1), :] = ys_ref[pl.ds(d1_ref[0, 0, base + t], 1), :]
        return carry

    lax.fori_loop(0, tsub, move, 0, unroll=8)
    lo0, hi0 = _unpack_bf16_pairs(g0_ref[...])
    lo1, hi1 = _unpack_bf16_pairs(g1_ref[...])
    half = g0_ref.shape[1]
    w0 = jnp.concatenate([w0_ref[...]] * (half // LANES), axis=1)
    w1 = jnp.concatenate([w1_ref[...]] * (half // LANES), axis=1)
    o_ref[:, :half] = x1_ref[:, :half] + mod_ref[0, 5:6, :half] * (w0 * lo0 + w1 * lo1)
    o_ref[:, half:] = x1_ref[:, half:] + mod_ref[0, 5:6, half:] * (w0 * hi0 + w1 * hi1)


def _combine(d0, d1, ys, w0rep, w1rep, x1, mod, *, seq):
    N, D = x1.shape
    half = D // 2
    tm = MOE_TILE
    tsub = 512
    nt = N // tm
    ns = tm // tsub
    sub = lambda i, j: (i * ns + j, 0)
    return pl.pallas_call(
        functools.partial(_combine_kernel, tsub=tsub),
        grid=(nt, ns),
        in_specs=[
            pl.BlockSpec((1, 1, tm), lambda i, j: (i, 0, 0), memory_space=pltpu.SMEM),
            pl.BlockSpec((1, 1, tm), lambda i, j: (i, 0, 0), memory_space=pltpu.SMEM),
            pl.BlockSpec((MOE_ROWS, half), lambda i, j: (i, 0)),
            pl.BlockSpec((tsub, LANES), sub),
            pl.BlockSpec((tsub, LANES), sub),
            pl.BlockSpec((tsub, D), sub),
            pl.BlockSpec((1, 6, D), lambda i, j: ((i * tm + j * tsub) // seq, 0, 0)),
        ],
        out_specs=pl.BlockSpec((tsub, D), sub),
        out_shape=jax.ShapeDtypeStruct((N, D), F32),
        scratch_shapes=[pltpu.VMEM((tsub, half), jnp.uint32), pltpu.VMEM((tsub, half), jnp.uint32)],
        compiler_params=_cparams(("arbitrary", "arbitrary")),
        name="combine",
    )(d0, d1, ys, w0rep, w1rep, x1, mod)


def _alibi_slopes(first, count):
    hh = jnp.arange(first + 1, first + count + 1, dtype=F32)
    return jnp.exp2(-8.0 * hh / N_HEADS_TOTAL)


def _bd_const():
    idx = np.arange(LANES) // HEAD_DIM
    return jnp.asarray((idx[:, None] == idx[None, :]).astype(np.float32), BF16)


def _blkcols_const(T):
    blk = np.arange(T) // SEL_LEN
    m = np.where(np.arange(LANES)[None, :] == blk[:, None], NEG_INF, 0.0).astype(np.float32)
    return jnp.asarray(m, BF16)


def _overlap_const(T):
    nc = T // CMP_STRIDE
    n_sel = T // SEL_LEN
    cs = np.arange(nc) * CMP_STRIDE
    ss = np.arange(n_sel) * SEL_LEN
    ov = np.clip(np.minimum(cs[:, None] + CMP_LEN, ss[None, :] + SEL_LEN)
                 - np.maximum(cs[:, None], ss[None, :]), 0, None) / CMP_LEN
    return jnp.asarray(ov.T.astype(np.float32))


def _expand_const():
    ex = np.zeros((N_NSA_BRANCHES, LANES, D_NSA), np.float32)
    for br in range(N_NSA_BRANCHES):
        for hd in range(N_HEADS_NSA):
            ex[br, hd * N_NSA_BRANCHES + br, hd * HEAD_DIM:(hd + 1) * HEAD_DIM] = 1.0
    return jnp.asarray(ex, BF16)


def _compact_w_in(w):
    parts = [w[:, a:b] for a, b in (_C_QN, _C_QS, _C_KSL, _C_KW, _C_KS, _C_VSL, _C_VW, _C_VS, _C_KC, _C_VC, _C_GATE)]
    pad = jnp.zeros((w.shape[0], LANES - (_C_GATE[1] - _C_GATE[0])), w.dtype)
    return jnp.concatenate(parts + [pad], axis=1).astype(BF16)


def _qconst(slopes_q):
    hi = slopes_q.astype(BF16).astype(F32)
    lo = (slopes_q - hi).astype(BF16).astype(F32)
    cols = jnp.stack([64.0 * hi, 64.0 * lo, hi, lo], axis=1)
    return jnp.zeros((slopes_q.shape[0], LANES), F32).at[:, HEAD_DIM:HEAD_DIM + 4].set(cols)


def kernel(x, c, w_router, ada_w, ada_b, norm1_g, norm2_g, w_in, w_out, nsa_q_gain, nsa_k_gain, cmp_pos_k,
           cmp_pos_v, cmp_w1_k, cmp_w2_k, cmp_w1_v, cmp_w2_v, swa_q_gain, swa_k_gain, swa_sinks, beta_nsa,
           beta_swa, moe_w1, moe_w3, moe_w2):
    B, T, D = x.shape
    L = ada_w.shape[0]
    N = B * T
    nc = T // CMP_STRIDE
    assert D == D_MODEL and T % 512 == 0 and T <= 64 * 128 and B <= 8

    slopes_nsa = _alibi_slopes(N_HEADS_SWA, N_HEADS_NSA)
    slopes_swa = _alibi_slopes(0, N_HEADS_SWA)
    qconst = _qconst(jnp.concatenate([slopes_nsa, slopes_swa]))
    bd = _bd_const()
    blkcols = _blkcols_const(T)
    ov = _overlap_const(T)
    expand = _expand_const()
    wr_t = w_router.T
    tri = jnp.asarray(np.triu(np.ones((TRI, TRI), np.float32), 1), BF16)
    assert N % MOE_TILE == 0
    zero_sinks = jnp.zeros((N_HEADS_NSA,), F32)

    c_pad = jnp.zeros((8, D), F32).at[:B].set(c)
    mod_all = _ada(c_pad, ada_w, ada_b)[:, :B].reshape(L, B, 6, D)

    x2 = x.reshape(N, D)
    for l in range(L):
        mod = mod_all[l]
        tile2 = lambda g: jnp.tile(g, 2)
        gain_c = jnp.concatenate([
            jnp.tile(nsa_q_gain[l], N_HEADS_NSA) * ATTN_SCALE,
            jnp.tile(swa_q_gain[l], N_HEADS_SWA) * ATTN_SCALE,
            tile2(nsa_k_gain[l, 1]), tile2(nsa_k_gain[l, 2]), tile2(swa_k_gain[l])]).reshape(1, -1)
        proj = _inproj(x2, mod, norm1_g[l].reshape(1, D), _compact_w_in(w_in[l]), gain_c, qconst, bd, seq=T)

        kvc_raw = proj[:, SLOT_KC * LANES:(SLOT_VC + 1) * LANES]
        xc = kvc_raw.reshape(B, nc, CMP_STRIDE, 2, N_KV_NSA, HEAD_DIM).transpose(3, 0, 4, 1, 2, 5)
        xc = xc.reshape(2, B, N_KV_NSA, nc, CMP_STRIDE * HEAD_DIM)
        w1s = jnp.stack([cmp_w1_k[l], cmp_w1_v[l]]).astype(BF16)
        w2s = jnp.pad(jnp.stack([cmp_w2_k[l], cmp_w2_v[l]]), ((0, 0), (0, 0), (0, LANES - HEAD_DIM))).astype(BF16)
        pos = jnp.stack([cmp_pos_k[l].reshape(-1), cmp_pos_v[l].reshape(-1)])
        pos = jnp.broadcast_to(pos[:, None, :], (2, 8, CMP_LEN * HEAD_DIM)).astype(BF16)
        kgain = jnp.pad(nsa_k_gain[l, 0], (0, LANES - HEAD_DIM)).reshape(1, LANES)
        kvc = _cmpmlp(xc, w1s, w2s, pos, kgain)

        o_cmp, notsel = _cmpattn(proj, kvc, ov, B=B, T=T)
        o_slc = _selattn(proj, notsel, blkcols, B=B, T=T)
        o_win = _band(slopes_nsa, zero_sinks, proj, B=B, T=T, window=NSA_WINDOW, q_slot=SLOT_QN,
                      k_slot=SLOT_KW, v_slot=SLOT_VW, use_sinks=False, name="winattn")
        o_swa = _band(slopes_swa, swa_sinks[l], proj, B=B, T=T, window=SWA_WINDOW, q_slot=SLOT_QS,
                      k_slot=SLOT_KS, v_slot=SLOT_VS, use_sinks=True, name="swaattn")

        x1, u2, lt = _outproj(o_cmp, o_slc, o_win, o_swa, proj, x2, mod, expand,
                              beta_nsa[l].reshape(1, -1), beta_swa[l].reshape(1, -1), w_out[l].astype(BF16),
                              norm2_g[l].reshape(1, D), wr_t, seq=T)
        d0, d1, w0rep, w1rep, seg, nch = _route(lt, tri)
        ys = _experts(seg[:, :, 0].reshape(-1), nch[:, :, 0].reshape(-1), d0, d1, u2,
                      moe_w1[l].astype(BF16), moe_w3[l].astype(BF16), moe_w2[l].astype(BF16))
        x2 = _combine(d0, d1, ys, w0rep, w1rep, x1, mod, seq=T)
    return x2.reshape(B, T, D)
```

```python
import functools

import numpy as np
import jax
import jax.numpy as jnp
from jax import lax
from jax.experimental import pallas as pl
from jax.experimental.pallas import tpu as pltpu

D_MODEL = 1024
DEPTH = 2
HEAD_DIM = 64
N_HEADS_NSA = 8
N_KV_NSA = 2
G_NSA = N_HEADS_NSA // N_KV_NSA
N_HEADS_SWA = 8
N_KV_SWA = 2
G_SWA = N_HEADS_SWA // N_KV_SWA
D_NSA = N_HEADS_NSA * HEAD_DIM
D_SWA = N_HEADS_SWA * HEAD_DIM
D_MIX = D_NSA + D_SWA
N_HEADS_TOTAL = N_HEADS_NSA + N_HEADS_SWA
KV_NSA = N_KV_NSA * HEAD_DIM
KV_SWA = N_KV_SWA * HEAD_DIM
N_NSA_BRANCHES = 3
CMP_LEN = 32
CMP_STRIDE = 16
CMP_HIDDEN = 256
SEL_LEN = 64
SEL_TOPK = 16
NSA_WINDOW = 512
SWA_WINDOW = 128
N_EXPERTS = 16
N_GROUPS = 4
EXPERTS_PER_GROUP = N_EXPERTS // N_GROUPS
TOP_K = 2
D_EXPERT = 512
EPS = 1e-6
NEG_INF = -1e30
ATTN_SCALE = HEAD_DIM ** -0.5

F32 = jnp.float32
BF16 = jnp.bfloat16

LANES = 128
VMEM_LIMIT = 48 * 1024 * 1024

MOE_TILE = 2048
MOE_CHUNK = 256
SEG_ALIGN = 16
TRI = 256
MOE_ROWS = TOP_K * MOE_TILE + N_EXPERTS * SEG_ALIGN + MOE_CHUNK

SLOT_QN = 0
SLOT_QS = 8
SLOT_KSL = 16
SLOT_KW = 18
SLOT_KS = 20
SLOT_VSL = 22
SLOT_VW = 24
SLOT_VS = 26
SLOT_GATE = 28
N_SLOTS = 29
N_CHUNKS = 17
N_NORM_CHUNKS = 11

_C_QN = (0, 512)
_C_KC = (512, 640)
_C_VC = (640, 768)
_C_KSL = (768, 896)
_C_VSL = (896, 1024)
_C_KW = (1024, 1152)
_C_VW = (1152, 1280)
_C_GATE = (1280, 1304)
_C_QS = (1304, 1816)
_C_KS = (1816, 1944)
_C_VS = (1944, 2072)


def _cparams(sem):
    return pltpu.CompilerParams(dimension_semantics=sem, vmem_limit_bytes=VMEM_LIMIT)


def _dot(a, b, **kw):
    return jnp.dot(a, b, preferred_element_type=F32, **kw)


def _dot_nt(a, b):
    return lax.dot_general(a, b, (((1,), (1,)), ((), ())), preferred_element_type=F32)


def _sigmoid(x):
    return 1.0 / (1.0 + jnp.exp(-x))


def _pack_bf16_pairs(x):
    n = x.shape[1] // 2
    return pltpu.pack_elementwise([x[:, :n], x[:, n:]], packed_dtype=BF16)


def _unpack_bf16_pairs(w):
    lo = pltpu.unpack_elementwise(w, index=0, packed_dtype=BF16, unpacked_dtype=F32)
    hi = pltpu.unpack_elementwise(w, index=1, packed_dtype=BF16, unpacked_dtype=F32)
    return lo, hi


def _ada_kernel(c_ref, w_ref, b_ref, o_ref):
    c = c_ref[...]
    cond = c * _sigmoid(c)
    o_ref[0] = _dot(cond.astype(BF16), w_ref[0].astype(BF16)) + b_ref[0]


def _ada(c_pad, ada_w, ada_b):
    L, D, N6 = ada_w.shape
    tn = 1536
    return pl.pallas_call(
        _ada_kernel,
        grid=(L, N6 // tn),
        in_specs=[
            pl.BlockSpec((8, D), lambda l, j: (0, 0)),
            pl.BlockSpec((1, D, tn), lambda l, j: (l, 0, j)),
            pl.BlockSpec((1, 1, tn), lambda l, j: (l, 0, j)),
        ],
        out_specs=pl.BlockSpec((1, 8, tn), lambda l, j: (l, 0, j)),
        out_shape=jax.ShapeDtypeStruct((L, 8, N6), F32),
        compiler_params=_cparams(("arbitrary", "arbitrary")),
        name="ada",
    )(c_pad, ada_w, ada_b.reshape(L, 1, N6))


def _inproj_kernel(x_ref, mod_ref, g_ref, w_ref, gain_ref, qc_ref, bd_ref, o_ref, xc_ref, scr_ref, *, tm, seq):
    i = pl.program_id(0)
    x = x_ref[...]
    ms = jnp.mean(x * x, axis=-1, keepdims=True)
    u = x * lax.rsqrt(ms + EPS) * g_ref[...]
    u = u * (1.0 + mod_ref[0, 1:2, :]) + mod_ref[0, 0:1, :]
    y = _dot(u.astype(BF16), w_ref[...])

    lane = lax.broadcasted_iota(jnp.int32, (tm, LANES), 1)
    low = lane < HEAD_DIM
    t = lax.broadcasted_iota(jnp.int32, (tm, LANES), 0) + lax.rem(i * tm, seq)
    hi_part = (t >> 6).astype(F32)
    lo_part = (t & 63).astype(F32)
    poscols = jnp.where((lane == 64) | (lane == 65), hi_part,
                        jnp.where((lane == 66) | (lane == 67), lo_part, 0.0))
    bd = bd_ref[...]

    def put(slot, val):
        o_ref[:, slot * LANES:(slot + 1) * LANES] = val.astype(o_ref.dtype)

    for c in range(N_CHUNKS):
        blk = y[:, c * LANES:(c + 1) * LANES]
        if c < N_NORM_CHUNKS:
            msb = _dot((blk * blk).astype(BF16), bd) * (1.0 / HEAD_DIM)
            blk = blk * lax.rsqrt(msb + EPS) * gain_ref[:, c * LANES:(c + 1) * LANES]
        if c < 14:
            rolled = pltpu.roll(blk, HEAD_DIM, axis=1)
            if c < 8:
                f0 = qc_ref[2 * c:2 * c + 1, :]
                f1 = qc_ref[2 * c + 1:2 * c + 2, :]
                s0 = 2 * c
            elif c < N_NORM_CHUNKS:
                f0 = f1 = poscols
                s0 = SLOT_KSL + 2 * (c - 8)
            else:
                f0 = f1 = 1.0
                s0 = SLOT_VSL + 2 * (c - N_NORM_CHUNKS)
            put(s0, jnp.where(low, blk, f0))
            put(s0 + 1, jnp.where(low, rolled, f1))
        elif c == 16:
            put(SLOT_GATE, blk)
        else:
            scr_ref[...] = blk
            for tok in range(CMP_STRIDE):
                xc_ref[c - 14, :, tok * LANES:(tok + 1) * LANES] = scr_ref[
                    pl.ds(tok, tm // CMP_STRIDE, stride=CMP_STRIDE), :].astype(xc_ref.dtype)


def _inproj(x2, mod, norm_g, w_c, gain_c, qconst, bd, *, seq):
    N, D = x2.shape
    tm = 512
    tiles_per_seq = seq // tm
    kern = functools.partial(_inproj_kernel, tm=tm, seq=seq)
    nchunk = tm // CMP_STRIDE
    xc_shape = jax.ShapeDtypeStruct((2, N // CMP_STRIDE, CMP_STRIDE * LANES), BF16)
    return pl.pallas_call(
        kern,
        grid=(N // tm,),
        in_specs=[
            pl.BlockSpec((tm, D), lambda i: (i, 0)),
            pl.BlockSpec((1, 6, D), lambda i: (i // tiles_per_seq, 0, 0)),
            pl.BlockSpec((1, D), lambda i: (0, 0)),
            pl.BlockSpec((D, N_CHUNKS * LANES), lambda i: (0, 0)),
            pl.BlockSpec((1, N_NORM_CHUNKS * LANES), lambda i: (0, 0)),
            pl.BlockSpec((16, LANES), lambda i: (0, 0)),
            pl.BlockSpec((LANES, LANES), lambda i: (0, 0)),
        ],
        out_specs=[
            pl.BlockSpec((tm, N_SLOTS * LANES), lambda i: (i, 0)),
            pl.BlockSpec((2, nchunk, CMP_STRIDE * LANES), lambda i: (0, i, 0)),
        ],
        out_shape=[jax.ShapeDtypeStruct((N, N_SLOTS * LANES), BF16), xc_shape],
        scratch_shapes=[pltpu.VMEM((tm, LANES), F32)],
        compiler_params=_cparams(("arbitrary",)),
        name="inproj",
    )(x2, mod, norm_g, w_c, gain_c, qconst, bd)


def _gelu_tanh(x):
    return 0.5 * x * (1.0 + jnp.tanh(np.sqrt(2.0 / np.pi).astype(np.float32) * (x + 0.044715 * (x * x * x))))


def _cmpmlp_kernel(x_ref, w1_ref, w2_ref, pos_ref, gain_ref, o_ref, *, nc):
    kv = pl.program_id(0)
    xc = x_ref[0]
    half = CMP_STRIDE * LANES
    first = _dot(xc, w1_ref[0, :half, :])
    second = _dot(xc, w1_ref[0, half:, :])
    bias = _dot(pos_ref[0], w1_ref[0])[0:1]
    h = first + pltpu.roll(second, nc - 1, axis=0) + bias
    h = _gelu_tanh(h)
    z2 = _dot(h.astype(BF16), w2_ref[0])
    lane = lax.broadcasted_iota(jnp.int32, (nc, LANES), 1)
    last = lax.broadcasted_iota(jnp.int32, (nc, LANES), 0) * CMP_STRIDE + (CMP_LEN - 1)
    poscols = jnp.where((lane == 64) | (lane == 65), (last >> 6).astype(F32),
                        jnp.where((lane == 66) | (lane == 67), (last & 63).astype(F32), 0.0))
    for hd in range(N_KV_NSA):
        z = z2[:, hd * LANES:(hd + 1) * LANES]
        zn = z * lax.rsqrt(jnp.sum(z * z, axis=-1, keepdims=True) * (1.0 / HEAD_DIM) + EPS) * gain_ref[...]
        o_ref[0, 0, hd] = jnp.where(kv == 0, zn + poscols, z + jnp.where(lane >= HEAD_DIM, 1.0, 0.0))


def _cmpmlp(xc, w1s, w2s, pos, gain, *, B):
    _, nchunks, K = xc.shape
    nc = nchunks // B
    hid = N_KV_NSA * CMP_HIDDEN
    kern = functools.partial(_cmpmlp_kernel, nc=nc)
    return pl.pallas_call(
        kern,
        grid=(2, B),
        in_specs=[
            pl.BlockSpec((1, nc, K), lambda kv, b: (kv, b, 0)),
            pl.BlockSpec((1, 2 * K, hid), lambda kv, b: (kv, 0, 0)),
            pl.BlockSpec((1, hid, N_KV_NSA * LANES), lambda kv, b: (kv, 0, 0)),
            pl.BlockSpec((1, 8, 2 * K), lambda kv, b: (kv, 0, 0)),
            pl.BlockSpec((1, LANES), lambda kv, b: (0, 0)),
        ],
        out_specs=pl.BlockSpec((1, 1, N_KV_NSA, nc, LANES), lambda kv, b: (kv, b, 0, 0, 0)),
        out_shape=jax.ShapeDtypeStruct((2, B, N_KV_NSA, nc, LANES), F32),
        compiler_params=_cparams(("arbitrary", "arbitrary")),
        name="cmpmlp",
    )(xc, w1s, w2s, pos, gain)


def _normalize_pack(pvs, extra=None):
    lane = lax.broadcasted_iota(jnp.int32, pvs[0].shape, 1)
    low = lane < HEAD_DIM
    pairs = []
    for p in range(len(pvs) // 2):
        even, odd = pvs[2 * p], pvs[2 * p + 1]
        den_e = even if extra is None else even + extra[2 * p]
        den_o = odd if extra is None else odd + extra[2 * p + 1]
        o_e = even * (1.0 / pltpu.roll(den_e, HEAD_DIM, axis=1))
        o_o = pltpu.roll(odd, HEAD_DIM, axis=1) * (1.0 / den_o)
        pairs.append(jnp.where(low, o_e, o_o))
    return jnp.concatenate(pairs, axis=-1)


def _cmpattn_kernel(q_ref, kc_ref, vc_ref, ov_ref, o_ref, ns_ref, *, tq, nc, n_cmp, n_sel, top):
    q0 = pl.program_id(2) * tq
    kc = kc_ref[0, 0, 0].astype(BF16)
    vc = vc_ref[0, 0, 0].astype(BF16)

    t_col = q0 + lax.broadcasted_iota(jnp.int32, (tq, 1), 0)
    row_valid = (t_col >= CMP_LEN - 1).astype(F32)
    t_row = q0 + lax.broadcasted_iota(jnp.int32, (1, tq), 1)
    n_col = lax.broadcasted_iota(jnp.int32, (nc, 1), 0)
    valid_t = (t_row - (n_col * CMP_STRIDE + (CMP_LEN - 1)) >= 0) & (n_col < n_cmp)
    bias_t = jnp.where(valid_t, 0.0, NEG_INF)
    row_valid_t = (t_row >= CMP_LEN - 1).astype(F32)

    outs = []
    psum_t = jnp.zeros((nc, tq), F32)
    for g in range(G_NSA):
        qg = q_ref[:, g * LANES:(g + 1) * LANES]
        st = _dot_nt(kc, qg) + bias_t
        et = jnp.exp(st - jnp.max(st, axis=0, keepdims=True))
        outs.append(lax.dot_general(et.astype(BF16), vc, (((0,), (0,)), ((), ())), preferred_element_type=F32))
        psum_t = psum_t + et * (1.0 / jnp.sum(et, axis=0, keepdims=True))
    o_ref[...] = (_normalize_pack(outs) * row_valid).astype(o_ref.dtype)

    psum_t = psum_t * row_valid_t
    imp_t = _dot(ov_ref[...], psum_t, precision=lax.Precision.HIGHEST)
    j = lax.broadcasted_iota(jnp.int32, (n_sel, 1), 0)
    cur = t_row >> 6
    forced = (j == 0) | (j == cur) | (j == cur - 1)
    v = jnp.where(forced, 1e9, jnp.where(j > cur, NEG_INF, imp_t))
    n_grp = n_sel // 8
    vg = [v[8 * r:8 * r + 8, :] for r in range(n_grp)]
    jg = lax.broadcasted_iota(jnp.int32, (8, 1), 0)
    cnt = [jnp.zeros((8, tq), F32) for _ in range(n_grp)]
    for jp in range(n_sel):
        vj = v[jp:jp + 1, :]
        for r in range(n_grp):
            ge = jnp.where(vj >= vg[r], 1.0, 0.0)
            gt = jnp.where(vj > vg[r], 1.0, 0.0)
            if 8 * r > jp:
                inc = ge
            elif 8 * r + 7 < jp:
                inc = gt
            else:
                inc = jnp.where(jg + 8 * r > jp, ge, gt)
            cnt[r] = cnt[r] + inc
    notsel_t = jnp.concatenate([jnp.where(cn < top, 0.0, 1.0) for cn in cnt], axis=0)
    if n_sel < LANES:
        notsel_t = jnp.concatenate([notsel_t, jnp.zeros((LANES - n_sel, tq), F32)], axis=0)
    ns_ref[0, 0] = notsel_t.T.astype(ns_ref.dtype)


def _cmpattn(proj, kvc, ov, *, B, T):
    tq = 256
    nt = T // tq
    nc = T // CMP_STRIDE
    n_cmp = (T - CMP_LEN) // CMP_STRIDE + 1
    n_sel = T // SEL_LEN
    top = min(SEL_TOPK, n_sel)
    assert n_sel <= LANES and n_sel % 8 == 0
    kern = functools.partial(_cmpattn_kernel, tq=tq, nc=nc, n_cmp=n_cmp, n_sel=n_sel, top=top)
    return pl.pallas_call(
        kern,
        grid=(B, N_KV_NSA, nt),
        in_specs=[
            pl.BlockSpec((tq, G_NSA * LANES), lambda b, h, i: (b * nt + i, h)),
            pl.BlockSpec((1, 1, 1, nc, LANES), lambda b, h, i: (0, b, h, 0, 0)),
            pl.BlockSpec((1, 1, 1, nc, LANES), lambda b, h, i: (1, b, h, 0, 0)),
            pl.BlockSpec((n_sel, nc), lambda b, h, i: (0, 0)),
        ],
        out_specs=[
            pl.BlockSpec((tq, G_NSA * HEAD_DIM), lambda b, h, i: (b * nt + i, h)),
            pl.BlockSpec((1, 1, tq, LANES), lambda b, h, i: (b, h, i, 0)),
        ],
        out_shape=[
            jax.ShapeDtypeStruct((B * T, D_NSA), BF16),
            jax.ShapeDtypeStruct((B, N_KV_NSA, T, LANES), BF16),
        ],
        compiler_params=_cparams(("arbitrary", "arbitrary", "arbitrary")),
        name="cmpattn",
    )(proj, kvc, kvc, ov)


def _selattn_kernel(q_ref, ns_ref, k_ref, v_ref, bc_ref, o_ref, *, tq, tk, n_chain):
    q0 = pl.program_id(2) * tq
    ns = ns_ref[0, 0]
    tc = tq // n_chain
    qx = [jnp.concatenate(
        [jnp.concatenate([q_ref[c * tc:(c + 1) * tc, g * LANES:(g + 1) * LANES], ns[c * tc:(c + 1) * tc]], axis=-1)
         for g in range(G_NSA)], axis=0) for c in range(n_chain)]
    m_rows = G_NSA * tc
    diag = q0 // tk

    def step(kt, carry, masked):
        start = pl.multiple_of(kt * tk, tk)
        kx = jnp.concatenate([k_ref[pl.ds(start, tk), :], bc_ref[pl.ds(start, tk), :]], axis=-1)
        vx = v_ref[pl.ds(start, tk), :]
        out = []
        for c in range(n_chain):
            m, acc = carry[c]
            s = _dot_nt(qx[c], kx)
            if masked:
                r = lax.broadcasted_iota(jnp.int32, (tc, tk), 0) + c * tc
                col = lax.broadcasted_iota(jnp.int32, (tc, tk), 1)
                s = s + jnp.concatenate([jnp.where(col <= r, 0.0, NEG_INF)] * G_NSA, axis=0)
            m_new = jnp.maximum(m, jnp.max(s, axis=-1, keepdims=True))
            p = jnp.exp(s - m_new).astype(BF16)
            out.append((m_new, jnp.exp(m - m_new) * acc + _dot(p, vx)))
        return tuple(out)

    init = tuple((jnp.full((m_rows, 1), NEG_INF, F32), jnp.zeros((m_rows, LANES), F32)) for _ in range(n_chain))
    carry = lax.fori_loop(0, diag, functools.partial(step, masked=False), init)
    carry = step(diag, carry, True)
    for c in range(n_chain):
        acc = carry[c][1]
        o_ref[c * tc:(c + 1) * tc, :] = _normalize_pack(
            [acc[g * tc:(g + 1) * tc] for g in range(G_NSA)]).astype(o_ref.dtype)


def _selattn(proj, notsel, blkcols, *, B, T):
    tq = 512
    tk = 512
    nt = T // tq
    assert T % tk == 0 and tk == tq
    kern = functools.partial(_selattn_kernel, tq=tq, tk=tk, n_chain=1)
    return pl.pallas_call(
        kern,
        grid=(B, N_KV_NSA, nt),
        in_specs=[
            pl.BlockSpec((tq, G_NSA * LANES), lambda b, h, i: (b * nt + i, h)),
            pl.BlockSpec((1, 1, tq, LANES), lambda b, h, i: (b, h, i, 0)),
            pl.BlockSpec((T, LANES), lambda b, h, i: (b, SLOT_KSL + h)),
            pl.BlockSpec((T, LANES), lambda b, h, i: (b, SLOT_VSL + h)),
            pl.BlockSpec((T, LANES), lambda b, h, i: (0, 0)),
        ],
        out_specs=pl.BlockSpec((tq, G_NSA * HEAD_DIM), lambda b, h, i: (b * nt + i, h)),
        out_shape=jax.ShapeDtypeStruct((B * T, D_NSA), BF16),
        compiler_params=_cparams(("arbitrary", "arbitrary", "arbitrary")),
        name="selattn",
    )(proj, notsel, proj, proj, blkcols)


def _band_kernel(sl_ref, sink_ref, q_ref, k_ref, v_ref, o_ref, *, tq, n_sub, window, seq, use_sinks):
    h = pl.program_id(1)
    span = window + tq
    groups = q_ref.shape[1] // LANES
    r = lax.broadcasted_iota(jnp.int32, (tq, span), 0)
    col = lax.broadcasted_iota(jnp.int32, (tq, span), 1)
    for sub in range(n_sub):
        q0 = (pl.program_id(2) * n_sub + sub) * tq
        start = pl.multiple_of(jnp.clip(q0 - window, 0, seq - span), tq)
        rows = pl.ds(sub * tq, tq)
        qx = jnp.concatenate([q_ref[rows, g * LANES:(g + 1) * LANES] for g in range(groups)], axis=0)
        d = (q0 - start) + r - col
        bias = jnp.where((d >= 0) & (d < window), 0.0, NEG_INF)
        s = _dot_nt(qx, k_ref[pl.ds(start, span), :]) + jnp.concatenate([bias] * groups, axis=0)
        m = jnp.broadcast_to(jnp.max(s, axis=-1, keepdims=True), (groups * tq, LANES))
        if use_sinks:
            t_rep = (q0 + lax.broadcasted_iota(jnp.int32, (tq, LANES), 0)).astype(F32)
            sink = jnp.concatenate(
                [sink_ref[h * groups + g] + sl_ref[h * groups + g] * t_rep for g in range(groups)], axis=0)
            m = jnp.maximum(m, sink)
        e = jnp.exp(s - jnp.concatenate([m] * (span // LANES), axis=1)).astype(BF16)
        pv = _dot(e, v_ref[pl.ds(start, span), :])
        extra = None
        if use_sinks:
            esink = jnp.exp(sink - m)
            extra = [esink[g * tq:(g + 1) * tq] for g in range(groups)]
        o_ref[rows, :] = _normalize_pack([pv[g * tq:(g + 1) * tq] for g in range(groups)], extra).astype(o_ref.dtype)


def _band(slopes, sinks, proj, *, B, T, window, q_slot, k_slot, v_slot, use_sinks, name):
    tq = 128
    n_sub = 4
    tb = tq * n_sub
    nt = T // tb
    groups = 4
    assert window % tq == 0 and window + tq <= T and T % tb == 0
    qb = q_slot // groups
    kern = functools.partial(_band_kernel, tq=tq, n_sub=n_sub, window=window, seq=T, use_sinks=use_sinks)
    return pl.pallas_call(
        kern,
        grid=(B, 2, nt),
        in_specs=[
            pl.BlockSpec(memory_space=pltpu.SMEM),
            pl.BlockSpec(memory_space=pltpu.SMEM),
            pl.BlockSpec((tb, groups * LANES), lambda b, h, i: (b * nt + i, qb + h)),
            pl.BlockSpec((T, LANES), lambda b, h, i: (b, k_slot + h)),
            pl.BlockSpec((T, LANES), lambda b, h, i: (b, v_slot + h)),
        ],
        out_specs=pl.BlockSpec((tb, groups * HEAD_DIM), lambda b, h, i: (b * nt + i, h)),
        out_shape=jax.ShapeDtypeStruct((B * T, 2 * groups * HEAD_DIM), BF16),
        compiler_params=_cparams(("arbitrary", "arbitrary", "arbitrary")),
        name=name,
    )(slopes, sinks, proj, proj, proj)


def _outproj_kernel(ocmp_ref, oslc_ref, owin_ref, oswa_ref, gate_ref, x_ref, mod_ref, ex_ref, bn_ref, bs_ref,
                    w_ref, g2_ref, wr_ref, x1_ref, u2_ref, lt_ref):
    sg = _sigmoid(gate_ref[...].astype(F32)).astype(BF16)
    o_nsa = (_dot(sg, ex_ref[0]) * ocmp_ref[...].astype(F32)
             + _dot(sg, ex_ref[1]) * oslc_ref[...].astype(F32)
             + _dot(sg, ex_ref[2]) * owin_ref[...].astype(F32))
    o_swa = oswa_ref[...].astype(F32)
    n1 = o_nsa * lax.rsqrt(jnp.mean(o_nsa * o_nsa, axis=-1, keepdims=True) + EPS) * bn_ref[...]
    n2 = o_swa * lax.rsqrt(jnp.mean(o_swa * o_swa, axis=-1, keepdims=True) + EPS) * bs_ref[...]
    merged = jnp.concatenate([n1, n2], axis=-1).astype(BF16)
    y = _dot(merged, w_ref[...])
    x1 = x_ref[...] + mod_ref[0, 2:3, :] * y
    x1_ref[...] = x1
    u2 = x1 * lax.rsqrt(jnp.mean(x1 * x1, axis=-1, keepdims=True) + EPS) * g2_ref[...]
    u2 = u2 * (1.0 + mod_ref[0, 4:5, :]) + mod_ref[0, 3:4, :]
    u2_ref[...] = _pack_bf16_pairs(u2)
    lt_ref[...] = lax.dot_general(wr_ref[...], u2, (((1,), (1,)), ((), ())), preferred_element_type=F32,
                                  precision=lax.Precision.HIGHEST)


def _outproj(ocmp, oslc, owin, oswa, proj, x2, mod, expand, beta_n, beta_s, w_out, norm2_g, wr_t, *, seq):
    N, D = x2.shape
    tm = 512
    tiles_per_seq = seq // tm
    row = lambda i: (i, 0)
    const2 = lambda i: (0, 0)
    return pl.pallas_call(
        _outproj_kernel,
        grid=(N // tm,),
        in_specs=[
            pl.BlockSpec((tm, D_NSA), row),
            pl.BlockSpec((tm, D_NSA), row),
            pl.BlockSpec((tm, D_NSA), row),
            pl.BlockSpec((tm, D_SWA), row),
            pl.BlockSpec((tm, LANES), lambda i: (i, SLOT_GATE)),
            pl.BlockSpec((tm, D), row),
            pl.BlockSpec((1, 6, D), lambda i: (i // tiles_per_seq, 0, 0)),
            pl.BlockSpec((N_NSA_BRANCHES, LANES, D_NSA), lambda i: (0, 0, 0)),
            pl.BlockSpec((1, D_NSA), const2),
            pl.BlockSpec((1, D_SWA), const2),
            pl.BlockSpec((D_MIX, D), const2),
            pl.BlockSpec((1, D), const2),
            pl.BlockSpec((N_EXPERTS, D), const2),
        ],
        out_specs=[
            pl.BlockSpec((tm, D), row),
            pl.BlockSpec((tm, D // 2), row),
            pl.BlockSpec((N_EXPERTS, tm), lambda i: (0, i)),
        ],
        out_shape=[
            jax.ShapeDtypeStruct((N, D), F32),
            jax.ShapeDtypeStruct((N, D // 2), jnp.uint32),
            jax.ShapeDtypeStruct((N_EXPERTS, N), F32),
        ],
        compiler_params=_cparams(("arbitrary",)),
        name="outproj",
    )(ocmp, oslc, owin, oswa, proj, x2, mod, expand, beta_n, beta_s, w_out, norm2_g, wr_t)


def _route_kernel(lt_ref, tri_ref, d0_ref, d1_ref, w0_ref, w1_ref, seg_ref, nch_ref, tail_ref, *, tm):
    lt = lt_ref[...]
    e = jnp.exp(lt - jnp.max(lt, axis=0, keepdims=True))
    aff = e / jnp.sum(e, axis=0, keepdims=True)
    rows = [aff[k:k + 1, :] for k in range(N_EXPERTS)]
    epg = EXPERTS_PER_GROUP
    scores = []
    for gr in range(N_GROUPS):
        xs = rows[gr * epg:(gr + 1) * epg]
        top1 = functools.reduce(jnp.maximum, xs)
        second = None
        for a in range(epg):
            for b in range(a + 1, epg):
                mn = jnp.minimum(xs[a], xs[b])
                second = mn if second is None else jnp.maximum(second, mn)
        scores.append(top1 + second)
    taken = None
    in_group = []
    for gr in range(N_GROUPS):
        best = None
        for o in range(gr + 1, N_GROUPS):
            c = scores[gr] >= scores[o]
            best = c if best is None else (best & c)
        if best is None:
            best = jnp.full(scores[gr].shape, True)
        sel = best if taken is None else (best & ~taken)
        taken = sel if taken is None else (taken | sel)
        in_group.append(sel)
    ys = []
    for k in range(epg):
        yk = rows[k]
        for gr in range(1, N_GROUPS):
            yk = jnp.where(in_group[gr], rows[gr * epg + k], yk)
        ys.append(yk)
    chosen = []
    for k in range(epg):
        rk = jnp.zeros_like(ys[k])
        for o in range(epg):
            if o == k:
                continue
            before = (ys[o] >= ys[k]) if o < k else (ys[o] > ys[k])
            rk = rk + jnp.where(before, 1.0, 0.0)
        chosen.append(rk < TOP_K)
    total = functools.reduce(lambda a, b: a + b, [jnp.where(chosen[k], ys[k], 0.0) for k in range(epg)])
    first, seen = [], None
    for k in range(epg):
        first.append(chosen[k] if seen is None else (chosen[k] & ~seen))
        seen = chosen[k] if seen is None else (seen | chosen[k])
    second = [chosen[k] & ~first[k] for k in range(epg)]
    inv_total = 1.0 / total
    w_first = functools.reduce(lambda a, b: a + b, [jnp.where(first[k], ys[k], 0.0) for k in range(epg)]) * inv_total
    w_second = functools.reduce(lambda a, b: a + b, [jnp.where(second[k], ys[k], 0.0) for k in range(epg)]) * inv_total
    f_rows, s_rows = [], []
    for ex in range(N_EXPERTS):
        gr, k = divmod(ex, epg)
        f_rows.append(jnp.where(in_group[gr] & first[k], 1.0, 0.0))
        s_rows.append(jnp.where(in_group[gr] & second[k], 1.0, 0.0))
    fmat = jnp.concatenate(f_rows, axis=0)
    smat = jnp.concatenate(s_rows, axis=0)
    cmat = fmat + smat
    carry = jnp.zeros((N_EXPERTS, 1), F32)
    ranks = []
    for blk in range(tm // TRI):
        cb = cmat[:, blk * TRI:(blk + 1) * TRI]
        ranks.append(_dot(cb.astype(BF16), tri_ref[...]) + carry)
        carry = carry + jnp.sum(cb, axis=1, keepdims=True)
    rank = jnp.concatenate(ranks, axis=1)
    padded = jnp.floor((carry + (SEG_ALIGN - 1)) * (1.0 / SEG_ALIGN)) * SEG_ALIGN
    seg_rows, run = [], jnp.zeros((1, 1), F32)
    for ex in range(N_EXPERTS):
        seg_rows.append(run)
        run = run + padded[ex:ex + 1]
    seg = jnp.concatenate(seg_rows, axis=0)
    dest = rank + seg
    d0_ref[0] = jnp.sum(fmat * dest, axis=0, keepdims=True).astype(jnp.int32)
    d1_ref[0] = jnp.sum(smat * dest, axis=0, keepdims=True).astype(jnp.int32)
    w0_ref[...] = jnp.broadcast_to(w_first, (LANES, tm)).T
    w1_ref[...] = jnp.broadcast_to(w_second, (LANES, tm)).T
    seg_ref[0] = jnp.broadcast_to(seg, (N_EXPERTS, LANES)).astype(jnp.int32)
    whole = jnp.floor(carry * (1.0 / MOE_CHUNK))
    rem = carry - whole * MOE_CHUNK
    n_full = whole + jnp.where(rem > MOE_CHUNK // 2, 1.0, 0.0)
    half_tail = jnp.where((rem > 0) & (rem <= MOE_CHUNK // 2), 1.0, 0.0)
    nch_ref[0] = jnp.broadcast_to(n_full, (N_EXPERTS, LANES)).astype(jnp.int32)
    tail_ref[0] = jnp.broadcast_to(half_tail, (N_EXPERTS, LANES)).astype(jnp.int32)


def _route(lt, tri):
    E, N = lt.shape
    tm = MOE_TILE
    nt = N // tm
    return pl.pallas_call(
        functools.partial(_route_kernel, tm=tm),
        grid=(nt,),
        in_specs=[pl.BlockSpec((E, tm), lambda i: (0, i)), pl.BlockSpec((TRI, TRI), lambda i: (0, 0))],
        out_specs=[
            pl.BlockSpec((1, 1, tm), lambda i: (i, 0, 0)),
            pl.BlockSpec((1, 1, tm), lambda i: (i, 0, 0)),
            pl.BlockSpec((tm, LANES), lambda i: (i, 0)),
            pl.BlockSpec((tm, LANES), lambda i: (i, 0)),
            pl.BlockSpec((1, E, LANES), lambda i: (i, 0, 0)),
            pl.BlockSpec((1, E, LANES), lambda i: (i, 0, 0)),
            pl.BlockSpec((1, E, LANES), lambda i: (i, 0, 0)),
        ],
        out_shape=[
            jax.ShapeDtypeStruct((nt, 1, tm), jnp.int32),
            jax.ShapeDtypeStruct((nt, 1, tm), jnp.int32),
            jax.ShapeDtypeStruct((N, LANES), F32),
            jax.ShapeDtypeStruct((N, LANES), F32),
            jax.ShapeDtypeStruct((nt, E, LANES), jnp.int32),
            jax.ShapeDtypeStruct((nt, E, LANES), jnp.int32),
            jax.ShapeDtypeStruct((nt, E, LANES), jnp.int32),
        ],
        compiler_params=_cparams(("arbitrary",)),
        name="route",
    )(lt, tri)


def _experts_kernel(seg_ref, nch_ref, tail_ref, d0_ref, d1_ref, u_ref, w1_ref, w3_ref, w2_ref, o_ref, xs_ref, *, tm,
                    chunk):
    i = pl.program_id(0)
    ex = pl.program_id(1)

    @pl.when(ex == 0)
    def _():
        xs_ref[...] = jnp.zeros_like(xs_ref)
        o_ref[...] = jnp.zeros_like(o_ref)

        def move(t, carry):
            row = u_ref[pl.ds(t, 1), :]
            xs_ref[pl.ds(d0_ref[0, 0, t], 1), :] = row
            xs_ref[pl.ds(d1_ref[0, 0, t], 1), :] = row
            return carry

        lax.fori_loop(0, tm, move, 0, unroll=8)

    start = seg_ref[i * N_EXPERTS + ex]
    n_full = nch_ref[i * N_EXPERTS + ex]

    def do_rows(r0, rows):
        lo, hi = _unpack_bf16_pairs(xs_ref[pl.ds(r0, rows), :])
        xb = jnp.concatenate([lo, hi], axis=-1).astype(BF16)
        a = _dot(xb, w1_ref[0])
        hmid = a * _sigmoid(a) * _dot(xb, w3_ref[0])
        o_ref[pl.ds(r0, rows), :] = _pack_bf16_pairs(_dot(hmid.astype(BF16), w2_ref[0]))

    def do_chunk(j, carry):
        do_rows(pl.multiple_of(start + j * chunk, SEG_ALIGN), chunk)
        return carry

    lax.fori_loop(0, n_full, do_chunk, 0)

    @pl.when(tail_ref[i * N_EXPERTS + ex] > 0)
    def _():
        do_rows(pl.multiple_of(start + n_full * chunk, SEG_ALIGN), chunk // 2)


def _experts(seg, nch, tail, d0, d1, u2p, w1, w3, w2):
    N, half = u2p.shape
    D = 2 * half
    E = w1.shape[0]
    tm = MOE_TILE
    nt = N // tm
    kern = functools.partial(_experts_kernel, tm=tm, chunk=MOE_CHUNK)
    grid_spec = pltpu.PrefetchScalarGridSpec(
        num_scalar_prefetch=3,
        grid=(nt, E),
        in_specs=[
            pl.BlockSpec((1, 1, tm), lambda i, e, *_: (i, 0, 0), memory_space=pltpu.SMEM),
            pl.BlockSpec((1, 1, tm), lambda i, e, *_: (i, 0, 0), memory_space=pltpu.SMEM),
            pl.BlockSpec((tm, half), lambda i, e, *_: (i, 0)),
            pl.BlockSpec((1, D, D_EXPERT), lambda i, e, *_: (e, 0, 0)),
            pl.BlockSpec((1, D, D_EXPERT), lambda i, e, *_: (e, 0, 0)),
            pl.BlockSpec((1, D_EXPERT, D), lambda i, e, *_: (e, 0, 0)),
        ],
        out_specs=pl.BlockSpec((MOE_ROWS, half), lambda i, e, *_: (i, 0)),
        scratch_shapes=[pltpu.VMEM((MOE_ROWS, half), jnp.uint32)],
    )
    return pl.pallas_call(
        kern,
        grid_spec=grid_spec,
        out_shape=jax.ShapeDtypeStruct((nt * MOE_ROWS, half), jnp.uint32),
        compiler_params=_cparams(("arbitrary", "arbitrary")),
        name="experts",
    )(seg, nch, tail, d0, d1, u2p, w1, w3, w2)


def _combine_kernel(d0_ref, d1_ref, ys_ref, w0_ref, w1_ref, x1_ref, mod_ref, o_ref, g0_ref, g1_ref, *, tsub):
    base = pl.program_id(1) * tsub

    def move(t, carry):
        g0_ref[pl.ds(t, 1), :] = ys_ref[pl.ds(d0_ref[0, 0, base + t], 1), :]
        g1_ref[pl.ds(t, 1), :] = ys_ref[pl.ds(d1_ref[0, 0, base + t], 1), :]
        return carry

    lax.fori_loop(0, tsub, move, 0, unroll=8)
    lo0, hi0 = _unpack_bf16_pairs(g0_ref[...])
    lo1, hi1 = _unpack_bf16_pairs(g1_ref[...])
    half = g0_ref.shape[1]
    w0 = jnp.concatenate([w0_ref[...]] * (half // LANES), axis=1)
    w1 = jnp.concatenate([w1_ref[...]] * (half // LANES), axis=1)
    o_ref[:, :half] = x1_ref[:, :half] + mod_ref[0, 5:6, :half] * (w0 * lo0 + w1 * lo1)
    o_ref[:, half:] = x1_ref[:, half:] + mod_ref[0, 5:6, half:] * (w0 * hi0 + w1 * hi1)


def _combine(d0, d1, ys, w0rep, w1rep, x1, mod, *, seq):
    N, D = x1.shape
    half = D // 2
    tm = MOE_TILE
    tsub = 512
    nt = N // tm
    ns = tm // tsub
    sub = lambda i, j: (i * ns + j, 0)
    return pl.pallas_call(
        functools.partial(_combine_kernel, tsub=tsub),
        grid=(nt, ns),
        in_specs=[
            pl.BlockSpec((1, 1, tm), lambda i, j: (i, 0, 0), memory_space=pltpu.SMEM),
            pl.BlockSpec((1, 1, tm), lambda i, j: (i, 0, 0), memory_space=pltpu.SMEM),
            pl.BlockSpec((MOE_ROWS, half), lambda i, j: (i, 0)),
            pl.BlockSpec((tsub, LANES), sub),
            pl.BlockSpec((tsub, LANES), sub),
            pl.BlockSpec((tsub, D), sub),
            pl.BlockSpec((1, 6, D), lambda i, j: ((i * tm + j * tsub) // seq, 0, 0)),
        ],
        out_specs=pl.BlockSpec((tsub, D), sub),
        out_shape=jax.ShapeDtypeStruct((N, D), F32),
        scratch_shapes=[pltpu.VMEM((tsub, half), jnp.uint32), pltpu.VMEM((tsub, half), jnp.uint32)],
        compiler_params=_cparams(("arbitrary", "arbitrary")),
        name="combine",
    )(d0, d1, ys, w0rep, w1rep, x1, mod)


def _alibi_slopes(first, count):
    hh = jnp.arange(first + 1, first + count + 1, dtype=F32)
    return jnp.exp2(-8.0 * hh / N_HEADS_TOTAL)


def _bd_const():
    idx = np.arange(LANES) // HEAD_DIM
    return jnp.asarray((idx[:, None] == idx[None, :]).astype(np.float32), BF16)


def _blkcols_const(T):
    blk = np.arange(T) // SEL_LEN
    m = np.where(np.arange(LANES)[None, :] == blk[:, None], NEG_INF, 0.0).astype(np.float32)
    return jnp.asarray(m, BF16)


def _overlap_const(T):
    nc = T // CMP_STRIDE
    n_sel = T // SEL_LEN
    cs = np.arange(nc) * CMP_STRIDE
    ss = np.arange(n_sel) * SEL_LEN
    ov = np.clip(np.minimum(cs[:, None] + CMP_LEN, ss[None, :] + SEL_LEN)
                 - np.maximum(cs[:, None], ss[None, :]), 0, None) / CMP_LEN
    return jnp.asarray(ov.T.astype(np.float32))


def _expand_const():
    ex = np.zeros((N_NSA_BRANCHES, LANES, D_NSA), np.float32)
    for br in range(N_NSA_BRANCHES):
        for hd in range(N_HEADS_NSA):
            ex[br, hd * N_NSA_BRANCHES + br, hd * HEAD_DIM:(hd + 1) * HEAD_DIM] = 1.0
    return jnp.asarray(ex, BF16)


def _compact_w_in(w):
    parts = [w[:, a:b] for a, b in (_C_QN, _C_QS, _C_KSL, _C_KW, _C_KS, _C_VSL, _C_VW, _C_VS, _C_KC, _C_VC, _C_GATE)]
    pad = jnp.zeros((w.shape[0], LANES - (_C_GATE[1] - _C_GATE[0])), w.dtype)
    return jnp.concatenate(parts + [pad], axis=1).astype(BF16)


def _qconst(slopes_q):
    hi = slopes_q.astype(BF16).astype(F32)
    lo = (slopes_q - hi).astype(BF16).astype(F32)
    cols = jnp.stack([64.0 * hi, 64.0 * lo, hi, lo], axis=1)
    return jnp.zeros((slopes_q.shape[0], LANES), F32).at[:, HEAD_DIM:HEAD_DIM + 4].set(cols)


def kernel(x, c, w_router, ada_w, ada_b, norm1_g, norm2_g, w_in, w_out, nsa_q_gain, nsa_k_gain, cmp_pos_k,
           cmp_pos_v, cmp_w1_k, cmp_w2_k, cmp_w1_v, cmp_w2_v, swa_q_gain, swa_k_gain, swa_sinks, beta_nsa,
           beta_swa, moe_w1, moe_w3, moe_w2):
    B, T, D = x.shape
    L = ada_w.shape[0]
    N = B * T
    nc = T // CMP_STRIDE
    assert D == D_MODEL and T % 512 == 0 and T <= 64 * 128 and B <= 8

    slopes_nsa = _alibi_slopes(N_HEADS_SWA, N_HEADS_NSA)
    slopes_swa = _alibi_slopes(0, N_HEADS_SWA)
    qconst = _qconst(jnp.concatenate([slopes_nsa, slopes_swa]))
    bd = _bd_const()
    blkcols = _blkcols_const(T)
    ov = _overlap_const(T)
    expand = _expand_const()
    wr_t = w_router.T
    tri = jnp.asarray(np.triu(np.ones((TRI, TRI), np.float32), 1), BF16)
    assert N % MOE_TILE == 0
    zero_sinks = jnp.zeros((N_HEADS_NSA,), F32)

    c_pad = jnp.zeros((8, D), F32).at[:B].set(c)
    mod_all = _ada(c_pad, ada_w, ada_b)[:, :B].reshape(L, B, 6, D)

    x2 = x.reshape(N, D)
    for l in range(L):
        mod = mod_all[l]
        tile2 = lambda g: jnp.tile(g, 2)
        gain_c = jnp.concatenate([
            jnp.tile(nsa_q_gain[l], N_HEADS_NSA) * ATTN_SCALE,
            jnp.tile(swa_q_gain[l], N_HEADS_SWA) * ATTN_SCALE,
            tile2(nsa_k_gain[l, 1]), tile2(nsa_k_gain[l, 2]), tile2(swa_k_gain[l])]).reshape(1, -1)
        proj, xc = _inproj(x2, mod, norm1_g[l].reshape(1, D), _compact_w_in(w_in[l]), gain_c, qconst, bd, seq=T)

        eye = jnp.eye(N_KV_NSA, dtype=F32)
        w1s = jnp.stack([cmp_w1_k[l], cmp_w1_v[l]]).reshape(2, CMP_LEN, HEAD_DIM, CMP_HIDDEN)
        w1s = jnp.einsum('vldj,hk->vlhdkj', w1s, eye).reshape(
            2, CMP_LEN * N_KV_NSA * HEAD_DIM, N_KV_NSA * CMP_HIDDEN).astype(BF16)
        w2s = jnp.pad(jnp.stack([cmp_w2_k[l], cmp_w2_v[l]]), ((0, 0), (0, 0), (0, LANES - HEAD_DIM)))
        w2s = jnp.einsum('vjc,hk->vhjkc', w2s, eye).reshape(
            2, N_KV_NSA * CMP_HIDDEN, N_KV_NSA * LANES).astype(BF16)
        pos = jnp.stack([cmp_pos_k[l], cmp_pos_v[l]])
        pos = jnp.broadcast_to(pos[:, None, :, None, :], (2, 8, CMP_LEN, N_KV_NSA, HEAD_DIM))
        pos = pos.reshape(2, 8, CMP_LEN * N_KV_NSA * HEAD_DIM).astype(BF16)
        kgain = jnp.pad(nsa_k_gain[l, 0], (0, LANES - HEAD_DIM)).reshape(1, LANES)
        kvc = _cmpmlp(xc, w1s, w2s, pos, kgain, B=B)

        o_cmp, notsel = _cmpattn(proj, kvc, ov, B=B, T=T)
        o_slc = _selattn(proj, notsel, blkcols, B=B, T=T)
        o_win = _band(slopes_nsa, zero_sinks, proj, B=B, T=T, window=NSA_WINDOW, q_slot=SLOT_QN,
                      k_slot=SLOT_KW, v_slot=SLOT_VW, use_sinks=False, name="winattn")
        o_swa = _band(slopes_swa, swa_sinks[l], proj, B=B, T=T, window=SWA_WINDOW, q_slot=SLOT_QS,
                      k_slot=SLOT_KS, v_slot=SLOT_VS, use_sinks=True, name="swaattn")

        x1, u2, lt = _outproj(o_cmp, o_slc, o_win, o_swa, proj, x2, mod, expand,
                              beta_nsa[l].reshape(1, -1), beta_swa[l].reshape(1, -1), w_out[l].astype(BF16),
                              norm2_g[l].reshape(1, D), wr_t, seq=T)
        d0, d1, w0rep, w1rep, seg, nch, tail = _route(lt, tri)
        ys = _experts(seg[:, :, 0].reshape(-1), nch[:, :, 0].reshape(-1), tail[:, :, 0].reshape(-1), d0, d1, u2,
                      moe_w1[l].astype(BF16), moe_w3[l].astype(BF16), moe_w2[l].astype(BF16))
        x2 = _combine(d0, d1, ys, w0rep, w1rep, x1, mod, seq=T)
    return x2.reshape(B, T, D)
```

```python
import functools

import numpy as np
import jax
import jax.numpy as jnp
from jax import lax
from jax.experimental import pallas as pl
from jax.experimental.pallas import tpu as pltpu

D_MODEL = 1024
DEPTH = 2
HEAD_DIM = 64
N_HEADS_NSA = 8
N_KV_NSA = 2
G_NSA = N_HEADS_NSA // N_KV_NSA
N_HEADS_SWA = 8
N_KV_SWA = 2
G_SWA = N_HEADS_SWA // N_KV_SWA
D_NSA = N_HEADS_NSA * HEAD_DIM
D_SWA = N_HEADS_SWA * HEAD_DIM
D_MIX = D_NSA + D_SWA
N_HEADS_TOTAL = N_HEADS_NSA + N_HEADS_SWA
KV_NSA = N_KV_NSA * HEAD_DIM
KV_SWA = N_KV_SWA * HEAD_DIM
N_NSA_BRANCHES = 3
CMP_LEN = 32
CMP_STRIDE = 16
CMP_HIDDEN = 256
SEL_LEN = 64
SEL_TOPK = 16
NSA_WINDOW = 512
SWA_WINDOW = 128
N_EXPERTS = 16
N_GROUPS = 4
EXPERTS_PER_GROUP = N_EXPERTS // N_GROUPS
TOP_K = 2
D_EXPERT = 512
EPS = 1e-6
NEG_INF = -1e30
ATTN_SCALE = HEAD_DIM ** -0.5

F32 = jnp.float32
BF16 = jnp.bfloat16

LANES = 128
VMEM_LIMIT = 48 * 1024 * 1024

MOE_TILE = 2048
MOE_CHUNK = 256
SEG_ALIGN = 16
TRI = 256
MOE_ROWS = TOP_K * MOE_TILE + N_EXPERTS * SEG_ALIGN + MOE_CHUNK

SLOT_QN = 0
SLOT_QS = 8
SLOT_KSL = 16
SLOT_KW = 18
SLOT_KS = 20
SLOT_VSL = 22
SLOT_VW = 24
SLOT_VS = 26
SLOT_GATE = 28
N_SLOTS = 29
N_CHUNKS = 17
N_NORM_CHUNKS = 11

_C_QN = (0, 512)
_C_KC = (512, 640)
_C_VC = (640, 768)
_C_KSL = (768, 896)
_C_VSL = (896, 1024)
_C_KW = (1024, 1152)
_C_VW = (1152, 1280)
_C_GATE = (1280, 1304)
_C_QS = (1304, 1816)
_C_KS = (1816, 1944)
_C_VS = (1944, 2072)


def _cparams(sem):
    return pltpu.CompilerParams(dimension_semantics=sem, vmem_limit_bytes=VMEM_LIMIT)


def _dot(a, b, **kw):
    return jnp.dot(a, b, preferred_element_type=F32, **kw)


def _dot_nt(a, b):
    return lax.dot_general(a, b, (((1,), (1,)), ((), ())), preferred_element_type=F32)


def _sigmoid(x):
    return 1.0 / (1.0 + jnp.exp(-x))


def _pack_bf16_pairs(x):
    n = x.shape[1] // 2
    return pltpu.pack_elementwise([x[:, :n], x[:, n:]], packed_dtype=BF16)


def _unpack_bf16_pairs(w):
    lo = pltpu.unpack_elementwise(w, index=0, packed_dtype=BF16, unpacked_dtype=F32)
    hi = pltpu.unpack_elementwise(w, index=1, packed_dtype=BF16, unpacked_dtype=F32)
    return lo, hi


def _ada_kernel(c_ref, w_ref, b_ref, o_ref):
    c = c_ref[...]
    cond = c * _sigmoid(c)
    o_ref[0] = _dot(cond.astype(BF16), w_ref[0].astype(BF16)) + b_ref[0]


def _ada(c_pad, ada_w, ada_b):
    L, D, N6 = ada_w.shape
    tn = 1536
    return pl.pallas_call(
        _ada_kernel,
        grid=(L, N6 // tn),
        in_specs=[
            pl.BlockSpec((8, D), lambda l, j: (0, 0)),
            pl.BlockSpec((1, D, tn), lambda l, j: (l, 0, j)),
            pl.BlockSpec((1, 1, tn), lambda l, j: (l, 0, j)),
        ],
        out_specs=pl.BlockSpec((1, 8, tn), lambda l, j: (l, 0, j)),
        out_shape=jax.ShapeDtypeStruct((L, 8, N6), F32),
        compiler_params=_cparams(("arbitrary", "arbitrary")),
        name="ada",
    )(c_pad, ada_w, ada_b.reshape(L, 1, N6))


def _inproj_kernel(x_ref, mod_ref, g_ref, w_ref, gain_ref, qc_ref, bd_ref, o_ref, xc_ref, scr_ref, *, tm, seq):
    i = pl.program_id(0)
    x = x_ref[...]
    ms = jnp.mean(x * x, axis=-1, keepdims=True)
    u = x * lax.rsqrt(ms + EPS) * g_ref[...]
    u = u * (1.0 + mod_ref[0, 1:2, :]) + mod_ref[0, 0:1, :]
    y = _dot(u.astype(BF16), w_ref[0])

    lane = lax.broadcasted_iota(jnp.int32, (tm, LANES), 1)
    low = lane < HEAD_DIM
    t = lax.broadcasted_iota(jnp.int32, (tm, LANES), 0) + lax.rem(i * tm, seq)
    hi_part = (t >> 6).astype(F32)
    lo_part = (t & 63).astype(F32)
    poscols = jnp.where((lane == 64) | (lane == 65), hi_part,
                        jnp.where((lane == 66) | (lane == 67), lo_part, 0.0))
    bd = bd_ref[...]

    def put(slot, val):
        o_ref[:, slot * LANES:(slot + 1) * LANES] = val.astype(o_ref.dtype)

    for c in range(N_CHUNKS):
        blk = y[:, c * LANES:(c + 1) * LANES]
        if c < N_NORM_CHUNKS:
            msb = _dot((blk * blk).astype(BF16), bd) * (1.0 / HEAD_DIM)
            blk = blk * lax.rsqrt(msb + EPS) * gain_ref[:, c * LANES:(c + 1) * LANES]
        if c < 14:
            rolled = pltpu.roll(blk, HEAD_DIM, axis=1)
            if c < 8:
                f0 = qc_ref[2 * c:2 * c + 1, :]
                f1 = qc_ref[2 * c + 1:2 * c + 2, :]
                s0 = 2 * c
            elif c < N_NORM_CHUNKS:
                f0 = f1 = poscols
                s0 = SLOT_KSL + 2 * (c - 8)
            else:
                f0 = f1 = 1.0
                s0 = SLOT_VSL + 2 * (c - N_NORM_CHUNKS)
            put(s0, jnp.where(low, blk, f0))
            put(s0 + 1, jnp.where(low, rolled, f1))
        elif c == 16:
            put(SLOT_GATE, blk)
        else:
            scr_ref[...] = blk
            nck = tm // CMP_STRIDE
            low_c = lax.broadcasted_iota(jnp.int32, (nck, LANES), 1) < HEAD_DIM
            head_cols = CMP_STRIDE * HEAD_DIM
            for pair in range(CMP_STRIDE // 2):
                ta = scr_ref[pl.ds(2 * pair, nck, stride=CMP_STRIDE), :]
                tb = scr_ref[pl.ds(2 * pair + 1, nck, stride=CMP_STRIDE), :]
                h0 = jnp.where(low_c, ta, pltpu.roll(tb, HEAD_DIM, axis=1))
                h1 = jnp.where(low_c, pltpu.roll(ta, HEAD_DIM, axis=1), tb)
                xc_ref[c - 14, :, pair * LANES:(pair + 1) * LANES] = h0.astype(xc_ref.dtype)
                xc_ref[c - 14, :, head_cols + pair * LANES:head_cols + (pair + 1) * LANES] = h1.astype(xc_ref.dtype)


def _inproj(x2, mod, norm_g, w_c, gain_c, qconst, bd, *, seq, layer):
    N, D = x2.shape
    tm = 512
    tiles_per_seq = seq // tm
    kern = functools.partial(_inproj_kernel, tm=tm, seq=seq)
    nchunk = tm // CMP_STRIDE
    xc_shape = jax.ShapeDtypeStruct((2, N // CMP_STRIDE, CMP_STRIDE * LANES), BF16)
    return pl.pallas_call(
        kern,
        grid=(N // tm,),
        in_specs=[
            pl.BlockSpec((tm, D), lambda i: (i, 0)),
            pl.BlockSpec((1, 6, D), lambda i: (i // tiles_per_seq, 0, 0)),
            pl.BlockSpec((1, D), lambda i: (0, 0)),
            pl.BlockSpec((1, D, N_CHUNKS * LANES), lambda i: (layer, 0, 0)),
            pl.BlockSpec((1, N_NORM_CHUNKS * LANES), lambda i: (0, 0)),
            pl.BlockSpec((16, LANES), lambda i: (0, 0)),
            pl.BlockSpec((LANES, LANES), lambda i: (0, 0)),
        ],
        out_specs=[
            pl.BlockSpec((tm, N_SLOTS * LANES), lambda i: (i, 0)),
            pl.BlockSpec((2, nchunk, CMP_STRIDE * LANES), lambda i: (0, i, 0)),
        ],
        out_shape=[jax.ShapeDtypeStruct((N, N_SLOTS * LANES), BF16), xc_shape],
        scratch_shapes=[pltpu.VMEM((tm, LANES), F32)],
        compiler_params=_cparams(("arbitrary",)),
        name="inproj",
    )(x2, mod, norm_g, w_c, gain_c, qconst, bd)


def _gelu_tanh(x):
    return 0.5 * x * (1.0 + jnp.tanh(np.sqrt(2.0 / np.pi).astype(np.float32) * (x + 0.044715 * (x * x * x))))


def _cmpmlp_kernel(x_ref, w1_ref, w2_ref, pos_ref, gain_ref, o_ref, *, nc):
    kv = pl.program_id(0)
    half = CMP_STRIDE * HEAD_DIM
    bias = _dot(pos_ref[0], w1_ref[0])[0:1]
    lane = lax.broadcasted_iota(jnp.int32, (nc, LANES), 1)
    last = lax.broadcasted_iota(jnp.int32, (nc, LANES), 0) * CMP_STRIDE + (CMP_LEN - 1)
    poscols = jnp.where((lane == 64) | (lane == 65), (last >> 6).astype(F32),
                        jnp.where((lane == 66) | (lane == 67), (last & 63).astype(F32), 0.0))
    for hd in range(N_KV_NSA):
        xc = x_ref[0, :, hd * half:(hd + 1) * half]
        first = _dot(xc, w1_ref[0, :half, :])
        second = _dot(xc, w1_ref[0, half:, :])
        h = _gelu_tanh(first + pltpu.roll(second, nc - 1, axis=0) + bias)
        z = _dot(h.astype(BF16), w2_ref[0])
        zn = z * lax.rsqrt(jnp.sum(z * z, axis=-1, keepdims=True) * (1.0 / HEAD_DIM) + EPS) * gain_ref[...]
        o_ref[0, 0, hd] = jnp.where(kv == 0, zn + poscols, z + jnp.where(lane >= HEAD_DIM, 1.0, 0.0))


def _cmpmlp(xc, w1s, w2s, pos, gain, *, B, layer):
    _, nchunks, K = xc.shape
    nc = nchunks // B
    win = CMP_LEN * HEAD_DIM
    kern = functools.partial(_cmpmlp_kernel, nc=nc)
    return pl.pallas_call(
        kern,
        grid=(2, B),
        in_specs=[
            pl.BlockSpec((1, nc, K), lambda kv, b: (kv, b, 0)),
            pl.BlockSpec((1, win, CMP_HIDDEN), lambda kv, b: (2 * layer + kv, 0, 0)),
            pl.BlockSpec((1, CMP_HIDDEN, LANES), lambda kv, b: (2 * layer + kv, 0, 0)),
            pl.BlockSpec((1, 8, win), lambda kv, b: (2 * layer + kv, 0, 0)),
            pl.BlockSpec((1, LANES), lambda kv, b: (0, 0)),
        ],
        out_specs=pl.BlockSpec((1, 1, N_KV_NSA, nc, LANES), lambda kv, b: (kv, b, 0, 0, 0)),
        out_shape=jax.ShapeDtypeStruct((2, B, N_KV_NSA, nc, LANES), F32),
        compiler_params=_cparams(("arbitrary", "arbitrary")),
        name="cmpmlp",
    )(xc, w1s, w2s, pos, gain)


def _normalize_pack(pvs, extra=None):
    lane = lax.broadcasted_iota(jnp.int32, pvs[0].shape, 1)
    low = lane < HEAD_DIM
    pairs = []
    for p in range(len(pvs) // 2):
        even, odd = pvs[2 * p], pvs[2 * p + 1]
        den_e = even if extra is None else even + extra[2 * p]
        den_o = odd if extra is None else odd + extra[2 * p + 1]
        o_e = even * (1.0 / pltpu.roll(den_e, HEAD_DIM, axis=1))
        o_o = pltpu.roll(odd, HEAD_DIM, axis=1) * (1.0 / den_o)
        pairs.append(jnp.where(low, o_e, o_o))
    return jnp.concatenate(pairs, axis=-1)


def _cmpattn_kernel(q_ref, kc_ref, vc_ref, ov_ref, o_ref, ns_ref, *, tq, nc, n_cmp, n_sel, top):
    q0 = pl.program_id(2) * tq
    kc = kc_ref[0, 0, 0].astype(BF16)
    vc = vc_ref[0, 0, 0].astype(BF16)

    t_col = q0 + lax.broadcasted_iota(jnp.int32, (tq, 1), 0)
    row_valid = (t_col >= CMP_LEN - 1).astype(F32)
    t_row = q0 + lax.broadcasted_iota(jnp.int32, (1, tq), 1)
    n_col = lax.broadcasted_iota(jnp.int32, (nc, 1), 0)
    valid_t = (t_row - (n_col * CMP_STRIDE + (CMP_LEN - 1)) >= 0) & (n_col < n_cmp)
    bias_t = jnp.where(valid_t, 0.0, NEG_INF)
    row_valid_t = (t_row >= CMP_LEN - 1).astype(F32)

    outs = []
    psum_t = jnp.zeros((nc, tq), F32)
    for g in range(G_NSA):
        qg = q_ref[:, g * LANES:(g + 1) * LANES]
        st = _dot_nt(kc, qg) + bias_t
        et = jnp.exp(st - jnp.max(st, axis=0, keepdims=True))
        outs.append(lax.dot_general(et.astype(BF16), vc, (((0,), (0,)), ((), ())), preferred_element_type=F32))
        psum_t = psum_t + et * (1.0 / jnp.sum(et, axis=0, keepdims=True))
    o_ref[...] = (_normalize_pack(outs) * row_valid).astype(o_ref.dtype)

    psum_t = psum_t * row_valid_t
    imp_t = _dot(ov_ref[...], psum_t, precision=lax.Precision.HIGHEST)
    j = lax.broadcasted_iota(jnp.int32, (n_sel, 1), 0)
    cur = t_row >> 6
    forced = (j == 0) | (j == cur) | (j == cur - 1)
    v = jnp.where(forced, 1e9, jnp.where(j > cur, NEG_INF, imp_t))
    n_grp = n_sel // 8
    vg = [v[8 * r:8 * r + 8, :] for r in range(n_grp)]
    jg = lax.broadcasted_iota(jnp.int32, (8, 1), 0)
    cnt = [jnp.zeros((8, tq), F32) for _ in range(n_grp)]
    for jp in range(n_sel):
        vj = v[jp:jp + 1, :]
        for r in range(n_grp):
            ge = jnp.where(vj >= vg[r], 1.0, 0.0)
            gt = jnp.where(vj > vg[r], 1.0, 0.0)
            if 8 * r > jp:
                inc = ge
            elif 8 * r + 7 < jp:
                inc = gt
            else:
                inc = jnp.where(jg + 8 * r > jp, ge, gt)
            cnt[r] = cnt[r] + inc
    notsel_t = jnp.concatenate([jnp.where(cn < top, 0.0, 1.0) for cn in cnt], axis=0)
    if n_sel < LANES:
        notsel_t = jnp.concatenate([notsel_t, jnp.zeros((LANES - n_sel, tq), F32)], axis=0)
    ns_ref[0, 0] = notsel_t.T.astype(ns_ref.dtype)


def _cmpattn(proj, kvc, ov, *, B, T):
    tq = 256
    nt = T // tq
    nc = T // CMP_STRIDE
    n_cmp = (T - CMP_LEN) // CMP_STRIDE + 1
    n_sel = T // SEL_LEN
    top = min(SEL_TOPK, n_sel)
    assert n_sel <= LANES and n_sel % 8 == 0
    kern = functools.partial(_cmpattn_kernel, tq=tq, nc=nc, n_cmp=n_cmp, n_sel=n_sel, top=top)
    return pl.pallas_call(
        kern,
        grid=(B, N_KV_NSA, nt),
        in_specs=[
            pl.BlockSpec((tq, G_NSA * LANES), lambda b, h, i: (b * nt + i, h)),
            pl.BlockSpec((1, 1, 1, nc, LANES), lambda b, h, i: (0, b, h, 0, 0)),
            pl.BlockSpec((1, 1, 1, nc, LANES), lambda b, h, i: (1, b, h, 0, 0)),
            pl.BlockSpec((n_sel, nc), lambda b, h, i: (0, 0)),
        ],
        out_specs=[
            pl.BlockSpec((tq, G_NSA * HEAD_DIM), lambda b, h, i: (b * nt + i, h)),
            pl.BlockSpec((1, 1, tq, LANES), lambda b, h, i: (b, h, i, 0)),
        ],
        out_shape=[
            jax.ShapeDtypeStruct((B * T, D_NSA), BF16),
            jax.ShapeDtypeStruct((B, N_KV_NSA, T, LANES), BF16),
        ],
        compiler_params=_cparams(("arbitrary", "arbitrary", "arbitrary")),
        name="cmpattn",
    )(proj, kvc, kvc, ov)


def _selattn_kernel(q_ref, ns_ref, k_ref, v_ref, bc_ref, o_ref, *, tq, tk, n_chain):
    q0 = pl.program_id(2) * tq
    ns = ns_ref[0, 0]
    tc = tq // n_chain
    qx = [jnp.concatenate(
        [jnp.concatenate([q_ref[c * tc:(c + 1) * tc, g * LANES:(g + 1) * LANES], ns[c * tc:(c + 1) * tc]], axis=-1)
         for g in range(G_NSA)], axis=0) for c in range(n_chain)]
    m_rows = G_NSA * tc
    diag = q0 // tk

    def step(kt, carry, masked):
        start = pl.multiple_of(kt * tk, tk)
        kx = jnp.concatenate([k_ref[pl.ds(start, tk), :], bc_ref[pl.ds(start, tk), :]], axis=-1)
        vx = v_ref[pl.ds(start, tk), :]
        out = []
        for c in range(n_chain):
            m, acc = carry[c]
            s = _dot_nt(qx[c], kx)
            if masked:
                r = lax.broadcasted_iota(jnp.int32, (tc, tk), 0) + c * tc
                col = lax.broadcasted_iota(jnp.int32, (tc, tk), 1)
                s = s + jnp.concatenate([jnp.where(col <= r, 0.0, NEG_INF)] * G_NSA, axis=0)
            m_new = jnp.maximum(m, jnp.max(s, axis=-1, keepdims=True))
            p = jnp.exp(s - m_new).astype(BF16)
            out.append((m_new, jnp.exp(m - m_new) * acc + _dot(p, vx)))
        return tuple(out)

    init = tuple((jnp.full((m_rows, 1), NEG_INF, F32), jnp.zeros((m_rows, LANES), F32)) for _ in range(n_chain))
    carry = lax.fori_loop(0, diag, functools.partial(step, masked=False), init)
    carry = step(diag, carry, True)
    for c in range(n_chain):
        acc = carry[c][1]
        o_ref[c * tc:(c + 1) * tc, :] = _normalize_pack(
            [acc[g * tc:(g + 1) * tc] for g in range(G_NSA)]).astype(o_ref.dtype)


def _selattn(proj, notsel, blkcols, *, B, T):
    tq = 512
    tk = 512
    nt = T // tq
    assert T % tk == 0 and tk == tq
    kern = functools.partial(_selattn_kernel, tq=tq, tk=tk, n_chain=1)
    return pl.pallas_call(
        kern,
        grid=(B, N_KV_NSA, nt),
        in_specs=[
            pl.BlockSpec((tq, G_NSA * LANES), lambda b, h, i: (b * nt + i, h)),
            pl.BlockSpec((1, 1, tq, LANES), lambda b, h, i: (b, h, i, 0)),
            pl.BlockSpec((T, LANES), lambda b, h, i: (b, SLOT_KSL + h)),
            pl.BlockSpec((T, LANES), lambda b, h, i: (b, SLOT_VSL + h)),
            pl.BlockSpec((T, LANES), lambda b, h, i: (0, 0)),
        ],
        out_specs=pl.BlockSpec((tq, G_NSA * HEAD_DIM), lambda b, h, i: (b * nt + i, h)),
        out_shape=jax.ShapeDtypeStruct((B * T, D_NSA), BF16),
        compiler_params=_cparams(("arbitrary", "arbitrary", "arbitrary")),
        name="selattn",
    )(proj, notsel, proj, proj, blkcols)


def _band_kernel(sl_ref, sink_ref, q_ref, k_ref, v_ref, o_ref, *, tq, n_sub, window, seq, use_sinks):
    h = pl.program_id(1)
    span = window + tq
    groups = q_ref.shape[1] // LANES
    r = lax.broadcasted_iota(jnp.int32, (tq, span), 0)
    col = lax.broadcasted_iota(jnp.int32, (tq, span), 1)
    for sub in range(n_sub):
        q0 = (pl.program_id(2) * n_sub + sub) * tq
        start = pl.multiple_of(jnp.clip(q0 - window, 0, seq - span), tq)
        rows = pl.ds(sub * tq, tq)
        qx = jnp.concatenate([q_ref[rows, g * LANES:(g + 1) * LANES] for g in range(groups)], axis=0)
        d = (q0 - start) + r - col
        bias = jnp.where((d >= 0) & (d < window), 0.0, NEG_INF)
        s = _dot_nt(qx, k_ref[pl.ds(start, span), :]) + jnp.concatenate([bias] * groups, axis=0)
        m = jnp.broadcast_to(jnp.max(s, axis=-1, keepdims=True), (groups * tq, LANES))
        if use_sinks:
            t_rep = (q0 + lax.broadcasted_iota(jnp.int32, (tq, LANES), 0)).astype(F32)
            sink = jnp.concatenate(
                [sink_ref[h * groups + g] + sl_ref[h * groups + g] * t_rep for g in range(groups)], axis=0)
            m = jnp.maximum(m, sink)
        e = jnp.exp(s - jnp.concatenate([m] * (span // LANES), axis=1)).astype(BF16)
        pv = _dot(e, v_ref[pl.ds(start, span), :])
        extra = None
        if use_sinks:
            esink = jnp.exp(sink - m)
            extra = [esink[g * tq:(g + 1) * tq] for g in range(groups)]
        o_ref[rows, :] = _normalize_pack([pv[g * tq:(g + 1) * tq] for g in range(groups)], extra).astype(o_ref.dtype)


def _band(slopes, sinks, proj, *, B, T, window, q_slot, k_slot, v_slot, use_sinks, name):
    tq = 128
    n_sub = 4
    tb = tq * n_sub
    nt = T // tb
    groups = 4
    assert window % tq == 0 and window + tq <= T and T % tb == 0
    qb = q_slot // groups
    kern = functools.partial(_band_kernel, tq=tq, n_sub=n_sub, window=window, seq=T, use_sinks=use_sinks)
    return pl.pallas_call(
        kern,
        grid=(B, 2, nt),
        in_specs=[
            pl.BlockSpec(memory_space=pltpu.SMEM),
            pl.BlockSpec(memory_space=pltpu.SMEM),
            pl.BlockSpec((tb, groups * LANES), lambda b, h, i: (b * nt + i, qb + h)),
            pl.BlockSpec((T, LANES), lambda b, h, i: (b, k_slot + h)),
            pl.BlockSpec((T, LANES), lambda b, h, i: (b, v_slot + h)),
        ],
        out_specs=pl.BlockSpec((tb, groups * HEAD_DIM), lambda b, h, i: (b * nt + i, h)),
        out_shape=jax.ShapeDtypeStruct((B * T, 2 * groups * HEAD_DIM), BF16),
        compiler_params=_cparams(("arbitrary", "arbitrary", "arbitrary")),
        name=name,
    )(slopes, sinks, proj, proj, proj)


def _outproj_kernel(ocmp_ref, oslc_ref, owin_ref, oswa_ref, gate_ref, x_ref, mod_ref, ex_ref, bn_ref, bs_ref,
                    w_ref, g2_ref, wr_ref, x1_ref, u2_ref, lt_ref):
    sg = _sigmoid(gate_ref[...].astype(F32)).astype(BF16)
    o_nsa = (_dot(sg, ex_ref[0]) * ocmp_ref[...].astype(F32)
             + _dot(sg, ex_ref[1]) * oslc_ref[...].astype(F32)
             + _dot(sg, ex_ref[2]) * owin_ref[...].astype(F32))
    o_swa = oswa_ref[...].astype(F32)
    n1 = o_nsa * lax.rsqrt(jnp.mean(o_nsa * o_nsa, axis=-1, keepdims=True) + EPS) * bn_ref[...]
    n2 = o_swa * lax.rsqrt(jnp.mean(o_swa * o_swa, axis=-1, keepdims=True) + EPS) * bs_ref[...]
    merged = jnp.concatenate([n1, n2], axis=-1).astype(BF16)
    y = _dot(merged, w_ref[0])
    x1 = x_ref[...] + mod_ref[0, 2:3, :] * y
    x1_ref[...] = x1
    u2 = x1 * lax.rsqrt(jnp.mean(x1 * x1, axis=-1, keepdims=True) + EPS) * g2_ref[...]
    u2 = u2 * (1.0 + mod_ref[0, 4:5, :]) + mod_ref[0, 3:4, :]
    u2_ref[...] = _pack_bf16_pairs(u2)
    lt_ref[...] = lax.dot_general(wr_ref[...], u2, (((1,), (1,)), ((), ())), preferred_element_type=F32,
                                  precision=lax.Precision.HIGHEST)


def _outproj(ocmp, oslc, owin, oswa, proj, x2, mod, expand, beta_n, beta_s, w_out, norm2_g, wr_t, *, seq, layer):
    N, D = x2.shape
    tm = 512
    tiles_per_seq = seq // tm
    row = lambda i: (i, 0)
    const2 = lambda i: (0, 0)
    return pl.pallas_call(
        _outproj_kernel,
        grid=(N // tm,),
        in_specs=[
            pl.BlockSpec((tm, D_NSA), row),
            pl.BlockSpec((tm, D_NSA), row),
            pl.BlockSpec((tm, D_NSA), row),
            pl.BlockSpec((tm, D_SWA), row),
            pl.BlockSpec((tm, LANES), lambda i: (i, SLOT_GATE)),
            pl.BlockSpec((tm, D), row),
            pl.BlockSpec((1, 6, D), lambda i: (i // tiles_per_seq, 0, 0)),
            pl.BlockSpec((N_NSA_BRANCHES, LANES, D_NSA), lambda i: (0, 0, 0)),
            pl.BlockSpec((1, D_NSA), const2),
            pl.BlockSpec((1, D_SWA), const2),
            pl.BlockSpec((1, D_MIX, D), lambda i: (layer, 0, 0)),
            pl.BlockSpec((1, D), const2),
            pl.BlockSpec((N_EXPERTS, D), const2),
        ],
        out_specs=[
            pl.BlockSpec((tm, D), row),
            pl.BlockSpec((tm, D // 2), row),
            pl.BlockSpec((N_EXPERTS, tm), lambda i: (0, i)),
        ],
        out_shape=[
            jax.ShapeDtypeStruct((N, D), F32),
            jax.ShapeDtypeStruct((N, D // 2), jnp.uint32),
            jax.ShapeDtypeStruct((N_EXPERTS, N), F32),
        ],
        compiler_params=_cparams(("arbitrary",)),
        name="outproj",
    )(ocmp, oslc, owin, oswa, proj, x2, mod, expand, beta_n, beta_s, w_out, norm2_g, wr_t)


def _route_kernel(lt_ref, tri_ref, d0_ref, d1_ref, w0_ref, w1_ref, seg_ref, nch_ref, tail_ref, *, tm):
    lt = lt_ref[...]
    e = jnp.exp(lt - jnp.max(lt, axis=0, keepdims=True))
    aff = e / jnp.sum(e, axis=0, keepdims=True)
    rows = [aff[k:k + 1, :] for k in range(N_EXPERTS)]
    epg = EXPERTS_PER_GROUP
    scores = []
    for gr in range(N_GROUPS):
        xs = rows[gr * epg:(gr + 1) * epg]
        top1 = functools.reduce(jnp.maximum, xs)
        second = None
        for a in range(epg):
            for b in range(a + 1, epg):
                mn = jnp.minimum(xs[a], xs[b])
                second = mn if second is None else jnp.maximum(second, mn)
        scores.append(top1 + second)
    taken = None
    in_group = []
    for gr in range(N_GROUPS):
        best = None
        for o in range(gr + 1, N_GROUPS):
            c = scores[gr] >= scores[o]
            best = c if best is None else (best & c)
        if best is None:
            best = jnp.full(scores[gr].shape, True)
        sel = best if taken is None else (best & ~taken)
        taken = sel if taken is None else (taken | sel)
        in_group.append(sel)
    ys = []
    for k in range(epg):
        yk = rows[k]
        for gr in range(1, N_GROUPS):
            yk = jnp.where(in_group[gr], rows[gr * epg + k], yk)
        ys.append(yk)
    chosen = []
    for k in range(epg):
        rk = jnp.zeros_like(ys[k])
        for o in range(epg):
            if o == k:
                continue
            before = (ys[o] >= ys[k]) if o < k else (ys[o] > ys[k])
            rk = rk + jnp.where(before, 1.0, 0.0)
        chosen.append(rk < TOP_K)
    total = functools.reduce(lambda a, b: a + b, [jnp.where(chosen[k], ys[k], 0.0) for k in range(epg)])
    first, seen = [], None
    for k in range(epg):
        first.append(chosen[k] if seen is None else (chosen[k] & ~seen))
        seen = chosen[k] if seen is None else (seen | chosen[k])
    second = [chosen[k] & ~first[k] for k in range(epg)]
    inv_total = 1.0 / total
    w_first = functools.reduce(lambda a, b: a + b, [jnp.where(first[k], ys[k], 0.0) for k in range(epg)]) * inv_total
    w_second = functools.reduce(lambda a, b: a + b, [jnp.where(second[k], ys[k], 0.0) for k in range(epg)]) * inv_total
    f_rows, s_rows = [], []
    for ex in range(N_EXPERTS):
        gr, k = divmod(ex, epg)
        f_rows.append(jnp.where(in_group[gr] & first[k], 1.0, 0.0))
        s_rows.append(jnp.where(in_group[gr] & second[k], 1.0, 0.0))
    fmat = jnp.concatenate(f_rows, axis=0)
    smat = jnp.concatenate(s_rows, axis=0)
    cmat = fmat + smat
    carry = jnp.zeros((N_EXPERTS, 1), F32)
    ranks = []
    for blk in range(tm // TRI):
        cb = cmat[:, blk * TRI:(blk + 1) * TRI]
        ranks.append(_dot(cb.astype(BF16), tri_ref[...]) + carry)
        carry = carry + jnp.sum(cb, axis=1, keepdims=True)
    rank = jnp.concatenate(ranks, axis=1)
    padded = jnp.floor((carry + (SEG_ALIGN - 1)) * (1.0 / SEG_ALIGN)) * SEG_ALIGN
    seg_rows, run = [], jnp.zeros((1, 1), F32)
    for ex in range(N_EXPERTS):
        seg_rows.append(run)
        run = run + padded[ex:ex + 1]
    seg = jnp.concatenate(seg_rows, axis=0)
    dest = rank + seg
    d0_ref[0] = jnp.sum(fmat * dest, axis=0, keepdims=True).astype(jnp.int32)
    d1_ref[0] = jnp.sum(smat * dest, axis=0, keepdims=True).astype(jnp.int32)
    w0_ref[...] = jnp.broadcast_to(w_first, (LANES, tm)).T
    w1_ref[...] = jnp.broadcast_to(w_second, (LANES, tm)).T
    seg_ref[0] = jnp.broadcast_to(seg, (N_EXPERTS, LANES)).astype(jnp.int32)
    whole = jnp.floor(carry * (1.0 / MOE_CHUNK))
    rem = carry - whole * MOE_CHUNK
    n_full = whole + jnp.where(rem > MOE_CHUNK // 2, 1.0, 0.0)
    half_tail = jnp.where((rem > 0) & (rem <= MOE_CHUNK // 2), 1.0, 0.0)
    nch_ref[0] = jnp.broadcast_to(n_full, (N_EXPERTS, LANES)).astype(jnp.int32)
    tail_ref[0] = jnp.broadcast_to(half_tail, (N_EXPERTS, LANES)).astype(jnp.int32)


def _route(lt, tri):
    E, N = lt.shape
    tm = MOE_TILE
    nt = N // tm
    return pl.pallas_call(
        functools.partial(_route_kernel, tm=tm),
        grid=(nt,),
        in_specs=[pl.BlockSpec((E, tm), lambda i: (0, i)), pl.BlockSpec((TRI, TRI), lambda i: (0, 0))],
        out_specs=[
            pl.BlockSpec((1, 1, tm), lambda i: (i, 0, 0)),
            pl.BlockSpec((1, 1, tm), lambda i: (i, 0, 0)),
            pl.BlockSpec((tm, LANES), lambda i: (i, 0)),
            pl.BlockSpec((tm, LANES), lambda i: (i, 0)),
            pl.BlockSpec((1, E, LANES), lambda i: (i, 0, 0)),
            pl.BlockSpec((1, E, LANES), lambda i: (i, 0, 0)),
            pl.BlockSpec((1, E, LANES), lambda i: (i, 0, 0)),
        ],
        out_shape=[
            jax.ShapeDtypeStruct((nt, 1, tm), jnp.int32),
            jax.ShapeDtypeStruct((nt, 1, tm), jnp.int32),
            jax.ShapeDtypeStruct((N, LANES), F32),
            jax.ShapeDtypeStruct((N, LANES), F32),
            jax.ShapeDtypeStruct((nt, E, LANES), jnp.int32),
            jax.ShapeDtypeStruct((nt, E, LANES), jnp.int32),
            jax.ShapeDtypeStruct((nt, E, LANES), jnp.int32),
        ],
        compiler_params=_cparams(("arbitrary",)),
        name="route",
    )(lt, tri)


def _experts_kernel(seg_ref, nch_ref, tail_ref, d0_ref, d1_ref, u_ref, w1_ref, w3_ref, w2_ref, o_ref, xs_ref, *, tm,
                    chunk):
    i = pl.program_id(0)
    ex = pl.program_id(1)

    @pl.when(ex == 0)
    def _():
        xs_ref[...] = jnp.zeros_like(xs_ref)
        o_ref[...] = jnp.zeros_like(o_ref)

        def move(t, carry):
            row = u_ref[pl.ds(t, 1), :]
            xs_ref[pl.ds(d0_ref[0, 0, t], 1), :] = row
            xs_ref[pl.ds(d1_ref[0, 0, t], 1), :] = row
            return carry

        lax.fori_loop(0, tm, move, 0, unroll=8)

    start = seg_ref[i * N_EXPERTS + ex]
    n_full = nch_ref[i * N_EXPERTS + ex]

    def do_rows(r0, rows):
        lo, hi = _unpack_bf16_pairs(xs_ref[pl.ds(r0, rows), :])
        xb = jnp.concatenate([lo, hi], axis=-1).astype(BF16)
        a = _dot(xb, w1_ref[0, 0])
        hmid = a * _sigmoid(a) * _dot(xb, w3_ref[0, 0])
        o_ref[pl.ds(r0, rows), :] = _pack_bf16_pairs(_dot(hmid.astype(BF16), w2_ref[0, 0]))

    def do_chunk(j, carry):
        do_rows(pl.multiple_of(start + j * chunk, SEG_ALIGN), chunk)
        return carry

    lax.fori_loop(0, n_full, do_chunk, 0)

    @pl.when(tail_ref[i * N_EXPERTS + ex] > 0)
    def _():
        do_rows(pl.multiple_of(start + n_full * chunk, SEG_ALIGN), chunk // 2)


def _experts(seg, nch, tail, d0, d1, u2p, w1, w3, w2, *, layer):
    N, half = u2p.shape
    D = 2 * half
    E = w1.shape[1]
    tm = MOE_TILE
    nt = N // tm
    kern = functools.partial(_experts_kernel, tm=tm, chunk=MOE_CHUNK)
    grid_spec = pltpu.PrefetchScalarGridSpec(
        num_scalar_prefetch=3,
        grid=(nt, E),
        in_specs=[
            pl.BlockSpec((1, 1, tm), lambda i, e, *_: (i, 0, 0), memory_space=pltpu.SMEM),
            pl.BlockSpec((1, 1, tm), lambda i, e, *_: (i, 0, 0), memory_space=pltpu.SMEM),
            pl.BlockSpec((tm, half), lambda i, e, *_: (i, 0)),
            pl.BlockSpec((1, 1, D, D_EXPERT), lambda i, e, *_: (layer, e, 0, 0)),
            pl.BlockSpec((1, 1, D, D_EXPERT), lambda i, e, *_: (layer, e, 0, 0)),
            pl.BlockSpec((1, 1, D_EXPERT, D), lambda i, e, *_: (layer, e, 0, 0)),
        ],
        out_specs=pl.BlockSpec((MOE_ROWS, half), lambda i, e, *_: (i, 0)),
        scratch_shapes=[pltpu.VMEM((MOE_ROWS, half), jnp.uint32)],
    )
    return pl.pallas_call(
        kern,
        grid_spec=grid_spec,
        out_shape=jax.ShapeDtypeStruct((nt * MOE_ROWS, half), jnp.uint32),
        compiler_params=_cparams(("arbitrary", "arbitrary")),
        name="experts",
    )(seg, nch, tail, d0, d1, u2p, w1, w3, w2)


def _combine_kernel(d0_ref, d1_ref, ys_ref, w0_ref, w1_ref, x1_ref, mod_ref, o_ref, g0_ref, g1_ref, *, tsub):
    base = pl.program_id(1) * tsub

    def move(t, carry):
        g0_ref[pl.ds(t, 1), :] = ys_ref[pl.ds(d0_ref[0, 0, base + t], 1), :]
        g1_ref[pl.ds(t, 1), :] = ys_ref[pl.ds(d1_ref[0, 0, base + t], 1), :]
        return carry

    lax.fori_loop(0, tsub, move, 0, unroll=8)
    lo0, hi0 = _unpack_bf16_pairs(g0_ref[...])
    lo1, hi1 = _unpack_bf16_pairs(g1_ref[...])
    half = g0_ref.shape[1]
    w0 = jnp.concatenate([w0_ref[...]] * (half // LANES), axis=1)
    w1 = jnp.concatenate([w1_ref[...]] * (half // LANES), axis=1)
    o_ref[:, :half] = x1_ref[:, :half] + mod_ref[0, 5:6, :half] * (w0 * lo0 + w1 * lo1)
    o_ref[:, half:] = x1_ref[:, half:] + mod_ref[0, 5:6, half:] * (w0 * hi0 + w1 * hi1)


def _combine(d0, d1, ys, w0rep, w1rep, x1, mod, *, seq):
    N, D = x1.shape
    half = D // 2
    tm = MOE_TILE
    tsub = 512
    nt = N // tm
    ns = tm // tsub
    sub = lambda i, j: (i * ns + j, 0)
    return pl.pallas_call(
        functools.partial(_combine_kernel, tsub=tsub),
        grid=(nt, ns),
        in_specs=[
            pl.BlockSpec((1, 1, tm), lambda i, j: (i, 0, 0), memory_space=pltpu.SMEM),
            pl.BlockSpec((1, 1, tm), lambda i, j: (i, 0, 0), memory_space=pltpu.SMEM),
            pl.BlockSpec((MOE_ROWS, half), lambda i, j: (i, 0)),
            pl.BlockSpec((tsub, LANES), sub),
            pl.BlockSpec((tsub, LANES), sub),
            pl.BlockSpec((tsub, D), sub),
            pl.BlockSpec((1, 6, D), lambda i, j: ((i * tm + j * tsub) // seq, 0, 0)),
        ],
        out_specs=pl.BlockSpec((tsub, D), sub),
        out_shape=jax.ShapeDtypeStruct((N, D), F32),
        scratch_shapes=[pltpu.VMEM((tsub, half), jnp.uint32), pltpu.VMEM((tsub, half), jnp.uint32)],
        compiler_params=_cparams(("arbitrary", "arbitrary")),
        name="combine",
    )(d0, d1, ys, w0rep, w1rep, x1, mod)


def _alibi_slopes(first, count):
    hh = jnp.arange(first + 1, first + count + 1, dtype=F32)
    return jnp.exp2(-8.0 * hh / N_HEADS_TOTAL)


def _bd_const():
    idx = np.arange(LANES) // HEAD_DIM
    return jnp.asarray((idx[:, None] == idx[None, :]).astype(np.float32), BF16)


def _blkcols_const(T):
    blk = np.arange(T) // SEL_LEN
    m = np.where(np.arange(LANES)[None, :] == blk[:, None], NEG_INF, 0.0).astype(np.float32)
    return jnp.asarray(m, BF16)


def _overlap_const(T):
    nc = T // CMP_STRIDE
    n_sel = T // SEL_LEN
    cs = np.arange(nc) * CMP_STRIDE
    ss = np.arange(n_sel) * SEL_LEN
    ov = np.clip(np.minimum(cs[:, None] + CMP_LEN, ss[None, :] + SEL_LEN)
                 - np.maximum(cs[:, None], ss[None, :]), 0, None) / CMP_LEN
    return jnp.asarray(ov.T.astype(np.float32))


def _expand_const():
    ex = np.zeros((N_NSA_BRANCHES, LANES, D_NSA), np.float32)
    for br in range(N_NSA_BRANCHES):
        for hd in range(N_HEADS_NSA):
            ex[br, hd * N_NSA_BRANCHES + br, hd * HEAD_DIM:(hd + 1) * HEAD_DIM] = 1.0
    return jnp.asarray(ex, BF16)


def _compact_w_in(w):
    parts = [w[..., a:b] for a, b in (_C_QN, _C_QS, _C_KSL, _C_KW, _C_KS, _C_VSL, _C_VW, _C_VS, _C_KC, _C_VC, _C_GATE)]
    pad = jnp.zeros(w.shape[:-1] + (LANES - (_C_GATE[1] - _C_GATE[0]),), w.dtype)
    return jnp.concatenate(parts + [pad], axis=-1).astype(BF16)


def _qconst(slopes_q):
    hi = slopes_q.astype(BF16).astype(F32)
    lo = (slopes_q - hi).astype(BF16).astype(F32)
    cols = jnp.stack([64.0 * hi, 64.0 * lo, hi, lo], axis=1)
    return jnp.zeros((slopes_q.shape[0], LANES), F32).at[:, HEAD_DIM:HEAD_DIM + 4].set(cols)


def kernel(x, c, w_router, ada_w, ada_b, norm1_g, norm2_g, w_in, w_out, nsa_q_gain, nsa_k_gain, cmp_pos_k,
           cmp_pos_v, cmp_w1_k, cmp_w2_k, cmp_w1_v, cmp_w2_v, swa_q_gain, swa_k_gain, swa_sinks, beta_nsa,
           beta_swa, moe_w1, moe_w3, moe_w2):
    B, T, D = x.shape
    L = ada_w.shape[0]
    N = B * T
    nc = T // CMP_STRIDE
    assert D == D_MODEL and T % 512 == 0 and T <= 64 * 128 and B <= 8

    slopes_nsa = _alibi_slopes(N_HEADS_SWA, N_HEADS_NSA)
    slopes_swa = _alibi_slopes(0, N_HEADS_SWA)
    qconst = _qconst(jnp.concatenate([slopes_nsa, slopes_swa]))
    bd = _bd_const()
    blkcols = _blkcols_const(T)
    ov = _overlap_const(T)
    expand = _expand_const()
    wr_t = w_router.T
    tri = jnp.asarray(np.triu(np.ones((TRI, TRI), np.float32), 1), BF16)
    assert N % MOE_TILE == 0
    moe_w1b, moe_w3b, moe_w2b = moe_w1.astype(BF16), moe_w3.astype(BF16), moe_w2.astype(BF16)
    w_outb = w_out.astype(BF16)
    w_inb = _compact_w_in(w_in)
    cmp_w1s = jnp.stack([cmp_w1_k, cmp_w1_v], axis=1).reshape(2 * L, CMP_LEN * HEAD_DIM, CMP_HIDDEN).astype(BF16)
    cmp_w2s = jnp.pad(jnp.stack([cmp_w2_k, cmp_w2_v], axis=1).reshape(2 * L, CMP_HIDDEN, HEAD_DIM),
                      ((0, 0), (0, 0), (0, LANES - HEAD_DIM))).astype(BF16)
    cmp_pos = jnp.stack([cmp_pos_k, cmp_pos_v], axis=1).reshape(2 * L, 1, CMP_LEN * HEAD_DIM)
    cmp_pos = jnp.broadcast_to(cmp_pos, (2 * L, 8, CMP_LEN * HEAD_DIM)).astype(BF16)
    zero_sinks = jnp.zeros((N_HEADS_NSA,), F32)

    c_pad = jnp.zeros((8, D), F32).at[:B].set(c)
    mod_all = _ada(c_pad, ada_w, ada_b)[:, :B].reshape(L, B, 6, D)

    x2 = x.reshape(N, D)
    for l in range(L):
        mod = mod_all[l]
        tile2 = lambda g: jnp.tile(g, 2)
        gain_c = jnp.concatenate([
            jnp.tile(nsa_q_gain[l], N_HEADS_NSA) * ATTN_SCALE,
            jnp.tile(swa_q_gain[l], N_HEADS_SWA) * ATTN_SCALE,
            tile2(nsa_k_gain[l, 1]), tile2(nsa_k_gain[l, 2]), tile2(swa_k_gain[l])]).reshape(1, -1)
        proj, xc = _inproj(x2, mod, norm1_g[l].reshape(1, D), w_inb, gain_c, qconst, bd, seq=T, layer=l)
        kgain = jnp.pad(nsa_k_gain[l, 0], (0, LANES - HEAD_DIM)).reshape(1, LANES)
        kvc = _cmpmlp(xc, cmp_w1s, cmp_w2s, cmp_pos, kgain, B=B, layer=l)

        o_cmp, notsel = _cmpattn(proj, kvc, ov, B=B, T=T)
        o_slc = _selattn(proj, notsel, blkcols, B=B, T=T)
        o_win = _band(slopes_nsa, zero_sinks, proj, B=B, T=T, window=NSA_WINDOW, q_slot=SLOT_QN,
                      k_slot=SLOT_KW, v_slot=SLOT_VW, use_sinks=False, name="winattn")
        o_swa = _band(slopes_swa, swa_sinks[l], proj, B=B, T=T, window=SWA_WINDOW, q_slot=SLOT_QS,
                      k_slot=SLOT_KS, v_slot=SLOT_VS, use_sinks=True, name="swaattn")

        x1, u2, lt = _outproj(o_cmp, o_slc, o_win, o_swa, proj, x2, mod, expand,
                              beta_nsa[l].reshape(1, -1), beta_swa[l].reshape(1, -1), w_outb,
                              norm2_g[l].reshape(1, D), wr_t, seq=T, layer=l)
        d0, d1, w0rep, w1rep, seg, nch, tail = _route(lt, tri)
        ys = _experts(seg[:, :, 0].reshape(-1), nch[:, :, 0].reshape(-1), tail[:, :, 0].reshape(-1), d0, d1, u2,
                      moe_w1b, moe_w3b, moe_w2b, layer=l)
        x2 = _combine(d0, d1, ys, w0rep, w1rep, x1, mod, seq=T)
    return x2.reshape(B, T, D)
```

```python
import functools

import numpy as np
import jax
import jax.numpy as jnp
from jax import lax
from jax.experimental import pallas as pl
from jax.experimental.pallas import tpu as pltpu

D_MODEL = 1024
DEPTH = 2
HEAD_DIM = 64
N_HEADS_NSA = 8
N_KV_NSA = 2
G_NSA = N_HEADS_NSA // N_KV_NSA
N_HEADS_SWA = 8
N_KV_SWA = 2
G_SWA = N_HEADS_SWA // N_KV_SWA
D_NSA = N_HEADS_NSA * HEAD_DIM
D_SWA = N_HEADS_SWA * HEAD_DIM
D_MIX = D_NSA + D_SWA
N_HEADS_TOTAL = N_HEADS_NSA + N_HEADS_SWA
KV_NSA = N_KV_NSA * HEAD_DIM
KV_SWA = N_KV_SWA * HEAD_DIM
N_NSA_BRANCHES = 3
CMP_LEN = 32
CMP_STRIDE = 16
CMP_HIDDEN = 256
SEL_LEN = 64
SEL_TOPK = 16
NSA_WINDOW = 512
SWA_WINDOW = 128
N_EXPERTS = 16
N_GROUPS = 4
EXPERTS_PER_GROUP = N_EXPERTS // N_GROUPS
TOP_K = 2
D_EXPERT = 512
EPS = 1e-6
NEG_INF = -1e30
ATTN_SCALE = HEAD_DIM ** -0.5

F32 = jnp.float32
BF16 = jnp.bfloat16

LANES = 128
VMEM_LIMIT = 48 * 1024 * 1024

MOE_TILE = 2048
MOE_CHUNK = 256
SEG_ALIGN = 16
TRI = 256
MOE_ROWS = TOP_K * MOE_TILE + N_EXPERTS * SEG_ALIGN + MOE_CHUNK
EXPERTS_PER_STEP = 2
MOE_VMEM_LIMIT = 56 * 1024 * 1024

SLOT_QN = 0
SLOT_QS = 8
SLOT_KSL = 16
SLOT_KW = 18
SLOT_KS = 20
SLOT_VSL = 22
SLOT_VW = 24
SLOT_VS = 26
SLOT_GATE = 28
N_SLOTS = 29
N_CHUNKS = 17
N_NORM_CHUNKS = 11

_C_QN = (0, 512)
_C_KC = (512, 640)
_C_VC = (640, 768)
_C_KSL = (768, 896)
_C_VSL = (896, 1024)
_C_KW = (1024, 1152)
_C_VW = (1152, 1280)
_C_GATE = (1280, 1304)
_C_QS = (1304, 1816)
_C_KS = (1816, 1944)
_C_VS = (1944, 2072)


def _cparams(sem):
    return pltpu.CompilerParams(dimension_semantics=sem, vmem_limit_bytes=VMEM_LIMIT)


def _dot(a, b, **kw):
    return jnp.dot(a, b, preferred_element_type=F32, **kw)


def _dot_nt(a, b):
    return lax.dot_general(a, b, (((1,), (1,)), ((), ())), preferred_element_type=F32)


def _sigmoid(x):
    return 1.0 / (1.0 + jnp.exp(-x))


def _pack_bf16_pairs(x):
    n = x.shape[1] // 2
    return pltpu.pack_elementwise([x[:, :n], x[:, n:]], packed_dtype=BF16)


def _unpack_bf16_pairs(w):
    lo = pltpu.unpack_elementwise(w, index=0, packed_dtype=BF16, unpacked_dtype=F32)
    hi = pltpu.unpack_elementwise(w, index=1, packed_dtype=BF16, unpacked_dtype=F32)
    return lo, hi


def _ada_kernel(c_ref, w_ref, b_ref, o_ref):
    c = c_ref[...]
    cond = c * _sigmoid(c)
    o_ref[0] = _dot(cond.astype(BF16), w_ref[0].astype(BF16)) + b_ref[0]


def _ada(c_pad, ada_w, ada_b):
    L, D, N6 = ada_w.shape
    tn = 1536
    return pl.pallas_call(
        _ada_kernel,
        grid=(L, N6 // tn),
        in_specs=[
            pl.BlockSpec((8, D), lambda l, j: (0, 0)),
            pl.BlockSpec((1, D, tn), lambda l, j: (l, 0, j)),
            pl.BlockSpec((1, 1, tn), lambda l, j: (l, 0, j)),
        ],
        out_specs=pl.BlockSpec((1, 8, tn), lambda l, j: (l, 0, j)),
        out_shape=jax.ShapeDtypeStruct((L, 8, N6), F32),
        compiler_params=_cparams(("arbitrary", "arbitrary")),
        name="ada",
    )(c_pad, ada_w, ada_b.reshape(L, 1, N6))


def _inproj_kernel(x_ref, mod_ref, g_ref, w_ref, gain_ref, qc_ref, bd_ref, o_ref, xc_ref, scr_ref, *, tm, seq):
    i = pl.program_id(0)
    x = x_ref[...]
    ms = jnp.mean(x * x, axis=-1, keepdims=True)
    u = x * lax.rsqrt(ms + EPS) * g_ref[...]
    u = u * (1.0 + mod_ref[0, 1:2, :]) + mod_ref[0, 0:1, :]
    y = _dot(u.astype(BF16), w_ref[0])

    lane = lax.broadcasted_iota(jnp.int32, (tm, LANES), 1)
    low = lane < HEAD_DIM
    t = lax.broadcasted_iota(jnp.int32, (tm, LANES), 0) + lax.rem(i * tm, seq)
    hi_part = (t >> 6).astype(F32)
    lo_part = (t & 63).astype(F32)
    poscols = jnp.where((lane == 64) | (lane == 65), hi_part,
                        jnp.where((lane == 66) | (lane == 67), lo_part, 0.0))
    bd = bd_ref[...]

    def put(slot, val):
        o_ref[:, slot * LANES:(slot + 1) * LANES] = val.astype(o_ref.dtype)

    for c in range(N_CHUNKS):
        blk = y[:, c * LANES:(c + 1) * LANES]
        if c < N_NORM_CHUNKS:
            msb = _dot((blk * blk).astype(BF16), bd) * (1.0 / HEAD_DIM)
            blk = blk * lax.rsqrt(msb + EPS) * gain_ref[:, c * LANES:(c + 1) * LANES]
        if c < 14:
            rolled = pltpu.roll(blk, HEAD_DIM, axis=1)
            if c < 8:
                f0 = qc_ref[2 * c:2 * c + 1, :]
                f1 = qc_ref[2 * c + 1:2 * c + 2, :]
                s0 = 2 * c
            elif c < N_NORM_CHUNKS:
                f0 = f1 = poscols
                s0 = SLOT_KSL + 2 * (c - 8)
            else:
                f0 = f1 = 1.0
                s0 = SLOT_VSL + 2 * (c - N_NORM_CHUNKS)
            put(s0, jnp.where(low, blk, f0))
            put(s0 + 1, jnp.where(low, rolled, f1))
        elif c == 16:
            put(SLOT_GATE, blk)
        else:
            scr_ref[...] = blk
            nck = tm // CMP_STRIDE
            low_c = lax.broadcasted_iota(jnp.int32, (nck, LANES), 1) < HEAD_DIM
            head_cols = CMP_STRIDE * HEAD_DIM
            for pair in range(CMP_STRIDE // 2):
                ta = scr_ref[pl.ds(2 * pair, nck, stride=CMP_STRIDE), :]
                tb = scr_ref[pl.ds(2 * pair + 1, nck, stride=CMP_STRIDE), :]
                h0 = jnp.where(low_c, ta, pltpu.roll(tb, HEAD_DIM, axis=1))
                h1 = jnp.where(low_c, pltpu.roll(ta, HEAD_DIM, axis=1), tb)
                xc_ref[c - 14, :, pair * LANES:(pair + 1) * LANES] = h0.astype(xc_ref.dtype)
                xc_ref[c - 14, :, head_cols + pair * LANES:head_cols + (pair + 1) * LANES] = h1.astype(xc_ref.dtype)


def _inproj(x2, mod, norm_g, w_c, gain_c, qconst, bd, *, seq, layer):
    N, D = x2.shape
    tm = 512
    tiles_per_seq = seq // tm
    kern = functools.partial(_inproj_kernel, tm=tm, seq=seq)
    nchunk = tm // CMP_STRIDE
    xc_shape = jax.ShapeDtypeStruct((2, N // CMP_STRIDE, CMP_STRIDE * LANES), BF16)
    return pl.pallas_call(
        kern,
        grid=(N // tm,),
        in_specs=[
            pl.BlockSpec((tm, D), lambda i: (i, 0)),
            pl.BlockSpec((1, 6, D), lambda i: (i // tiles_per_seq, 0, 0)),
            pl.BlockSpec((1, D), lambda i: (0, 0)),
            pl.BlockSpec((1, D, N_CHUNKS * LANES), lambda i: (layer, 0, 0)),
            pl.BlockSpec((1, N_NORM_CHUNKS * LANES), lambda i: (0, 0)),
            pl.BlockSpec((16, LANES), lambda i: (0, 0)),
            pl.BlockSpec((LANES, LANES), lambda i: (0, 0)),
        ],
        out_specs=[
            pl.BlockSpec((tm, N_SLOTS * LANES), lambda i: (i, 0)),
            pl.BlockSpec((2, nchunk, CMP_STRIDE * LANES), lambda i: (0, i, 0)),
        ],
        out_shape=[jax.ShapeDtypeStruct((N, N_SLOTS * LANES), BF16), xc_shape],
        scratch_shapes=[pltpu.VMEM((tm, LANES), F32)],
        compiler_params=_cparams(("arbitrary",)),
        name="inproj",
    )(x2, mod, norm_g, w_c, gain_c, qconst, bd)


def _gelu_tanh(x):
    return 0.5 * x * (1.0 + jnp.tanh(np.sqrt(2.0 / np.pi).astype(np.float32) * (x + 0.044715 * (x * x * x))))


def _cmpmlp_kernel(x_ref, w1_ref, w2_ref, pos_ref, gain_ref, o_ref, *, nc):
    kv = pl.program_id(0)
    half = CMP_STRIDE * HEAD_DIM
    bias = _dot(pos_ref[0], w1_ref[0])[0:1]
    lane = lax.broadcasted_iota(jnp.int32, (nc, LANES), 1)
    last = lax.broadcasted_iota(jnp.int32, (nc, LANES), 0) * CMP_STRIDE + (CMP_LEN - 1)
    poscols = jnp.where((lane == 64) | (lane == 65), (last >> 6).astype(F32),
                        jnp.where((lane == 66) | (lane == 67), (last & 63).astype(F32), 0.0))
    for hd in range(N_KV_NSA):
        xc = x_ref[0, :, hd * half:(hd + 1) * half]
        first = _dot(xc, w1_ref[0, :half, :])
        second = _dot(xc, w1_ref[0, half:, :])
        h = _gelu_tanh(first + pltpu.roll(second, nc - 1, axis=0) + bias)
        z = _dot(h.astype(BF16), w2_ref[0])
        zn = z * lax.rsqrt(jnp.sum(z * z, axis=-1, keepdims=True) * (1.0 / HEAD_DIM) + EPS) * gain_ref[...]
        o_ref[0, 0, hd] = jnp.where(kv == 0, zn + poscols, z + jnp.where(lane >= HEAD_DIM, 1.0, 0.0))


def _cmpmlp(xc, w1s, w2s, pos, gain, *, B, layer):
    _, nchunks, K = xc.shape
    nc = nchunks // B
    win = CMP_LEN * HEAD_DIM
    kern = functools.partial(_cmpmlp_kernel, nc=nc)
    return pl.pallas_call(
        kern,
        grid=(2, B),
        in_specs=[
            pl.BlockSpec((1, nc, K), lambda kv, b: (kv, b, 0)),
            pl.BlockSpec((1, win, CMP_HIDDEN), lambda kv, b: (2 * layer + kv, 0, 0)),
            pl.BlockSpec((1, CMP_HIDDEN, LANES), lambda kv, b: (2 * layer + kv, 0, 0)),
            pl.BlockSpec((1, 8, win), lambda kv, b: (2 * layer + kv, 0, 0)),
            pl.BlockSpec((1, LANES), lambda kv, b: (0, 0)),
        ],
        out_specs=pl.BlockSpec((1, 1, N_KV_NSA, nc, LANES), lambda kv, b: (kv, b, 0, 0, 0)),
        out_shape=jax.ShapeDtypeStruct((2, B, N_KV_NSA, nc, LANES), F32),
        compiler_params=_cparams(("arbitrary", "arbitrary")),
        name="cmpmlp",
    )(xc, w1s, w2s, pos, gain)


def _normalize_pack(pvs, extra=None):
    lane = lax.broadcasted_iota(jnp.int32, pvs[0].shape, 1)
    low = lane < HEAD_DIM
    pairs = []
    for p in range(len(pvs) // 2):
        even, odd = pvs[2 * p], pvs[2 * p + 1]
        den_e = even if extra is None else even + extra[2 * p]
        den_o = odd if extra is None else odd + extra[2 * p + 1]
        o_e = even * (1.0 / pltpu.roll(den_e, HEAD_DIM, axis=1))
        o_o = pltpu.roll(odd, HEAD_DIM, axis=1) * (1.0 / den_o)
        pairs.append(jnp.where(low, o_e, o_o))
    return jnp.concatenate(pairs, axis=-1)


def _cmpattn_kernel(q_ref, kc_ref, vc_ref, ov_ref, o_ref, ns_ref, *, tq, nc, n_cmp, n_sel, top):
    q0 = pl.program_id(2) * tq
    kc = kc_ref[0, 0, 0].astype(BF16)
    vc = vc_ref[0, 0, 0].astype(BF16)

    t_col = q0 + lax.broadcasted_iota(jnp.int32, (tq, 1), 0)
    row_valid = (t_col >= CMP_LEN - 1).astype(F32)
    t_row = q0 + lax.broadcasted_iota(jnp.int32, (1, tq), 1)
    n_col = lax.broadcasted_iota(jnp.int32, (nc, 1), 0)
    valid_t = (t_row - (n_col * CMP_STRIDE + (CMP_LEN - 1)) >= 0) & (n_col < n_cmp)
    bias_t = jnp.where(valid_t, 0.0, NEG_INF)
    row_valid_t = (t_row >= CMP_LEN - 1).astype(F32)

    outs = []
    psum_t = jnp.zeros((nc, tq), F32)
    for g in range(G_NSA):
        qg = q_ref[:, g * LANES:(g + 1) * LANES]
        st = _dot_nt(kc, qg) + bias_t
        et = jnp.exp(st - jnp.max(st, axis=0, keepdims=True))
        outs.append(lax.dot_general(et.astype(BF16), vc, (((0,), (0,)), ((), ())), preferred_element_type=F32))
        psum_t = psum_t + et * (1.0 / jnp.sum(et, axis=0, keepdims=True))
    o_ref[...] = (_normalize_pack(outs) * row_valid).astype(o_ref.dtype)

    psum_t = psum_t * row_valid_t
    imp_t = _dot(ov_ref[...], psum_t, precision=lax.Precision.HIGHEST)
    j = lax.broadcasted_iota(jnp.int32, (n_sel, 1), 0)
    cur = t_row >> 6
    forced = (j == 0) | (j == cur) | (j == cur - 1)
    v = jnp.where(forced, 1e9, jnp.where(j > cur, NEG_INF, imp_t))
    n_grp = n_sel // 8
    vg = [v[8 * r:8 * r + 8, :] for r in range(n_grp)]
    jg = lax.broadcasted_iota(jnp.int32, (8, 1), 0)
    cnt = [jnp.zeros((8, tq), F32) for _ in range(n_grp)]
    for jp in range(n_sel):
        vj = v[jp:jp + 1, :]
        for r in range(n_grp):
            ge = jnp.where(vj >= vg[r], 1.0, 0.0)
            gt = jnp.where(vj > vg[r], 1.0, 0.0)
            if 8 * r > jp:
                inc = ge
            elif 8 * r + 7 < jp:
                inc = gt
            else:
                inc = jnp.where(jg + 8 * r > jp, ge, gt)
            cnt[r] = cnt[r] + inc
    notsel_t = jnp.concatenate([jnp.where(cn < top, 0.0, 1.0) for cn in cnt], axis=0)
    if n_sel < LANES:
        notsel_t = jnp.concatenate([notsel_t, jnp.zeros((LANES - n_sel, tq), F32)], axis=0)
    ns_ref[0, 0] = notsel_t.T.astype(ns_ref.dtype)


def _cmpattn(proj, kvc, ov, *, B, T):
    tq = 256
    nt = T // tq
    nc = T // CMP_STRIDE
    n_cmp = (T - CMP_LEN) // CMP_STRIDE + 1
    n_sel = T // SEL_LEN
    top = min(SEL_TOPK, n_sel)
    assert n_sel <= LANES and n_sel % 8 == 0
    kern = functools.partial(_cmpattn_kernel, tq=tq, nc=nc, n_cmp=n_cmp, n_sel=n_sel, top=top)
    return pl.pallas_call(
        kern,
        grid=(B, N_KV_NSA, nt),
        in_specs=[
            pl.BlockSpec((tq, G_NSA * LANES), lambda b, h, i: (b * nt + i, h)),
            pl.BlockSpec((1, 1, 1, nc, LANES), lambda b, h, i: (0, b, h, 0, 0)),
            pl.BlockSpec((1, 1, 1, nc, LANES), lambda b, h, i: (1, b, h, 0, 0)),
            pl.BlockSpec((n_sel, nc), lambda b, h, i: (0, 0)),
        ],
        out_specs=[
            pl.BlockSpec((tq, G_NSA * HEAD_DIM), lambda b, h, i: (b * nt + i, h)),
            pl.BlockSpec((1, 1, tq, LANES), lambda b, h, i: (b, h, i, 0)),
        ],
        out_shape=[
            jax.ShapeDtypeStruct((B * T, D_NSA), BF16),
            jax.ShapeDtypeStruct((B, N_KV_NSA, T, LANES), BF16),
        ],
        compiler_params=_cparams(("arbitrary", "arbitrary", "arbitrary")),
        name="cmpattn",
    )(proj, kvc, kvc, ov)


def _selattn_kernel(q_ref, ns_ref, k_ref, v_ref, bc_ref, o_ref, qx_ref, s0_ref, s1_ref, mx0_ref, mx1_ref, m_ref,
                    acc_ref, *, tq, tk):
    diag = pl.program_id(2)
    ns = ns_ref[0, 0]
    for g in range(G_NSA):
        qx_ref[g * tq:(g + 1) * tq, :] = jnp.concatenate([q_ref[:, g * LANES:(g + 1) * LANES], ns], axis=-1)
    m_ref[...] = jnp.full(m_ref.shape, NEG_INF, F32)
    acc_ref[...] = jnp.zeros(acc_ref.shape, F32)

    def produce(s_ref, mx_ref, kt, masked):
        start = pl.multiple_of(kt * tk, tk)
        kx = jnp.concatenate([k_ref[pl.ds(start, tk), :], bc_ref[pl.ds(start, tk), :]], axis=-1)
        s = _dot_nt(qx_ref[...], kx)
        if masked:
            r = lax.broadcasted_iota(jnp.int32, (tq, tk), 0)
            col = lax.broadcasted_iota(jnp.int32, (tq, tk), 1)
            s = s + jnp.concatenate([jnp.where(col <= r, 0.0, NEG_INF)] * G_NSA, axis=0)
        s_ref[...] = s
        mx_ref[...] = jnp.broadcast_to(jnp.max(s, axis=-1, keepdims=True), mx_ref.shape)

    def consume(s_ref, mx_ref, kt):
        m = m_ref[...]
        m_new = jnp.maximum(m, mx_ref[...])
        p = jnp.exp(s_ref[...] - jnp.concatenate([m_new] * (tk // LANES), axis=1)).astype(BF16)
        vx = v_ref[pl.ds(pl.multiple_of(kt * tk, tk), tk), :]
        acc_ref[...] = jnp.exp(m - m_new) * acc_ref[...] + _dot(p, vx)
        m_ref[...] = m_new

    def tile(k):
        return k - 1

    produce(s0_ref, mx0_ref, diag, True)

    def pair(jj, carry):
        k = 2 * jj
        produce(s1_ref, mx1_ref, tile(k + 1), False)
        consume(s0_ref, mx0_ref, jnp.where(k == 0, diag, tile(k)))
        produce(s0_ref, mx0_ref, tile(k + 2), False)
        consume(s1_ref, mx1_ref, tile(k + 1))
        return carry

    n_pairs = lax.shift_right_logical(diag, 1)
    lax.fori_loop(0, n_pairs, pair, 0)
    k_last = 2 * n_pairs

    @pl.when(k_last == diag)
    def _():
        consume(s0_ref, mx0_ref, jnp.where(diag == 0, diag, tile(diag)))

    @pl.when(k_last != diag)
    def _():
        produce(s1_ref, mx1_ref, tile(diag), False)
        consume(s0_ref, mx0_ref, jnp.where(k_last == 0, diag, tile(k_last)))
        consume(s1_ref, mx1_ref, tile(diag))

    acc = acc_ref[...]
    o_ref[...] = _normalize_pack([acc[g * tq:(g + 1) * tq] for g in range(G_NSA)]).astype(o_ref.dtype)


def _selattn(proj, notsel, blkcols, *, B, T):
    tq = 512
    tk = 512
    nt = T // tq
    m_rows = G_NSA * tq
    assert T % tk == 0 and tk == tq
    kern = functools.partial(_selattn_kernel, tq=tq, tk=tk)
    return pl.pallas_call(
        kern,
        grid=(B, N_KV_NSA, nt),
        in_specs=[
            pl.BlockSpec((tq, G_NSA * LANES), lambda b, h, i: (b * nt + i, h)),
            pl.BlockSpec((1, 1, tq, LANES), lambda b, h, i: (b, h, i, 0)),
            pl.BlockSpec((T, LANES), lambda b, h, i: (b, SLOT_KSL + h)),
            pl.BlockSpec((T, LANES), lambda b, h, i: (b, SLOT_VSL + h)),
            pl.BlockSpec((T, LANES), lambda b, h, i: (0, 0)),
        ],
        out_specs=pl.BlockSpec((tq, G_NSA * HEAD_DIM), lambda b, h, i: (b * nt + i, h)),
        out_shape=jax.ShapeDtypeStruct((B * T, D_NSA), BF16),
        scratch_shapes=[
            pltpu.VMEM((m_rows, 2 * LANES), BF16),
            pltpu.VMEM((m_rows, tk), F32),
            pltpu.VMEM((m_rows, tk), F32),
            pltpu.VMEM((m_rows, LANES), F32),
            pltpu.VMEM((m_rows, LANES), F32),
            pltpu.VMEM((m_rows, LANES), F32),
            pltpu.VMEM((m_rows, LANES), F32),
        ],
        compiler_params=_cparams(("arbitrary", "arbitrary", "arbitrary")),
        name="selattn",
    )(proj, notsel, proj, proj, blkcols)


def _band_kernel(sl_ref, sink_ref, q_ref, k_ref, v_ref, o_ref, *, tq, n_sub, window, seq, use_sinks):
    h = pl.program_id(1)
    span = window + tq
    groups = q_ref.shape[1] // LANES
    r = lax.broadcasted_iota(jnp.int32, (tq, span), 0)
    col = lax.broadcasted_iota(jnp.int32, (tq, span), 1)
    for sub in range(n_sub):
        q0 = (pl.program_id(2) * n_sub + sub) * tq
        start = pl.multiple_of(jnp.clip(q0 - window, 0, seq - span), tq)
        rows = pl.ds(sub * tq, tq)
        qx = jnp.concatenate([q_ref[rows, g * LANES:(g + 1) * LANES] for g in range(groups)], axis=0)
        d = (q0 - start) + r - col
        bias = jnp.where((d >= 0) & (d < window), 0.0, NEG_INF)
        s = _dot_nt(qx, k_ref[pl.ds(start, span), :]) + jnp.concatenate([bias] * groups, axis=0)
        m = jnp.broadcast_to(jnp.max(s, axis=-1, keepdims=True), (groups * tq, LANES))
        if use_sinks:
            t_rep = (q0 + lax.broadcasted_iota(jnp.int32, (tq, LANES), 0)).astype(F32)
            sink = jnp.concatenate(
                [sink_ref[h * groups + g] + sl_ref[h * groups + g] * t_rep for g in range(groups)], axis=0)
            m = jnp.maximum(m, sink)
        e = jnp.exp(s - jnp.concatenate([m] * (span // LANES), axis=1)).astype(BF16)
        pv = _dot(e, v_ref[pl.ds(start, span), :])
        extra = None
        if use_sinks:
            esink = jnp.exp(sink - m)
            extra = [esink[g * tq:(g + 1) * tq] for g in range(groups)]
        o_ref[rows, :] = _normalize_pack([pv[g * tq:(g + 1) * tq] for g in range(groups)], extra).astype(o_ref.dtype)


def _band(slopes, sinks, proj, *, B, T, window, q_slot, k_slot, v_slot, use_sinks, n_sub, name):
    tq = 128
    n_sub = min(n_sub, T // tq)
    tb = tq * n_sub
    nt = T // tb
    groups = 4
    assert window % tq == 0 and window + tq <= T and T % tb == 0
    qb = q_slot // groups
    kern = functools.partial(_band_kernel, tq=tq, n_sub=n_sub, window=window, seq=T, use_sinks=use_sinks)
    return pl.pallas_call(
        kern,
        grid=(B, 2, nt),
        in_specs=[
            pl.BlockSpec(memory_space=pltpu.SMEM),
            pl.BlockSpec(memory_space=pltpu.SMEM),
            pl.BlockSpec((tb, groups * LANES), lambda b, h, i: (b * nt + i, qb + h)),
            pl.BlockSpec((T, LANES), lambda b, h, i: (b, k_slot + h)),
            pl.BlockSpec((T, LANES), lambda b, h, i: (b, v_slot + h)),
        ],
        out_specs=pl.BlockSpec((tb, groups * HEAD_DIM), lambda b, h, i: (b * nt + i, h)),
        out_shape=jax.ShapeDtypeStruct((B * T, 2 * groups * HEAD_DIM), BF16),
        compiler_params=_cparams(("arbitrary", "arbitrary", "arbitrary")),
        name=name,
    )(slopes, sinks, proj, proj, proj)


def _outproj_kernel(ocmp_ref, oslc_ref, owin_ref, oswa_ref, gate_ref, x_ref, mod_ref, ex_ref, bn_ref, bs_ref,
                    w_ref, g2_ref, wr_ref, x1_ref, u2_ref, lt_ref):
    sg = _sigmoid(gate_ref[...].astype(F32)).astype(BF16)
    o_nsa = (_dot(sg, ex_ref[0]) * ocmp_ref[...].astype(F32)
             + _dot(sg, ex_ref[1]) * oslc_ref[...].astype(F32)
             + _dot(sg, ex_ref[2]) * owin_ref[...].astype(F32))
    o_swa = oswa_ref[...].astype(F32)
    n1 = o_nsa * lax.rsqrt(jnp.mean(o_nsa * o_nsa, axis=-1, keepdims=True) + EPS) * bn_ref[...]
    n2 = o_swa * lax.rsqrt(jnp.mean(o_swa * o_swa, axis=-1, keepdims=True) + EPS) * bs_ref[...]
    merged = jnp.concatenate([n1, n2], axis=-1).astype(BF16)
    y = _dot(merged, w_ref[0])
    x1 = x_ref[...] + mod_ref[0, 2:3, :] * y
    x1_ref[...] = x1
    u2 = x1 * lax.rsqrt(jnp.mean(x1 * x1, axis=-1, keepdims=True) + EPS) * g2_ref[...]
    u2 = u2 * (1.0 + mod_ref[0, 4:5, :]) + mod_ref[0, 3:4, :]
    u2_ref[...] = _pack_bf16_pairs(u2)
    lt_ref[...] = lax.dot_general(wr_ref[...], u2, (((1,), (1,)), ((), ())), preferred_element_type=F32,
                                  precision=lax.Precision.HIGHEST)


def _outproj(ocmp, oslc, owin, oswa, proj, x2, mod, expand, beta_n, beta_s, w_out, norm2_g, wr_t, *, seq, layer):
    N, D = x2.shape
    tm = 512
    tiles_per_seq = seq // tm
    row = lambda i: (i, 0)
    const2 = lambda i: (0, 0)
    return pl.pallas_call(
        _outproj_kernel,
        grid=(N // tm,),
        in_specs=[
            pl.BlockSpec((tm, D_NSA), row),
            pl.BlockSpec((tm, D_NSA), row),
            pl.BlockSpec((tm, D_NSA), row),
            pl.BlockSpec((tm, D_SWA), row),
            pl.BlockSpec((tm, LANES), lambda i: (i, SLOT_GATE)),
            pl.BlockSpec((tm, D), row),
            pl.BlockSpec((1, 6, D), lambda i: (i // tiles_per_seq, 0, 0)),
            pl.BlockSpec((N_NSA_BRANCHES, LANES, D_NSA), lambda i: (0, 0, 0)),
            pl.BlockSpec((1, D_NSA), const2),
            pl.BlockSpec((1, D_SWA), const2),
            pl.BlockSpec((1, D_MIX, D), lambda i: (layer, 0, 0)),
            pl.BlockSpec((1, D), const2),
            pl.BlockSpec((N_EXPERTS, D), const2),
        ],
        out_specs=[
            pl.BlockSpec((tm, D), row),
            pl.BlockSpec((tm, D // 2), row),
            pl.BlockSpec((N_EXPERTS, tm), lambda i: (0, i)),
        ],
        out_shape=[
            jax.ShapeDtypeStruct((N, D), F32),
            jax.ShapeDtypeStruct((N, D // 2), jnp.uint32),
            jax.ShapeDtypeStruct((N_EXPERTS, N), F32),
        ],
        compiler_params=_cparams(("arbitrary",)),
        name="outproj",
    )(ocmp, oslc, owin, oswa, proj, x2, mod, expand, beta_n, beta_s, w_out, norm2_g, wr_t)


def _route_kernel(lt_ref, tri_ref, d0_ref, d1_ref, w0_ref, w1_ref, seg_ref, nch_ref, tail_ref, *, tm):
    lt = lt_ref[...]
    e = jnp.exp(lt - jnp.max(lt, axis=0, keepdims=True))
    aff = e / jnp.sum(e, axis=0, keepdims=True)
    rows = [aff[k:k + 1, :] for k in range(N_EXPERTS)]
    epg = EXPERTS_PER_GROUP
    scores = []
    for gr in range(N_GROUPS):
        xs = rows[gr * epg:(gr + 1) * epg]
        top1 = functools.reduce(jnp.maximum, xs)
        second = None
        for a in range(epg):
            for b in range(a + 1, epg):
                mn = jnp.minimum(xs[a], xs[b])
                second = mn if second is None else jnp.maximum(second, mn)
        scores.append(top1 + second)
    taken = None
    in_group = []
    for gr in range(N_GROUPS):
        best = None
        for o in range(gr + 1, N_GROUPS):
            c = scores[gr] >= scores[o]
            best = c if best is None else (best & c)
        if best is None:
            best = jnp.full(scores[gr].shape, True)
        sel = best if taken is None else (best & ~taken)
        taken = sel if taken is None else (taken | sel)
        in_group.append(sel)
    ys = []
    for k in range(epg):
        yk = rows[k]
        for gr in range(1, N_GROUPS):
            yk = jnp.where(in_group[gr], rows[gr * epg + k], yk)
        ys.append(yk)
    chosen = []
    for k in range(epg):
        rk = jnp.zeros_like(ys[k])
        for o in range(epg):
            if o == k:
                continue
            before = (ys[o] >= ys[k]) if o < k else (ys[o] > ys[k])
            rk = rk + jnp.where(before, 1.0, 0.0)
        chosen.append(rk < TOP_K)
    total = functools.reduce(lambda a, b: a + b, [jnp.where(chosen[k], ys[k], 0.0) for k in range(epg)])
    first, seen = [], None
    for k in range(epg):
        first.append(chosen[k] if seen is None else (chosen[k] & ~seen))
        seen = chosen[k] if seen is None else (seen | chosen[k])
    second = [chosen[k] & ~first[k] for k in range(epg)]
    inv_total = 1.0 / total
    w_first = functools.reduce(lambda a, b: a + b, [jnp.where(first[k], ys[k], 0.0) for k in range(epg)]) * inv_total
    w_second = functools.reduce(lambda a, b: a + b, [jnp.where(second[k], ys[k], 0.0) for k in range(epg)]) * inv_total
    f_rows, s_rows = [], []
    for ex in range(N_EXPERTS):
        gr, k = divmod(ex, epg)
        f_rows.append(jnp.where(in_group[gr] & first[k], 1.0, 0.0))
        s_rows.append(jnp.where(in_group[gr] & second[k], 1.0, 0.0))
    fmat = jnp.concatenate(f_rows, axis=0)
    smat = jnp.concatenate(s_rows, axis=0)
    cmat = fmat + smat
    carry = jnp.zeros((N_EXPERTS, 1), F32)
    ranks = []
    for blk in range(tm // TRI):
        cb = cmat[:, blk * TRI:(blk + 1) * TRI]
        ranks.append(_dot(cb.astype(BF16), tri_ref[...]) + carry)
        carry = carry + jnp.sum(cb, axis=1, keepdims=True)
    rank = jnp.concatenate(ranks, axis=1)
    padded = jnp.floor((carry + (SEG_ALIGN - 1)) * (1.0 / SEG_ALIGN)) * SEG_ALIGN
    seg_rows, run = [], jnp.zeros((1, 1), F32)
    for ex in range(N_EXPERTS):
        seg_rows.append(run)
        run = run + padded[ex:ex + 1]
    seg = jnp.concatenate(seg_rows, axis=0)
    dest = rank + seg
    d0_ref[0] = jnp.sum(fmat * dest, axis=0, keepdims=True).astype(jnp.int32)
    d1_ref[0] = jnp.sum(smat * dest, axis=0, keepdims=True).astype(jnp.int32)
    w0_ref[...] = jnp.broadcast_to(w_first, (LANES, tm)).T
    w1_ref[...] = jnp.broadcast_to(w_second, (LANES, tm)).T
    seg_ref[0] = jnp.broadcast_to(seg, (N_EXPERTS, LANES)).astype(jnp.int32)
    whole = jnp.floor(carry * (1.0 / MOE_CHUNK))
    rem = carry - whole * MOE_CHUNK
    n_full = whole + jnp.where(rem > MOE_CHUNK // 2, 1.0, 0.0)
    half_tail = jnp.where((rem > 0) & (rem <= MOE_CHUNK // 2), 1.0, 0.0)
    nch_ref[0] = jnp.broadcast_to(n_full, (N_EXPERTS, LANES)).astype(jnp.int32)
    tail_ref[0] = jnp.broadcast_to(half_tail, (N_EXPERTS, LANES)).astype(jnp.int32)


def _route(lt, tri):
    E, N = lt.shape
    tm = MOE_TILE
    nt = N // tm
    return pl.pallas_call(
        functools.partial(_route_kernel, tm=tm),
        grid=(nt,),
        in_specs=[pl.BlockSpec((E, tm), lambda i: (0, i)), pl.BlockSpec((TRI, TRI), lambda i: (0, 0))],
        out_specs=[
            pl.BlockSpec((1, 1, tm), lambda i: (i, 0, 0)),
            pl.BlockSpec((1, 1, tm), lambda i: (i, 0, 0)),
            pl.BlockSpec((tm, LANES), lambda i: (i, 0)),
            pl.BlockSpec((tm, LANES), lambda i: (i, 0)),
            pl.BlockSpec((1, E, LANES), lambda i: (i, 0, 0)),
            pl.BlockSpec((1, E, LANES), lambda i: (i, 0, 0)),
            pl.BlockSpec((1, E, LANES), lambda i: (i, 0, 0)),
        ],
        out_shape=[
            jax.ShapeDtypeStruct((nt, 1, tm), jnp.int32),
            jax.ShapeDtypeStruct((nt, 1, tm), jnp.int32),
            jax.ShapeDtypeStruct((N, LANES), F32),
            jax.ShapeDtypeStruct((N, LANES), F32),
            jax.ShapeDtypeStruct((nt, E, LANES), jnp.int32),
            jax.ShapeDtypeStruct((nt, E, LANES), jnp.int32),
            jax.ShapeDtypeStruct((nt, E, LANES), jnp.int32),
        ],
        compiler_params=_cparams(("arbitrary",)),
        name="route",
    )(lt, tri)


def _experts_kernel(seg_ref, nch_ref, tail_ref, d0_ref, d1_ref, u_ref, w1_ref, w3_ref, w2_ref, o_ref, xs_ref, *, tm,
                    chunk):
    i = pl.program_id(0)
    step = pl.program_id(1)

    @pl.when(step == 0)
    def _():
        xs_ref[...] = jnp.zeros_like(xs_ref)
        o_ref[...] = jnp.zeros_like(o_ref)

        def move(t, carry):
            row = u_ref[pl.ds(t, 1), :]
            xs_ref[pl.ds(d0_ref[0, 0, t], 1), :] = row
            xs_ref[pl.ds(d1_ref[0, 0, t], 1), :] = row
            return carry

        lax.fori_loop(0, tm, move, 0, unroll=8)

    for sub in range(EXPERTS_PER_STEP):
        slot = i * N_EXPERTS + step * EXPERTS_PER_STEP + sub
        start = seg_ref[slot]
        n_full = nch_ref[slot]

        def do_rows(r0, rows, sub=sub):
            lo, hi = _unpack_bf16_pairs(xs_ref[pl.ds(r0, rows), :])
            xb = jnp.concatenate([lo, hi], axis=-1).astype(BF16)
            a = _dot(xb, w1_ref[0, sub])
            hmid = a * _sigmoid(a) * _dot(xb, w3_ref[0, sub])
            o_ref[pl.ds(r0, rows), :] = _pack_bf16_pairs(_dot(hmid.astype(BF16), w2_ref[0, sub]))

        def do_chunk(j, carry, start=start, do_rows=do_rows):
            do_rows(pl.multiple_of(start + j * chunk, SEG_ALIGN), chunk)
            return carry

        lax.fori_loop(0, n_full, do_chunk, 0)

        @pl.when(tail_ref[slot] > 0)
        def _(start=start, n_full=n_full, do_rows=do_rows):
            do_rows(pl.multiple_of(start + n_full * chunk, SEG_ALIGN), chunk // 2)


def _experts(seg, nch, tail, d0, d1, u2p, w1, w3, w2, *, layer):
    N, half = u2p.shape
    D = 2 * half
    E = w1.shape[1]
    tm = MOE_TILE
    nt = N // tm
    kern = functools.partial(_experts_kernel, tm=tm, chunk=MOE_CHUNK)
    grid_spec = pltpu.PrefetchScalarGridSpec(
        num_scalar_prefetch=3,
        grid=(nt, E // EXPERTS_PER_STEP),
        in_specs=[
            pl.BlockSpec((1, 1, tm), lambda i, e, *_: (i, 0, 0), memory_space=pltpu.SMEM),
            pl.BlockSpec((1, 1, tm), lambda i, e, *_: (i, 0, 0), memory_space=pltpu.SMEM),
            pl.BlockSpec((tm, half), lambda i, e, *_: (i, 0)),
            pl.BlockSpec((1, EXPERTS_PER_STEP, D, D_EXPERT), lambda i, e, *_: (layer, e, 0, 0)),
            pl.BlockSpec((1, EXPERTS_PER_STEP, D, D_EXPERT), lambda i, e, *_: (layer, e, 0, 0)),
            pl.BlockSpec((1, EXPERTS_PER_STEP, D_EXPERT, D), lambda i, e, *_: (layer, e, 0, 0)),
        ],
        out_specs=pl.BlockSpec((MOE_ROWS, half), lambda i, e, *_: (i, 0)),
        scratch_shapes=[pltpu.VMEM((MOE_ROWS, half), jnp.uint32)],
    )
    return pl.pallas_call(
        kern,
        grid_spec=grid_spec,
        out_shape=jax.ShapeDtypeStruct((nt * MOE_ROWS, half), jnp.uint32),
        compiler_params=pltpu.CompilerParams(dimension_semantics=("arbitrary", "arbitrary"),
                                             vmem_limit_bytes=MOE_VMEM_LIMIT),
        name="experts",
    )(seg, nch, tail, d0, d1, u2p, w1, w3, w2)


def _combine_kernel(d0_ref, d1_ref, ys_ref, w0_ref, w1_ref, x1_ref, mod_ref, o_ref, g0_ref, g1_ref, *, tsub):
    base = pl.program_id(1) * tsub

    def move(t, carry):
        g0_ref[pl.ds(t, 1), :] = ys_ref[pl.ds(d0_ref[0, 0, base + t], 1), :]
        g1_ref[pl.ds(t, 1), :] = ys_ref[pl.ds(d1_ref[0, 0, base + t], 1), :]
        return carry

    lax.fori_loop(0, tsub, move, 0, unroll=8)
    lo0, hi0 = _unpack_bf16_pairs(g0_ref[...])
    lo1, hi1 = _unpack_bf16_pairs(g1_ref[...])
    half = g0_ref.shape[1]
    w0 = jnp.concatenate([w0_ref[...]] * (half // LANES), axis=1)
    w1 = jnp.concatenate([w1_ref[...]] * (half // LANES), axis=1)
    o_ref[:, :half] = x1_ref[:, :half] + mod_ref[0, 5:6, :half] * (w0 * lo0 + w1 * lo1)
    o_ref[:, half:] = x1_ref[:, half:] + mod_ref[0, 5:6, half:] * (w0 * hi0 + w1 * hi1)


def _combine(d0, d1, ys, w0rep, w1rep, x1, mod, *, seq):
    N, D = x1.shape
    half = D // 2
    tm = MOE_TILE
    tsub = 512
    nt = N // tm
    ns = tm // tsub
    sub = lambda i, j: (i * ns + j, 0)
    return pl.pallas_call(
        functools.partial(_combine_kernel, tsub=tsub),
        grid=(nt, ns),
        in_specs=[
            pl.BlockSpec((1, 1, tm), lambda i, j: (i, 0, 0), memory_space=pltpu.SMEM),
            pl.BlockSpec((1, 1, tm), lambda i, j: (i, 0, 0), memory_space=pltpu.SMEM),
            pl.BlockSpec((MOE_ROWS, half), lambda i, j: (i, 0)),
            pl.BlockSpec((tsub, LANES), sub),
            pl.BlockSpec((tsub, LANES), sub),
            pl.BlockSpec((tsub, D), sub),
            pl.BlockSpec((1, 6, D), lambda i, j: ((i * tm + j * tsub) // seq, 0, 0)),
        ],
        out_specs=pl.BlockSpec((tsub, D), sub),
        out_shape=jax.ShapeDtypeStruct((N, D), F32),
        scratch_shapes=[pltpu.VMEM((tsub, half), jnp.uint32), pltpu.VMEM((tsub, half), jnp.uint32)],
        compiler_params=_cparams(("arbitrary", "arbitrary")),
        name="combine",
    )(d0, d1, ys, w0rep, w1rep, x1, mod)


def _alibi_slopes(first, count):
    hh = jnp.arange(first + 1, first + count + 1, dtype=F32)
    return jnp.exp2(-8.0 * hh / N_HEADS_TOTAL)


def _bd_const():
    idx = np.arange(LANES) // HEAD_DIM
    return jnp.asarray((idx[:, None] == idx[None, :]).astype(np.float32), BF16)


def _blkcols_const(T):
    blk = np.arange(T) // SEL_LEN
    m = np.where(np.arange(LANES)[None, :] == blk[:, None], NEG_INF, 0.0).astype(np.float32)
    return jnp.asarray(m, BF16)


def _overlap_const(T):
    nc = T // CMP_STRIDE
    n_sel = T // SEL_LEN
    cs = np.arange(nc) * CMP_STRIDE
    ss = np.arange(n_sel) * SEL_LEN
    ov = np.clip(np.minimum(cs[:, None] + CMP_LEN, ss[None, :] + SEL_LEN)
                 - np.maximum(cs[:, None], ss[None, :]), 0, None) / CMP_LEN
    return jnp.asarray(ov.T.astype(np.float32))


def _expand_const():
    ex = np.zeros((N_NSA_BRANCHES, LANES, D_NSA), np.float32)
    for br in range(N_NSA_BRANCHES):
        for hd in range(N_HEADS_NSA):
            ex[br, hd * N_NSA_BRANCHES + br, hd * HEAD_DIM:(hd + 1) * HEAD_DIM] = 1.0
    return jnp.asarray(ex, BF16)


def _compact_w_in(w):
    parts = [w[..., a:b] for a, b in (_C_QN, _C_QS, _C_KSL, _C_KW, _C_KS, _C_VSL, _C_VW, _C_VS, _C_KC, _C_VC, _C_GATE)]
    pad = jnp.zeros(w.shape[:-1] + (LANES - (_C_GATE[1] - _C_GATE[0]),), w.dtype)
    return jnp.concatenate(parts + [pad], axis=-1).astype(BF16)


def _qconst(slopes_q):
    hi = slopes_q.astype(BF16).astype(F32)
    lo = (slopes_q - hi).astype(BF16).astype(F32)
    cols = jnp.stack([64.0 * hi, 64.0 * lo, hi, lo], axis=1)
    return jnp.zeros((slopes_q.shape[0], LANES), F32).at[:, HEAD_DIM:HEAD_DIM + 4].set(cols)


def kernel(x, c, w_router, ada_w, ada_b, norm1_g, norm2_g, w_in, w_out, nsa_q_gain, nsa_k_gain, cmp_pos_k,
           cmp_pos_v, cmp_w1_k, cmp_w2_k, cmp_w1_v, cmp_w2_v, swa_q_gain, swa_k_gain, swa_sinks, beta_nsa,
           beta_swa, moe_w1, moe_w3, moe_w2):
    B, T, D = x.shape
    L = ada_w.shape[0]
    N = B * T
    nc = T // CMP_STRIDE
    assert D == D_MODEL and T % 512 == 0 and T <= 64 * 128 and B <= 8

    slopes_nsa = _alibi_slopes(N_HEADS_SWA, N_HEADS_NSA)
    slopes_swa = _alibi_slopes(0, N_HEADS_SWA)
    qconst = _qconst(jnp.concatenate([slopes_nsa, slopes_swa]))
    bd = _bd_const()
    blkcols = _blkcols_const(T)
    ov = _overlap_const(T)
    expand = _expand_const()
    wr_t = w_router.T
    tri = jnp.asarray(np.triu(np.ones((TRI, TRI), np.float32), 1), BF16)
    assert N % MOE_TILE == 0
    moe_w1b, moe_w3b, moe_w2b = moe_w1.astype(BF16), moe_w3.astype(BF16), moe_w2.astype(BF16)
    w_outb = w_out.astype(BF16)
    w_inb = _compact_w_in(w_in)
    cmp_w1s = jnp.stack([cmp_w1_k, cmp_w1_v], axis=1).reshape(2 * L, CMP_LEN * HEAD_DIM, CMP_HIDDEN).astype(BF16)
    cmp_w2s = jnp.pad(jnp.stack([cmp_w2_k, cmp_w2_v], axis=1).reshape(2 * L, CMP_HIDDEN, HEAD_DIM),
                      ((0, 0), (0, 0), (0, LANES - HEAD_DIM))).astype(BF16)
    cmp_pos = jnp.stack([cmp_pos_k, cmp_pos_v], axis=1).reshape(2 * L, 1, CMP_LEN * HEAD_DIM)
    cmp_pos = jnp.broadcast_to(cmp_pos, (2 * L, 8, CMP_LEN * HEAD_DIM)).astype(BF16)
    zero_sinks = jnp.zeros((N_HEADS_NSA,), F32)

    c_pad = jnp.zeros((8, D), F32).at[:B].set(c)
    mod_all = _ada(c_pad, ada_w, ada_b)[:, :B].reshape(L, B, 6, D)

    x2 = x.reshape(N, D)
    for l in range(L):
        mod = mod_all[l]
        tile2 = lambda g: jnp.tile(g, 2)
        gain_c = jnp.concatenate([
            jnp.tile(nsa_q_gain[l], N_HEADS_NSA) * ATTN_SCALE,
            jnp.tile(swa_q_gain[l], N_HEADS_SWA) * ATTN_SCALE,
            tile2(nsa_k_gain[l, 1]), tile2(nsa_k_gain[l, 2]), tile2(swa_k_gain[l])]).reshape(1, -1)
        proj, xc = _inproj(x2, mod, norm1_g[l].reshape(1, D), w_inb, gain_c, qconst, bd, seq=T, layer=l)
        kgain = jnp.pad(nsa_k_gain[l, 0], (0, LANES - HEAD_DIM)).reshape(1, LANES)
        kvc = _cmpmlp(xc, cmp_w1s, cmp_w2s, cmp_pos, kgain, B=B, layer=l)

        o_cmp, notsel = _cmpattn(proj, kvc, ov, B=B, T=T)
        o_slc = _selattn(proj, notsel, blkcols, B=B, T=T)
        o_win = _band(slopes_nsa, zero_sinks, proj, B=B, T=T, window=NSA_WINDOW, q_slot=SLOT_QN,
                      k_slot=SLOT_KW, v_slot=SLOT_VW, use_sinks=False, n_sub=16, name="winattn")
        o_swa = _band(slopes_swa, swa_sinks[l], proj, B=B, T=T, window=SWA_WINDOW, q_slot=SLOT_QS,
                      k_slot=SLOT_KS, v_slot=SLOT_VS, use_sinks=True, n_sub=16, name="swaattn")

        x1, u2, lt = _outproj(o_cmp, o_slc, o_win, o_swa, proj, x2, mod, expand,
                              beta_nsa[l].reshape(1, -1), beta_swa[l].reshape(1, -1), w_outb,
                              norm2_g[l].reshape(1, D), wr_t, seq=T, layer=l)
        d0, d1, w0rep, w1rep, seg, nch, tail = _route(lt, tri)
        ys = _experts(seg[:, :, 0].reshape(-1), nch[:, :, 0].reshape(-1), tail[:, :, 0].reshape(-1), d0, d1, u2,
                      moe_w1b, moe_w3b, moe_w2b, layer=l)
        x2 = _combine(d0, d1, ys, w0rep, w1rep, x1, mod, seq=T)
    return x2.reshape(B, T, D)
```

```python
import functools

import numpy as np
import jax
import jax.numpy as jnp
from jax import lax
from jax.experimental import pallas as pl
from jax.experimental.pallas import tpu as pltpu

D_MODEL = 1024
DEPTH = 2
HEAD_DIM = 64
N_HEADS_NSA = 8
N_KV_NSA = 2
G_NSA = N_HEADS_NSA // N_KV_NSA
N_HEADS_SWA = 8
N_KV_SWA = 2
G_SWA = N_HEADS_SWA // N_KV_SWA
D_NSA = N_HEADS_NSA * HEAD_DIM
D_SWA = N_HEADS_SWA * HEAD_DIM
D_MIX = D_NSA + D_SWA
N_HEADS_TOTAL = N_HEADS_NSA + N_HEADS_SWA
KV_NSA = N_KV_NSA * HEAD_DIM
KV_SWA = N_KV_SWA * HEAD_DIM
N_NSA_BRANCHES = 3
CMP_LEN = 32
CMP_STRIDE = 16
CMP_HIDDEN = 256
SEL_LEN = 64
SEL_TOPK = 16
NSA_WINDOW = 512
SWA_WINDOW = 128
N_EXPERTS = 16
N_GROUPS = 4
EXPERTS_PER_GROUP = N_EXPERTS // N_GROUPS
TOP_K = 2
D_EXPERT = 512
EPS = 1e-6
NEG_INF = -1e30
ATTN_SCALE = HEAD_DIM ** -0.5

F32 = jnp.float32
BF16 = jnp.bfloat16

LANES = 128
VMEM_LIMIT = 48 * 1024 * 1024

MOE_TILE = 2048
MOE_CHUNK = 256
SEG_ALIGN = 16
TRI = 256
MOE_ROWS = TOP_K * MOE_TILE + N_EXPERTS * SEG_ALIGN + MOE_CHUNK
EXPERTS_PER_STEP = 2
MOE_VMEM_LIMIT = 56 * 1024 * 1024

SLOT_QN = 0
SLOT_QS = 8
SLOT_KSL = 16
SLOT_KW = 18
SLOT_KS = 20
SLOT_VSL = 22
SLOT_VW = 24
SLOT_VS = 26
SLOT_GATE = 28
N_SLOTS = 29
N_CHUNKS = 17
N_NORM_CHUNKS = 11

_C_QN = (0, 512)
_C_KC = (512, 640)
_C_VC = (640, 768)
_C_KSL = (768, 896)
_C_VSL = (896, 1024)
_C_KW = (1024, 1152)
_C_VW = (1152, 1280)
_C_GATE = (1280, 1304)
_C_QS = (1304, 1816)
_C_KS = (1816, 1944)
_C_VS = (1944, 2072)


def _cparams(sem):
    return pltpu.CompilerParams(dimension_semantics=sem, vmem_limit_bytes=VMEM_LIMIT)


def _dot(a, b, **kw):
    return jnp.dot(a, b, preferred_element_type=F32, **kw)


def _dot_nt(a, b):
    return lax.dot_general(a, b, (((1,), (1,)), ((), ())), preferred_element_type=F32)


def _sigmoid(x):
    return 1.0 / (1.0 + jnp.exp(-x))


def _pack_bf16_pairs(x):
    n = x.shape[1] // 2
    return pltpu.pack_elementwise([x[:, :n], x[:, n:]], packed_dtype=BF16)


def _unpack_bf16_pairs(w):
    lo = pltpu.unpack_elementwise(w, index=0, packed_dtype=BF16, unpacked_dtype=F32)
    hi = pltpu.unpack_elementwise(w, index=1, packed_dtype=BF16, unpacked_dtype=F32)
    return lo, hi


def _ada_kernel(c_ref, w_ref, b_ref, o_ref):
    c = c_ref[...]
    cond = c * _sigmoid(c)
    o_ref[0] = _dot(cond.astype(BF16), w_ref[0].astype(BF16)) + b_ref[0]


def _ada(c_pad, ada_w, ada_b):
    L, D, N6 = ada_w.shape
    tn = 1536
    return pl.pallas_call(
        _ada_kernel,
        grid=(L, N6 // tn),
        in_specs=[
            pl.BlockSpec((8, D), lambda l, j: (0, 0)),
            pl.BlockSpec((1, D, tn), lambda l, j: (l, 0, j)),
            pl.BlockSpec((1, 1, tn), lambda l, j: (l, 0, j)),
        ],
        out_specs=pl.BlockSpec((1, 8, tn), lambda l, j: (l, 0, j)),
        out_shape=jax.ShapeDtypeStruct((L, 8, N6), F32),
        compiler_params=_cparams(("arbitrary", "arbitrary")),
        name="ada",
    )(c_pad, ada_w, ada_b.reshape(L, 1, N6))


def _inproj_kernel(x_ref, mod_ref, g_ref, w_ref, gain_ref, qc_ref, bd_ref, o_ref, xc_ref, scr_ref, *, tm, seq):
    i = pl.program_id(0)
    x = x_ref[...]
    ms = jnp.mean(x * x, axis=-1, keepdims=True)
    u = x * lax.rsqrt(ms + EPS) * g_ref[...]
    u = u * (1.0 + mod_ref[0, 1:2, :]) + mod_ref[0, 0:1, :]
    y = _dot(u.astype(BF16), w_ref[0])

    lane = lax.broadcasted_iota(jnp.int32, (tm, LANES), 1)
    low = lane < HEAD_DIM
    t = lax.broadcasted_iota(jnp.int32, (tm, LANES), 0) + lax.rem(i * tm, seq)
    hi_part = (t >> 6).astype(F32)
    lo_part = (t & 63).astype(F32)
    poscols = jnp.where((lane == 64) | (lane == 65), hi_part,
                        jnp.where((lane == 66) | (lane == 67), lo_part, 0.0))
    bd = bd_ref[...]

    def put(slot, val):
        o_ref[:, slot * LANES:(slot + 1) * LANES] = val.astype(o_ref.dtype)

    normed = {}
    for c0 in range(0, N_NORM_CHUNKS, 2):
        width = min(2, N_NORM_CHUNKS - c0) * LANES
        wide = y[:, c0 * LANES:c0 * LANES + width]
        msb = _dot((wide * wide).astype(BF16), bd[:width, :width]) * (1.0 / HEAD_DIM)
        wide = wide * lax.rsqrt(msb + EPS) * gain_ref[:, c0 * LANES:c0 * LANES + width]
        for k in range(width // LANES):
            normed[c0 + k] = wide[:, k * LANES:(k + 1) * LANES]

    for c in range(N_CHUNKS):
        blk = normed[c] if c < N_NORM_CHUNKS else y[:, c * LANES:(c + 1) * LANES]
        if c < 14:
            rolled = pltpu.roll(blk, HEAD_DIM, axis=1)
            if c < 8:
                f0 = qc_ref[2 * c:2 * c + 1, :]
                f1 = qc_ref[2 * c + 1:2 * c + 2, :]
                s0 = 2 * c
            elif c < N_NORM_CHUNKS:
                f0 = f1 = poscols
                s0 = SLOT_KSL + 2 * (c - 8)
            else:
                f0 = f1 = 1.0
                s0 = SLOT_VSL + 2 * (c - N_NORM_CHUNKS)
            put(s0, jnp.where(low, blk, f0))
            put(s0 + 1, jnp.where(low, rolled, f1))
        elif c == 16:
            put(SLOT_GATE, blk)
        else:
            scr_ref[...] = blk
            nck = tm // CMP_STRIDE
            low_c = lax.broadcasted_iota(jnp.int32, (nck, LANES), 1) < HEAD_DIM
            head_cols = CMP_STRIDE * HEAD_DIM
            for pair in range(CMP_STRIDE // 2):
                ta = scr_ref[pl.ds(2 * pair, nck, stride=CMP_STRIDE), :]
                tb = scr_ref[pl.ds(2 * pair + 1, nck, stride=CMP_STRIDE), :]
                h0 = jnp.where(low_c, ta, pltpu.roll(tb, HEAD_DIM, axis=1))
                h1 = jnp.where(low_c, pltpu.roll(ta, HEAD_DIM, axis=1), tb)
                xc_ref[c - 14, :, pair * LANES:(pair + 1) * LANES] = h0.astype(xc_ref.dtype)
                xc_ref[c - 14, :, head_cols + pair * LANES:head_cols + (pair + 1) * LANES] = h1.astype(xc_ref.dtype)


def _inproj(x2, mod, norm_g, w_c, gain_c, qconst, bd, *, seq, layer):
    N, D = x2.shape
    tm = 512
    tiles_per_seq = seq // tm
    kern = functools.partial(_inproj_kernel, tm=tm, seq=seq)
    nchunk = tm // CMP_STRIDE
    xc_shape = jax.ShapeDtypeStruct((2, N // CMP_STRIDE, CMP_STRIDE * LANES), BF16)
    return pl.pallas_call(
        kern,
        grid=(N // tm,),
        in_specs=[
            pl.BlockSpec((tm, D), lambda i: (i, 0)),
            pl.BlockSpec((1, 6, D), lambda i: (i // tiles_per_seq, 0, 0)),
            pl.BlockSpec((1, D), lambda i: (0, 0)),
            pl.BlockSpec((1, D, N_CHUNKS * LANES), lambda i: (layer, 0, 0)),
            pl.BlockSpec((1, N_NORM_CHUNKS * LANES), lambda i: (0, 0)),
            pl.BlockSpec((16, LANES), lambda i: (0, 0)),
            pl.BlockSpec((2 * LANES, 2 * LANES), lambda i: (0, 0)),
        ],
        out_specs=[
            pl.BlockSpec((tm, N_SLOTS * LANES), lambda i: (i, 0)),
            pl.BlockSpec((2, nchunk, CMP_STRIDE * LANES), lambda i: (0, i, 0)),
        ],
        out_shape=[jax.ShapeDtypeStruct((N, N_SLOTS * LANES), BF16), xc_shape],
        scratch_shapes=[pltpu.VMEM((tm, LANES), F32)],
        compiler_params=_cparams(("arbitrary",)),
        name="inproj",
    )(x2, mod, norm_g, w_c, gain_c, qconst, bd)


def _gelu_tanh(x):
    return 0.5 * x * (1.0 + jnp.tanh(np.sqrt(2.0 / np.pi).astype(np.float32) * (x + 0.044715 * (x * x * x))))


def _cmpmlp_kernel(x_ref, w1_ref, w2_ref, pos_ref, gain_ref, o_ref, *, nc):
    kv = pl.program_id(0)
    half = CMP_STRIDE * HEAD_DIM
    bias = _dot(pos_ref[0], w1_ref[0])[0:1]
    lane = lax.broadcasted_iota(jnp.int32, (nc, LANES), 1)
    last = lax.broadcasted_iota(jnp.int32, (nc, LANES), 0) * CMP_STRIDE + (CMP_LEN - 1)
    poscols = jnp.where((lane == 64) | (lane == 65), (last >> 6).astype(F32),
                        jnp.where((lane == 66) | (lane == 67), (last & 63).astype(F32), 0.0))
    for hd in range(N_KV_NSA):
        xc = x_ref[0, :, hd * half:(hd + 1) * half]
        first = _dot(xc, w1_ref[0, :half, :])
        second = _dot(xc, w1_ref[0, half:, :])
        h = _gelu_tanh(first + pltpu.roll(second, nc - 1, axis=0) + bias)
        z = _dot(h.astype(BF16), w2_ref[0])
        zn = z * lax.rsqrt(jnp.sum(z * z, axis=-1, keepdims=True) * (1.0 / HEAD_DIM) + EPS) * gain_ref[...]
        o_ref[0, 0, hd] = jnp.where(kv == 0, zn + poscols, z + jnp.where(lane >= HEAD_DIM, 1.0, 0.0))


def _cmpmlp(xc, w1s, w2s, pos, gain, *, B, layer):
    _, nchunks, K = xc.shape
    nc = nchunks // B
    win = CMP_LEN * HEAD_DIM
    kern = functools.partial(_cmpmlp_kernel, nc=nc)
    return pl.pallas_call(
        kern,
        grid=(2, B),
        in_specs=[
            pl.BlockSpec((1, nc, K), lambda kv, b: (kv, b, 0)),
            pl.BlockSpec((1, win, CMP_HIDDEN), lambda kv, b: (2 * layer + kv, 0, 0)),
            pl.BlockSpec((1, CMP_HIDDEN, LANES), lambda kv, b: (2 * layer + kv, 0, 0)),
            pl.BlockSpec((1, 8, win), lambda kv, b: (2 * layer + kv, 0, 0)),
            pl.BlockSpec((1, LANES), lambda kv, b: (0, 0)),
        ],
        out_specs=pl.BlockSpec((1, 1, N_KV_NSA, nc, LANES), lambda kv, b: (kv, b, 0, 0, 0)),
        out_shape=jax.ShapeDtypeStruct((2, B, N_KV_NSA, nc, LANES), F32),
        compiler_params=_cparams(("arbitrary", "arbitrary")),
        name="cmpmlp",
    )(xc, w1s, w2s, pos, gain)


def _normalize_pack(pvs, extra=None):
    lane = lax.broadcasted_iota(jnp.int32, pvs[0].shape, 1)
    low = lane < HEAD_DIM
    pairs = []
    for p in range(len(pvs) // 2):
        even, odd = pvs[2 * p], pvs[2 * p + 1]
        den_e = even if extra is None else even + extra[2 * p]
        den_o = odd if extra is None else odd + extra[2 * p + 1]
        o_e = even * (1.0 / pltpu.roll(den_e, HEAD_DIM, axis=1))
        o_o = pltpu.roll(odd, HEAD_DIM, axis=1) * (1.0 / den_o)
        pairs.append(jnp.where(low, o_e, o_o))
    return jnp.concatenate(pairs, axis=-1)


def _cmpattn_kernel(q_ref, kc_ref, vc_ref, ov_ref, o_alias, ns_alias, o_ref, ns_ref, *, tq, n_sub, nc, n_cmp, n_sel,
                    top, tile0):
    del o_alias, ns_alias
    kc = kc_ref[0, 0, 0].astype(BF16)
    vc = vc_ref[0, 0, 0].astype(BF16)
    for sub in range(n_sub):
        q0 = ((tile0 + pl.program_id(2)) * n_sub + sub) * tq
        rows = pl.ds(sub * tq, tq)
        _cmpattn_subtile(q_ref.at[rows, :], kc, vc, ov_ref, o_ref.at[rows, :], ns_ref.at[0, 0, rows, :], q0,
                         tq=tq, nc=nc, n_cmp=n_cmp, n_sel=n_sel, top=top)


def _cmpattn_subtile(q_ref, kc, vc, ov_ref, o_ref, ns_ref, q0, *, tq, nc, n_cmp, n_sel, top):
    t_col = q0 + lax.broadcasted_iota(jnp.int32, (tq, 1), 0)
    row_valid = (t_col >= CMP_LEN - 1).astype(F32)
    t_row = q0 + lax.broadcasted_iota(jnp.int32, (1, tq), 1)
    n_col = lax.broadcasted_iota(jnp.int32, (nc, 1), 0)
    valid_t = (t_row - (n_col * CMP_STRIDE + (CMP_LEN - 1)) >= 0) & (n_col < n_cmp)
    bias_t = jnp.where(valid_t, 0.0, NEG_INF)
    row_valid_t = (t_row >= CMP_LEN - 1).astype(F32)

    outs = []
    psum_t = jnp.zeros((nc, tq), F32)
    for g in range(G_NSA):
        qg = q_ref[:, g * LANES:(g + 1) * LANES]
        st = _dot_nt(kc, qg) + bias_t
        et = jnp.exp(st - jnp.max(st, axis=0, keepdims=True))
        outs.append(lax.dot_general(et.astype(BF16), vc, (((0,), (0,)), ((), ())), preferred_element_type=F32))
        psum_t = psum_t + et * (1.0 / jnp.sum(et, axis=0, keepdims=True))
    o_ref[...] = (_normalize_pack(outs) * row_valid).astype(o_ref.dtype)

    psum_t = psum_t * row_valid_t
    imp_t = _dot(ov_ref[...], psum_t, precision=lax.Precision.HIGHEST)
    j = lax.broadcasted_iota(jnp.int32, (n_sel, 1), 0)
    cur = t_row >> 6
    forced = (j == 0) | (j == cur) | (j == cur - 1)
    v = jnp.where(forced, 1e9, jnp.where(j > cur, NEG_INF, imp_t))
    n_grp = n_sel // 8
    vg = [v[8 * r:8 * r + 8, :] for r in range(n_grp)]
    jg = lax.broadcasted_iota(jnp.int32, (8, 1), 0)
    cnt = [jnp.zeros((8, tq), F32) for _ in range(n_grp)]
    for jp in range(n_sel):
        vj = v[jp:jp + 1, :]
        for r in range(n_grp):
            ge = jnp.where(vj >= vg[r], 1.0, 0.0)
            gt = jnp.where(vj > vg[r], 1.0, 0.0)
            if 8 * r > jp:
                inc = ge
            elif 8 * r + 7 < jp:
                inc = gt
            else:
                inc = jnp.where(jg + 8 * r > jp, ge, gt)
            cnt[r] = cnt[r] + inc
    notsel_t = jnp.concatenate([jnp.where(cn < top, 0.0, 1.0) for cn in cnt], axis=0)
    if n_sel < LANES:
        notsel_t = jnp.concatenate([notsel_t, jnp.zeros((LANES - n_sel, tq), F32)], axis=0)
    ns_ref[...] = notsel_t.T.astype(ns_ref.dtype)


def _cmpattn(proj, kvc, ov, *, B, T):
    tq = 256
    nt = T // tq
    nc = T // CMP_STRIDE
    n_cmp = (T - CMP_LEN) // CMP_STRIDE + 1
    n_sel = T // SEL_LEN
    top = min(SEL_TOPK, n_sel)
    assert n_sel <= LANES and n_sel % 8 == 0
    n_parts = 4 if (nt % 4 == 0 and n_sel % 32 == 0) else 1
    n_sub = 4 if (nt // n_parts) % 4 == 0 else 1
    tb = n_sub * tq
    steps = nt // n_sub
    sp = steps // n_parts
    o_cmp = jnp.zeros((B * T, D_NSA), BF16)
    notsel = jnp.zeros((B, N_KV_NSA, T, LANES), BF16)
    for part in range(n_parts):
        nc_p = nc * (part + 1) // n_parts
        ns_p = n_sel * (part + 1) // n_parts
        kern = functools.partial(_cmpattn_kernel, tq=tq, n_sub=n_sub, nc=nc_p, n_cmp=n_cmp, n_sel=ns_p, top=top,
                                 tile0=part * sp)
        qrow = lambda b, h, i, part=part: (b * steps + part * sp + i, h)
        o_cmp, notsel = pl.pallas_call(
            kern,
            grid=(B, N_KV_NSA, sp),
            in_specs=[
                pl.BlockSpec((tb, G_NSA * LANES), qrow),
                pl.BlockSpec((1, 1, 1, nc_p, LANES), lambda b, h, i: (0, b, h, 0, 0)),
                pl.BlockSpec((1, 1, 1, nc_p, LANES), lambda b, h, i: (1, b, h, 0, 0)),
                pl.BlockSpec((ns_p, nc_p), lambda b, h, i: (0, 0)),
                pl.BlockSpec(memory_space=pl.ANY),
                pl.BlockSpec(memory_space=pl.ANY),
            ],
            out_specs=[
                pl.BlockSpec((tb, G_NSA * HEAD_DIM), qrow),
                pl.BlockSpec((1, 1, tb, LANES), lambda b, h, i, part=part: (b, h, part * sp + i, 0)),
            ],
            out_shape=[
                jax.ShapeDtypeStruct((B * T, D_NSA), BF16),
                jax.ShapeDtypeStruct((B, N_KV_NSA, T, LANES), BF16),
            ],
            input_output_aliases={4: 0, 5: 1},
            compiler_params=_cparams(("arbitrary", "arbitrary", "arbitrary")),
            name="cmpattn",
        )(proj, kvc, kvc, ov[:ns_p, :nc_p], o_cmp, notsel)
    return o_cmp, notsel


def _selattn_kernel(q_ref, ns_ref, k_ref, v_ref, bc_ref, o_ref, qx_ref, s0_ref, s1_ref, mx0_ref, mx1_ref, m_ref,
                    acc_ref, *, tq, tk):
    diag = pl.program_id(2)
    ns = ns_ref[0, 0]
    for g in range(G_NSA):
        qx_ref[g * tq:(g + 1) * tq, :] = jnp.concatenate([q_ref[:, g * LANES:(g + 1) * LANES], ns], axis=-1)
    m_ref[...] = jnp.full(m_ref.shape, NEG_INF, F32)
    acc_ref[...] = jnp.zeros(acc_ref.shape, F32)

    def produce(s_ref, mx_ref, kt, masked):
        start = pl.multiple_of(kt * tk, tk)
        kx = jnp.concatenate([k_ref[pl.ds(start, tk), :], bc_ref[pl.ds(start, tk), :]], axis=-1)
        s = _dot_nt(qx_ref[...], kx)
        if masked:
            r = lax.broadcasted_iota(jnp.int32, (tq, tk), 0)
            col = lax.broadcasted_iota(jnp.int32, (tq, tk), 1)
            s = s + jnp.concatenate([jnp.where(col <= r, 0.0, NEG_INF)] * G_NSA, axis=0)
        s_ref[...] = s
        mx_ref[...] = jnp.broadcast_to(jnp.max(s, axis=-1, keepdims=True), mx_ref.shape)

    def consume(s_ref, mx_ref, kt):
        m = m_ref[...]
        m_new = jnp.maximum(m, mx_ref[...])
        p = jnp.exp(s_ref[...] - jnp.concatenate([m_new] * (tk // LANES), axis=1)).astype(BF16)
        vx = v_ref[pl.ds(pl.multiple_of(kt * tk, tk), tk), :]
        acc_ref[...] = jnp.exp(m - m_new) * acc_ref[...] + _dot(p, vx)
        m_ref[...] = m_new

    def tile(k):
        return k - 1

    produce(s0_ref, mx0_ref, diag, True)

    def pair(jj, carry):
        k = 2 * jj
        produce(s1_ref, mx1_ref, tile(k + 1), False)
        consume(s0_ref, mx0_ref, jnp.where(k == 0, diag, tile(k)))
        produce(s0_ref, mx0_ref, tile(k + 2), False)
        consume(s1_ref, mx1_ref, tile(k + 1))
        return carry

    n_pairs = lax.shift_right_logical(diag, 1)
    lax.fori_loop(0, n_pairs, pair, 0)
    k_last = 2 * n_pairs

    @pl.when(k_last == diag)
    def _():
        consume(s0_ref, mx0_ref, jnp.where(diag == 0, diag, tile(diag)))

    @pl.when(k_last != diag)
    def _():
        produce(s1_ref, mx1_ref, tile(diag), False)
        consume(s0_ref, mx0_ref, jnp.where(k_last == 0, diag, tile(k_last)))
        consume(s1_ref, mx1_ref, tile(diag))

    acc = acc_ref[...]
    o_ref[...] = _normalize_pack([acc[g * tq:(g + 1) * tq] for g in range(G_NSA)]).astype(o_ref.dtype)


def _selattn(proj, notsel, blkcols, *, B, T):
    tq = 512
    tk = 512
    nt = T // tq
    m_rows = G_NSA * tq
    assert T % tk == 0 and tk == tq
    kern = functools.partial(_selattn_kernel, tq=tq, tk=tk)
    return pl.pallas_call(
        kern,
        grid=(B, N_KV_NSA, nt),
        in_specs=[
            pl.BlockSpec((tq, G_NSA * LANES), lambda b, h, i: (b * nt + i, h)),
            pl.BlockSpec((1, 1, tq, LANES), lambda b, h, i: (b, h, i, 0)),
            pl.BlockSpec((T, LANES), lambda b, h, i: (b, SLOT_KSL + h)),
            pl.BlockSpec((T, LANES), lambda b, h, i: (b, SLOT_VSL + h)),
            pl.BlockSpec((T, LANES), lambda b, h, i: (0, 0)),
        ],
        out_specs=pl.BlockSpec((tq, G_NSA * HEAD_DIM), lambda b, h, i: (b * nt + i, h)),
        out_shape=jax.ShapeDtypeStruct((B * T, D_NSA), BF16),
        scratch_shapes=[
            pltpu.VMEM((m_rows, 2 * LANES), BF16),
            pltpu.VMEM((m_rows, tk), F32),
            pltpu.VMEM((m_rows, tk), F32),
            pltpu.VMEM((m_rows, LANES), F32),
            pltpu.VMEM((m_rows, LANES), F32),
            pltpu.VMEM((m_rows, LANES), F32),
            pltpu.VMEM((m_rows, LANES), F32),
        ],
        compiler_params=_cparams(("arbitrary", "arbitrary", "arbitrary")),
        name="selattn",
    )(proj, notsel, proj, proj, blkcols)


def _band_kernel(sl_ref, sink_ref, q_ref, k_ref, v_ref, o_ref, *, tq, n_sub, window, seq, use_sinks):
    h = pl.program_id(1)
    span = window + tq
    groups = q_ref.shape[1] // LANES
    r = lax.broadcasted_iota(jnp.int32, (tq, span), 0)
    col = lax.broadcasted_iota(jnp.int32, (tq, span), 1)
    for sub in range(n_sub):
        q0 = (pl.program_id(2) * n_sub + sub) * tq
        start = pl.multiple_of(jnp.clip(q0 - window, 0, seq - span), tq)
        rows = pl.ds(sub * tq, tq)
        qx = jnp.concatenate([q_ref[rows, g * LANES:(g + 1) * LANES] for g in range(groups)], axis=0)
        d = (q0 - start) + r - col
        bias = jnp.where((d >= 0) & (d < window), 0.0, NEG_INF)
        s = _dot_nt(qx, k_ref[pl.ds(start, span), :]) + jnp.concatenate([bias] * groups, axis=0)
        m = jnp.broadcast_to(jnp.max(s, axis=-1, keepdims=True), (groups * tq, LANES))
        if use_sinks:
            t_rep = (q0 + lax.broadcasted_iota(jnp.int32, (tq, LANES), 0)).astype(F32)
            sink = jnp.concatenate(
                [sink_ref[h * groups + g] + sl_ref[h * groups + g] * t_rep for g in range(groups)], axis=0)
            m = jnp.maximum(m, sink)
        e = jnp.exp(s - jnp.concatenate([m] * (span // LANES), axis=1)).astype(BF16)
        pv = _dot(e, v_ref[pl.ds(start, span), :])
        extra = None
        if use_sinks:
            esink = jnp.exp(sink - m)
            extra = [esink[g * tq:(g + 1) * tq] for g in range(groups)]
        o_ref[rows, :] = _normalize_pack([pv[g * tq:(g + 1) * tq] for g in range(groups)], extra).astype(o_ref.dtype)


def _band(slopes, sinks, proj, *, B, T, window, q_slot, k_slot, v_slot, use_sinks, n_sub, name):
    tq = 128
    n_sub = min(n_sub, T // tq)
    tb = tq * n_sub
    nt = T // tb
    groups = 4
    assert window % tq == 0 and window + tq <= T and T % tb == 0
    qb = q_slot // groups
    kern = functools.partial(_band_kernel, tq=tq, n_sub=n_sub, window=window, seq=T, use_sinks=use_sinks)
    return pl.pallas_call(
        kern,
        grid=(B, 2, nt),
        in_specs=[
            pl.BlockSpec(memory_space=pltpu.SMEM),
            pl.BlockSpec(memory_space=pltpu.SMEM),
            pl.BlockSpec((tb, groups * LANES), lambda b, h, i: (b * nt + i, qb + h)),
            pl.BlockSpec((T, LANES), lambda b, h, i: (b, k_slot + h)),
            pl.BlockSpec((T, LANES), lambda b, h, i: (b, v_slot + h)),
        ],
        out_specs=pl.BlockSpec((tb, groups * HEAD_DIM), lambda b, h, i: (b * nt + i, h)),
        out_shape=jax.ShapeDtypeStruct((B * T, 2 * groups * HEAD_DIM), BF16),
        compiler_params=_cparams(("arbitrary", "arbitrary", "arbitrary")),
        name=name,
    )(slopes, sinks, proj, proj, proj)


def _outproj_kernel(ocmp_ref, oslc_ref, owin_ref, oswa_ref, gate_ref, x_ref, mod_ref, ex_ref, bn_ref, bs_ref,
                    w_ref, g2_ref, wr_ref, x1_ref, u2_ref, lt_ref):
    sg = _sigmoid(gate_ref[...].astype(F32)).astype(BF16)
    o_nsa = (_dot(sg, ex_ref[0]) * ocmp_ref[...].astype(F32)
             + _dot(sg, ex_ref[1]) * oslc_ref[...].astype(F32)
             + _dot(sg, ex_ref[2]) * owin_ref[...].astype(F32))
    o_swa = oswa_ref[...].astype(F32)
    n1 = o_nsa * lax.rsqrt(jnp.mean(o_nsa * o_nsa, axis=-1, keepdims=True) + EPS) * bn_ref[...]
    n2 = o_swa * lax.rsqrt(jnp.mean(o_swa * o_swa, axis=-1, keepdims=True) + EPS) * bs_ref[...]
    merged = jnp.concatenate([n1, n2], axis=-1).astype(BF16)
    y = _dot(merged, w_ref[0])
    x1 = x_ref[...] + mod_ref[0, 2:3, :] * y
    x1_ref[...] = x1
    u2 = x1 * lax.rsqrt(jnp.mean(x1 * x1, axis=-1, keepdims=True) + EPS) * g2_ref[...]
    u2 = u2 * (1.0 + mod_ref[0, 4:5, :]) + mod_ref[0, 3:4, :]
    u2_ref[...] = _pack_bf16_pairs(u2)
    u_hi = u2.astype(BF16)
    u_lo = (u2 - u_hi.astype(F32)).astype(BF16)
    wr = wr_ref[...]
    w_hi = wr.astype(BF16)
    w_lo = (wr - w_hi.astype(F32)).astype(BF16)
    lt_ref[...] = _dot_nt(w_hi, u_hi) + _dot_nt(w_hi, u_lo) + _dot_nt(w_lo, u_hi)


def _outproj(ocmp, oslc, owin, oswa, proj, x2, mod, expand, beta_n, beta_s, w_out, norm2_g, wr_t, *, seq, layer):
    N, D = x2.shape
    tm = 512
    tiles_per_seq = seq // tm
    row = lambda i: (i, 0)
    const2 = lambda i: (0, 0)
    return pl.pallas_call(
        _outproj_kernel,
        grid=(N // tm,),
        in_specs=[
            pl.BlockSpec((tm, D_NSA), row),
            pl.BlockSpec((tm, D_NSA), row),
            pl.BlockSpec((tm, D_NSA), row),
            pl.BlockSpec((tm, D_SWA), row),
            pl.BlockSpec((tm, LANES), lambda i: (i, SLOT_GATE)),
            pl.BlockSpec((tm, D), row),
            pl.BlockSpec((1, 6, D), lambda i: (i // tiles_per_seq, 0, 0)),
            pl.BlockSpec((N_NSA_BRANCHES, LANES, D_NSA), lambda i: (0, 0, 0)),
            pl.BlockSpec((1, D_NSA), const2),
            pl.BlockSpec((1, D_SWA), const2),
            pl.BlockSpec((1, D_MIX, D), lambda i: (layer, 0, 0)),
            pl.BlockSpec((1, D), const2),
            pl.BlockSpec((N_EXPERTS, D), const2),
        ],
        out_specs=[
            pl.BlockSpec((tm, D), row),
            pl.BlockSpec((tm, D // 2), row),
            pl.BlockSpec((N_EXPERTS, tm), lambda i: (0, i)),
        ],
        out_shape=[
            jax.ShapeDtypeStruct((N, D), F32),
            jax.ShapeDtypeStruct((N, D // 2), jnp.uint32),
            jax.ShapeDtypeStruct((N_EXPERTS, N), F32),
        ],
        compiler_params=_cparams(("arbitrary",)),
        name="outproj",
    )(ocmp, oslc, owin, oswa, proj, x2, mod, expand, beta_n, beta_s, w_out, norm2_g, wr_t)


def _route_kernel(lt_ref, tri_ref, d0_ref, d1_ref, w0_ref, w1_ref, seg_ref, nch_ref, tail_ref, *, tm):
    lt = lt_ref[...]
    e = jnp.exp(lt - jnp.max(lt, axis=0, keepdims=True))
    aff = e / jnp.sum(e, axis=0, keepdims=True)
    rows = [aff[k:k + 1, :] for k in range(N_EXPERTS)]
    epg = EXPERTS_PER_GROUP
    scores = []
    for gr in range(N_GROUPS):
        xs = rows[gr * epg:(gr + 1) * epg]
        top1 = functools.reduce(jnp.maximum, xs)
        second = None
        for a in range(epg):
            for b in range(a + 1, epg):
                mn = jnp.minimum(xs[a], xs[b])
                second = mn if second is None else jnp.maximum(second, mn)
        scores.append(top1 + second)
    taken = None
    in_group = []
    for gr in range(N_GROUPS):
        best = None
        for o in range(gr + 1, N_GROUPS):
            c = scores[gr] >= scores[o]
            best = c if best is None else (best & c)
        if best is None:
            best = jnp.full(scores[gr].shape, True)
        sel = best if taken is None else (best & ~taken)
        taken = sel if taken is None else (taken | sel)
        in_group.append(sel)
    ys = []
    for k in range(epg):
        yk = rows[k]
        for gr in range(1, N_GROUPS):
            yk = jnp.where(in_group[gr], rows[gr * epg + k], yk)
        ys.append(yk)
    chosen = []
    for k in range(epg):
        rk = jnp.zeros_like(ys[k])
        for o in range(epg):
            if o == k:
                continue
            before = (ys[o] >= ys[k]) if o < k else (ys[o] > ys[k])
            rk = rk + jnp.where(before, 1.0, 0.0)
        chosen.append(rk < TOP_K)
    total = functools.reduce(lambda a, b: a + b, [jnp.where(chosen[k], ys[k], 0.0) for k in range(epg)])
    first, seen = [], None
    for k in range(epg):
        first.append(chosen[k] if seen is None else (chosen[k] & ~seen))
        seen = chosen[k] if seen is None else (seen | chosen[k])
    second = [chosen[k] & ~first[k] for k in range(epg)]
    inv_total = 1.0 / total
    w_first = functools.reduce(lambda a, b: a + b, [jnp.where(first[k], ys[k], 0.0) for k in range(epg)]) * inv_total
    w_second = functools.reduce(lambda a, b: a + b, [jnp.where(second[k], ys[k], 0.0) for k in range(epg)]) * inv_total
    f_rows, s_rows = [], []
    for ex in range(N_EXPERTS):
        gr, k = divmod(ex, epg)
        f_rows.append(jnp.where(in_group[gr] & first[k], 1.0, 0.0))
        s_rows.append(jnp.where(in_group[gr] & second[k], 1.0, 0.0))
    fmat = jnp.concatenate(f_rows, axis=0)
    smat = jnp.concatenate(s_rows, axis=0)
    cmat = fmat + smat
    carry = jnp.zeros((N_EXPERTS, 1), F32)
    ranks = []
    for blk in range(tm // TRI):
        cb = cmat[:, blk * TRI:(blk + 1) * TRI]
        ranks.append(_dot(cb.astype(BF16), tri_ref[...]) + carry)
        carry = carry + jnp.sum(cb, axis=1, keepdims=True)
    rank = jnp.concatenate(ranks, axis=1)
    padded = jnp.floor((carry + (SEG_ALIGN - 1)) * (1.0 / SEG_ALIGN)) * SEG_ALIGN
    seg_rows, run = [], jnp.zeros((1, 1), F32)
    for ex in range(N_EXPERTS):
        seg_rows.append(run)
        run = run + padded[ex:ex + 1]
    seg = jnp.concatenate(seg_rows, axis=0)
    dest = rank + seg
    d0_ref[0] = jnp.sum(fmat * dest, axis=0, keepdims=True).astype(jnp.int32)
    d1_ref[0] = jnp.sum(smat * dest, axis=0, keepdims=True).astype(jnp.int32)
    w0_ref[...] = jnp.broadcast_to(w_first, (LANES, tm)).T
    w1_ref[...] = jnp.broadcast_to(w_second, (LANES, tm)).T
    seg_ref[0] = jnp.broadcast_to(seg, (N_EXPERTS, LANES)).astype(jnp.int32)
    whole = jnp.floor(carry * (1.0 / MOE_CHUNK))
    rem = carry - whole * MOE_CHUNK
    n_full = whole + jnp.where(rem > MOE_CHUNK // 2, 1.0, 0.0)
    half_tail = jnp.where((rem > 0) & (rem <= MOE_CHUNK // 2), 1.0, 0.0)
    nch_ref[0] = jnp.broadcast_to(n_full, (N_EXPERTS, LANES)).astype(jnp.int32)
    tail_ref[0] = jnp.broadcast_to(half_tail, (N_EXPERTS, LANES)).astype(jnp.int32)


def _route(lt, tri):
    E, N = lt.shape
    tm = MOE_TILE
    nt = N // tm
    return pl.pallas_call(
        functools.partial(_route_kernel, tm=tm),
        grid=(nt,),
        in_specs=[pl.BlockSpec((E, tm), lambda i: (0, i)), pl.BlockSpec((TRI, TRI), lambda i: (0, 0))],
        out_specs=[
            pl.BlockSpec((1, 1, tm), lambda i: (i, 0, 0)),
            pl.BlockSpec((1, 1, tm), lambda i: (i, 0, 0)),
            pl.BlockSpec((tm, LANES), lambda i: (i, 0)),
            pl.BlockSpec((tm, LANES), lambda i: (i, 0)),
            pl.BlockSpec((1, E, LANES), lambda i: (i, 0, 0)),
            pl.BlockSpec((1, E, LANES), lambda i: (i, 0, 0)),
            pl.BlockSpec((1, E, LANES), lambda i: (i, 0, 0)),
        ],
        out_shape=[
            jax.ShapeDtypeStruct((nt, 1, tm), jnp.int32),
            jax.ShapeDtypeStruct((nt, 1, tm), jnp.int32),
            jax.ShapeDtypeStruct((N, LANES), F32),
            jax.ShapeDtypeStruct((N, LANES), F32),
            jax.ShapeDtypeStruct((nt, E, LANES), jnp.int32),
            jax.ShapeDtypeStruct((nt, E, LANES), jnp.int32),
            jax.ShapeDtypeStruct((nt, E, LANES), jnp.int32),
        ],
        compiler_params=_cparams(("arbitrary",)),
        name="route",
    )(lt, tri)


def _experts_kernel(seg_ref, nch_ref, tail_ref, d0_ref, d1_ref, u_ref, w1_ref, w3_ref, w2_ref, o_ref, xs_ref, *, tm,
                    chunk):
    i = pl.program_id(0)
    step = pl.program_id(1)

    @pl.when(step == 0)
    def _():
        xs_ref[...] = jnp.zeros_like(xs_ref)
        o_ref[...] = jnp.zeros_like(o_ref)

        def move(grp, carry):
            t0 = pl.multiple_of(grp * 8, 8)
            for k in range(8):
                row = u_ref[pl.ds(t0 + k, 1), :]
                xs_ref[pl.ds(d0_ref[0, 0, t0 + k], 1), :] = row
                xs_ref[pl.ds(d1_ref[0, 0, t0 + k], 1), :] = row
            return carry

        lax.fori_loop(0, tm // 8, move, 0)

    for sub in range(EXPERTS_PER_STEP):
        slot = i * N_EXPERTS + step * EXPERTS_PER_STEP + sub
        start = seg_ref[slot]
        n_full = nch_ref[slot]

        def do_rows(r0, rows, sub=sub):
            lo, hi = _unpack_bf16_pairs(xs_ref[pl.ds(r0, rows), :])
            xb = jnp.concatenate([lo, hi], axis=-1).astype(BF16)
            a = _dot(xb, w1_ref[0, sub])
            hmid = a * _sigmoid(a) * _dot(xb, w3_ref[0, sub])
            o_ref[pl.ds(r0, rows), :] = _pack_bf16_pairs(_dot(hmid.astype(BF16), w2_ref[0, sub]))

        def do_chunk(j, carry, start=start, do_rows=do_rows):
            do_rows(pl.multiple_of(start + j * chunk, SEG_ALIGN), chunk)
            return carry

        lax.fori_loop(0, n_full, do_chunk, 0)

        @pl.when(tail_ref[slot] > 0)
        def _(start=start, n_full=n_full, do_rows=do_rows):
            do_rows(pl.multiple_of(start + n_full * chunk, SEG_ALIGN), chunk // 2)


def _experts(seg, nch, tail, d0, d1, u2p, w1, w3, w2, *, layer):
    N, half = u2p.shape
    D = 2 * half
    E = w1.shape[1]
    tm = MOE_TILE
    nt = N // tm
    kern = functools.partial(_experts_kernel, tm=tm, chunk=MOE_CHUNK)
    grid_spec = pltpu.PrefetchScalarGridSpec(
        num_scalar_prefetch=3,
        grid=(nt, E // EXPERTS_PER_STEP),
        in_specs=[
            pl.BlockSpec((1, 1, tm), lambda i, e, *_: (i, 0, 0), memory_space=pltpu.SMEM),
            pl.BlockSpec((1, 1, tm), lambda i, e, *_: (i, 0, 0), memory_space=pltpu.SMEM),
            pl.BlockSpec((tm, half), lambda i, e, *_: (i, 0)),
            pl.BlockSpec((1, EXPERTS_PER_STEP, D, D_EXPERT), lambda i, e, *_: (layer, e, 0, 0)),
            pl.BlockSpec((1, EXPERTS_PER_STEP, D, D_EXPERT), lambda i, e, *_: (layer, e, 0, 0)),
            pl.BlockSpec((1, EXPERTS_PER_STEP, D_EXPERT, D), lambda i, e, *_: (layer, e, 0, 0)),
        ],
        out_specs=pl.BlockSpec((MOE_ROWS, half), lambda i, e, *_: (i, 0)),
        scratch_shapes=[pltpu.VMEM((MOE_ROWS, half), jnp.uint32)],
    )
    return pl.pallas_call(
        kern,
        grid_spec=grid_spec,
        out_shape=jax.ShapeDtypeStruct((nt * MOE_ROWS, half), jnp.uint32),
        compiler_params=pltpu.CompilerParams(dimension_semantics=("arbitrary", "arbitrary"),
                                             vmem_limit_bytes=MOE_VMEM_LIMIT),
        name="experts",
    )(seg, nch, tail, d0, d1, u2p, w1, w3, w2)


def _combine_kernel(d0_ref, d1_ref, ys_ref, w0_ref, w1_ref, x1_ref, mod_ref, o_ref, g0_ref, g1_ref, *, tsub):
    base = pl.program_id(1) * tsub

    def move(grp, carry):
        t0 = pl.multiple_of(grp * 8, 8)
        for k in range(8):
            g0_ref[pl.ds(t0 + k, 1), :] = ys_ref[pl.ds(d0_ref[0, 0, base + t0 + k], 1), :]
            g1_ref[pl.ds(t0 + k, 1), :] = ys_ref[pl.ds(d1_ref[0, 0, base + t0 + k], 1), :]
        return carry

    lax.fori_loop(0, tsub // 8, move, 0)
    lo0, hi0 = _unpack_bf16_pairs(g0_ref[...])
    lo1, hi1 = _unpack_bf16_pairs(g1_ref[...])
    half = g0_ref.shape[1]
    w0 = jnp.concatenate([w0_ref[...]] * (half // LANES), axis=1)
    w1 = jnp.concatenate([w1_ref[...]] * (half // LANES), axis=1)
    o_ref[:, :half] = x1_ref[:, :half] + mod_ref[0, 5:6, :half] * (w0 * lo0 + w1 * lo1)
    o_ref[:, half:] = x1_ref[:, half:] + mod_ref[0, 5:6, half:] * (w0 * hi0 + w1 * hi1)


def _combine(d0, d1, ys, w0rep, w1rep, x1, mod, *, seq):
    N, D = x1.shape
    half = D // 2
    tm = MOE_TILE
    tsub = 512
    nt = N // tm
    ns = tm // tsub
    sub = lambda i, j: (i * ns + j, 0)
    return pl.pallas_call(
        functools.partial(_combine_kernel, tsub=tsub),
        grid=(nt, ns),
        in_specs=[
            pl.BlockSpec((1, 1, tm), lambda i, j: (i, 0, 0), memory_space=pltpu.SMEM),
            pl.BlockSpec((1, 1, tm), lambda i, j: (i, 0, 0), memory_space=pltpu.SMEM),
            pl.BlockSpec((MOE_ROWS, half), lambda i, j: (i, 0)),
            pl.BlockSpec((tsub, LANES), sub),
            pl.BlockSpec((tsub, LANES), sub),
            pl.BlockSpec((tsub, D), sub),
            pl.BlockSpec((1, 6, D), lambda i, j: ((i * tm + j * tsub) // seq, 0, 0)),
        ],
        out_specs=pl.BlockSpec((tsub, D), sub),
        out_shape=jax.ShapeDtypeStruct((N, D), F32),
        scratch_shapes=[pltpu.VMEM((tsub, half), jnp.uint32), pltpu.VMEM((tsub, half), jnp.uint32)],
        compiler_params=_cparams(("arbitrary", "arbitrary")),
        name="combine",
    )(d0, d1, ys, w0rep, w1rep, x1, mod)


def _alibi_slopes(first, count):
    hh = jnp.arange(first + 1, first + count + 1, dtype=F32)
    return jnp.exp2(-8.0 * hh / N_HEADS_TOTAL)


def _bd_const():
    idx = np.arange(2 * LANES) // HEAD_DIM
    return jnp.asarray((idx[:, None] == idx[None, :]).astype(np.float32), BF16)


def _blkcols_const(T):
    blk = np.arange(T) // SEL_LEN
    m = np.where(np.arange(LANES)[None, :] == blk[:, None], NEG_INF, 0.0).astype(np.float32)
    return jnp.asarray(m, BF16)


def _overlap_const(T):
    nc = T // CMP_STRIDE
    n_sel = T // SEL_LEN
    cs = np.arange(nc) * CMP_STRIDE
    ss = np.arange(n_sel) * SEL_LEN
    ov = np.clip(np.minimum(cs[:, None] + CMP_LEN, ss[None, :] + SEL_LEN)
                 - np.maximum(cs[:, None], ss[None, :]), 0, None) / CMP_LEN
    return jnp.asarray(ov.T.astype(np.float32))


def _expand_const():
    ex = np.zeros((N_NSA_BRANCHES, LANES, D_NSA), np.float32)
    for br in range(N_NSA_BRANCHES):
        for hd in range(N_HEADS_NSA):
            ex[br, hd * N_NSA_BRANCHES + br, hd * HEAD_DIM:(hd + 1) * HEAD_DIM] = 1.0
    return jnp.asarray(ex, BF16)


def _compact_w_in(w):
    parts = [w[..., a:b] for a, b in (_C_QN, _C_QS, _C_KSL, _C_KW, _C_KS, _C_VSL, _C_VW, _C_VS, _C_KC, _C_VC, _C_GATE)]
    pad = jnp.zeros(w.shape[:-1] + (LANES - (_C_GATE[1] - _C_GATE[0]),), w.dtype)
    return jnp.concatenate(parts + [pad], axis=-1).astype(BF16)


def _qconst(slopes_q):
    hi = slopes_q.astype(BF16).astype(F32)
    lo = (slopes_q - hi).astype(BF16).astype(F32)
    cols = jnp.stack([64.0 * hi, 64.0 * lo, hi, lo], axis=1)
    return jnp.zeros((slopes_q.shape[0], LANES), F32).at[:, HEAD_DIM:HEAD_DIM + 4].set(cols)


def kernel(x, c, w_router, ada_w, ada_b, norm1_g, norm2_g, w_in, w_out, nsa_q_gain, nsa_k_gain, cmp_pos_k,
           cmp_pos_v, cmp_w1_k, cmp_w2_k, cmp_w1_v, cmp_w2_v, swa_q_gain, swa_k_gain, swa_sinks, beta_nsa,
           beta_swa, moe_w1, moe_w3, moe_w2):
    B, T, D = x.shape
    L = ada_w.shape[0]
    N = B * T
    nc = T // CMP_STRIDE
    assert D == D_MODEL and T % 512 == 0 and T <= 64 * 128 and B <= 8

    slopes_nsa = _alibi_slopes(N_HEADS_SWA, N_HEADS_NSA)
    slopes_swa = _alibi_slopes(0, N_HEADS_SWA)
    qconst = _qconst(jnp.concatenate([slopes_nsa, slopes_swa]))
    bd = _bd_const()
    blkcols = _blkcols_const(T)
    ov = _overlap_const(T)
    expand = _expand_const()
    wr_t = w_router.T
    tri = jnp.asarray(np.triu(np.ones((TRI, TRI), np.float32), 1), BF16)
    assert N % MOE_TILE == 0
    moe_w1b, moe_w3b, moe_w2b = moe_w1.astype(BF16), moe_w3.astype(BF16), moe_w2.astype(BF16)
    w_outb = w_out.astype(BF16)
    w_inb = _compact_w_in(w_in)
    cmp_w1s = jnp.stack([cmp_w1_k, cmp_w1_v], axis=1).reshape(2 * L, CMP_LEN * HEAD_DIM, CMP_HIDDEN).astype(BF16)
    cmp_w2s = jnp.pad(jnp.stack([cmp_w2_k, cmp_w2_v], axis=1).reshape(2 * L, CMP_HIDDEN, HEAD_DIM),
                      ((0, 0), (0, 0), (0, LANES - HEAD_DIM))).astype(BF16)
    cmp_pos = jnp.stack([cmp_pos_k, cmp_pos_v], axis=1).reshape(2 * L, 1, CMP_LEN * HEAD_DIM)
    cmp_pos = jnp.broadcast_to(cmp_pos, (2 * L, 8, CMP_LEN * HEAD_DIM)).astype(BF16)
    zero_sinks = jnp.zeros((N_HEADS_NSA,), F32)

    c_pad = jnp.zeros((8, D), F32).at[:B].set(c)
    mod_all = _ada(c_pad, ada_w, ada_b)[:, :B].reshape(L, B, 6, D)

    x2 = x.reshape(N, D)
    for l in range(L):
        mod = mod_all[l]
        tile2 = lambda g: jnp.tile(g, 2)
        gain_c = jnp.concatenate([
            jnp.tile(nsa_q_gain[l], N_HEADS_NSA) * ATTN_SCALE,
            jnp.tile(swa_q_gain[l], N_HEADS_SWA) * ATTN_SCALE,
            tile2(nsa_k_gain[l, 1]), tile2(nsa_k_gain[l, 2]), tile2(swa_k_gain[l])]).reshape(1, -1)
        proj, xc = _inproj(x2, mod, norm1_g[l].reshape(1, D), w_inb, gain_c, qconst, bd, seq=T, layer=l)
        kgain = jnp.pad(nsa_k_gain[l, 0], (0, LANES - HEAD_DIM)).reshape(1, LANES)
        kvc = _cmpmlp(xc, cmp_w1s, cmp_w2s, cmp_pos, kgain, B=B, layer=l)

        o_cmp, notsel = _cmpattn(proj, kvc, ov, B=B, T=T)
        o_slc = _selattn(proj, notsel, blkcols, B=B, T=T)
        o_win = _band(slopes_nsa, zero_sinks, proj, B=B, T=T, window=NSA_WINDOW, q_slot=SLOT_QN,
                      k_slot=SLOT_KW, v_slot=SLOT_VW, use_sinks=False, n_sub=16, name="winattn")
        o_swa = _band(slopes_swa, swa_sinks[l], proj, B=B, T=T, window=SWA_WINDOW, q_slot=SLOT_QS,
                      k_slot=SLOT_KS, v_slot=SLOT_VS, use_sinks=True, n_sub=16, name="swaattn")

        x1, u2, lt = _outproj(o_cmp, o_slc, o_win, o_swa, proj, x2, mod, expand,
                              beta_nsa[l].reshape(1, -1), beta_swa[l].reshape(1, -1), w_outb,
                              norm2_g[l].reshape(1, D), wr_t, seq=T, layer=l)
        d0, d1, w0rep, w1rep, seg, nch, tail = _route(lt, tri)
        ys = _experts(seg[:, :, 0].reshape(-1), nch[:, :, 0].reshape(-1), tail[:, :, 0].reshape(-1), d0, d1, u2,
                      moe_w1b, moe_w3b, moe_w2b, layer=l)
        x2 = _combine(d0, d1, ys, w0rep, w1rep, x1, mod, seq=T)
    return x2.reshape(B, T, D)
```

```python
import functools

import numpy as np
import jax
import jax.numpy as jnp
from jax import lax
from jax.experimental import pallas as pl
from jax.experimental.pallas import tpu as pltpu

D_MODEL = 1024
DEPTH = 2
HEAD_DIM = 64
N_HEADS_NSA = 8
N_KV_NSA = 2
G_NSA = N_HEADS_NSA // N_KV_NSA
N_HEADS_SWA = 8
N_KV_SWA = 2
G_SWA = N_HEADS_SWA // N_KV_SWA
D_NSA = N_HEADS_NSA * HEAD_DIM
D_SWA = N_HEADS_SWA * HEAD_DIM
D_MIX = D_NSA + D_SWA
N_HEADS_TOTAL = N_HEADS_NSA + N_HEADS_SWA
KV_NSA = N_KV_NSA * HEAD_DIM
KV_SWA = N_KV_SWA * HEAD_DIM
N_NSA_BRANCHES = 3
CMP_LEN = 32
CMP_STRIDE = 16
CMP_HIDDEN = 256
SEL_LEN = 64
SEL_TOPK = 16
NSA_WINDOW = 512
SWA_WINDOW = 128
N_EXPERTS = 16
N_GROUPS = 4
EXPERTS_PER_GROUP = N_EXPERTS // N_GROUPS
TOP_K = 2
D_EXPERT = 512
EPS = 1e-6
NEG_INF = -1e30
ATTN_SCALE = HEAD_DIM ** -0.5

F32 = jnp.float32
BF16 = jnp.bfloat16

LANES = 128
VMEM_LIMIT = 48 * 1024 * 1024

MOE_TILE = 2048
MOE_CHUNK = 256
SEG_ALIGN = 16
TRI = 256
MOE_ROWS = TOP_K * MOE_TILE + N_EXPERTS * SEG_ALIGN + MOE_CHUNK
EXPERTS_PER_STEP = 2
MOE_VMEM_LIMIT = 56 * 1024 * 1024

SLOT_QN = 0
SLOT_QS = 8
SLOT_KSL = 16
SLOT_KW = 18
SLOT_KS = 20
SLOT_VSL = 22
SLOT_VW = 24
SLOT_VS = 26
SLOT_GATE = 28
N_SLOTS = 29
N_CHUNKS = 17
N_NORM_CHUNKS = 11

_C_QN = (0, 512)
_C_KC = (512, 640)
_C_VC = (640, 768)
_C_KSL = (768, 896)
_C_VSL = (896, 1024)
_C_KW = (1024, 1152)
_C_VW = (1152, 1280)
_C_GATE = (1280, 1304)
_C_QS = (1304, 1816)
_C_KS = (1816, 1944)
_C_VS = (1944, 2072)


def _cparams(sem):
    return pltpu.CompilerParams(dimension_semantics=sem, vmem_limit_bytes=VMEM_LIMIT)


def _dot(a, b, **kw):
    return jnp.dot(a, b, preferred_element_type=F32, **kw)


def _dot_nt(a, b):
    return lax.dot_general(a, b, (((1,), (1,)), ((), ())), preferred_element_type=F32)


def _sigmoid(x):
    return 1.0 / (1.0 + jnp.exp(-x))


def _pack_bf16_pairs(x):
    n = x.shape[1] // 2
    return pltpu.pack_elementwise([x[:, :n], x[:, n:]], packed_dtype=BF16)


def _unpack_bf16_pairs(w):
    lo = pltpu.unpack_elementwise(w, index=0, packed_dtype=BF16, unpacked_dtype=F32)
    hi = pltpu.unpack_elementwise(w, index=1, packed_dtype=BF16, unpacked_dtype=F32)
    return lo, hi


def _ada_kernel(c_ref, w_ref, b_ref, o_ref):
    c = c_ref[...]
    cond = c * _sigmoid(c)
    o_ref[0] = _dot(cond.astype(BF16), w_ref[0].astype(BF16)) + b_ref[0]


def _ada(c_pad, ada_w, ada_b):
    L, D, N6 = ada_w.shape
    tn = 1536
    return pl.pallas_call(
        _ada_kernel,
        grid=(L, N6 // tn),
        in_specs=[
            pl.BlockSpec((8, D), lambda l, j: (0, 0)),
            pl.BlockSpec((1, D, tn), lambda l, j: (l, 0, j)),
            pl.BlockSpec((1, 1, tn), lambda l, j: (l, 0, j)),
        ],
        out_specs=pl.BlockSpec((1, 8, tn), lambda l, j: (l, 0, j)),
        out_shape=jax.ShapeDtypeStruct((L, 8, N6), F32),
        compiler_params=_cparams(("arbitrary", "arbitrary")),
        name="ada",
    )(c_pad, ada_w, ada_b.reshape(L, 1, N6))


def _inproj_kernel(x_ref, mod_ref, g_ref, w_ref, gain_ref, qc_ref, bd_ref, o_ref, xc_ref, scr_ref, *, tm, seq):
    i = pl.program_id(0)
    x = x_ref[...]
    ms = jnp.mean(x * x, axis=-1, keepdims=True)
    u = x * lax.rsqrt(ms + EPS) * g_ref[...]
    u = u * (1.0 + mod_ref[0, 1:2, :]) + mod_ref[0, 0:1, :]
    y = _dot(u.astype(BF16), w_ref[0])

    lane = lax.broadcasted_iota(jnp.int32, (tm, LANES), 1)
    low = lane < HEAD_DIM
    t = lax.broadcasted_iota(jnp.int32, (tm, LANES), 0) + lax.rem(i * tm, seq)
    hi_part = (t >> 6).astype(F32)
    lo_part = (t & 63).astype(F32)
    poscols = jnp.where((lane == 64) | (lane == 65), hi_part,
                        jnp.where((lane == 66) | (lane == 67), lo_part, 0.0))
    bd = bd_ref[...]

    def put(slot, val):
        o_ref[:, slot * LANES:(slot + 1) * LANES] = val.astype(o_ref.dtype)

    normed = {}
    for c0 in range(0, N_NORM_CHUNKS, 2):
        width = min(2, N_NORM_CHUNKS - c0) * LANES
        wide = y[:, c0 * LANES:c0 * LANES + width]
        msb = _dot((wide * wide).astype(BF16), bd[:width, :width]) * (1.0 / HEAD_DIM)
        wide = wide * lax.rsqrt(msb + EPS) * gain_ref[:, c0 * LANES:c0 * LANES + width]
        for k in range(width // LANES):
            normed[c0 + k] = wide[:, k * LANES:(k + 1) * LANES]

    for c in range(N_CHUNKS):
        blk = normed[c] if c < N_NORM_CHUNKS else y[:, c * LANES:(c + 1) * LANES]
        if c < 14:
            rolled = pltpu.roll(blk, HEAD_DIM, axis=1)
            if c < 8:
                f0 = qc_ref[2 * c:2 * c + 1, :]
                f1 = qc_ref[2 * c + 1:2 * c + 2, :]
                s0 = 2 * c
            elif c < N_NORM_CHUNKS:
                f0 = f1 = poscols
                s0 = SLOT_KSL + 2 * (c - 8)
            else:
                f0 = f1 = 1.0
                s0 = SLOT_VSL + 2 * (c - N_NORM_CHUNKS)
            put(s0, jnp.where(low, blk, f0))
            put(s0 + 1, jnp.where(low, rolled, f1))
        elif c == 16:
            put(SLOT_GATE, blk)
        else:
            scr_ref[...] = blk
            nck = tm // CMP_STRIDE
            low_c = lax.broadcasted_iota(jnp.int32, (nck, LANES), 1) < HEAD_DIM
            head_cols = CMP_STRIDE * HEAD_DIM
            for pair in range(CMP_STRIDE // 2):
                ta = scr_ref[pl.ds(2 * pair, nck, stride=CMP_STRIDE), :]
                tb = scr_ref[pl.ds(2 * pair + 1, nck, stride=CMP_STRIDE), :]
                h0 = jnp.where(low_c, ta, pltpu.roll(tb, HEAD_DIM, axis=1))
                h1 = jnp.where(low_c, pltpu.roll(ta, HEAD_DIM, axis=1), tb)
                xc_ref[c - 14, :, pair * LANES:(pair + 1) * LANES] = h0.astype(xc_ref.dtype)
                xc_ref[c - 14, :, head_cols + pair * LANES:head_cols + (pair + 1) * LANES] = h1.astype(xc_ref.dtype)


def _inproj(x2, mod, norm_g, w_c, gain_c, qconst, bd, *, seq, layer):
    N, D = x2.shape
    tm = 512
    tiles_per_seq = seq // tm
    kern = functools.partial(_inproj_kernel, tm=tm, seq=seq)
    nchunk = tm // CMP_STRIDE
    xc_shape = jax.ShapeDtypeStruct((2, N // CMP_STRIDE, CMP_STRIDE * LANES), BF16)
    return pl.pallas_call(
        kern,
        grid=(N // tm,),
        in_specs=[
            pl.BlockSpec((tm, D), lambda i: (i, 0)),
            pl.BlockSpec((1, 6, D), lambda i: (i // tiles_per_seq, 0, 0)),
            pl.BlockSpec((1, D), lambda i: (0, 0)),
            pl.BlockSpec((1, D, N_CHUNKS * LANES), lambda i: (layer, 0, 0)),
            pl.BlockSpec((1, N_NORM_CHUNKS * LANES), lambda i: (0, 0)),
            pl.BlockSpec((16, LANES), lambda i: (0, 0)),
            pl.BlockSpec((2 * LANES, 2 * LANES), lambda i: (0, 0)),
        ],
        out_specs=[
            pl.BlockSpec((tm, N_SLOTS * LANES), lambda i: (i, 0)),
            pl.BlockSpec((2, nchunk, CMP_STRIDE * LANES), lambda i: (0, i, 0)),
        ],
        out_shape=[jax.ShapeDtypeStruct((N, N_SLOTS * LANES), BF16), xc_shape],
        scratch_shapes=[pltpu.VMEM((tm, LANES), F32)],
        compiler_params=_cparams(("arbitrary",)),
        name="inproj",
    )(x2, mod, norm_g, w_c, gain_c, qconst, bd)


def _gelu_tanh(x):
    return 0.5 * x * (1.0 + jnp.tanh(np.sqrt(2.0 / np.pi).astype(np.float32) * (x + 0.044715 * (x * x * x))))


def _cmpmlp_kernel(x_ref, w1_ref, w2_ref, pos_ref, gain_ref, o_ref, *, nc):
    kv = pl.program_id(0)
    half = CMP_STRIDE * HEAD_DIM
    bias = _dot(pos_ref[0], w1_ref[0])[0:1]
    lane = lax.broadcasted_iota(jnp.int32, (nc, LANES), 1)
    last = lax.broadcasted_iota(jnp.int32, (nc, LANES), 0) * CMP_STRIDE + (CMP_LEN - 1)
    poscols = jnp.where((lane == 64) | (lane == 65), (last >> 6).astype(F32),
                        jnp.where((lane == 66) | (lane == 67), (last & 63).astype(F32), 0.0))
    for hd in range(N_KV_NSA):
        xc = x_ref[0, :, hd * half:(hd + 1) * half]
        first = _dot(xc, w1_ref[0, :half, :])
        second = _dot(xc, w1_ref[0, half:, :])
        h = _gelu_tanh(first + pltpu.roll(second, nc - 1, axis=0) + bias)
        z = _dot(h.astype(BF16), w2_ref[0])
        zn = z * lax.rsqrt(jnp.sum(z * z, axis=-1, keepdims=True) * (1.0 / HEAD_DIM) + EPS) * gain_ref[...]
        o_ref[0, 0, hd] = jnp.where(kv == 0, zn + poscols, z + jnp.where(lane >= HEAD_DIM, 1.0, 0.0))


def _cmpmlp(xc, w1s, w2s, pos, gain, *, B, layer):
    _, nchunks, K = xc.shape
    nc = nchunks // B
    win = CMP_LEN * HEAD_DIM
    kern = functools.partial(_cmpmlp_kernel, nc=nc)
    return pl.pallas_call(
        kern,
        grid=(2, B),
        in_specs=[
            pl.BlockSpec((1, nc, K), lambda kv, b: (kv, b, 0)),
            pl.BlockSpec((1, win, CMP_HIDDEN), lambda kv, b: (2 * layer + kv, 0, 0)),
            pl.BlockSpec((1, CMP_HIDDEN, LANES), lambda kv, b: (2 * layer + kv, 0, 0)),
            pl.BlockSpec((1, 8, win), lambda kv, b: (2 * layer + kv, 0, 0)),
            pl.BlockSpec((1, LANES), lambda kv, b: (0, 0)),
        ],
        out_specs=pl.BlockSpec((1, 1, N_KV_NSA, nc, LANES), lambda kv, b: (kv, b, 0, 0, 0)),
        out_shape=jax.ShapeDtypeStruct((2, B, N_KV_NSA, nc, LANES), F32),
        compiler_params=_cparams(("arbitrary", "arbitrary")),
        name="cmpmlp",
    )(xc, w1s, w2s, pos, gain)


def _normalize_pack(pvs, extra=None):
    lane = lax.broadcasted_iota(jnp.int32, pvs[0].shape, 1)
    low = lane < HEAD_DIM
    pairs = []
    for p in range(len(pvs) // 2):
        even, odd = pvs[2 * p], pvs[2 * p + 1]
        den_e = even if extra is None else even + extra[2 * p]
        den_o = odd if extra is None else odd + extra[2 * p + 1]
        o_e = even * (1.0 / pltpu.roll(den_e, HEAD_DIM, axis=1))
        o_o = pltpu.roll(odd, HEAD_DIM, axis=1) * (1.0 / den_o)
        pairs.append(jnp.where(low, o_e, o_o))
    return jnp.concatenate(pairs, axis=-1)


def _cmpattn_kernel(q_ref, kc_ref, vc_ref, ov_ref, o_alias, ns_alias, o_ref, ns_ref, *, tq, n_sub, nc, n_cmp, n_sel,
                    top, tile0):
    del o_alias, ns_alias
    kc = kc_ref[0, 0, 0].astype(BF16)
    vc = vc_ref[0, 0, 0].astype(BF16)
    for sub in range(n_sub):
        q0 = ((tile0 + pl.program_id(2)) * n_sub + sub) * tq
        rows = pl.ds(sub * tq, tq)
        _cmpattn_subtile(q_ref.at[rows, :], kc, vc, ov_ref, o_ref.at[rows, :], ns_ref.at[0, 0, rows, :], q0,
                         tq=tq, nc=nc, n_cmp=n_cmp, n_sel=n_sel, top=top)


def _cmpattn_subtile(q_ref, kc, vc, ov_ref, o_ref, ns_ref, q0, *, tq, nc, n_cmp, n_sel, top):
    t_col = q0 + lax.broadcasted_iota(jnp.int32, (tq, 1), 0)
    row_valid = (t_col >= CMP_LEN - 1).astype(F32)
    t_row = q0 + lax.broadcasted_iota(jnp.int32, (1, tq), 1)
    n_col = lax.broadcasted_iota(jnp.int32, (nc, 1), 0)
    valid_t = (t_row - (n_col * CMP_STRIDE + (CMP_LEN - 1)) >= 0) & (n_col < n_cmp)
    bias_t = jnp.where(valid_t, 0.0, NEG_INF)
    row_valid_t = (t_row >= CMP_LEN - 1).astype(F32)

    outs = []
    psum_t = jnp.zeros((nc, tq), F32)
    for g in range(G_NSA):
        qg = q_ref[:, g * LANES:(g + 1) * LANES]
        st = _dot_nt(kc, qg) + bias_t
        et = jnp.exp(st - jnp.max(st, axis=0, keepdims=True))
        outs.append(lax.dot_general(et.astype(BF16), vc, (((0,), (0,)), ((), ())), preferred_element_type=F32))
        psum_t = psum_t + et * (1.0 / jnp.sum(et, axis=0, keepdims=True))
    o_ref[...] = (_normalize_pack(outs) * row_valid).astype(o_ref.dtype)

    psum_t = psum_t * row_valid_t
    imp_t = _dot(ov_ref[...], psum_t, precision=lax.Precision.HIGHEST)
    j = lax.broadcasted_iota(jnp.int32, (n_sel, 1), 0)
    cur = t_row >> 6
    forced = (j == 0) | (j == cur) | (j == cur - 1)
    v = jnp.where(forced, 1e9, jnp.where(j > cur, NEG_INF, imp_t))
    n_grp = n_sel // 8
    vg = [v[8 * r:8 * r + 8, :] for r in range(n_grp)]
    jg = lax.broadcasted_iota(jnp.int32, (8, 1), 0)
    cnt = [jnp.zeros((8, tq), F32) for _ in range(n_grp)]
    for jp in range(n_sel):
        vj = v[jp:jp + 1, :]
        for r in range(n_grp):
            ge = jnp.where(vj >= vg[r], 1.0, 0.0)
            gt = jnp.where(vj > vg[r], 1.0, 0.0)
            if 8 * r > jp:
                inc = ge
            elif 8 * r + 7 < jp:
                inc = gt
            else:
                inc = jnp.where(jg + 8 * r > jp, ge, gt)
            cnt[r] = cnt[r] + inc
    notsel_t = jnp.concatenate([jnp.where(cn < top, 0.0, 1.0) for cn in cnt], axis=0)
    if n_sel < LANES:
        notsel_t = jnp.concatenate([notsel_t, jnp.zeros((LANES - n_sel, tq), F32)], axis=0)
    ns_ref[...] = notsel_t.T.astype(ns_ref.dtype)


def _cmpattn(proj, kvc, ov, *, B, T):
    tq = 256
    nt = T // tq
    nc = T // CMP_STRIDE
    n_cmp = (T - CMP_LEN) // CMP_STRIDE + 1
    n_sel = T // SEL_LEN
    top = min(SEL_TOPK, n_sel)
    assert n_sel <= LANES and n_sel % 8 == 0
    n_parts = 4 if (nt % 4 == 0 and n_sel % 32 == 0) else 1
    n_sub = 4 if (nt // n_parts) % 4 == 0 else 1
    tb = n_sub * tq
    steps = nt // n_sub
    sp = steps // n_parts
    o_cmp = jnp.zeros((B * T, D_NSA), BF16)
    notsel = jnp.zeros((B, N_KV_NSA, T, LANES), BF16)
    for part in range(n_parts):
        nc_p = nc * (part + 1) // n_parts
        ns_p = n_sel * (part + 1) // n_parts
        kern = functools.partial(_cmpattn_kernel, tq=tq, n_sub=n_sub, nc=nc_p, n_cmp=n_cmp, n_sel=ns_p, top=top,
                                 tile0=part * sp)
        qrow = lambda b, h, i, part=part: (b * steps + part * sp + i, h)
        o_cmp, notsel = pl.pallas_call(
            kern,
            grid=(B, N_KV_NSA, sp),
            in_specs=[
                pl.BlockSpec((tb, G_NSA * LANES), qrow),
                pl.BlockSpec((1, 1, 1, nc_p, LANES), lambda b, h, i: (0, b, h, 0, 0)),
                pl.BlockSpec((1, 1, 1, nc_p, LANES), lambda b, h, i: (1, b, h, 0, 0)),
                pl.BlockSpec((ns_p, nc_p), lambda b, h, i: (0, 0)),
                pl.BlockSpec(memory_space=pl.ANY),
                pl.BlockSpec(memory_space=pl.ANY),
            ],
            out_specs=[
                pl.BlockSpec((tb, G_NSA * HEAD_DIM), qrow),
                pl.BlockSpec((1, 1, tb, LANES), lambda b, h, i, part=part: (b, h, part * sp + i, 0)),
            ],
            out_shape=[
                jax.ShapeDtypeStruct((B * T, D_NSA), BF16),
                jax.ShapeDtypeStruct((B, N_KV_NSA, T, LANES), BF16),
            ],
            input_output_aliases={4: 0, 5: 1},
            compiler_params=_cparams(("arbitrary", "arbitrary", "arbitrary")),
            name="cmpattn",
        )(proj, kvc, kvc, ov[:ns_p, :nc_p], o_cmp, notsel)
    return o_cmp, notsel


def _selattn_kernel(q_ref, ns_ref, k_ref, v_ref, bc_ref, o_ref, qx_ref, s0_ref, s1_ref, mx0_ref, mx1_ref, m_ref,
                    acc_ref, *, tq, tk):
    diag = pl.program_id(2)
    ns = ns_ref[0, 0]
    for g in range(G_NSA):
        qx_ref[g * tq:(g + 1) * tq, :] = jnp.concatenate([q_ref[:, g * LANES:(g + 1) * LANES], ns], axis=-1)
    m_ref[...] = jnp.full(m_ref.shape, NEG_INF, F32)
    acc_ref[...] = jnp.zeros(acc_ref.shape, F32)

    def produce(s_ref, mx_ref, kt, masked):
        start = pl.multiple_of(kt * tk, tk)
        kx = jnp.concatenate([k_ref[pl.ds(start, tk), :], bc_ref[pl.ds(start, tk), :]], axis=-1)
        s = _dot_nt(qx_ref[...], kx)
        if masked:
            r = lax.broadcasted_iota(jnp.int32, (tq, tk), 0)
            col = lax.broadcasted_iota(jnp.int32, (tq, tk), 1)
            s = s + jnp.concatenate([jnp.where(col <= r, 0.0, NEG_INF)] * G_NSA, axis=0)
        s_ref[...] = s
        mx_ref[...] = jnp.broadcast_to(jnp.max(s, axis=-1, keepdims=True), mx_ref.shape)

    def consume(s_ref, mx_ref, kt):
        m = m_ref[...]
        m_new = jnp.maximum(m, mx_ref[...])
        p = jnp.exp(s_ref[...] - jnp.concatenate([m_new] * (tk // LANES), axis=1)).astype(BF16)
        vx = v_ref[pl.ds(pl.multiple_of(kt * tk, tk), tk), :]
        acc_ref[...] = jnp.exp(m - m_new) * acc_ref[...] + _dot(p, vx)
        m_ref[...] = m_new

    def tile(k):
        return k - 1

    produce(s0_ref, mx0_ref, diag, True)

    def pair(jj, carry):
        k = 2 * jj
        produce(s1_ref, mx1_ref, tile(k + 1), False)
        consume(s0_ref, mx0_ref, jnp.where(k == 0, diag, tile(k)))
        produce(s0_ref, mx0_ref, tile(k + 2), False)
        consume(s1_ref, mx1_ref, tile(k + 1))
        return carry

    n_pairs = lax.shift_right_logical(diag, 1)
    lax.fori_loop(0, n_pairs, pair, 0)
    k_last = 2 * n_pairs

    @pl.when(k_last == diag)
    def _():
        consume(s0_ref, mx0_ref, jnp.where(diag == 0, diag, tile(diag)))

    @pl.when(k_last != diag)
    def _():
        produce(s1_ref, mx1_ref, tile(diag), False)
        consume(s0_ref, mx0_ref, jnp.where(k_last == 0, diag, tile(k_last)))
        consume(s1_ref, mx1_ref, tile(diag))

    acc = acc_ref[...]
    o_ref[...] = _normalize_pack([acc[g * tq:(g + 1) * tq] for g in range(G_NSA)]).astype(o_ref.dtype)


def _selattn(proj, notsel, blkcols, *, B, T):
    tq = 512
    tk = 512
    nt = T // tq
    m_rows = G_NSA * tq
    assert T % tk == 0 and tk == tq
    kern = functools.partial(_selattn_kernel, tq=tq, tk=tk)
    return pl.pallas_call(
        kern,
        grid=(B, N_KV_NSA, nt),
        in_specs=[
            pl.BlockSpec((tq, G_NSA * LANES), lambda b, h, i: (b * nt + i, h)),
            pl.BlockSpec((1, 1, tq, LANES), lambda b, h, i: (b, h, i, 0)),
            pl.BlockSpec((T, LANES), lambda b, h, i: (b, SLOT_KSL + h)),
            pl.BlockSpec((T, LANES), lambda b, h, i: (b, SLOT_VSL + h)),
            pl.BlockSpec((T, LANES), lambda b, h, i: (0, 0)),
        ],
        out_specs=pl.BlockSpec((tq, G_NSA * HEAD_DIM), lambda b, h, i: (b * nt + i, h)),
        out_shape=jax.ShapeDtypeStruct((B * T, D_NSA), BF16),
        scratch_shapes=[
            pltpu.VMEM((m_rows, 2 * LANES), BF16),
            pltpu.VMEM((m_rows, tk), F32),
            pltpu.VMEM((m_rows, tk), F32),
            pltpu.VMEM((m_rows, LANES), F32),
            pltpu.VMEM((m_rows, LANES), F32),
            pltpu.VMEM((m_rows, LANES), F32),
            pltpu.VMEM((m_rows, LANES), F32),
        ],
        compiler_params=_cparams(("arbitrary", "arbitrary", "arbitrary")),
        name="selattn",
    )(proj, notsel, proj, proj, blkcols)


def _band_kernel(sl_ref, sink_ref, q_ref, k_ref, v_ref, o_ref, *, tq, n_sub, window, seq, use_sinks):
    h = pl.program_id(1)
    span = window + tq
    groups = q_ref.shape[1] // LANES
    r = lax.broadcasted_iota(jnp.int32, (tq, span), 0)
    col = lax.broadcasted_iota(jnp.int32, (tq, span), 1)
    for sub in range(n_sub):
        q0 = (pl.program_id(2) * n_sub + sub) * tq
        start = pl.multiple_of(jnp.clip(q0 - window, 0, seq - span), tq)
        rows = pl.ds(sub * tq, tq)
        qx = jnp.concatenate([q_ref[rows, g * LANES:(g + 1) * LANES] for g in range(groups)], axis=0)
        d = (q0 - start) + r - col
        bias = jnp.where((d >= 0) & (d < window), 0.0, NEG_INF)
        s = _dot_nt(qx, k_ref[pl.ds(start, span), :]) + jnp.concatenate([bias] * groups, axis=0)
        m = jnp.broadcast_to(jnp.max(s, axis=-1, keepdims=True), (groups * tq, LANES))
        if use_sinks:
            t_rep = (q0 + lax.broadcasted_iota(jnp.int32, (tq, LANES), 0)).astype(F32)
            sink = jnp.concatenate(
                [sink_ref[h * groups + g] + sl_ref[h * groups + g] * t_rep for g in range(groups)], axis=0)
            m = jnp.maximum(m, sink)
        e = jnp.exp(s - jnp.concatenate([m] * (span // LANES), axis=1)).astype(BF16)
        pv = _dot(e, v_ref[pl.ds(start, span), :])
        extra = None
        if use_sinks:
            esink = jnp.exp(sink - m)
            extra = [esink[g * tq:(g + 1) * tq] for g in range(groups)]
        o_ref[rows, :] = _normalize_pack([pv[g * tq:(g + 1) * tq] for g in range(groups)], extra).astype(o_ref.dtype)


def _band(slopes, sinks, proj, *, B, T, window, q_slot, k_slot, v_slot, use_sinks, n_sub, name):
    tq = 128
    n_sub = min(n_sub, T // tq)
    tb = tq * n_sub
    nt = T // tb
    groups = 4
    assert window % tq == 0 and window + tq <= T and T % tb == 0
    qb = q_slot // groups
    kern = functools.partial(_band_kernel, tq=tq, n_sub=n_sub, window=window, seq=T, use_sinks=use_sinks)
    return pl.pallas_call(
        kern,
        grid=(B, 2, nt),
        in_specs=[
            pl.BlockSpec(memory_space=pltpu.SMEM),
            pl.BlockSpec(memory_space=pltpu.SMEM),
            pl.BlockSpec((tb, groups * LANES), lambda b, h, i: (b * nt + i, qb + h)),
            pl.BlockSpec((T, LANES), lambda b, h, i: (b, k_slot + h)),
            pl.BlockSpec((T, LANES), lambda b, h, i: (b, v_slot + h)),
        ],
        out_specs=pl.BlockSpec((tb, groups * HEAD_DIM), lambda b, h, i: (b * nt + i, h)),
        out_shape=jax.ShapeDtypeStruct((B * T, 2 * groups * HEAD_DIM), BF16),
        compiler_params=_cparams(("arbitrary", "arbitrary", "arbitrary")),
        name=name,
    )(slopes, sinks, proj, proj, proj)


def _outproj_kernel(ocmp_ref, oslc_ref, owin_ref, oswa_ref, gate_ref, x_ref, mod_ref, ex_ref, bn_ref, bs_ref,
                    w_ref, g2_ref, wr_ref, x1_ref, u2_ref, lt_ref):
    sg = _sigmoid(gate_ref[...].astype(F32)).astype(BF16)
    o_nsa = (_dot(sg, ex_ref[0]) * ocmp_ref[...].astype(F32)
             + _dot(sg, ex_ref[1]) * oslc_ref[...].astype(F32)
             + _dot(sg, ex_ref[2]) * owin_ref[...].astype(F32))
    o_swa = oswa_ref[...].astype(F32)
    n1 = o_nsa * lax.rsqrt(jnp.mean(o_nsa * o_nsa, axis=-1, keepdims=True) + EPS) * bn_ref[...]
    n2 = o_swa * lax.rsqrt(jnp.mean(o_swa * o_swa, axis=-1, keepdims=True) + EPS) * bs_ref[...]
    merged = jnp.concatenate([n1, n2], axis=-1).astype(BF16)
    y = _dot(merged, w_ref[0])
    x1 = x_ref[...] + mod_ref[0, 2:3, :] * y
    x1_ref[...] = x1
    u2 = x1 * lax.rsqrt(jnp.mean(x1 * x1, axis=-1, keepdims=True) + EPS) * g2_ref[...]
    u2 = u2 * (1.0 + mod_ref[0, 4:5, :]) + mod_ref[0, 3:4, :]
    u2_ref[...] = _pack_bf16_pairs(u2)
    u_hi = u2.astype(BF16)
    u_lo = (u2 - u_hi.astype(F32)).astype(BF16)
    wr = wr_ref[...]
    w_hi = wr.astype(BF16)
    w_lo = (wr - w_hi.astype(F32)).astype(BF16)
    lt_ref[...] = _dot_nt(w_hi, u_hi) + _dot_nt(w_hi, u_lo) + _dot_nt(w_lo, u_hi)


def _outproj(ocmp, oslc, owin, oswa, proj, x2, mod, expand, beta_n, beta_s, w_out, norm2_g, wr_t, *, seq, layer):
    N, D = x2.shape
    tm = 512
    tiles_per_seq = seq // tm
    row = lambda i: (i, 0)
    const2 = lambda i: (0, 0)
    return pl.pallas_call(
        _outproj_kernel,
        grid=(N // tm,),
        in_specs=[
            pl.BlockSpec((tm, D_NSA), row),
            pl.BlockSpec((tm, D_NSA), row),
            pl.BlockSpec((tm, D_NSA), row),
            pl.BlockSpec((tm, D_SWA), row),
            pl.BlockSpec((tm, LANES), lambda i: (i, SLOT_GATE)),
            pl.BlockSpec((tm, D), row),
            pl.BlockSpec((1, 6, D), lambda i: (i // tiles_per_seq, 0, 0)),
            pl.BlockSpec((N_NSA_BRANCHES, LANES, D_NSA), lambda i: (0, 0, 0)),
            pl.BlockSpec((1, D_NSA), const2),
            pl.BlockSpec((1, D_SWA), const2),
            pl.BlockSpec((1, D_MIX, D), lambda i: (layer, 0, 0)),
            pl.BlockSpec((1, D), const2),
            pl.BlockSpec((N_EXPERTS, D), const2),
        ],
        out_specs=[
            pl.BlockSpec((tm, D), row),
            pl.BlockSpec((tm, D // 2), row),
            pl.BlockSpec((N_EXPERTS, tm), lambda i: (0, i)),
        ],
        out_shape=[
            jax.ShapeDtypeStruct((N, D), F32),
            jax.ShapeDtypeStruct((N, D // 2), jnp.uint32),
            jax.ShapeDtypeStruct((N_EXPERTS, N), F32),
        ],
        compiler_params=_cparams(("arbitrary",)),
        name="outproj",
    )(ocmp, oslc, owin, oswa, proj, x2, mod, expand, beta_n, beta_s, w_out, norm2_g, wr_t)


def _route_kernel(lt_ref, tri_ref, d0_ref, d1_ref, w0_ref, w1_ref, seg_ref, nch_ref, tail_ref, *, tm):
    lt = lt_ref[...]
    e = jnp.exp(lt - jnp.max(lt, axis=0, keepdims=True))
    aff = e / jnp.sum(e, axis=0, keepdims=True)
    rows = [aff[k:k + 1, :] for k in range(N_EXPERTS)]
    epg = EXPERTS_PER_GROUP
    scores = []
    for gr in range(N_GROUPS):
        xs = rows[gr * epg:(gr + 1) * epg]
        top1 = functools.reduce(jnp.maximum, xs)
        second = None
        for a in range(epg):
            for b in range(a + 1, epg):
                mn = jnp.minimum(xs[a], xs[b])
                second = mn if second is None else jnp.maximum(second, mn)
        scores.append(top1 + second)
    taken = None
    in_group = []
    for gr in range(N_GROUPS):
        best = None
        for o in range(gr + 1, N_GROUPS):
            c = scores[gr] >= scores[o]
            best = c if best is None else (best & c)
        if best is None:
            best = jnp.full(scores[gr].shape, True)
        sel = best if taken is None else (best & ~taken)
        taken = sel if taken is None else (taken | sel)
        in_group.append(sel)
    ys = []
    for k in range(epg):
        yk = rows[k]
        for gr in range(1, N_GROUPS):
            yk = jnp.where(in_group[gr], rows[gr * epg + k], yk)
        ys.append(yk)
    chosen = []
    for k in range(epg):
        rk = jnp.zeros_like(ys[k])
        for o in range(epg):
            if o == k:
                continue
            before = (ys[o] >= ys[k]) if o < k else (ys[o] > ys[k])
            rk = rk + jnp.where(before, 1.0, 0.0)
        chosen.append(rk < TOP_K)
    total = functools.reduce(lambda a, b: a + b, [jnp.where(chosen[k], ys[k], 0.0) for k in range(epg)])
    first, seen = [], None
    for k in range(epg):
        first.append(chosen[k] if seen is None else (chosen[k] & ~seen))
        seen = chosen[k] if seen is None else (seen | chosen[k])
    second = [chosen[k] & ~first[k] for k in range(epg)]
    inv_total = 1.0 / total
    w_first = functools.reduce(lambda a, b: a + b, [jnp.where(first[k], ys[k], 0.0) for k in range(epg)]) * inv_total
    w_second = functools.reduce(lambda a, b: a + b, [jnp.where(second[k], ys[k], 0.0) for k in range(epg)]) * inv_total
    f_rows, s_rows = [], []
    for ex in range(N_EXPERTS):
        gr, k = divmod(ex, epg)
        f_rows.append(jnp.where(in_group[gr] & first[k], 1.0, 0.0))
        s_rows.append(jnp.where(in_group[gr] & second[k], 1.0, 0.0))
    fmat = jnp.concatenate(f_rows, axis=0)
    smat = jnp.concatenate(s_rows, axis=0)
    cmat = fmat + smat
    carry = jnp.zeros((N_EXPERTS, 1), F32)
    ranks = []
    for blk in range(tm // TRI):
        cb = cmat[:, blk * TRI:(blk + 1) * TRI]
        ranks.append(_dot(cb.astype(BF16), tri_ref[...]) + carry)
        carry = carry + jnp.sum(cb, axis=1, keepdims=True)
    rank = jnp.concatenate(ranks, axis=1)
    padded = jnp.floor((carry + (SEG_ALIGN - 1)) * (1.0 / SEG_ALIGN)) * SEG_ALIGN
    seg_rows, run = [], jnp.zeros((1, 1), F32)
    for ex in range(N_EXPERTS):
        seg_rows.append(run)
        run = run + padded[ex:ex + 1]
    seg = jnp.concatenate(seg_rows, axis=0)
    dest = rank + seg
    d0_ref[0] = jnp.sum(fmat * dest, axis=0, keepdims=True).astype(jnp.int32)
    d1_ref[0] = jnp.sum(smat * dest, axis=0, keepdims=True).astype(jnp.int32)
    w0_ref[...] = jnp.broadcast_to(w_first, (LANES, tm)).T
    w1_ref[...] = jnp.broadcast_to(w_second, (LANES, tm)).T
    seg_ref[0] = jnp.broadcast_to(seg, (N_EXPERTS, LANES)).astype(jnp.int32)
    whole = jnp.floor(carry * (1.0 / MOE_CHUNK))
    rem = carry - whole * MOE_CHUNK
    n_full = whole + jnp.where(rem > MOE_CHUNK // 2, 1.0, 0.0)
    half_tail = jnp.where((rem > 0) & (rem <= MOE_CHUNK // 2), 1.0, 0.0)
    nch_ref[0] = jnp.broadcast_to(n_full, (N_EXPERTS, LANES)).astype(jnp.int32)
    tail_ref[0] = jnp.broadcast_to(half_tail, (N_EXPERTS, LANES)).astype(jnp.int32)


def _route(lt, tri):
    E, N = lt.shape
    tm = MOE_TILE
    nt = N // tm
    return pl.pallas_call(
        functools.partial(_route_kernel, tm=tm),
        grid=(nt,),
        in_specs=[pl.BlockSpec((E, tm), lambda i: (0, i)), pl.BlockSpec((TRI, TRI), lambda i: (0, 0))],
        out_specs=[
            pl.BlockSpec((1, 1, tm), lambda i: (i, 0, 0)),
            pl.BlockSpec((1, 1, tm), lambda i: (i, 0, 0)),
            pl.BlockSpec((tm, LANES), lambda i: (i, 0)),
            pl.BlockSpec((tm, LANES), lambda i: (i, 0)),
            pl.BlockSpec((1, E, LANES), lambda i: (i, 0, 0)),
            pl.BlockSpec((1, E, LANES), lambda i: (i, 0, 0)),
            pl.BlockSpec((1, E, LANES), lambda i: (i, 0, 0)),
        ],
        out_shape=[
            jax.ShapeDtypeStruct((nt, 1, tm), jnp.int32),
            jax.ShapeDtypeStruct((nt, 1, tm), jnp.int32),
            jax.ShapeDtypeStruct((N, LANES), F32),
            jax.ShapeDtypeStruct((N, LANES), F32),
            jax.ShapeDtypeStruct((nt, E, LANES), jnp.int32),
            jax.ShapeDtypeStruct((nt, E, LANES), jnp.int32),
            jax.ShapeDtypeStruct((nt, E, LANES), jnp.int32),
        ],
        compiler_params=_cparams(("arbitrary",)),
        name="route",
    )(lt, tri)


def _experts_kernel(seg_ref, nch_ref, tail_ref, d0_ref, d1_ref, u_ref, w1_ref, w3_ref, w2_ref, o_ref, xs_ref, *, tm,
                    chunk):
    i = pl.program_id(0)
    step = pl.program_id(1)

    @pl.when(step == 0)
    def _():
        xs_ref[...] = jnp.zeros_like(xs_ref)
        o_ref[...] = jnp.zeros_like(o_ref)

        def move(grp, carry):
            t0 = pl.multiple_of(grp * 8, 8)
            blk = u_ref[pl.ds(t0, 8), :]
            for k in range(8):
                row = blk[k:k + 1, :]
                xs_ref[pl.ds(d0_ref[0, 0, t0 + k], 1), :] = row
                xs_ref[pl.ds(d1_ref[0, 0, t0 + k], 1), :] = row
            return carry

        lax.fori_loop(0, tm // 8, move, 0)

    for sub in range(EXPERTS_PER_STEP):
        slot = i * N_EXPERTS + step * EXPERTS_PER_STEP + sub
        start = seg_ref[slot]
        n_full = nch_ref[slot]

        def do_rows(r0, rows, sub=sub):
            lo, hi = _unpack_bf16_pairs(xs_ref[pl.ds(r0, rows), :])
            xb = jnp.concatenate([lo, hi], axis=-1).astype(BF16)
            a = _dot(xb, w1_ref[0, sub])
            hmid = a * _sigmoid(a) * _dot(xb, w3_ref[0, sub])
            o_ref[pl.ds(r0, rows), :] = _pack_bf16_pairs(_dot(hmid.astype(BF16), w2_ref[0, sub]))

        def do_chunk(j, carry, start=start, do_rows=do_rows):
            do_rows(pl.multiple_of(start + j * chunk, SEG_ALIGN), chunk)
            return carry

        lax.fori_loop(0, n_full, do_chunk, 0)

        @pl.when(tail_ref[slot] > 0)
        def _(start=start, n_full=n_full, do_rows=do_rows):
            do_rows(pl.multiple_of(start + n_full * chunk, SEG_ALIGN), chunk // 2)


def _experts(seg, nch, tail, d0, d1, u2p, w1, w3, w2, *, layer):
    N, half = u2p.shape
    D = 2 * half
    E = w1.shape[1]
    tm = MOE_TILE
    nt = N // tm
    kern = functools.partial(_experts_kernel, tm=tm, chunk=MOE_CHUNK)
    grid_spec = pltpu.PrefetchScalarGridSpec(
        num_scalar_prefetch=3,
        grid=(nt, E // EXPERTS_PER_STEP),
        in_specs=[
            pl.BlockSpec((1, 1, tm), lambda i, e, *_: (i, 0, 0), memory_space=pltpu.SMEM),
            pl.BlockSpec((1, 1, tm), lambda i, e, *_: (i, 0, 0), memory_space=pltpu.SMEM),
            pl.BlockSpec((tm, half), lambda i, e, *_: (i, 0)),
            pl.BlockSpec((1, EXPERTS_PER_STEP, D, D_EXPERT), lambda i, e, *_: (layer, e, 0, 0)),
            pl.BlockSpec((1, EXPERTS_PER_STEP, D, D_EXPERT), lambda i, e, *_: (layer, e, 0, 0)),
            pl.BlockSpec((1, EXPERTS_PER_STEP, D_EXPERT, D), lambda i, e, *_: (layer, e, 0, 0)),
        ],
        out_specs=pl.BlockSpec((MOE_ROWS, half), lambda i, e, *_: (i, 0)),
        scratch_shapes=[pltpu.VMEM((MOE_ROWS, half), jnp.uint32)],
    )
    return pl.pallas_call(
        kern,
        grid_spec=grid_spec,
        out_shape=jax.ShapeDtypeStruct((nt * MOE_ROWS, half), jnp.uint32),
        compiler_params=pltpu.CompilerParams(dimension_semantics=("arbitrary", "arbitrary"),
                                             vmem_limit_bytes=MOE_VMEM_LIMIT),
        name="experts",
    )(seg, nch, tail, d0, d1, u2p, w1, w3, w2)


def _combine_kernel(d0_ref, d1_ref, ys_ref, w0_ref, w1_ref, x1_ref, mod_ref, o_ref, g0_ref, g1_ref, *, tsub):
    base = pl.program_id(1) * tsub

    def move(grp, carry):
        t0 = pl.multiple_of(grp * 8, 8)
        for d_ref, g_ref in ((d0_ref, g0_ref), (d1_ref, g1_ref)):
            rows = [ys_ref[pl.ds(d_ref[0, 0, base + t0 + k], 1), :] for k in range(8)]
            g_ref[pl.ds(t0, 8), :] = jnp.concatenate(rows, axis=0)
        return carry

    lax.fori_loop(0, tsub // 8, move, 0)
    lo0, hi0 = _unpack_bf16_pairs(g0_ref[...])
    lo1, hi1 = _unpack_bf16_pairs(g1_ref[...])
    half = g0_ref.shape[1]
    w0 = jnp.concatenate([w0_ref[...]] * (half // LANES), axis=1)
    w1 = jnp.concatenate([w1_ref[...]] * (half // LANES), axis=1)
    o_ref[:, :half] = x1_ref[:, :half] + mod_ref[0, 5:6, :half] * (w0 * lo0 + w1 * lo1)
    o_ref[:, half:] = x1_ref[:, half:] + mod_ref[0, 5:6, half:] * (w0 * hi0 + w1 * hi1)


def _combine(d0, d1, ys, w0rep, w1rep, x1, mod, *, seq):
    N, D = x1.shape
    half = D // 2
    tm = MOE_TILE
    tsub = 512
    nt = N // tm
    ns = tm // tsub
    sub = lambda i, j: (i * ns + j, 0)
    return pl.pallas_call(
        functools.partial(_combine_kernel, tsub=tsub),
        grid=(nt, ns),
        in_specs=[
            pl.BlockSpec((1, 1, tm), lambda i, j: (i, 0, 0), memory_space=pltpu.SMEM),
            pl.BlockSpec((1, 1, tm), lambda i, j: (i, 0, 0), memory_space=pltpu.SMEM),
            pl.BlockSpec((MOE_ROWS, half), lambda i, j: (i, 0)),
            pl.BlockSpec((tsub, LANES), sub),
            pl.BlockSpec((tsub, LANES), sub),
            pl.BlockSpec((tsub, D), sub),
            pl.BlockSpec((1, 6, D), lambda i, j: ((i * tm + j * tsub) // seq, 0, 0)),
        ],
        out_specs=pl.BlockSpec((tsub, D), sub),
        out_shape=jax.ShapeDtypeStruct((N, D), F32),
        scratch_shapes=[pltpu.VMEM((tsub, half), jnp.uint32), pltpu.VMEM((tsub, half), jnp.uint32)],
        compiler_params=_cparams(("arbitrary", "arbitrary")),
        name="combine",
    )(d0, d1, ys, w0rep, w1rep, x1, mod)


def _alibi_slopes(first, count):
    hh = jnp.arange(first + 1, first + count + 1, dtype=F32)
    return jnp.exp2(-8.0 * hh / N_HEADS_TOTAL)


def _bd_const():
    idx = np.arange(2 * LANES) // HEAD_DIM
    return jnp.asarray((idx[:, None] == idx[None, :]).astype(np.float32), BF16)


def _blkcols_const(T):
    blk = np.arange(T) // SEL_LEN
    m = np.where(np.arange(LANES)[None, :] == blk[:, None], NEG_INF, 0.0).astype(np.float32)
    return jnp.asarray(m, BF16)


def _overlap_const(T):
    nc = T // CMP_STRIDE
    n_sel = T // SEL_LEN
    cs = np.arange(nc) * CMP_STRIDE
    ss = np.arange(n_sel) * SEL_LEN
    ov = np.clip(np.minimum(cs[:, None] + CMP_LEN, ss[None, :] + SEL_LEN)
                 - np.maximum(cs[:, None], ss[None, :]), 0, None) / CMP_LEN
    return jnp.asarray(ov.T.astype(np.float32))


def _expand_const():
    ex = np.zeros((N_NSA_BRANCHES, LANES, D_NSA), np.float32)
    for br in range(N_NSA_BRANCHES):
        for hd in range(N_HEADS_NSA):
            ex[br, hd * N_NSA_BRANCHES + br, hd * HEAD_DIM:(hd + 1) * HEAD_DIM] = 1.0
    return jnp.asarray(ex, BF16)


def _compact_w_in(w):
    parts = [w[..., a:b] for a, b in (_C_QN, _C_QS, _C_KSL, _C_KW, _C_KS, _C_VSL, _C_VW, _C_VS, _C_KC, _C_VC, _C_GATE)]
    pad = jnp.zeros(w.shape[:-1] + (LANES - (_C_GATE[1] - _C_GATE[0]),), w.dtype)
    return jnp.concatenate(parts + [pad], axis=-1).astype(BF16)


def _qconst(slopes_q):
    hi = slopes_q.astype(BF16).astype(F32)
    lo = (slopes_q - hi).astype(BF16).astype(F32)
    cols = jnp.stack([64.0 * hi, 64.0 * lo, hi, lo], axis=1)
    return jnp.zeros((slopes_q.shape[0], LANES), F32).at[:, HEAD_DIM:HEAD_DIM + 4].set(cols)


def kernel(x, c, w_router, ada_w, ada_b, norm1_g, norm2_g, w_in, w_out, nsa_q_gain, nsa_k_gain, cmp_pos_k,
           cmp_pos_v, cmp_w1_k, cmp_w2_k, cmp_w1_v, cmp_w2_v, swa_q_gain, swa_k_gain, swa_sinks, beta_nsa,
           beta_swa, moe_w1, moe_w3, moe_w2):
    B, T, D = x.shape
    L = ada_w.shape[0]
    N = B * T
    nc = T // CMP_STRIDE
    assert D == D_MODEL and T % 512 == 0 and T <= 64 * 128 and B <= 8

    slopes_nsa = _alibi_slopes(N_HEADS_SWA, N_HEADS_NSA)
    slopes_swa = _alibi_slopes(0, N_HEADS_SWA)
    qconst = _qconst(jnp.concatenate([slopes_nsa, slopes_swa]))
    bd = _bd_const()
    blkcols = _blkcols_const(T)
    ov = _overlap_const(T)
    expand = _expand_const()
    wr_t = w_router.T
    tri = jnp.asarray(np.triu(np.ones((TRI, TRI), np.float32), 1), BF16)
    assert N % MOE_TILE == 0
    moe_w1b, moe_w3b, moe_w2b = moe_w1.astype(BF16), moe_w3.astype(BF16), moe_w2.astype(BF16)
    w_outb = w_out.astype(BF16)
    w_inb = _compact_w_in(w_in)
    cmp_w1s = jnp.stack([cmp_w1_k, cmp_w1_v], axis=1).reshape(2 * L, CMP_LEN * HEAD_DIM, CMP_HIDDEN).astype(BF16)
    cmp_w2s = jnp.pad(jnp.stack([cmp_w2_k, cmp_w2_v], axis=1).reshape(2 * L, CMP_HIDDEN, HEAD_DIM),
                      ((0, 0), (0, 0), (0, LANES - HEAD_DIM))).astype(BF16)
    cmp_pos = jnp.stack([cmp_pos_k, cmp_pos_v], axis=1).reshape(2 * L, 1, CMP_LEN * HEAD_DIM)
    cmp_pos = jnp.broadcast_to(cmp_pos, (2 * L, 8, CMP_LEN * HEAD_DIM)).astype(BF16)
    zero_sinks = jnp.zeros((N_HEADS_NSA,), F32)

    c_pad = jnp.zeros((8, D), F32).at[:B].set(c)
    mod_all = _ada(c_pad, ada_w, ada_b)[:, :B].reshape(L, B, 6, D)

    x2 = x.reshape(N, D)
    for l in range(L):
        mod = mod_all[l]
        tile2 = lambda g: jnp.tile(g, 2)
        gain_c = jnp.concatenate([
            jnp.tile(nsa_q_gain[l], N_HEADS_NSA) * ATTN_SCALE,
            jnp.tile(swa_q_gain[l], N_HEADS_SWA) * ATTN_SCALE,
            tile2(nsa_k_gain[l, 1]), tile2(nsa_k_gain[l, 2]), tile2(swa_k_gain[l])]).reshape(1, -1)
        proj, xc = _inproj(x2, mod, norm1_g[l].reshape(1, D), w_inb, gain_c, qconst, bd, seq=T, layer=l)
        kgain = jnp.pad(nsa_k_gain[l, 0], (0, LANES - HEAD_DIM)).reshape(1, LANES)
        kvc = _cmpmlp(xc, cmp_w1s, cmp_w2s, cmp_pos, kgain, B=B, layer=l)

        o_cmp, notsel = _cmpattn(proj, kvc, ov, B=B, T=T)
        o_slc = _selattn(proj, notsel, blkcols, B=B, T=T)
        o_win = _band(slopes_nsa, zero_sinks, proj, B=B, T=T, window=NSA_WINDOW, q_slot=SLOT_QN,
                      k_slot=SLOT_KW, v_slot=SLOT_VW, use_sinks=False, n_sub=16, name="winattn")
        o_swa = _band(slopes_swa, swa_sinks[l], proj, B=B, T=T, window=SWA_WINDOW, q_slot=SLOT_QS,
                      k_slot=SLOT_KS, v_slot=SLOT_VS, use_sinks=True, n_sub=16, name="swaattn")

        x1, u2, lt = _outproj(o_cmp, o_slc, o_win, o_swa, proj, x2, mod, expand,
                              beta_nsa[l].reshape(1, -1), beta_swa[l].reshape(1, -1), w_outb,
                              norm2_g[l].reshape(1, D), wr_t, seq=T, layer=l)
        d0, d1, w0rep, w1rep, seg, nch, tail = _route(lt, tri)
        ys = _experts(seg[:, :, 0].reshape(-1), nch[:, :, 0].reshape(-1), tail[:, :, 0].reshape(-1), d0, d1, u2,
                      moe_w1b, moe_w3b, moe_w2b, layer=l)
        x2 = _combine(d0, d1, ys, w0rep, w1rep, x1, mod, seq=T)
    return x2.reshape(B, T, D)
```

```python
import functools

import numpy as np
import jax
import jax.numpy as jnp
from jax import lax
from jax.experimental import pallas as pl
from jax.experimental.pallas import tpu as pltpu

D_MODEL = 1024
DEPTH = 2
HEAD_DIM = 64
N_HEADS_NSA = 8
N_KV_NSA = 2
G_NSA = N_HEADS_NSA // N_KV_NSA
N_HEADS_SWA = 8
N_KV_SWA = 2
G_SWA = N_HEADS_SWA // N_KV_SWA
D_NSA = N_HEADS_NSA * HEAD_DIM
D_SWA = N_HEADS_SWA * HEAD_DIM
D_MIX = D_NSA + D_SWA
N_HEADS_TOTAL = N_HEADS_NSA + N_HEADS_SWA
KV_NSA = N_KV_NSA * HEAD_DIM
KV_SWA = N_KV_SWA * HEAD_DIM
N_NSA_BRANCHES = 3
CMP_LEN = 32
CMP_STRIDE = 16
CMP_HIDDEN = 256
SEL_LEN = 64
SEL_TOPK = 16
NSA_WINDOW = 512
SWA_WINDOW = 128
N_EXPERTS = 16
N_GROUPS = 4
EXPERTS_PER_GROUP = N_EXPERTS // N_GROUPS
TOP_K = 2
D_EXPERT = 512
EPS = 1e-6
NEG_INF = -1e30
ATTN_SCALE = HEAD_DIM ** -0.5

F32 = jnp.float32
BF16 = jnp.bfloat16

LANES = 128
VMEM_LIMIT = 48 * 1024 * 1024

MOE_TILE = 2048
MOE_CHUNK = 256
SEG_ALIGN = 16
TRI = 256
MOE_ROWS = TOP_K * MOE_TILE + N_EXPERTS * SEG_ALIGN + MOE_CHUNK
EXPERTS_PER_STEP = 2
MOE_VMEM_LIMIT = 56 * 1024 * 1024

SLOT_QN = 0
SLOT_QS = 8
SLOT_KSL = 16
SLOT_KW = 18
SLOT_KS = 20
SLOT_VSL = 22
SLOT_VW = 24
SLOT_VS = 26
SLOT_GATE = 28
N_SLOTS = 29
N_CHUNKS = 17
N_NORM_CHUNKS = 11

_C_QN = (0, 512)
_C_KC = (512, 640)
_C_VC = (640, 768)
_C_KSL = (768, 896)
_C_VSL = (896, 1024)
_C_KW = (1024, 1152)
_C_VW = (1152, 1280)
_C_GATE = (1280, 1304)
_C_QS = (1304, 1816)
_C_KS = (1816, 1944)
_C_VS = (1944, 2072)


def _cparams(sem):
    return pltpu.CompilerParams(dimension_semantics=sem, vmem_limit_bytes=VMEM_LIMIT)


def _dot(a, b, **kw):
    return jnp.dot(a, b, preferred_element_type=F32, **kw)


def _dot_nt(a, b):
    return lax.dot_general(a, b, (((1,), (1,)), ((), ())), preferred_element_type=F32)


def _sigmoid(x):
    return 1.0 / (1.0 + jnp.exp(-x))


def _pack_bf16_pairs(x):
    n = x.shape[1] // 2
    return pltpu.pack_elementwise([x[:, :n], x[:, n:]], packed_dtype=BF16)


def _unpack_bf16_pairs(w):
    lo = pltpu.unpack_elementwise(w, index=0, packed_dtype=BF16, unpacked_dtype=F32)
    hi = pltpu.unpack_elementwise(w, index=1, packed_dtype=BF16, unpacked_dtype=F32)
    return lo, hi


def _ada_kernel(c_ref, w_ref, b_ref, o_ref):
    c = c_ref[...]
    cond = c * _sigmoid(c)
    o_ref[0] = _dot(cond.astype(BF16), w_ref[0].astype(BF16)) + b_ref[0]


def _ada(c_pad, ada_w, ada_b):
    L, D, N6 = ada_w.shape
    tn = 1536
    return pl.pallas_call(
        _ada_kernel,
        grid=(L, N6 // tn),
        in_specs=[
            pl.BlockSpec((8, D), lambda l, j: (0, 0)),
            pl.BlockSpec((1, D, tn), lambda l, j: (l, 0, j)),
            pl.BlockSpec((1, 1, tn), lambda l, j: (l, 0, j)),
        ],
        out_specs=pl.BlockSpec((1, 8, tn), lambda l, j: (l, 0, j)),
        out_shape=jax.ShapeDtypeStruct((L, 8, N6), F32),
        compiler_params=_cparams(("arbitrary", "arbitrary")),
        name="ada",
    )(c_pad, ada_w, ada_b.reshape(L, 1, N6))


def _inproj_kernel(x_ref, mod_ref, g_ref, w_ref, gain_ref, qc_ref, bd_ref, o_ref, xc_ref, scr_ref, *, tm, seq):
    i = pl.program_id(0)
    x = x_ref[...]
    ms = jnp.mean(x * x, axis=-1, keepdims=True)
    u = x * lax.rsqrt(ms + EPS) * g_ref[...]
    u = u * (1.0 + mod_ref[0, 1:2, :]) + mod_ref[0, 0:1, :]
    y = _dot(u.astype(BF16), w_ref[0])

    lane = lax.broadcasted_iota(jnp.int32, (tm, LANES), 1)
    low = lane < HEAD_DIM
    t = lax.broadcasted_iota(jnp.int32, (tm, LANES), 0) + lax.rem(i * tm, seq)
    hi_part = (t >> 6).astype(F32)
    lo_part = (t & 63).astype(F32)
    poscols = jnp.where((lane == 64) | (lane == 65), hi_part,
                        jnp.where((lane == 66) | (lane == 67), lo_part, 0.0))
    bd = bd_ref[...]

    def put(slot, val):
        o_ref[:, slot * LANES:(slot + 1) * LANES] = val.astype(o_ref.dtype)

    normed = {}
    for c0 in range(0, N_NORM_CHUNKS, 2):
        width = min(2, N_NORM_CHUNKS - c0) * LANES
        wide = y[:, c0 * LANES:c0 * LANES + width]
        msb = _dot((wide * wide).astype(BF16), bd[:width, :width]) * (1.0 / HEAD_DIM)
        wide = wide * lax.rsqrt(msb + EPS) * gain_ref[:, c0 * LANES:c0 * LANES + width]
        for k in range(width // LANES):
            normed[c0 + k] = wide[:, k * LANES:(k + 1) * LANES]

    for c in range(N_CHUNKS):
        blk = normed[c] if c < N_NORM_CHUNKS else y[:, c * LANES:(c + 1) * LANES]
        if c < 14:
            rolled = pltpu.roll(blk, HEAD_DIM, axis=1)
            if c < 8:
                f0 = qc_ref[2 * c:2 * c + 1, :]
                f1 = qc_ref[2 * c + 1:2 * c + 2, :]
                s0 = 2 * c
            elif c < N_NORM_CHUNKS:
                f0 = f1 = poscols
                s0 = SLOT_KSL + 2 * (c - 8)
            else:
                f0 = f1 = 1.0
                s0 = SLOT_VSL + 2 * (c - N_NORM_CHUNKS)
            put(s0, jnp.where(low, blk, f0))
            put(s0 + 1, jnp.where(low, rolled, f1))
        elif c == 16:
            put(SLOT_GATE, blk)
        else:
            scr_ref[...] = blk
            nck = tm // CMP_STRIDE
            low_c = lax.broadcasted_iota(jnp.int32, (nck, LANES), 1) < HEAD_DIM
            head_cols = CMP_STRIDE * HEAD_DIM
            for pair in range(CMP_STRIDE // 2):
                ta = scr_ref[pl.ds(2 * pair, nck, stride=CMP_STRIDE), :]
                tb = scr_ref[pl.ds(2 * pair + 1, nck, stride=CMP_STRIDE), :]
                h0 = jnp.where(low_c, ta, pltpu.roll(tb, HEAD_DIM, axis=1))
                h1 = jnp.where(low_c, pltpu.roll(ta, HEAD_DIM, axis=1), tb)
                xc_ref[c - 14, :, pair * LANES:(pair + 1) * LANES] = h0.astype(xc_ref.dtype)
                xc_ref[c - 14, :, head_cols + pair * LANES:head_cols + (pair + 1) * LANES] = h1.astype(xc_ref.dtype)


def _inproj(x2, mod, norm_g, w_c, gain_c, qconst, bd, *, seq, layer):
    N, D = x2.shape
    tm = 1024
    tiles_per_seq = seq // tm
    kern = functools.partial(_inproj_kernel, tm=tm, seq=seq)
    nchunk = tm // CMP_STRIDE
    xc_shape = jax.ShapeDtypeStruct((2, N // CMP_STRIDE, CMP_STRIDE * LANES), BF16)
    return pl.pallas_call(
        kern,
        grid=(N // tm,),
        in_specs=[
            pl.BlockSpec((tm, D), lambda i: (i, 0)),
            pl.BlockSpec((1, 6, D), lambda i: (i // tiles_per_seq, 0, 0)),
            pl.BlockSpec((1, D), lambda i: (0, 0)),
            pl.BlockSpec((1, D, N_CHUNKS * LANES), lambda i: (layer, 0, 0)),
            pl.BlockSpec((1, N_NORM_CHUNKS * LANES), lambda i: (0, 0)),
            pl.BlockSpec((16, LANES), lambda i: (0, 0)),
            pl.BlockSpec((2 * LANES, 2 * LANES), lambda i: (0, 0)),
        ],
        out_specs=[
            pl.BlockSpec((tm, N_SLOTS * LANES), lambda i: (i, 0)),
            pl.BlockSpec((2, nchunk, CMP_STRIDE * LANES), lambda i: (0, i, 0)),
        ],
        out_shape=[jax.ShapeDtypeStruct((N, N_SLOTS * LANES), BF16), xc_shape],
        scratch_shapes=[pltpu.VMEM((tm, LANES), F32)],
        compiler_params=_cparams(("arbitrary",)),
        name="inproj",
    )(x2, mod, norm_g, w_c, gain_c, qconst, bd)


def _gelu_tanh(x):
    return 0.5 * x * (1.0 + jnp.tanh(np.sqrt(2.0 / np.pi).astype(np.float32) * (x + 0.044715 * (x * x * x))))


def _cmpmlp_kernel(x_ref, w1_ref, w2_ref, pos_ref, gain_ref, o_ref, *, nc):
    kv = pl.program_id(0)
    half = CMP_STRIDE * HEAD_DIM
    bias = _dot(pos_ref[0], w1_ref[0])[0:1]
    lane = lax.broadcasted_iota(jnp.int32, (nc, LANES), 1)
    last = lax.broadcasted_iota(jnp.int32, (nc, LANES), 0) * CMP_STRIDE + (CMP_LEN - 1)
    poscols = jnp.where((lane == 64) | (lane == 65), (last >> 6).astype(F32),
                        jnp.where((lane == 66) | (lane == 67), (last & 63).astype(F32), 0.0))
    for hd in range(N_KV_NSA):
        xc = x_ref[0, :, hd * half:(hd + 1) * half]
        first = _dot(xc, w1_ref[0, :half, :])
        second = _dot(xc, w1_ref[0, half:, :])
        h = _gelu_tanh(first + pltpu.roll(second, nc - 1, axis=0) + bias)
        z = _dot(h.astype(BF16), w2_ref[0])
        zn = z * lax.rsqrt(jnp.sum(z * z, axis=-1, keepdims=True) * (1.0 / HEAD_DIM) + EPS) * gain_ref[...]
        o_ref[0, 0, hd] = jnp.where(kv == 0, zn + poscols, z + jnp.where(lane >= HEAD_DIM, 1.0, 0.0))


def _cmpmlp(xc, w1s, w2s, pos, gain, *, B, layer):
    _, nchunks, K = xc.shape
    nc = nchunks // B
    win = CMP_LEN * HEAD_DIM
    kern = functools.partial(_cmpmlp_kernel, nc=nc)
    return pl.pallas_call(
        kern,
        grid=(2, B),
        in_specs=[
            pl.BlockSpec((1, nc, K), lambda kv, b: (kv, b, 0)),
            pl.BlockSpec((1, win, CMP_HIDDEN), lambda kv, b: (2 * layer + kv, 0, 0)),
            pl.BlockSpec((1, CMP_HIDDEN, LANES), lambda kv, b: (2 * layer + kv, 0, 0)),
            pl.BlockSpec((1, 8, win), lambda kv, b: (2 * layer + kv, 0, 0)),
            pl.BlockSpec((1, LANES), lambda kv, b: (0, 0)),
        ],
        out_specs=pl.BlockSpec((1, 1, N_KV_NSA, nc, LANES), lambda kv, b: (kv, b, 0, 0, 0)),
        out_shape=jax.ShapeDtypeStruct((2, B, N_KV_NSA, nc, LANES), F32),
        compiler_params=_cparams(("arbitrary", "arbitrary")),
        name="cmpmlp",
    )(xc, w1s, w2s, pos, gain)


def _normalize_pack(pvs, extra=None):
    lane = lax.broadcasted_iota(jnp.int32, pvs[0].shape, 1)
    low = lane < HEAD_DIM
    pairs = []
    for p in range(len(pvs) // 2):
        even, odd = pvs[2 * p], pvs[2 * p + 1]
        den_e = even if extra is None else even + extra[2 * p]
        den_o = odd if extra is None else odd + extra[2 * p + 1]
        o_e = even * (1.0 / pltpu.roll(den_e, HEAD_DIM, axis=1))
        o_o = pltpu.roll(odd, HEAD_DIM, axis=1) * (1.0 / den_o)
        pairs.append(jnp.where(low, o_e, o_o))
    return jnp.concatenate(pairs, axis=-1)


def _cmpattn_kernel(q_ref, kc_ref, vc_ref, ov_ref, o_alias, ns_alias, o_ref, ns_ref, *, tq, n_sub, nc, n_cmp, n_sel,
                    top, tile0):
    del o_alias, ns_alias
    kc = kc_ref[0, 0, 0].astype(BF16)
    vc = vc_ref[0, 0, 0].astype(BF16)
    for sub in range(n_sub):
        q0 = ((tile0 + pl.program_id(2)) * n_sub + sub) * tq
        rows = pl.ds(sub * tq, tq)
        _cmpattn_subtile(q_ref.at[rows, :], kc, vc, ov_ref, o_ref.at[rows, :], ns_ref.at[0, 0, rows, :], q0,
                         tq=tq, nc=nc, n_cmp=n_cmp, n_sel=n_sel, top=top)


def _cmpattn_subtile(q_ref, kc, vc, ov_ref, o_ref, ns_ref, q0, *, tq, nc, n_cmp, n_sel, top):
    t_col = q0 + lax.broadcasted_iota(jnp.int32, (tq, 1), 0)
    row_valid = (t_col >= CMP_LEN - 1).astype(F32)
    t_row = q0 + lax.broadcasted_iota(jnp.int32, (1, tq), 1)
    n_col = lax.broadcasted_iota(jnp.int32, (nc, 1), 0)
    valid_t = (t_row - (n_col * CMP_STRIDE + (CMP_LEN - 1)) >= 0) & (n_col < n_cmp)
    bias_t = jnp.where(valid_t, 0.0, NEG_INF)
    row_valid_t = (t_row >= CMP_LEN - 1).astype(F32)

    outs = []
    psum_t = jnp.zeros((nc, tq), F32)
    for g in range(G_NSA):
        qg = q_ref[:, g * LANES:(g + 1) * LANES]
        st = _dot_nt(kc, qg) + bias_t
        et = jnp.exp(st - jnp.max(st, axis=0, keepdims=True))
        outs.append(lax.dot_general(et.astype(BF16), vc, (((0,), (0,)), ((), ())), preferred_element_type=F32))
        psum_t = psum_t + et * (1.0 / jnp.sum(et, axis=0, keepdims=True))
    o_ref[...] = (_normalize_pack(outs) * row_valid).astype(o_ref.dtype)

    psum_t = psum_t * row_valid_t
    imp_t = _dot(ov_ref[...], psum_t, precision=lax.Precision.HIGHEST)
    j = lax.broadcasted_iota(jnp.int32, (n_sel, 1), 0)
    cur = t_row >> 6
    forced = (j == 0) | (j == cur) | (j == cur - 1)
    v = jnp.where(forced, 1e9, jnp.where(j > cur, NEG_INF, imp_t))
    n_grp = n_sel // 8
    vg = [v[8 * r:8 * r + 8, :] for r in range(n_grp)]
    jg = lax.broadcasted_iota(jnp.int32, (8, 1), 0)
    cnt = [jnp.zeros((8, tq), F32) for _ in range(n_grp)]
    for jp in range(n_sel):
        vj = v[jp:jp + 1, :]
        for r in range(n_grp):
            ge = jnp.where(vj >= vg[r], 1.0, 0.0)
            gt = jnp.where(vj > vg[r], 1.0, 0.0)
            if 8 * r > jp:
                inc = ge
            elif 8 * r + 7 < jp:
                inc = gt
            else:
                inc = jnp.where(jg + 8 * r > jp, ge, gt)
            cnt[r] = cnt[r] + inc
    notsel_t = jnp.concatenate([jnp.where(cn < top, 0.0, 1.0) for cn in cnt], axis=0)
    if n_sel < LANES:
        notsel_t = jnp.concatenate([notsel_t, jnp.zeros((LANES - n_sel, tq), F32)], axis=0)
    ns_ref[...] = notsel_t.T.astype(ns_ref.dtype)


def _cmpattn(proj, kvc, ov, *, B, T):
    tq = 256
    nt = T // tq
    nc = T // CMP_STRIDE
    n_cmp = (T - CMP_LEN) // CMP_STRIDE + 1
    n_sel = T // SEL_LEN
    top = min(SEL_TOPK, n_sel)
    assert n_sel <= LANES and n_sel % 8 == 0
    n_parts = 4 if (nt % 4 == 0 and n_sel % 32 == 0) else 1
    n_sub = 4 if (nt // n_parts) % 4 == 0 else 1
    tb = n_sub * tq
    steps = nt // n_sub
    sp = steps // n_parts
    o_cmp = jnp.zeros((B * T, D_NSA), BF16)
    notsel = jnp.zeros((B, N_KV_NSA, T, LANES), BF16)
    for part in range(n_parts):
        nc_p = nc * (part + 1) // n_parts
        ns_p = n_sel * (part + 1) // n_parts
        kern = functools.partial(_cmpattn_kernel, tq=tq, n_sub=n_sub, nc=nc_p, n_cmp=n_cmp, n_sel=ns_p, top=top,
                                 tile0=part * sp)
        qrow = lambda b, h, i, part=part: (b * steps + part * sp + i, h)
        o_cmp, notsel = pl.pallas_call(
            kern,
            grid=(B, N_KV_NSA, sp),
            in_specs=[
                pl.BlockSpec((tb, G_NSA * LANES), qrow),
                pl.BlockSpec((1, 1, 1, nc_p, LANES), lambda b, h, i: (0, b, h, 0, 0)),
                pl.BlockSpec((1, 1, 1, nc_p, LANES), lambda b, h, i: (1, b, h, 0, 0)),
                pl.BlockSpec((ns_p, nc_p), lambda b, h, i: (0, 0)),
                pl.BlockSpec(memory_space=pl.ANY),
                pl.BlockSpec(memory_space=pl.ANY),
            ],
            out_specs=[
                pl.BlockSpec((tb, G_NSA * HEAD_DIM), qrow),
                pl.BlockSpec((1, 1, tb, LANES), lambda b, h, i, part=part: (b, h, part * sp + i, 0)),
            ],
            out_shape=[
                jax.ShapeDtypeStruct((B * T, D_NSA), BF16),
                jax.ShapeDtypeStruct((B, N_KV_NSA, T, LANES), BF16),
            ],
            input_output_aliases={4: 0, 5: 1},
            compiler_params=_cparams(("arbitrary", "arbitrary", "arbitrary")),
            name="cmpattn",
        )(proj, kvc, kvc, ov[:ns_p, :nc_p], o_cmp, notsel)
    return o_cmp, notsel


def _selattn_kernel(q_ref, ns_ref, k_ref, v_ref, bc_ref, o_ref, qx_ref, s0_ref, s1_ref, mx0_ref, mx1_ref, m_ref,
                    acc_ref, *, tq, tk):
    diag = pl.program_id(2)
    ns = ns_ref[0, 0]
    for g in range(G_NSA):
        qx_ref[g * tq:(g + 1) * tq, :] = jnp.concatenate([q_ref[:, g * LANES:(g + 1) * LANES], ns], axis=-1)
    m_ref[...] = jnp.full(m_ref.shape, NEG_INF, F32)
    acc_ref[...] = jnp.zeros(acc_ref.shape, F32)

    def produce(s_ref, mx_ref, kt, masked):
        start = pl.multiple_of(kt * tk, tk)
        kx = jnp.concatenate([k_ref[pl.ds(start, tk), :], bc_ref[pl.ds(start, tk), :]], axis=-1)
        s = _dot_nt(qx_ref[...], kx)
        if masked:
            r = lax.broadcasted_iota(jnp.int32, (tq, tk), 0)
            col = lax.broadcasted_iota(jnp.int32, (tq, tk), 1)
            s = s + jnp.concatenate([jnp.where(col <= r, 0.0, NEG_INF)] * G_NSA, axis=0)
        s_ref[...] = s
        mx_ref[...] = jnp.broadcast_to(jnp.max(s, axis=-1, keepdims=True), mx_ref.shape)

    def consume(s_ref, mx_ref, kt):
        m = m_ref[...]
        m_new = jnp.maximum(m, mx_ref[...])
        p = jnp.exp(s_ref[...] - jnp.concatenate([m_new] * (tk // LANES), axis=1)).astype(BF16)
        vx = v_ref[pl.ds(pl.multiple_of(kt * tk, tk), tk), :]
        acc_ref[...] = jnp.exp(m - m_new) * acc_ref[...] + _dot(p, vx)
        m_ref[...] = m_new

    def tile(k):
        return k - 1

    produce(s0_ref, mx0_ref, diag, True)

    def pair(jj, carry):
        k = 2 * jj
        produce(s1_ref, mx1_ref, tile(k + 1), False)
        consume(s0_ref, mx0_ref, jnp.where(k == 0, diag, tile(k)))
        produce(s0_ref, mx0_ref, tile(k + 2), False)
        consume(s1_ref, mx1_ref, tile(k + 1))
        return carry

    n_pairs = lax.shift_right_logical(diag, 1)
    lax.fori_loop(0, n_pairs, pair, 0)
    k_last = 2 * n_pairs

    @pl.when(k_last == diag)
    def _():
        consume(s0_ref, mx0_ref, jnp.where(diag == 0, diag, tile(diag)))

    @pl.when(k_last != diag)
    def _():
        produce(s1_ref, mx1_ref, tile(diag), False)
        consume(s0_ref, mx0_ref, jnp.where(k_last == 0, diag, tile(k_last)))
        consume(s1_ref, mx1_ref, tile(diag))

    acc = acc_ref[...]
    o_ref[...] = _normalize_pack([acc[g * tq:(g + 1) * tq] for g in range(G_NSA)]).astype(o_ref.dtype)


def _selattn(proj, notsel, blkcols, *, B, T):
    tq = 512
    tk = 512
    nt = T // tq
    m_rows = G_NSA * tq
    assert T % tk == 0 and tk == tq
    kern = functools.partial(_selattn_kernel, tq=tq, tk=tk)
    return pl.pallas_call(
        kern,
        grid=(B, N_KV_NSA, nt),
        in_specs=[
            pl.BlockSpec((tq, G_NSA * LANES), lambda b, h, i: (b * nt + i, h)),
            pl.BlockSpec((1, 1, tq, LANES), lambda b, h, i: (b, h, i, 0)),
            pl.BlockSpec((T, LANES), lambda b, h, i: (b, SLOT_KSL + h)),
            pl.BlockSpec((T, LANES), lambda b, h, i: (b, SLOT_VSL + h)),
            pl.BlockSpec((T, LANES), lambda b, h, i: (0, 0)),
        ],
        out_specs=pl.BlockSpec((tq, G_NSA * HEAD_DIM), lambda b, h, i: (b * nt + i, h)),
        out_shape=jax.ShapeDtypeStruct((B * T, D_NSA), BF16),
        scratch_shapes=[
            pltpu.VMEM((m_rows, 2 * LANES), BF16),
            pltpu.VMEM((m_rows, tk), F32),
            pltpu.VMEM((m_rows, tk), F32),
            pltpu.VMEM((m_rows, LANES), F32),
            pltpu.VMEM((m_rows, LANES), F32),
            pltpu.VMEM((m_rows, LANES), F32),
            pltpu.VMEM((m_rows, LANES), F32),
        ],
        compiler_params=_cparams(("arbitrary", "arbitrary", "arbitrary")),
        name="selattn",
    )(proj, notsel, proj, proj, blkcols)


def _band_kernel(sl_ref, sink_ref, q_ref, k_ref, v_ref, o_ref, *, tq, n_sub, window, seq, use_sinks):
    h = pl.program_id(1)
    span = window + tq
    groups = q_ref.shape[1] // LANES
    r = lax.broadcasted_iota(jnp.int32, (tq, span), 0)
    col = lax.broadcasted_iota(jnp.int32, (tq, span), 1)
    for sub in range(n_sub):
        q0 = (pl.program_id(2) * n_sub + sub) * tq
        start = pl.multiple_of(jnp.clip(q0 - window, 0, seq - span), tq)
        rows = pl.ds(sub * tq, tq)
        qx = jnp.concatenate([q_ref[rows, g * LANES:(g + 1) * LANES] for g in range(groups)], axis=0)
        d = (q0 - start) + r - col
        bias = jnp.where((d >= 0) & (d < window), 0.0, NEG_INF)
        s = _dot_nt(qx, k_ref[pl.ds(start, span), :]) + jnp.concatenate([bias] * groups, axis=0)
        m = jnp.broadcast_to(jnp.max(s, axis=-1, keepdims=True), (groups * tq, LANES))
        if use_sinks:
            t_rep = (q0 + lax.broadcasted_iota(jnp.int32, (tq, LANES), 0)).astype(F32)
            sink = jnp.concatenate(
                [sink_ref[h * groups + g] + sl_ref[h * groups + g] * t_rep for g in range(groups)], axis=0)
            m = jnp.maximum(m, sink)
        e = jnp.exp(s - jnp.concatenate([m] * (span // LANES), axis=1)).astype(BF16)
        pv = _dot(e, v_ref[pl.ds(start, span), :])
        extra = None
        if use_sinks:
            esink = jnp.exp(sink - m)
            extra = [esink[g * tq:(g + 1) * tq] for g in range(groups)]
        o_ref[rows, :] = _normalize_pack([pv[g * tq:(g + 1) * tq] for g in range(groups)], extra).astype(o_ref.dtype)


def _band(slopes, sinks, proj, *, B, T, window, q_slot, k_slot, v_slot, use_sinks, n_sub, name):
    tq = 128
    n_sub = min(n_sub, T // tq)
    tb = tq * n_sub
    nt = T // tb
    groups = 4
    assert window % tq == 0 and window + tq <= T and T % tb == 0
    qb = q_slot // groups
    kern = functools.partial(_band_kernel, tq=tq, n_sub=n_sub, window=window, seq=T, use_sinks=use_sinks)
    return pl.pallas_call(
        kern,
        grid=(B, 2, nt),
        in_specs=[
            pl.BlockSpec(memory_space=pltpu.SMEM),
            pl.BlockSpec(memory_space=pltpu.SMEM),
            pl.BlockSpec((tb, groups * LANES), lambda b, h, i: (b * nt + i, qb + h)),
            pl.BlockSpec((T, LANES), lambda b, h, i: (b, k_slot + h)),
            pl.BlockSpec((T, LANES), lambda b, h, i: (b, v_slot + h)),
        ],
        out_specs=pl.BlockSpec((tb, groups * HEAD_DIM), lambda b, h, i: (b * nt + i, h)),
        out_shape=jax.ShapeDtypeStruct((B * T, 2 * groups * HEAD_DIM), BF16),
        compiler_params=_cparams(("arbitrary", "arbitrary", "arbitrary")),
        name=name,
    )(slopes, sinks, proj, proj, proj)


def _outproj_kernel(ocmp_ref, oslc_ref, owin_ref, oswa_ref, gate_ref, x_ref, mod_ref, ex_ref, bn_ref, bs_ref,
                    w_ref, g2_ref, wr_ref, x1_ref, u2_ref, lt_ref):
    sg = _sigmoid(gate_ref[...].astype(F32)).astype(BF16)
    o_nsa = (_dot(sg, ex_ref[0]) * ocmp_ref[...].astype(F32)
             + _dot(sg, ex_ref[1]) * oslc_ref[...].astype(F32)
             + _dot(sg, ex_ref[2]) * owin_ref[...].astype(F32))
    o_swa = oswa_ref[...].astype(F32)
    n1 = o_nsa * lax.rsqrt(jnp.mean(o_nsa * o_nsa, axis=-1, keepdims=True) + EPS) * bn_ref[...]
    n2 = o_swa * lax.rsqrt(jnp.mean(o_swa * o_swa, axis=-1, keepdims=True) + EPS) * bs_ref[...]
    merged = jnp.concatenate([n1, n2], axis=-1).astype(BF16)
    y = _dot(merged, w_ref[0])
    x1 = x_ref[...] + mod_ref[0, 2:3, :] * y
    x1_ref[...] = x1
    u2 = x1 * lax.rsqrt(jnp.mean(x1 * x1, axis=-1, keepdims=True) + EPS) * g2_ref[...]
    u2 = u2 * (1.0 + mod_ref[0, 4:5, :]) + mod_ref[0, 3:4, :]
    u2_ref[...] = _pack_bf16_pairs(u2)
    u_hi = u2.astype(BF16)
    u_lo = (u2 - u_hi.astype(F32)).astype(BF16)
    wr = wr_ref[...]
    w_hi = wr.astype(BF16)
    w_lo = (wr - w_hi.astype(F32)).astype(BF16)
    lt_ref[...] = _dot_nt(w_hi, u_hi) + _dot_nt(w_hi, u_lo) + _dot_nt(w_lo, u_hi)


def _outproj(ocmp, oslc, owin, oswa, proj, x2, mod, expand, beta_n, beta_s, w_out, norm2_g, wr_t, *, seq, layer):
    N, D = x2.shape
    tm = 1024
    tiles_per_seq = seq // tm
    row = lambda i: (i, 0)
    const2 = lambda i: (0, 0)
    return pl.pallas_call(
        _outproj_kernel,
        grid=(N // tm,),
        in_specs=[
            pl.BlockSpec((tm, D_NSA), row),
            pl.BlockSpec((tm, D_NSA), row),
            pl.BlockSpec((tm, D_NSA), row),
            pl.BlockSpec((tm, D_SWA), row),
            pl.BlockSpec((tm, LANES), lambda i: (i, SLOT_GATE)),
            pl.BlockSpec((tm, D), row),
            pl.BlockSpec((1, 6, D), lambda i: (i // tiles_per_seq, 0, 0)),
            pl.BlockSpec((N_NSA_BRANCHES, LANES, D_NSA), lambda i: (0, 0, 0)),
            pl.BlockSpec((1, D_NSA), const2),
            pl.BlockSpec((1, D_SWA), const2),
            pl.BlockSpec((1, D_MIX, D), lambda i: (layer, 0, 0)),
            pl.BlockSpec((1, D), const2),
            pl.BlockSpec((N_EXPERTS, D), const2),
        ],
        out_specs=[
            pl.BlockSpec((tm, D), row),
            pl.BlockSpec((tm, D // 2), row),
            pl.BlockSpec((N_EXPERTS, tm), lambda i: (0, i)),
        ],
        out_shape=[
            jax.ShapeDtypeStruct((N, D), F32),
            jax.ShapeDtypeStruct((N, D // 2), jnp.uint32),
            jax.ShapeDtypeStruct((N_EXPERTS, N), F32),
        ],
        compiler_params=_cparams(("arbitrary",)),
        name="outproj",
    )(ocmp, oslc, owin, oswa, proj, x2, mod, expand, beta_n, beta_s, w_out, norm2_g, wr_t)


def _route_kernel(lt_ref, tri_ref, d0_ref, d1_ref, w0_ref, w1_ref, seg_ref, nch_ref, tail_ref, *, tm):
    lt = lt_ref[...]
    e = jnp.exp(lt - jnp.max(lt, axis=0, keepdims=True))
    aff = e / jnp.sum(e, axis=0, keepdims=True)
    rows = [aff[k:k + 1, :] for k in range(N_EXPERTS)]
    epg = EXPERTS_PER_GROUP
    scores = []
    for gr in range(N_GROUPS):
        xs = rows[gr * epg:(gr + 1) * epg]
        top1 = functools.reduce(jnp.maximum, xs)
        second = None
        for a in range(epg):
            for b in range(a + 1, epg):
                mn = jnp.minimum(xs[a], xs[b])
                second = mn if second is None else jnp.maximum(second, mn)
        scores.append(top1 + second)
    taken = None
    in_group = []
    for gr in range(N_GROUPS):
        best = None
        for o in range(gr + 1, N_GROUPS):
            c = scores[gr] >= scores[o]
            best = c if best is None else (best & c)
        if best is None:
            best = jnp.full(scores[gr].shape, True)
        sel = best if taken is None else (best & ~taken)
        taken = sel if taken is None else (taken | sel)
        in_group.append(sel)
    ys = []
    for k in range(epg):
        yk = rows[k]
        for gr in range(1, N_GROUPS):
            yk = jnp.where(in_group[gr], rows[gr * epg + k], yk)
        ys.append(yk)
    chosen = []
    for k in range(epg):
        rk = jnp.zeros_like(ys[k])
        for o in range(epg):
            if o == k:
                continue
            before = (ys[o] >= ys[k]) if o < k else (ys[o] > ys[k])
            rk = rk + jnp.where(before, 1.0, 0.0)
        chosen.append(rk < TOP_K)
    total = functools.reduce(lambda a, b: a + b, [jnp.where(chosen[k], ys[k], 0.0) for k in range(epg)])
    first, seen = [], None
    for k in range(epg):
        first.append(chosen[k] if seen is None else (chosen[k] & ~seen))
        seen = chosen[k] if seen is None else (seen | chosen[k])
    second = [chosen[k] & ~first[k] for k in range(epg)]
    inv_total = 1.0 / total
    w_first = functools.reduce(lambda a, b: a + b, [jnp.where(first[k], ys[k], 0.0) for k in range(epg)]) * inv_total
    w_second = functools.reduce(lambda a, b: a + b, [jnp.where(second[k], ys[k], 0.0) for k in range(epg)]) * inv_total
    f_rows, s_rows = [], []
    for ex in range(N_EXPERTS):
        gr, k = divmod(ex, epg)
        f_rows.append(jnp.where(in_group[gr] & first[k], 1.0, 0.0))
        s_rows.append(jnp.where(in_group[gr] & second[k], 1.0, 0.0))
    fmat = jnp.concatenate(f_rows, axis=0)
    smat = jnp.concatenate(s_rows, axis=0)
    cmat = fmat + smat
    carry = jnp.zeros((N_EXPERTS, 1), F32)
    ranks = []
    for blk in range(tm // TRI):
        cb = cmat[:, blk * TRI:(blk + 1) * TRI]
        ranks.append(_dot(cb.astype(BF16), tri_ref[...]) + carry)
        carry = carry + jnp.sum(cb, axis=1, keepdims=True)
    rank = jnp.concatenate(ranks, axis=1)
    padded = jnp.floor((carry + (SEG_ALIGN - 1)) * (1.0 / SEG_ALIGN)) * SEG_ALIGN
    seg_rows, run = [], jnp.zeros((1, 1), F32)
    for ex in range(N_EXPERTS):
        seg_rows.append(run)
        run = run + padded[ex:ex + 1]
    seg = jnp.concatenate(seg_rows, axis=0)
    dest = rank + seg
    d0_ref[0] = jnp.sum(fmat * dest, axis=0, keepdims=True).astype(jnp.int32)
    d1_ref[0] = jnp.sum(smat * dest, axis=0, keepdims=True).astype(jnp.int32)
    w0_ref[...] = jnp.broadcast_to(w_first, (LANES, tm)).T
    w1_ref[...] = jnp.broadcast_to(w_second, (LANES, tm)).T
    seg_ref[0] = jnp.broadcast_to(seg, (N_EXPERTS, LANES)).astype(jnp.int32)
    whole = jnp.floor(carry * (1.0 / MOE_CHUNK))
    rem = carry - whole * MOE_CHUNK
    n_full = whole + jnp.where(rem > MOE_CHUNK // 2, 1.0, 0.0)
    half_tail = jnp.where((rem > 0) & (rem <= MOE_CHUNK // 2), 1.0, 0.0)
    nch_ref[0] = jnp.broadcast_to(n_full, (N_EXPERTS, LANES)).astype(jnp.int32)
    tail_ref[0] = jnp.broadcast_to(half_tail, (N_EXPERTS, LANES)).astype(jnp.int32)


def _route(lt, tri):
    E, N = lt.shape
    tm = MOE_TILE
    nt = N // tm
    return pl.pallas_call(
        functools.partial(_route_kernel, tm=tm),
        grid=(nt,),
        in_specs=[pl.BlockSpec((E, tm), lambda i: (0, i)), pl.BlockSpec((TRI, TRI), lambda i: (0, 0))],
        out_specs=[
            pl.BlockSpec((1, 1, tm), lambda i: (i, 0, 0)),
            pl.BlockSpec((1, 1, tm), lambda i: (i, 0, 0)),
            pl.BlockSpec((tm, LANES), lambda i: (i, 0)),
            pl.BlockSpec((tm, LANES), lambda i: (i, 0)),
            pl.BlockSpec((1, E, LANES), lambda i: (i, 0, 0)),
            pl.BlockSpec((1, E, LANES), lambda i: (i, 0, 0)),
            pl.BlockSpec((1, E, LANES), lambda i: (i, 0, 0)),
        ],
        out_shape=[
            jax.ShapeDtypeStruct((nt, 1, tm), jnp.int32),
            jax.ShapeDtypeStruct((nt, 1, tm), jnp.int32),
            jax.ShapeDtypeStruct((N, LANES), F32),
            jax.ShapeDtypeStruct((N, LANES), F32),
            jax.ShapeDtypeStruct((nt, E, LANES), jnp.int32),
            jax.ShapeDtypeStruct((nt, E, LANES), jnp.int32),
            jax.ShapeDtypeStruct((nt, E, LANES), jnp.int32),
        ],
        compiler_params=_cparams(("arbitrary",)),
        name="route",
    )(lt, tri)


def _experts_kernel(seg_ref, nch_ref, tail_ref, d0_ref, d1_ref, u_ref, w1_ref, w3_ref, w2_ref, o_ref, xs_ref, *, tm,
                    chunk):
    i = pl.program_id(0)
    step = pl.program_id(1)

    @pl.when(step == 0)
    def _():
        xs_ref[...] = jnp.zeros_like(xs_ref)
        o_ref[...] = jnp.zeros_like(o_ref)

        def move(grp, carry):
            t0 = pl.multiple_of(grp * 8, 8)
            blk = u_ref[pl.ds(t0, 8), :]
            for k in range(8):
                row = blk[k:k + 1, :]
                xs_ref[pl.ds(d0_ref[0, 0, t0 + k], 1), :] = row
                xs_ref[pl.ds(d1_ref[0, 0, t0 + k], 1), :] = row
            return carry

        lax.fori_loop(0, tm // 8, move, 0)

    for sub in range(EXPERTS_PER_STEP):
        slot = i * N_EXPERTS + step * EXPERTS_PER_STEP + sub
        start = seg_ref[slot]
        n_full = nch_ref[slot]

        def do_rows(r0, rows, sub=sub):
            lo, hi = _unpack_bf16_pairs(xs_ref[pl.ds(r0, rows), :])
            xb = jnp.concatenate([lo, hi], axis=-1).astype(BF16)
            a = _dot(xb, w1_ref[0, sub])
            hmid = a * _sigmoid(a) * _dot(xb, w3_ref[0, sub])
            o_ref[pl.ds(r0, rows), :] = _pack_bf16_pairs(_dot(hmid.astype(BF16), w2_ref[0, sub]))

        def do_chunk(j, carry, start=start, do_rows=do_rows):
            do_rows(pl.multiple_of(start + j * chunk, SEG_ALIGN), chunk)
            return carry

        lax.fori_loop(0, n_full, do_chunk, 0)

        @pl.when(tail_ref[slot] > 0)
        def _(start=start, n_full=n_full, do_rows=do_rows):
            do_rows(pl.multiple_of(start + n_full * chunk, SEG_ALIGN), chunk // 2)


def _experts(seg, nch, tail, d0, d1, u2p, w1, w3, w2, *, layer):
    N, half = u2p.shape
    D = 2 * half
    E = w1.shape[1]
    tm = MOE_TILE
    nt = N // tm
    kern = functools.partial(_experts_kernel, tm=tm, chunk=MOE_CHUNK)
    grid_spec = pltpu.PrefetchScalarGridSpec(
        num_scalar_prefetch=3,
        grid=(nt, E // EXPERTS_PER_STEP),
        in_specs=[
            pl.BlockSpec((1, 1, tm), lambda i, e, *_: (i, 0, 0), memory_space=pltpu.SMEM),
            pl.BlockSpec((1, 1, tm), lambda i, e, *_: (i, 0, 0), memory_space=pltpu.SMEM),
            pl.BlockSpec((tm, half), lambda i, e, *_: (i, 0)),
            pl.BlockSpec((1, EXPERTS_PER_STEP, D, D_EXPERT), lambda i, e, *_: (layer, e, 0, 0)),
            pl.BlockSpec((1, EXPERTS_PER_STEP, D, D_EXPERT), lambda i, e, *_: (layer, e, 0, 0)),
            pl.BlockSpec((1, EXPERTS_PER_STEP, D_EXPERT, D), lambda i, e, *_: (layer, e, 0, 0)),
        ],
        out_specs=pl.BlockSpec((MOE_ROWS, half), lambda i, e, *_: (i, 0)),
        scratch_shapes=[pltpu.VMEM((MOE_ROWS, half), jnp.uint32)],
    )
    return pl.pallas_call(
        kern,
        grid_spec=grid_spec,
        out_shape=jax.ShapeDtypeStruct((nt * MOE_ROWS, half), jnp.uint32),
        compiler_params=pltpu.CompilerParams(dimension_semantics=("arbitrary", "arbitrary"),
                                             vmem_limit_bytes=MOE_VMEM_LIMIT),
        name="experts",
    )(seg, nch, tail, d0, d1, u2p, w1, w3, w2)


def _combine_kernel(d0_ref, d1_ref, ys_ref, w0_ref, w1_ref, x1_ref, mod_ref, o_ref, g0_ref, g1_ref, *, tsub):
    base = pl.program_id(1) * tsub

    def move(grp, carry):
        t0 = pl.multiple_of(grp * 8, 8)
        for d_ref, g_ref in ((d0_ref, g0_ref), (d1_ref, g1_ref)):
            rows = [ys_ref[pl.ds(d_ref[0, 0, base + t0 + k], 1), :] for k in range(8)]
            g_ref[pl.ds(t0, 8), :] = jnp.concatenate(rows, axis=0)
        return carry

    lax.fori_loop(0, tsub // 8, move, 0)
    lo0, hi0 = _unpack_bf16_pairs(g0_ref[...])
    lo1, hi1 = _unpack_bf16_pairs(g1_ref[...])
    half = g0_ref.shape[1]
    w0 = jnp.concatenate([w0_ref[...]] * (half // LANES), axis=1)
    w1 = jnp.concatenate([w1_ref[...]] * (half // LANES), axis=1)
    o_ref[:, :half] = x1_ref[:, :half] + mod_ref[0, 5:6, :half] * (w0 * lo0 + w1 * lo1)
    o_ref[:, half:] = x1_ref[:, half:] + mod_ref[0, 5:6, half:] * (w0 * hi0 + w1 * hi1)


def _combine(d0, d1, ys, w0rep, w1rep, x1, mod, *, seq):
    N, D = x1.shape
    half = D // 2
    tm = MOE_TILE
    tsub = 1024
    nt = N // tm
    ns = tm // tsub
    sub = lambda i, j: (i * ns + j, 0)
    return pl.pallas_call(
        functools.partial(_combine_kernel, tsub=tsub),
        grid=(nt, ns),
        in_specs=[
            pl.BlockSpec((1, 1, tm), lambda i, j: (i, 0, 0), memory_space=pltpu.SMEM),
            pl.BlockSpec((1, 1, tm), lambda i, j: (i, 0, 0), memory_space=pltpu.SMEM),
            pl.BlockSpec((MOE_ROWS, half), lambda i, j: (i, 0)),
            pl.BlockSpec((tsub, LANES), sub),
            pl.BlockSpec((tsub, LANES), sub),
            pl.BlockSpec((tsub, D), sub),
            pl.BlockSpec((1, 6, D), lambda i, j: ((i * tm + j * tsub) // seq, 0, 0)),
        ],
        out_specs=pl.BlockSpec((tsub, D), sub),
        out_shape=jax.ShapeDtypeStruct((N, D), F32),
        scratch_shapes=[pltpu.VMEM((tsub, half), jnp.uint32), pltpu.VMEM((tsub, half), jnp.uint32)],
        compiler_params=_cparams(("arbitrary", "arbitrary")),
        name="combine",
    )(d0, d1, ys, w0rep, w1rep, x1, mod)


def _alibi_slopes(first, count):
    hh = jnp.arange(first + 1, first + count + 1, dtype=F32)
    return jnp.exp2(-8.0 * hh / N_HEADS_TOTAL)


def _bd_const():
    idx = np.arange(2 * LANES) // HEAD_DIM
    return jnp.asarray((idx[:, None] == idx[None, :]).astype(np.float32), BF16)


def _blkcols_const(T):
    blk = np.arange(T) // SEL_LEN
    m = np.where(np.arange(LANES)[None, :] == blk[:, None], NEG_INF, 0.0).astype(np.float32)
    return jnp.asarray(m, BF16)


def _overlap_const(T):
    nc = T // CMP_STRIDE
    n_sel = T // SEL_LEN
    cs = np.arange(nc) * CMP_STRIDE
    ss = np.arange(n_sel) * SEL_LEN
    ov = np.clip(np.minimum(cs[:, None] + CMP_LEN, ss[None, :] + SEL_LEN)
                 - np.maximum(cs[:, None], ss[None, :]), 0, None) / CMP_LEN
    return jnp.asarray(ov.T.astype(np.float32))


def _expand_const():
    ex = np.zeros((N_NSA_BRANCHES, LANES, D_NSA), np.float32)
    for br in range(N_NSA_BRANCHES):
        for hd in range(N_HEADS_NSA):
            ex[br, hd * N_NSA_BRANCHES + br, hd * HEAD_DIM:(hd + 1) * HEAD_DIM] = 1.0
    return jnp.asarray(ex, BF16)


def _compact_w_in(w):
    parts = [w[..., a:b] for a, b in (_C_QN, _C_QS, _C_KSL, _C_KW, _C_KS, _C_VSL, _C_VW, _C_VS, _C_KC, _C_VC, _C_GATE)]
    pad = jnp.zeros(w.shape[:-1] + (LANES - (_C_GATE[1] - _C_GATE[0]),), w.dtype)
    return jnp.concatenate(parts + [pad], axis=-1).astype(BF16)


def _qconst(slopes_q):
    hi = slopes_q.astype(BF16).astype(F32)
    lo = (slopes_q - hi).astype(BF16).astype(F32)
    cols = jnp.stack([64.0 * hi, 64.0 * lo, hi, lo], axis=1)
    return jnp.zeros((slopes_q.shape[0], LANES), F32).at[:, HEAD_DIM:HEAD_DIM + 4].set(cols)


def kernel(x, c, w_router, ada_w, ada_b, norm1_g, norm2_g, w_in, w_out, nsa_q_gain, nsa_k_gain, cmp_pos_k,
           cmp_pos_v, cmp_w1_k, cmp_w2_k, cmp_w1_v, cmp_w2_v, swa_q_gain, swa_k_gain, swa_sinks, beta_nsa,
           beta_swa, moe_w1, moe_w3, moe_w2):
    B, T, D = x.shape
    L = ada_w.shape[0]
    N = B * T
    nc = T // CMP_STRIDE
    assert D == D_MODEL and T % 1024 == 0 and T <= 64 * 128 and B <= 8

    slopes_nsa = _alibi_slopes(N_HEADS_SWA, N_HEADS_NSA)
    slopes_swa = _alibi_slopes(0, N_HEADS_SWA)
    qconst = _qconst(jnp.concatenate([slopes_nsa, slopes_swa]))
    bd = _bd_const()
    blkcols = _blkcols_const(T)
    ov = _overlap_const(T)
    expand = _expand_const()
    wr_t = w_router.T
    tri = jnp.asarray(np.triu(np.ones((TRI, TRI), np.float32), 1), BF16)
    assert N % MOE_TILE == 0
    moe_w1b, moe_w3b, moe_w2b = moe_w1.astype(BF16), moe_w3.astype(BF16), moe_w2.astype(BF16)
    w_outb = w_out.astype(BF16)
    w_inb = _compact_w_in(w_in)
    cmp_w1s = jnp.stack([cmp_w1_k, cmp_w1_v], axis=1).reshape(2 * L, CMP_LEN * HEAD_DIM, CMP_HIDDEN).astype(BF16)
    cmp_w2s = jnp.pad(jnp.stack([cmp_w2_k, cmp_w2_v], axis=1).reshape(2 * L, CMP_HIDDEN, HEAD_DIM),
                      ((0, 0), (0, 0), (0, LANES - HEAD_DIM))).astype(BF16)
    cmp_pos = jnp.stack([cmp_pos_k, cmp_pos_v], axis=1).reshape(2 * L, 1, CMP_LEN * HEAD_DIM)
    cmp_pos = jnp.broadcast_to(cmp_pos, (2 * L, 8, CMP_LEN * HEAD_DIM)).astype(BF16)
    zero_sinks = jnp.zeros((N_HEADS_NSA,), F32)

    c_pad = jnp.zeros((8, D), F32).at[:B].set(c)
    mod_all = _ada(c_pad, ada_w, ada_b)[:, :B].reshape(L, B, 6, D)

    x2 = x.reshape(N, D)
    for l in range(L):
        mod = mod_all[l]
        tile2 = lambda g: jnp.tile(g, 2)
        gain_c = jnp.concatenate([
            jnp.tile(nsa_q_gain[l], N_HEADS_NSA) * ATTN_SCALE,
            jnp.tile(swa_q_gain[l], N_HEADS_SWA) * ATTN_SCALE,
            tile2(nsa_k_gain[l, 1]), tile2(nsa_k_gain[l, 2]), tile2(swa_k_gain[l])]).reshape(1, -1)
        proj, xc = _inproj(x2, mod, norm1_g[l].reshape(1, D), w_inb, gain_c, qconst, bd, seq=T, layer=l)
        kgain = jnp.pad(nsa_k_gain[l, 0], (0, LANES - HEAD_DIM)).reshape(1, LANES)
        kvc = _cmpmlp(xc, cmp_w1s, cmp_w2s, cmp_pos, kgain, B=B, layer=l)

        o_cmp, notsel = _cmpattn(proj, kvc, ov, B=B, T=T)
        o_slc = _selattn(proj, notsel, blkcols, B=B, T=T)
        o_win = _band(slopes_nsa, zero_sinks, proj, B=B, T=T, window=NSA_WINDOW, q_slot=SLOT_QN,
                      k_slot=SLOT_KW, v_slot=SLOT_VW, use_sinks=False, n_sub=16, name="winattn")
        o_swa = _band(slopes_swa, swa_sinks[l], proj, B=B, T=T, window=SWA_WINDOW, q_slot=SLOT_QS,
                      k_slot=SLOT_KS, v_slot=SLOT_VS, use_sinks=True, n_sub=16, name="swaattn")

        x1, u2, lt = _outproj(o_cmp, o_slc, o_win, o_swa, proj, x2, mod, expand,
                              beta_nsa[l].reshape(1, -1), beta_swa[l].reshape(1, -1), w_outb,
                              norm2_g[l].reshape(1, D), wr_t, seq=T, layer=l)
        d0, d1, w0rep, w1rep, seg, nch, tail = _route(lt, tri)
        ys = _experts(seg[:, :, 0].reshape(-1), nch[:, :, 0].reshape(-1), tail[:, :, 0].reshape(-1), d0, d1, u2,
                      moe_w1b, moe_w3b, moe_w2b, layer=l)
        x2 = _combine(d0, d1, ys, w0rep, w1rep, x1, mod, seq=T)
    return x2.reshape(B, T, D)
```

```python
import functools

import numpy as np
import jax
import jax.numpy as jnp
from jax import lax
from jax.experimental import pallas as pl
from jax.experimental.pallas import tpu as pltpu

D_MODEL = 1024
HEAD_DIM = 64
N_HEADS_NSA = 8
N_KV_NSA = 2
G_NSA = N_HEADS_NSA // N_KV_NSA
N_HEADS_SWA = 8
D_NSA = N_HEADS_NSA * HEAD_DIM
D_SWA = N_HEADS_SWA * HEAD_DIM
D_MIX = D_NSA + D_SWA
N_HEADS_TOTAL = N_HEADS_NSA + N_HEADS_SWA
N_NSA_BRANCHES = 3
CMP_LEN = 32
CMP_STRIDE = 16
CMP_HIDDEN = 256
SEL_LEN = 64
SEL_TOPK = 16
NSA_WINDOW = 512
SWA_WINDOW = 128
N_EXPERTS = 16
N_GROUPS = 4
EXPERTS_PER_GROUP = N_EXPERTS // N_GROUPS
TOP_K = 2
D_EXPERT = 512
EPS = 1e-6
NEG_INF = -1e30
ATTN_SCALE = HEAD_DIM ** -0.5

F32 = jnp.float32
BF16 = jnp.bfloat16

LANES = 128
VMEM_LIMIT = 48 * 1024 * 1024

MOE_TILE = 2048
MOE_CHUNK = 256
SEG_ALIGN = 16
TRI = 256
MOE_ROWS = TOP_K * MOE_TILE + N_EXPERTS * SEG_ALIGN + MOE_CHUNK
EXPERTS_PER_STEP = 2
MOE_VMEM_LIMIT = 56 * 1024 * 1024

SLOT_QN = 0
SLOT_QS = 8
SLOT_KSL = 16
SLOT_KW = 18
SLOT_KS = 20
SLOT_VSL = 22
SLOT_VW = 24
SLOT_VS = 26
SLOT_GATE = 28
N_SLOTS = 29
N_CHUNKS = 17
N_NORM_CHUNKS = 11

_C_QN = (0, 512)
_C_KC = (512, 640)
_C_VC = (640, 768)
_C_KSL = (768, 896)
_C_VSL = (896, 1024)
_C_KW = (1024, 1152)
_C_VW = (1152, 1280)
_C_GATE = (1280, 1304)
_C_QS = (1304, 1816)
_C_KS = (1816, 1944)
_C_VS = (1944, 2072)


def _cparams(sem):
    return pltpu.CompilerParams(dimension_semantics=sem, vmem_limit_bytes=VMEM_LIMIT)


def _dot(a, b, **kw):
    return jnp.dot(a, b, preferred_element_type=F32, **kw)


def _dot_nt(a, b):
    return lax.dot_general(a, b, (((1,), (1,)), ((), ())), preferred_element_type=F32)


def _sigmoid(x):
    return 1.0 / (1.0 + jnp.exp(-x))


def _pack_bf16_pairs(x):
    n = x.shape[1] // 2
    return pltpu.pack_elementwise([x[:, :n], x[:, n:]], packed_dtype=BF16)


def _unpack_bf16_pairs(w):
    lo = pltpu.unpack_elementwise(w, index=0, packed_dtype=BF16, unpacked_dtype=F32)
    hi = pltpu.unpack_elementwise(w, index=1, packed_dtype=BF16, unpacked_dtype=F32)
    return lo, hi


def _ada_kernel(c_ref, w_ref, b_ref, o_ref):
    c = c_ref[...]
    cond = c * _sigmoid(c)
    o_ref[0] = _dot(cond.astype(BF16), w_ref[0].astype(BF16)) + b_ref[0]


def _ada(c_pad, ada_w, ada_b):
    L, D, N6 = ada_w.shape
    tn = 1536
    return pl.pallas_call(
        _ada_kernel,
        grid=(L, N6 // tn),
        in_specs=[
            pl.BlockSpec((8, D), lambda l, j: (0, 0)),
            pl.BlockSpec((1, D, tn), lambda l, j: (l, 0, j)),
            pl.BlockSpec((1, 1, tn), lambda l, j: (l, 0, j)),
        ],
        out_specs=pl.BlockSpec((1, 8, tn), lambda l, j: (l, 0, j)),
        out_shape=jax.ShapeDtypeStruct((L, 8, N6), F32),
        compiler_params=_cparams(("arbitrary", "arbitrary")),
        name="ada",
    )(c_pad, ada_w, ada_b.reshape(L, 1, N6))


def _inproj_kernel(x_ref, mod_ref, g_ref, w_ref, gain_ref, qc_ref, bd_ref, o_ref, xc_ref, scr_ref, *, tm, seq):
    i = pl.program_id(0)
    x = x_ref[...]
    ms = jnp.mean(x * x, axis=-1, keepdims=True)
    u = x * lax.rsqrt(ms + EPS) * g_ref[...]
    u = u * (1.0 + mod_ref[0, 1:2, :]) + mod_ref[0, 0:1, :]
    y = _dot(u.astype(BF16), w_ref[0])

    lane = lax.broadcasted_iota(jnp.int32, (tm, LANES), 1)
    low = lane < HEAD_DIM
    t = lax.broadcasted_iota(jnp.int32, (tm, LANES), 0) + lax.rem(i * tm, seq)
    hi_part = (t >> 6).astype(F32)
    lo_part = (t & 63).astype(F32)
    poscols = jnp.where((lane == 64) | (lane == 65), hi_part,
                        jnp.where((lane == 66) | (lane == 67), lo_part, 0.0))
    bd = bd_ref[...]

    def put(slot, val):
        o_ref[:, slot * LANES:(slot + 1) * LANES] = val.astype(o_ref.dtype)

    normed = {}
    for c0 in range(0, N_NORM_CHUNKS, 2):
        width = min(2, N_NORM_CHUNKS - c0) * LANES
        wide = y[:, c0 * LANES:c0 * LANES + width]
        msb = _dot((wide * wide).astype(BF16), bd[:width, :width]) * (1.0 / HEAD_DIM)
        wide = wide * lax.rsqrt(msb + EPS) * gain_ref[:, c0 * LANES:c0 * LANES + width]
        for k in range(width // LANES):
            normed[c0 + k] = wide[:, k * LANES:(k + 1) * LANES]

    for c in range(N_CHUNKS):
        blk = normed[c] if c < N_NORM_CHUNKS else y[:, c * LANES:(c + 1) * LANES]
        if c < 14:
            rolled = pltpu.roll(blk, HEAD_DIM, axis=1)
            if c < 8:
                f0 = qc_ref[2 * c:2 * c + 1, :]
                f1 = qc_ref[2 * c + 1:2 * c + 2, :]
                s0 = 2 * c
            elif c < N_NORM_CHUNKS:
                f0 = f1 = poscols
                s0 = SLOT_KSL + 2 * (c - 8)
            else:
                f0 = f1 = 1.0
                s0 = SLOT_VSL + 2 * (c - N_NORM_CHUNKS)
            put(s0, jnp.where(low, blk, f0))
            put(s0 + 1, jnp.where(low, rolled, f1))
        elif c == 16:
            put(SLOT_GATE, blk)
        else:
            scr_ref[...] = blk
            nck = tm // CMP_STRIDE
            low_c = lax.broadcasted_iota(jnp.int32, (nck, LANES), 1) < HEAD_DIM
            head_cols = CMP_STRIDE * HEAD_DIM
            for pair in range(CMP_STRIDE // 2):
                ta = scr_ref[pl.ds(2 * pair, nck, stride=CMP_STRIDE), :]
                tb = scr_ref[pl.ds(2 * pair + 1, nck, stride=CMP_STRIDE), :]
                h0 = jnp.where(low_c, ta, pltpu.roll(tb, HEAD_DIM, axis=1))
                h1 = jnp.where(low_c, pltpu.roll(ta, HEAD_DIM, axis=1), tb)
                xc_ref[c - 14, :, pair * LANES:(pair + 1) * LANES] = h0.astype(xc_ref.dtype)
                xc_ref[c - 14, :, head_cols + pair * LANES:head_cols + (pair + 1) * LANES] = h1.astype(xc_ref.dtype)


def _inproj(x2, mod, norm_g, w_c, gain_c, qconst, bd, *, seq, layer):
    N, D = x2.shape
    tm = 1024
    tiles_per_seq = seq // tm
    kern = functools.partial(_inproj_kernel, tm=tm, seq=seq)
    nchunk = tm // CMP_STRIDE
    xc_shape = jax.ShapeDtypeStruct((2, N // CMP_STRIDE, CMP_STRIDE * LANES), BF16)
    return pl.pallas_call(
        kern,
        grid=(N // tm,),
        in_specs=[
            pl.BlockSpec((tm, D), lambda i: (i, 0)),
            pl.BlockSpec((1, 6, D), lambda i: (i // tiles_per_seq, 0, 0)),
            pl.BlockSpec((1, D), lambda i: (0, 0)),
            pl.BlockSpec((1, D, N_CHUNKS * LANES), lambda i: (layer, 0, 0)),
            pl.BlockSpec((1, N_NORM_CHUNKS * LANES), lambda i: (0, 0)),
            pl.BlockSpec((16, LANES), lambda i: (0, 0)),
            pl.BlockSpec((2 * LANES, 2 * LANES), lambda i: (0, 0)),
        ],
        out_specs=[
            pl.BlockSpec((tm, N_SLOTS * LANES), lambda i: (i, 0)),
            pl.BlockSpec((2, nchunk, CMP_STRIDE * LANES), lambda i: (0, i, 0)),
        ],
        out_shape=[jax.ShapeDtypeStruct((N, N_SLOTS * LANES), BF16), xc_shape],
        scratch_shapes=[pltpu.VMEM((tm, LANES), F32)],
        compiler_params=_cparams(("arbitrary",)),
        name="inproj",
    )(x2, mod, norm_g, w_c, gain_c, qconst, bd)


def _gelu_tanh(x):
    return 0.5 * x * (1.0 + jnp.tanh(np.sqrt(2.0 / np.pi).astype(np.float32) * (x + 0.044715 * (x * x * x))))


def _cmpmlp_kernel(x_ref, w1_ref, w2_ref, pos_ref, gain_ref, o_ref, *, nc):
    kv = pl.program_id(0)
    half = CMP_STRIDE * HEAD_DIM
    bias = _dot(pos_ref[0], w1_ref[0])[0:1]
    lane = lax.broadcasted_iota(jnp.int32, (nc, LANES), 1)
    last = lax.broadcasted_iota(jnp.int32, (nc, LANES), 0) * CMP_STRIDE + (CMP_LEN - 1)
    poscols = jnp.where((lane == 64) | (lane == 65), (last >> 6).astype(F32),
                        jnp.where((lane == 66) | (lane == 67), (last & 63).astype(F32), 0.0))
    for hd in range(N_KV_NSA):
        xc = x_ref[0, :, hd * half:(hd + 1) * half]
        first = _dot(xc, w1_ref[0, :half, :])
        second = _dot(xc, w1_ref[0, half:, :])
        h = _gelu_tanh(first + pltpu.roll(second, nc - 1, axis=0) + bias)
        z = _dot(h.astype(BF16), w2_ref[0])
        zn = z * lax.rsqrt(jnp.sum(z * z, axis=-1, keepdims=True) * (1.0 / HEAD_DIM) + EPS) * gain_ref[...]
        o_ref[0, 0, hd] = jnp.where(kv == 0, zn + poscols, z + jnp.where(lane >= HEAD_DIM, 1.0, 0.0))


def _cmpmlp(xc, w1s, w2s, pos, gain, *, B, layer):
    _, nchunks, K = xc.shape
    nc = nchunks // B
    win = CMP_LEN * HEAD_DIM
    kern = functools.partial(_cmpmlp_kernel, nc=nc)
    return pl.pallas_call(
        kern,
        grid=(2, B),
        in_specs=[
            pl.BlockSpec((1, nc, K), lambda kv, b: (kv, b, 0)),
            pl.BlockSpec((1, win, CMP_HIDDEN), lambda kv, b: (2 * layer + kv, 0, 0)),
            pl.BlockSpec((1, CMP_HIDDEN, LANES), lambda kv, b: (2 * layer + kv, 0, 0)),
            pl.BlockSpec((1, 8, win), lambda kv, b: (2 * layer + kv, 0, 0)),
            pl.BlockSpec((1, LANES), lambda kv, b: (0, 0)),
        ],
        out_specs=pl.BlockSpec((1, 1, N_KV_NSA, nc, LANES), lambda kv, b: (kv, b, 0, 0, 0)),
        out_shape=jax.ShapeDtypeStruct((2, B, N_KV_NSA, nc, LANES), F32),
        compiler_params=_cparams(("arbitrary", "arbitrary")),
        name="cmpmlp",
    )(xc, w1s, w2s, pos, gain)


def _normalize_pack(pvs, extra=None):
    lane = lax.broadcasted_iota(jnp.int32, pvs[0].shape, 1)
    low = lane < HEAD_DIM
    pairs = []
    for p in range(len(pvs) // 2):
        even, odd = pvs[2 * p], pvs[2 * p + 1]
        den_e = even if extra is None else even + extra[2 * p]
        den_o = odd if extra is None else odd + extra[2 * p + 1]
        o_e = even * (1.0 / pltpu.roll(den_e, HEAD_DIM, axis=1))
        o_o = pltpu.roll(odd, HEAD_DIM, axis=1) * (1.0 / den_o)
        pairs.append(jnp.where(low, o_e, o_o))
    return jnp.concatenate(pairs, axis=-1)


def _cmpattn_kernel(q_ref, kc_ref, vc_ref, ov_ref, o_alias, ns_alias, o_ref, ns_ref, *, tq, n_sub, nc, n_cmp, n_sel,
                    top, tile0):
    del o_alias, ns_alias
    kc = kc_ref[0, 0, 0].astype(BF16)
    vc = vc_ref[0, 0, 0].astype(BF16)
    for sub in range(n_sub):
        q0 = ((tile0 + pl.program_id(2)) * n_sub + sub) * tq
        rows = pl.ds(sub * tq, tq)
        _cmpattn_subtile(q_ref.at[rows, :], kc, vc, ov_ref, o_ref.at[rows, :], ns_ref.at[0, 0, rows, :], q0,
                         tq=tq, nc=nc, n_cmp=n_cmp, n_sel=n_sel, top=top)


def _cmpattn_subtile(q_ref, kc, vc, ov_ref, o_ref, ns_ref, q0, *, tq, nc, n_cmp, n_sel, top):
    t_col = q0 + lax.broadcasted_iota(jnp.int32, (tq, 1), 0)
    row_valid = (t_col >= CMP_LEN - 1).astype(F32)
    t_row = q0 + lax.broadcasted_iota(jnp.int32, (1, tq), 1)
    n_col = lax.broadcasted_iota(jnp.int32, (nc, 1), 0)
    valid_t = (t_row - (n_col * CMP_STRIDE + (CMP_LEN - 1)) >= 0) & (n_col < n_cmp)
    bias_t = jnp.where(valid_t, 0.0, NEG_INF)
    row_valid_t = (t_row >= CMP_LEN - 1).astype(F32)

    outs = []
    psum_t = jnp.zeros((nc, tq), F32)
    for g in range(G_NSA):
        qg = q_ref[:, g * LANES:(g + 1) * LANES]
        st = _dot_nt(kc, qg) + bias_t
        et = jnp.exp(st - jnp.max(st, axis=0, keepdims=True))
        outs.append(lax.dot_general(et.astype(BF16), vc, (((0,), (0,)), ((), ())), preferred_element_type=F32))
        psum_t = psum_t + et * (1.0 / jnp.sum(et, axis=0, keepdims=True))
    o_ref[...] = (_normalize_pack(outs) * row_valid).astype(o_ref.dtype)

    psum_t = psum_t * row_valid_t
    imp_t = _dot(ov_ref[...], psum_t, precision=lax.Precision.HIGHEST)
    j = lax.broadcasted_iota(jnp.int32, (n_sel, 1), 0)
    cur = t_row >> 6
    forced = (j == 0) | (j == cur) | (j == cur - 1)
    v = jnp.where(forced, 1e9, jnp.where(j > cur, NEG_INF, imp_t))
    n_grp = n_sel // 8
    vg = [v[8 * r:8 * r + 8, :] for r in range(n_grp)]
    jg = lax.broadcasted_iota(jnp.int32, (8, 1), 0)
    cnt = [jnp.zeros((8, tq), F32) for _ in range(n_grp)]
    for jp in range(n_sel):
        vj = v[jp:jp + 1, :]
        for r in range(n_grp):
            ge = jnp.where(vj >= vg[r], 1.0, 0.0)
            gt = jnp.where(vj > vg[r], 1.0, 0.0)
            if 8 * r > jp:
                inc = ge
            elif 8 * r + 7 < jp:
                inc = gt
            else:
                inc = jnp.where(jg + 8 * r > jp, ge, gt)
            cnt[r] = cnt[r] + inc
    notsel_t = jnp.concatenate([jnp.where(cn < top, 0.0, 1.0) for cn in cnt], axis=0)
    if n_sel < LANES:
        notsel_t = jnp.concatenate([notsel_t, jnp.zeros((LANES - n_sel, tq), F32)], axis=0)
    ns_ref[...] = notsel_t.T.astype(ns_ref.dtype)


def _cmpattn(proj, kvc, ov, *, B, T):
    tq = 256
    nt = T // tq
    nc = T // CMP_STRIDE
    n_cmp = (T - CMP_LEN) // CMP_STRIDE + 1
    n_sel = T // SEL_LEN
    top = min(SEL_TOPK, n_sel)
    assert n_sel <= LANES and n_sel % 8 == 0
    n_parts = 4 if (nt % 4 == 0 and n_sel % 32 == 0) else 1
    n_sub = 4 if (nt // n_parts) % 4 == 0 else 1
    tb = n_sub * tq
    steps = nt // n_sub
    sp = steps // n_parts
    o_cmp = jnp.zeros((B * T, D_NSA), BF16)
    notsel = jnp.zeros((B, N_KV_NSA, T, LANES), BF16)
    for part in range(n_parts):
        nc_p = nc * (part + 1) // n_parts
        ns_p = n_sel * (part + 1) // n_parts
        kern = functools.partial(_cmpattn_kernel, tq=tq, n_sub=n_sub, nc=nc_p, n_cmp=n_cmp, n_sel=ns_p, top=top,
                                 tile0=part * sp)
        qrow = lambda b, h, i, part=part: (b * steps + part * sp + i, h)
        o_cmp, notsel = pl.pallas_call(
            kern,
            grid=(B, N_KV_NSA, sp),
            in_specs=[
                pl.BlockSpec((tb, G_NSA * LANES), qrow),
                pl.BlockSpec((1, 1, 1, nc_p, LANES), lambda b, h, i: (0, b, h, 0, 0)),
                pl.BlockSpec((1, 1, 1, nc_p, LANES), lambda b, h, i: (1, b, h, 0, 0)),
                pl.BlockSpec((ns_p, nc_p), lambda b, h, i: (0, 0)),
                pl.BlockSpec(memory_space=pl.ANY),
                pl.BlockSpec(memory_space=pl.ANY),
            ],
            out_specs=[
                pl.BlockSpec((tb, G_NSA * HEAD_DIM), qrow),
                pl.BlockSpec((1, 1, tb, LANES), lambda b, h, i, part=part: (b, h, part * sp + i, 0)),
            ],
            out_shape=[
                jax.ShapeDtypeStruct((B * T, D_NSA), BF16),
                jax.ShapeDtypeStruct((B, N_KV_NSA, T, LANES), BF16),
            ],
            input_output_aliases={4: 0, 5: 1},
            compiler_params=_cparams(("arbitrary", "arbitrary", "arbitrary")),
            name="cmpattn",
        )(proj, kvc, kvc, ov[:ns_p, :nc_p], o_cmp, notsel)
    return o_cmp, notsel


def _selattn_kernel(q_ref, ns_ref, k_ref, v_ref, bc_ref, o_ref, qx_ref, s0_ref, s1_ref, mx0_ref, mx1_ref, m_ref,
                    acc_ref, *, tq, tk):
    diag = pl.program_id(2)
    ns = ns_ref[0, 0]
    for g in range(G_NSA):
        qx_ref[g * tq:(g + 1) * tq, :] = jnp.concatenate([q_ref[:, g * LANES:(g + 1) * LANES], ns], axis=-1)
    m_ref[...] = jnp.full(m_ref.shape, NEG_INF, F32)
    acc_ref[...] = jnp.zeros(acc_ref.shape, F32)

    def produce(s_ref, mx_ref, kt, masked):
        start = pl.multiple_of(kt * tk, tk)
        kx = jnp.concatenate([k_ref[pl.ds(start, tk), :], bc_ref[pl.ds(start, tk), :]], axis=-1)
        s = _dot_nt(qx_ref[...], kx)
        if masked:
            r = lax.broadcasted_iota(jnp.int32, (tq, tk), 0)
            col = lax.broadcasted_iota(jnp.int32, (tq, tk), 1)
            s = s + jnp.concatenate([jnp.where(col <= r, 0.0, NEG_INF)] * G_NSA, axis=0)
        s_ref[...] = s
        mx_ref[...] = jnp.broadcast_to(jnp.max(s, axis=-1, keepdims=True), mx_ref.shape)

    def consume(s_ref, mx_ref, kt):
        m = m_ref[...]
        m_new = jnp.maximum(m, mx_ref[...])
        p = jnp.exp(s_ref[...] - jnp.concatenate([m_new] * (tk // LANES), axis=1)).astype(BF16)
        vx = v_ref[pl.ds(pl.multiple_of(kt * tk, tk), tk), :]
        acc_ref[...] = jnp.exp(m - m_new) * acc_ref[...] + _dot(p, vx)
        m_ref[...] = m_new

    def tile(k):
        return k - 1

    produce(s0_ref, mx0_ref, diag, True)

    def pair(jj, carry):
        k = 2 * jj
        produce(s1_ref, mx1_ref, tile(k + 1), False)
        consume(s0_ref, mx0_ref, jnp.where(k == 0, diag, tile(k)))
        produce(s0_ref, mx0_ref, tile(k + 2), False)
        consume(s1_ref, mx1_ref, tile(k + 1))
        return carry

    n_pairs = lax.shift_right_logical(diag, 1)
    lax.fori_loop(0, n_pairs, pair, 0)
    k_last = 2 * n_pairs

    @pl.when(k_last == diag)
    def _():
        consume(s0_ref, mx0_ref, jnp.where(diag == 0, diag, tile(diag)))

    @pl.when(k_last != diag)
    def _():
        produce(s1_ref, mx1_ref, tile(diag), False)
        consume(s0_ref, mx0_ref, jnp.where(k_last == 0, diag, tile(k_last)))
        consume(s1_ref, mx1_ref, tile(diag))

    acc = acc_ref[...]
    o_ref[...] = _normalize_pack([acc[g * tq:(g + 1) * tq] for g in range(G_NSA)]).astype(o_ref.dtype)


def _selattn(proj, notsel, blkcols, *, B, T):
    tq = 512
    tk = 512
    nt = T // tq
    m_rows = G_NSA * tq
    assert T % tk == 0 and tk == tq
    kern = functools.partial(_selattn_kernel, tq=tq, tk=tk)
    return pl.pallas_call(
        kern,
        grid=(B, N_KV_NSA, nt),
        in_specs=[
            pl.BlockSpec((tq, G_NSA * LANES), lambda b, h, i: (b * nt + i, h)),
            pl.BlockSpec((1, 1, tq, LANES), lambda b, h, i: (b, h, i, 0)),
            pl.BlockSpec((T, LANES), lambda b, h, i: (b, SLOT_KSL + h)),
            pl.BlockSpec((T, LANES), lambda b, h, i: (b, SLOT_VSL + h)),
            pl.BlockSpec((T, LANES), lambda b, h, i: (0, 0)),
        ],
        out_specs=pl.BlockSpec((tq, G_NSA * HEAD_DIM), lambda b, h, i: (b * nt + i, h)),
        out_shape=jax.ShapeDtypeStruct((B * T, D_NSA), BF16),
        scratch_shapes=[
            pltpu.VMEM((m_rows, 2 * LANES), BF16),
            pltpu.VMEM((m_rows, tk), F32),
            pltpu.VMEM((m_rows, tk), F32),
            pltpu.VMEM((m_rows, LANES), F32),
            pltpu.VMEM((m_rows, LANES), F32),
            pltpu.VMEM((m_rows, LANES), F32),
            pltpu.VMEM((m_rows, LANES), F32),
        ],
        compiler_params=_cparams(("arbitrary", "arbitrary", "arbitrary")),
        name="selattn",
    )(proj, notsel, proj, proj, blkcols)


def _band_kernel(sl_ref, sink_ref, q_ref, k_ref, v_ref, o_ref, *, tq, n_sub, window, seq, use_sinks):
    h = pl.program_id(1)
    span = window + tq
    groups = q_ref.shape[1] // LANES
    r = lax.broadcasted_iota(jnp.int32, (tq, span), 0)
    col = lax.broadcasted_iota(jnp.int32, (tq, span), 1)
    for sub in range(n_sub):
        q0 = (pl.program_id(2) * n_sub + sub) * tq
        start = pl.multiple_of(jnp.clip(q0 - window, 0, seq - span), tq)
        rows = pl.ds(sub * tq, tq)
        qx = jnp.concatenate([q_ref[rows, g * LANES:(g + 1) * LANES] for g in range(groups)], axis=0)
        d = (q0 - start) + r - col
        bias = jnp.where((d >= 0) & (d < window), 0.0, NEG_INF)
        s = _dot_nt(qx, k_ref[pl.ds(start, span), :]) + jnp.concatenate([bias] * groups, axis=0)
        m = jnp.broadcast_to(jnp.max(s, axis=-1, keepdims=True), (groups * tq, LANES))
        if use_sinks:
            t_rep = (q0 + lax.broadcasted_iota(jnp.int32, (tq, LANES), 0)).astype(F32)
            sink = jnp.concatenate(
                [sink_ref[h * groups + g] + sl_ref[h * groups + g] * t_rep for g in range(groups)], axis=0)
            m = jnp.maximum(m, sink)
        e = jnp.exp(s - jnp.concatenate([m] * (span // LANES), axis=1)).astype(BF16)
        pv = _dot(e, v_ref[pl.ds(start, span), :])
        extra = None
        if use_sinks:
            esink = jnp.exp(sink - m)
            extra = [esink[g * tq:(g + 1) * tq] for g in range(groups)]
        o_ref[rows, :] = _normalize_pack([pv[g * tq:(g + 1) * tq] for g in range(groups)], extra).astype(o_ref.dtype)


def _band(slopes, sinks, proj, *, B, T, window, q_slot, k_slot, v_slot, use_sinks, n_sub, name):
    tq = 128
    n_sub = min(n_sub, T // tq)
    tb = tq * n_sub
    nt = T // tb
    groups = 4
    assert window % tq == 0 and window + tq <= T and T % tb == 0
    qb = q_slot // groups
    kern = functools.partial(_band_kernel, tq=tq, n_sub=n_sub, window=window, seq=T, use_sinks=use_sinks)
    return pl.pallas_call(
        kern,
        grid=(B, 2, nt),
        in_specs=[
            pl.BlockSpec(memory_space=pltpu.SMEM),
            pl.BlockSpec(memory_space=pltpu.SMEM),
            pl.BlockSpec((tb, groups * LANES), lambda b, h, i: (b * nt + i, qb + h)),
            pl.BlockSpec((T, LANES), lambda b, h, i: (b, k_slot + h)),
            pl.BlockSpec((T, LANES), lambda b, h, i: (b, v_slot + h)),
        ],
        out_specs=pl.BlockSpec((tb, groups * HEAD_DIM), lambda b, h, i: (b * nt + i, h)),
        out_shape=jax.ShapeDtypeStruct((B * T, 2 * groups * HEAD_DIM), BF16),
        compiler_params=_cparams(("arbitrary", "arbitrary", "arbitrary")),
        name=name,
    )(slopes, sinks, proj, proj, proj)


def _outproj_kernel(ocmp_ref, oslc_ref, owin_ref, oswa_ref, gate_ref, x_ref, mod_ref, ex_ref, bn_ref, bs_ref,
                    w_ref, g2_ref, wr_ref, x1_ref, u2_ref, lt_ref):
    sg = _sigmoid(gate_ref[...].astype(F32)).astype(BF16)
    o_nsa = (_dot(sg, ex_ref[0]) * ocmp_ref[...].astype(F32)
             + _dot(sg, ex_ref[1]) * oslc_ref[...].astype(F32)
             + _dot(sg, ex_ref[2]) * owin_ref[...].astype(F32))
    o_swa = oswa_ref[...].astype(F32)
    n1 = o_nsa * lax.rsqrt(jnp.mean(o_nsa * o_nsa, axis=-1, keepdims=True) + EPS) * bn_ref[...]
    n2 = o_swa * lax.rsqrt(jnp.mean(o_swa * o_swa, axis=-1, keepdims=True) + EPS) * bs_ref[...]
    merged = jnp.concatenate([n1, n2], axis=-1).astype(BF16)
    y = _dot(merged, w_ref[0])
    x1 = x_ref[...] + mod_ref[0, 2:3, :] * y
    x1_ref[...] = x1
    u2 = x1 * lax.rsqrt(jnp.mean(x1 * x1, axis=-1, keepdims=True) + EPS) * g2_ref[...]
    u2 = u2 * (1.0 + mod_ref[0, 4:5, :]) + mod_ref[0, 3:4, :]
    u2_ref[...] = _pack_bf16_pairs(u2)
    u_hi = u2.astype(BF16)
    u_lo = (u2 - u_hi.astype(F32)).astype(BF16)
    wr = wr_ref[...]
    w_hi = wr.astype(BF16)
    w_lo = (wr - w_hi.astype(F32)).astype(BF16)
    lt_ref[...] = _dot_nt(w_hi, u_hi) + _dot_nt(w_hi, u_lo) + _dot_nt(w_lo, u_hi)


def _outproj(ocmp, oslc, owin, oswa, proj, x2, mod, expand, beta_n, beta_s, w_out, norm2_g, wr_t, *, seq, layer):
    N, D = x2.shape
    tm = 1024
    tiles_per_seq = seq // tm
    row = lambda i: (i, 0)
    const2 = lambda i: (0, 0)
    return pl.pallas_call(
        _outproj_kernel,
        grid=(N // tm,),
        in_specs=[
            pl.BlockSpec((tm, D_NSA), row),
            pl.BlockSpec((tm, D_NSA), row),
            pl.BlockSpec((tm, D_NSA), row),
            pl.BlockSpec((tm, D_SWA), row),
            pl.BlockSpec((tm, LANES), lambda i: (i, SLOT_GATE)),
            pl.BlockSpec((tm, D), row),
            pl.BlockSpec((1, 6, D), lambda i: (i // tiles_per_seq, 0, 0)),
            pl.BlockSpec((N_NSA_BRANCHES, LANES, D_NSA), lambda i: (0, 0, 0)),
            pl.BlockSpec((1, D_NSA), const2),
            pl.BlockSpec((1, D_SWA), const2),
            pl.BlockSpec((1, D_MIX, D), lambda i: (layer, 0, 0)),
            pl.BlockSpec((1, D), const2),
            pl.BlockSpec((N_EXPERTS, D), const2),
        ],
        out_specs=[
            pl.BlockSpec((tm, D), row),
            pl.BlockSpec((tm, D // 2), row),
            pl.BlockSpec((N_EXPERTS, tm), lambda i: (0, i)),
        ],
        out_shape=[
            jax.ShapeDtypeStruct((N, D), F32),
            jax.ShapeDtypeStruct((N, D // 2), jnp.uint32),
            jax.ShapeDtypeStruct((N_EXPERTS, N), F32),
        ],
        compiler_params=_cparams(("arbitrary",)),
        name="outproj",
    )(ocmp, oslc, owin, oswa, proj, x2, mod, expand, beta_n, beta_s, w_out, norm2_g, wr_t)


def _route_kernel(lt_ref, tri_ref, d0_ref, d1_ref, w0_ref, w1_ref, seg_ref, nch_ref, tail_ref, *, tm):
    lt = lt_ref[...]
    e = jnp.exp(lt - jnp.max(lt, axis=0, keepdims=True))
    aff = e / jnp.sum(e, axis=0, keepdims=True)
    rows = [aff[k:k + 1, :] for k in range(N_EXPERTS)]
    epg = EXPERTS_PER_GROUP
    scores = []
    for gr in range(N_GROUPS):
        xs = rows[gr * epg:(gr + 1) * epg]
        top1 = functools.reduce(jnp.maximum, xs)
        second = None
        for a in range(epg):
            for b in range(a + 1, epg):
                mn = jnp.minimum(xs[a], xs[b])
                second = mn if second is None else jnp.maximum(second, mn)
        scores.append(top1 + second)
    taken = None
    in_group = []
    for gr in range(N_GROUPS):
        best = None
        for o in range(gr + 1, N_GROUPS):
            c = scores[gr] >= scores[o]
            best = c if best is None else (best & c)
        if best is None:
            best = jnp.full(scores[gr].shape, True)
        sel = best if taken is None else (best & ~taken)
        taken = sel if taken is None else (taken | sel)
        in_group.append(sel)
    ys = []
    for k in range(epg):
        yk = rows[k]
        for gr in range(1, N_GROUPS):
            yk = jnp.where(in_group[gr], rows[gr * epg + k], yk)
        ys.append(yk)
    chosen = []
    for k in range(epg):
        rk = jnp.zeros_like(ys[k])
        for o in range(epg):
            if o == k:
                continue
            before = (ys[o] >= ys[k]) if o < k else (ys[o] > ys[k])
            rk = rk + jnp.where(before, 1.0, 0.0)
        chosen.append(rk < TOP_K)
    total = functools.reduce(lambda a, b: a + b, [jnp.where(chosen[k], ys[k], 0.0) for k in range(epg)])
    first, seen = [], None
    for k in range(epg):
        first.append(chosen[k] if seen is None else (chosen[k] & ~seen))
        seen = chosen[k] if seen is None else (seen | chosen[k])
    second = [chosen[k] & ~first[k] for k in range(epg)]
    inv_total = 1.0 / total
    w_first = functools.reduce(lambda a, b: a + b, [jnp.where(first[k], ys[k], 0.0) for k in range(epg)]) * inv_total
    w_second = functools.reduce(lambda a, b: a + b, [jnp.where(second[k], ys[k], 0.0) for k in range(epg)]) * inv_total
    f_rows, s_rows = [], []
    for ex in range(N_EXPERTS):
        gr, k = divmod(ex, epg)
        f_rows.append(jnp.where(in_group[gr] & first[k], 1.0, 0.0))
        s_rows.append(jnp.where(in_group[gr] & second[k], 1.0, 0.0))
    fmat = jnp.concatenate(f_rows, axis=0)
    smat = jnp.concatenate(s_rows, axis=0)
    cmat = fmat + smat
    carry = jnp.zeros((N_EXPERTS, 1), F32)
    ranks = []
    for blk in range(tm // TRI):
        cb = cmat[:, blk * TRI:(blk + 1) * TRI]
        ranks.append(_dot(cb.astype(BF16), tri_ref[...]) + carry)
        carry = carry + jnp.sum(cb, axis=1, keepdims=True)
    rank = jnp.concatenate(ranks, axis=1)
    padded = jnp.floor((carry + (SEG_ALIGN - 1)) * (1.0 / SEG_ALIGN)) * SEG_ALIGN
    seg_rows, run = [], jnp.zeros((1, 1), F32)
    for ex in range(N_EXPERTS):
        seg_rows.append(run)
        run = run + padded[ex:ex + 1]
    seg = jnp.concatenate(seg_rows, axis=0)
    dest = rank + seg
    d0_ref[0] = jnp.sum(fmat * dest, axis=0, keepdims=True).astype(jnp.int32)
    d1_ref[0] = jnp.sum(smat * dest, axis=0, keepdims=True).astype(jnp.int32)
    w0_ref[...] = jnp.broadcast_to(w_first, (LANES, tm)).T
    w1_ref[...] = jnp.broadcast_to(w_second, (LANES, tm)).T
    seg_ref[0] = jnp.broadcast_to(seg, (N_EXPERTS, LANES)).astype(jnp.int32)
    whole = jnp.floor(carry * (1.0 / MOE_CHUNK))
    rem = carry - whole * MOE_CHUNK
    n_full = whole + jnp.where(rem > MOE_CHUNK // 2, 1.0, 0.0)
    half_tail = jnp.where((rem > 0) & (rem <= MOE_CHUNK // 2), 1.0, 0.0)
    nch_ref[0] = jnp.broadcast_to(n_full, (N_EXPERTS, LANES)).astype(jnp.int32)
    tail_ref[0] = jnp.broadcast_to(half_tail, (N_EXPERTS, LANES)).astype(jnp.int32)


def _route(lt, tri):
    E, N = lt.shape
    tm = MOE_TILE
    nt = N // tm
    return pl.pallas_call(
        functools.partial(_route_kernel, tm=tm),
        grid=(nt,),
        in_specs=[pl.BlockSpec((E, tm), lambda i: (0, i)), pl.BlockSpec((TRI, TRI), lambda i: (0, 0))],
        out_specs=[
            pl.BlockSpec((1, 1, tm), lambda i: (i, 0, 0)),
            pl.BlockSpec((1, 1, tm), lambda i: (i, 0, 0)),
            pl.BlockSpec((tm, LANES), lambda i: (i, 0)),
            pl.BlockSpec((tm, LANES), lambda i: (i, 0)),
            pl.BlockSpec((1, E, LANES), lambda i: (i, 0, 0)),
            pl.BlockSpec((1, E, LANES), lambda i: (i, 0, 0)),
            pl.BlockSpec((1, E, LANES), lambda i: (i, 0, 0)),
        ],
        out_shape=[
            jax.ShapeDtypeStruct((nt, 1, tm), jnp.int32),
            jax.ShapeDtypeStruct((nt, 1, tm), jnp.int32),
            jax.ShapeDtypeStruct((N, LANES), F32),
            jax.ShapeDtypeStruct((N, LANES), F32),
            jax.ShapeDtypeStruct((nt, E, LANES), jnp.int32),
            jax.ShapeDtypeStruct((nt, E, LANES), jnp.int32),
            jax.ShapeDtypeStruct((nt, E, LANES), jnp.int32),
        ],
        compiler_params=_cparams(("arbitrary",)),
        name="route",
    )(lt, tri)


def _moe_kernel(seg_ref, nch_ref, tail_ref, d0_ref, d1_ref, u_ref, w1_ref, w3_ref, w2_ref, w0c_ref, w1c_ref, x1_ref,
                mod_ref, o_ref, xs_ref, ys_ref, g0_ref, g1_ref, *, tm, chunk, n_exp_steps, tsub):
    i = pl.program_id(0)
    step = pl.program_id(1)

    @pl.when(step == 0)
    def _():
        xs_ref[...] = jnp.zeros_like(xs_ref)
        ys_ref[...] = jnp.zeros_like(ys_ref)

        def move(grp, carry):
            t0 = pl.multiple_of(grp * 8, 8)
            blk = u_ref[pl.ds(t0, 8), :]
            for k in range(8):
                row = blk[k:k + 1, :]
                xs_ref[pl.ds(d0_ref[0, 0, t0 + k], 1), :] = row
                xs_ref[pl.ds(d1_ref[0, 0, t0 + k], 1), :] = row
            return carry

        lax.fori_loop(0, tm // 8, move, 0)

    @pl.when(step < n_exp_steps)
    def _():
        for sub in range(EXPERTS_PER_STEP):
            slot = i * N_EXPERTS + step * EXPERTS_PER_STEP + sub
            start = seg_ref[slot]
            n_full = nch_ref[slot]

            def do_rows(r0, rows, sub=sub):
                lo, hi = _unpack_bf16_pairs(xs_ref[pl.ds(r0, rows), :])
                xb = jnp.concatenate([lo, hi], axis=-1).astype(BF16)
                a = _dot(xb, w1_ref[0, sub])
                hmid = a * _sigmoid(a) * _dot(xb, w3_ref[0, sub])
                ys_ref[pl.ds(r0, rows), :] = _pack_bf16_pairs(_dot(hmid.astype(BF16), w2_ref[0, sub]))

            def do_chunk(j, carry, start=start, do_rows=do_rows):
                do_rows(pl.multiple_of(start + j * chunk, SEG_ALIGN), chunk)
                return carry

            lax.fori_loop(0, n_full, do_chunk, 0)

            @pl.when(tail_ref[slot] > 0)
            def _(start=start, n_full=n_full, do_rows=do_rows):
                do_rows(pl.multiple_of(start + n_full * chunk, SEG_ALIGN), chunk // 2)

    @pl.when(step >= n_exp_steps)
    def _():
        base = (step - n_exp_steps) * tsub

        def move(grp, carry):
            t0 = pl.multiple_of(grp * 8, 8)
            for d_ref, g_ref in ((d0_ref, g0_ref), (d1_ref, g1_ref)):
                rows = [ys_ref[pl.ds(d_ref[0, 0, base + t0 + k], 1), :] for k in range(8)]
                g_ref[pl.ds(t0, 8), :] = jnp.concatenate(rows, axis=0)
            return carry

        lax.fori_loop(0, tsub // 8, move, 0)
        lo0, hi0 = _unpack_bf16_pairs(g0_ref[...])
        lo1, hi1 = _unpack_bf16_pairs(g1_ref[...])
        half = g0_ref.shape[1]
        w0 = jnp.concatenate([w0c_ref[...]] * (half // LANES), axis=1)
        w1 = jnp.concatenate([w1c_ref[...]] * (half // LANES), axis=1)
        o_ref[:, :half] = x1_ref[:, :half] + mod_ref[0, 5:6, :half] * (w0 * lo0 + w1 * lo1)
        o_ref[:, half:] = x1_ref[:, half:] + mod_ref[0, 5:6, half:] * (w0 * hi0 + w1 * hi1)


def _moe(seg, nch, tail, d0, d1, u2p, w1, w3, w2, w0rep, w1rep, x1, mod, *, layer, seq):
    N, half = u2p.shape
    D = 2 * half
    E = w1.shape[1]
    tm = MOE_TILE
    tsub = 512
    nt = N // tm
    n_exp = E // EXPERTS_PER_STEP
    n_cmb = tm // tsub
    kern = functools.partial(_moe_kernel, tm=tm, chunk=MOE_CHUNK, n_exp_steps=n_exp, tsub=tsub)
    wexp = lambda i, s, *_: (layer, jnp.minimum(s, n_exp - 1), 0, 0)
    csub = lambda i, s, *_: (i * n_cmb + jnp.clip(s - n_exp, 0, n_cmb - 1), 0)
    grid_spec = pltpu.PrefetchScalarGridSpec(
        num_scalar_prefetch=3,
        grid=(nt, n_exp + n_cmb),
        in_specs=[
            pl.BlockSpec((1, 1, tm), lambda i, s, *_: (i, 0, 0), memory_space=pltpu.SMEM),
            pl.BlockSpec((1, 1, tm), lambda i, s, *_: (i, 0, 0), memory_space=pltpu.SMEM),
            pl.BlockSpec((tm, half), lambda i, s, *_: (i, 0)),
            pl.BlockSpec((1, EXPERTS_PER_STEP, D, D_EXPERT), wexp),
            pl.BlockSpec((1, EXPERTS_PER_STEP, D, D_EXPERT), wexp),
            pl.BlockSpec((1, EXPERTS_PER_STEP, D_EXPERT, D), wexp),
            pl.BlockSpec((tsub, LANES), csub),
            pl.BlockSpec((tsub, LANES), csub),
            pl.BlockSpec((tsub, D), csub),
            pl.BlockSpec((1, 6, D),
                         lambda i, s, *_: ((i * tm + jnp.clip(s - n_exp, 0, n_cmb - 1) * tsub) // seq, 0, 0)),
        ],
        out_specs=pl.BlockSpec((tsub, D), csub),
        scratch_shapes=[
            pltpu.VMEM((MOE_ROWS, half), jnp.uint32),
            pltpu.VMEM((MOE_ROWS, half), jnp.uint32),
            pltpu.VMEM((tsub, half), jnp.uint32),
            pltpu.VMEM((tsub, half), jnp.uint32),
        ],
    )
    return pl.pallas_call(
        kern,
        grid_spec=grid_spec,
        out_shape=jax.ShapeDtypeStruct((N, D), F32),
        compiler_params=pltpu.CompilerParams(dimension_semantics=("arbitrary", "arbitrary"),
                                             vmem_limit_bytes=MOE_VMEM_LIMIT),
        name="moe",
    )(seg, nch, tail, d0, d1, u2p, w1, w3, w2, w0rep, w1rep, x1, mod)


def _alibi_slopes(first, count):
    hh = jnp.arange(first + 1, first + count + 1, dtype=F32)
    return jnp.exp2(-8.0 * hh / N_HEADS_TOTAL)


def _bd_const():
    idx = np.arange(2 * LANES) // HEAD_DIM
    return jnp.asarray((idx[:, None] == idx[None, :]).astype(np.float32), BF16)


def _blkcols_const(T):
    blk = np.arange(T) // SEL_LEN
    m = np.where(np.arange(LANES)[None, :] == blk[:, None], NEG_INF, 0.0).astype(np.float32)
    return jnp.asarray(m, BF16)


def _overlap_const(T):
    nc = T // CMP_STRIDE
    n_sel = T // SEL_LEN
    cs = np.arange(nc) * CMP_STRIDE
    ss = np.arange(n_sel) * SEL_LEN
    ov = np.clip(np.minimum(cs[:, None] + CMP_LEN, ss[None, :] + SEL_LEN)
                 - np.maximum(cs[:, None], ss[None, :]), 0, None) / CMP_LEN
    return jnp.asarray(ov.T.astype(np.float32))


def _expand_const():
    ex = np.zeros((N_NSA_BRANCHES, LANES, D_NSA), np.float32)
    for br in range(N_NSA_BRANCHES):
        for hd in range(N_HEADS_NSA):
            ex[br, hd * N_NSA_BRANCHES + br, hd * HEAD_DIM:(hd + 1) * HEAD_DIM] = 1.0
    return jnp.asarray(ex, BF16)


def _compact_w_in(w):
    parts = [w[..., a:b] for a, b in (_C_QN, _C_QS, _C_KSL, _C_KW, _C_KS, _C_VSL, _C_VW, _C_VS, _C_KC, _C_VC, _C_GATE)]
    pad = jnp.zeros(w.shape[:-1] + (LANES - (_C_GATE[1] - _C_GATE[0]),), w.dtype)
    return jnp.concatenate(parts + [pad], axis=-1).astype(BF16)


def _qconst(slopes_q):
    hi = slopes_q.astype(BF16).astype(F32)
    lo = (slopes_q - hi).astype(BF16).astype(F32)
    cols = jnp.stack([64.0 * hi, 64.0 * lo, hi, lo], axis=1)
    return jnp.zeros((slopes_q.shape[0], LANES), F32).at[:, HEAD_DIM:HEAD_DIM + 4].set(cols)


def kernel(x, c, w_router, ada_w, ada_b, norm1_g, norm2_g, w_in, w_out, nsa_q_gain, nsa_k_gain, cmp_pos_k,
           cmp_pos_v, cmp_w1_k, cmp_w2_k, cmp_w1_v, cmp_w2_v, swa_q_gain, swa_k_gain, swa_sinks, beta_nsa,
           beta_swa, moe_w1, moe_w3, moe_w2):
    B, T, D = x.shape
    L = ada_w.shape[0]
    N = B * T
    nc = T // CMP_STRIDE
    assert D == D_MODEL and T % 1024 == 0 and T <= 64 * 128 and B <= 8

    slopes_nsa = _alibi_slopes(N_HEADS_SWA, N_HEADS_NSA)
    slopes_swa = _alibi_slopes(0, N_HEADS_SWA)
    qconst = _qconst(jnp.concatenate([slopes_nsa, slopes_swa]))
    bd = _bd_const()
    blkcols = _blkcols_const(T)
    ov = _overlap_const(T)
    expand = _expand_const()
    wr_t = w_router.T
    tri = jnp.asarray(np.triu(np.ones((TRI, TRI), np.float32), 1), BF16)
    assert N % MOE_TILE == 0
    moe_w1b, moe_w3b, moe_w2b = moe_w1.astype(BF16), moe_w3.astype(BF16), moe_w2.astype(BF16)
    w_outb = w_out.astype(BF16)
    w_inb = _compact_w_in(w_in)
    cmp_w1s = jnp.stack([cmp_w1_k, cmp_w1_v], axis=1).reshape(2 * L, CMP_LEN * HEAD_DIM, CMP_HIDDEN).astype(BF16)
    cmp_w2s = jnp.pad(jnp.stack([cmp_w2_k, cmp_w2_v], axis=1).reshape(2 * L, CMP_HIDDEN, HEAD_DIM),
                      ((0, 0), (0, 0), (0, LANES - HEAD_DIM))).astype(BF16)
    cmp_pos = jnp.stack([cmp_pos_k, cmp_pos_v], axis=1).reshape(2 * L, 1, CMP_LEN * HEAD_DIM)
    cmp_pos = jnp.broadcast_to(cmp_pos, (2 * L, 8, CMP_LEN * HEAD_DIM)).astype(BF16)
    zero_sinks = jnp.zeros((N_HEADS_NSA,), F32)

    c_pad = jnp.zeros((8, D), F32).at[:B].set(c)
    mod_all = _ada(c_pad, ada_w, ada_b)[:, :B].reshape(L, B, 6, D)

    x2 = x.reshape(N, D)
    for l in range(L):
        mod = mod_all[l]
        tile2 = lambda g: jnp.tile(g, 2)
        gain_c = jnp.concatenate([
            jnp.tile(nsa_q_gain[l], N_HEADS_NSA) * ATTN_SCALE,
            jnp.tile(swa_q_gain[l], N_HEADS_SWA) * ATTN_SCALE,
            tile2(nsa_k_gain[l, 1]), tile2(nsa_k_gain[l, 2]), tile2(swa_k_gain[l])]).reshape(1, -1)
        proj, xc = _inproj(x2, mod, norm1_g[l].reshape(1, D), w_inb, gain_c, qconst, bd, seq=T, layer=l)
        kgain = jnp.pad(nsa_k_gain[l, 0], (0, LANES - HEAD_DIM)).reshape(1, LANES)
        kvc = _cmpmlp(xc, cmp_w1s, cmp_w2s, cmp_pos, kgain, B=B, layer=l)

        o_cmp, notsel = _cmpattn(proj, kvc, ov, B=B, T=T)
        o_slc = _selattn(proj, notsel, blkcols, B=B, T=T)
        o_win = _band(slopes_nsa, zero_sinks, proj, B=B, T=T, window=NSA_WINDOW, q_slot=SLOT_QN,
                      k_slot=SLOT_KW, v_slot=SLOT_VW, use_sinks=False, n_sub=16, name="winattn")
        o_swa = _band(slopes_swa, swa_sinks[l], proj, B=B, T=T, window=SWA_WINDOW, q_slot=SLOT_QS,
                      k_slot=SLOT_KS, v_slot=SLOT_VS, use_sinks=True, n_sub=16, name="swaattn")

        x1, u2, lt = _outproj(o_cmp, o_slc, o_win, o_swa, proj, x2, mod, expand,
                              beta_nsa[l].reshape(1, -1), beta_swa[l].reshape(1, -1), w_outb,
                              norm2_g[l].reshape(1, D), wr_t, seq=T, layer=l)
        d0, d1, w0rep, w1rep, seg, nch, tail = _route(lt, tri)
        x2 = _moe(seg[:, :, 0].reshape(-1), nch[:, :, 0].reshape(-1), tail[:, :, 0].reshape(-1), d0, d1, u2,
                  moe_w1b, moe_w3b, moe_w2b, w0rep, w1rep, x1, mod, layer=l, seq=T)
    return x2.reshape(B, T, D)
```

```python
import functools

import numpy as np
import jax
import jax.numpy as jnp
from jax import lax
from jax.experimental import pallas as pl
from jax.experimental.pallas import tpu as pltpu

D_MODEL = 1024
HEAD_DIM = 64
N_HEADS_NSA = 8
N_KV_NSA = 2
G_NSA = N_HEADS_NSA // N_KV_NSA
N_HEADS_SWA = 8
D_NSA = N_HEADS_NSA * HEAD_DIM
D_SWA = N_HEADS_SWA * HEAD_DIM
D_MIX = D_NSA + D_SWA
N_HEADS_TOTAL = N_HEADS_NSA + N_HEADS_SWA
N_NSA_BRANCHES = 3
CMP_LEN = 32
CMP_STRIDE = 16
CMP_HIDDEN = 256
SEL_LEN = 64
SEL_TOPK = 16
NSA_WINDOW = 512
SWA_WINDOW = 128
N_EXPERTS = 16
N_GROUPS = 4
EXPERTS_PER_GROUP = N_EXPERTS // N_GROUPS
TOP_K = 2
D_EXPERT = 512
EPS = 1e-6
NEG_INF = -1e30
ATTN_SCALE = HEAD_DIM ** -0.5

F32 = jnp.float32
BF16 = jnp.bfloat16

LANES = 128
VMEM_LIMIT = 48 * 1024 * 1024

MOE_TILE = 2048
MOE_CHUNK = 320
SEG_ALIGN = 16
TRI = 256
MOE_ROWS = TOP_K * MOE_TILE + N_EXPERTS * SEG_ALIGN + MOE_CHUNK
EXPERTS_PER_STEP = 2
MOE_VMEM_LIMIT = 56 * 1024 * 1024

SLOT_QN = 0
SLOT_QS = 8
SLOT_KSL = 16
SLOT_KW = 18
SLOT_KS = 20
SLOT_VSL = 22
SLOT_VW = 24
SLOT_VS = 26
SLOT_GATE = 28
N_SLOTS = 29
N_CHUNKS = 17
N_NORM_CHUNKS = 11

_C_QN = (0, 512)
_C_KC = (512, 640)
_C_VC = (640, 768)
_C_KSL = (768, 896)
_C_VSL = (896, 1024)
_C_KW = (1024, 1152)
_C_VW = (1152, 1280)
_C_GATE = (1280, 1304)
_C_QS = (1304, 1816)
_C_KS = (1816, 1944)
_C_VS = (1944, 2072)


def _cparams(sem):
    return pltpu.CompilerParams(dimension_semantics=sem, vmem_limit_bytes=VMEM_LIMIT)


def _dot(a, b, **kw):
    return jnp.dot(a, b, preferred_element_type=F32, **kw)


def _dot_nt(a, b):
    return lax.dot_general(a, b, (((1,), (1,)), ((), ())), preferred_element_type=F32)


def _sigmoid(x):
    return 1.0 / (1.0 + jnp.exp(-x))


def _pack_bf16_pairs(x):
    n = x.shape[1] // 2
    return pltpu.pack_elementwise([x[:, :n], x[:, n:]], packed_dtype=BF16)


def _unpack_bf16_pairs(w):
    lo = pltpu.unpack_elementwise(w, index=0, packed_dtype=BF16, unpacked_dtype=F32)
    hi = pltpu.unpack_elementwise(w, index=1, packed_dtype=BF16, unpacked_dtype=F32)
    return lo, hi


def _ada_kernel(c_ref, w_ref, b_ref, o_ref):
    c = c_ref[...]
    cond = c * _sigmoid(c)
    o_ref[0] = _dot(cond.astype(BF16), w_ref[0].astype(BF16)) + b_ref[0]


def _ada(c_pad, ada_w, ada_b):
    L, D, N6 = ada_w.shape
    tn = 1536
    return pl.pallas_call(
        _ada_kernel,
        grid=(L, N6 // tn),
        in_specs=[
            pl.BlockSpec((8, D), lambda l, j: (0, 0)),
            pl.BlockSpec((1, D, tn), lambda l, j: (l, 0, j)),
            pl.BlockSpec((1, 1, tn), lambda l, j: (l, 0, j)),
        ],
        out_specs=pl.BlockSpec((1, 8, tn), lambda l, j: (l, 0, j)),
        out_shape=jax.ShapeDtypeStruct((L, 8, N6), F32),
        compiler_params=_cparams(("arbitrary", "arbitrary")),
        name="ada",
    )(c_pad, ada_w, ada_b.reshape(L, 1, N6))


def _inproj_kernel(x_ref, mod_ref, g_ref, w_ref, gain_ref, qc_ref, bd_ref, o_ref, xc_ref, scr_ref, *, tm, seq):
    i = pl.program_id(0)
    x = x_ref[...]
    ms = jnp.mean(x * x, axis=-1, keepdims=True)
    u = x * lax.rsqrt(ms + EPS) * g_ref[...]
    u = u * (1.0 + mod_ref[0, 1:2, :]) + mod_ref[0, 0:1, :]
    y = _dot(u.astype(BF16), w_ref[0])

    lane = lax.broadcasted_iota(jnp.int32, (tm, LANES), 1)
    low = lane < HEAD_DIM
    t = lax.broadcasted_iota(jnp.int32, (tm, LANES), 0) + lax.rem(i * tm, seq)
    hi_part = (t >> 6).astype(F32)
    lo_part = (t & 63).astype(F32)
    poscols = jnp.where((lane == 64) | (lane == 65), hi_part,
                        jnp.where((lane == 66) | (lane == 67), lo_part, 0.0))
    bd = bd_ref[...]

    def put(slot, val):
        o_ref[:, slot * LANES:(slot + 1) * LANES] = val.astype(o_ref.dtype)

    normed = {}
    for c0 in range(0, N_NORM_CHUNKS, 2):
        width = min(2, N_NORM_CHUNKS - c0) * LANES
        wide = y[:, c0 * LANES:c0 * LANES + width]
        msb = _dot((wide * wide).astype(BF16), bd[:width, :width]) * (1.0 / HEAD_DIM)
        wide = wide * lax.rsqrt(msb + EPS) * gain_ref[:, c0 * LANES:c0 * LANES + width]
        for k in range(width // LANES):
            normed[c0 + k] = wide[:, k * LANES:(k + 1) * LANES]

    for c in range(N_CHUNKS):
        blk = normed[c] if c < N_NORM_CHUNKS else y[:, c * LANES:(c + 1) * LANES]
        if c < 14:
            rolled = pltpu.roll(blk, HEAD_DIM, axis=1)
            if c < 8:
                f0 = qc_ref[2 * c:2 * c + 1, :]
                f1 = qc_ref[2 * c + 1:2 * c + 2, :]
                s0 = 2 * c
            elif c < N_NORM_CHUNKS:
                f0 = f1 = poscols
                s0 = SLOT_KSL + 2 * (c - 8)
            else:
                f0 = f1 = 1.0
                s0 = SLOT_VSL + 2 * (c - N_NORM_CHUNKS)
            put(s0, jnp.where(low, blk, f0))
            put(s0 + 1, jnp.where(low, rolled, f1))
        elif c == 16:
            put(SLOT_GATE, blk)
        else:
            scr_ref[...] = blk
            nck = tm // CMP_STRIDE
            low_c = lax.broadcasted_iota(jnp.int32, (nck, LANES), 1) < HEAD_DIM
            head_cols = CMP_STRIDE * HEAD_DIM
            for pair in range(CMP_STRIDE // 2):
                ta = scr_ref[pl.ds(2 * pair, nck, stride=CMP_STRIDE), :]
                tb = scr_ref[pl.ds(2 * pair + 1, nck, stride=CMP_STRIDE), :]
                h0 = jnp.where(low_c, ta, pltpu.roll(tb, HEAD_DIM, axis=1))
                h1 = jnp.where(low_c, pltpu.roll(ta, HEAD_DIM, axis=1), tb)
                xc_ref[c - 14, :, pair * LANES:(pair + 1) * LANES] = h0.astype(xc_ref.dtype)
                xc_ref[c - 14, :, head_cols + pair * LANES:head_cols + (pair + 1) * LANES] = h1.astype(xc_ref.dtype)


def _inproj(x2, mod, norm_g, w_c, gain_c, qconst, bd, *, seq, layer):
    N, D = x2.shape
    tm = 1024
    tiles_per_seq = seq // tm
    kern = functools.partial(_inproj_kernel, tm=tm, seq=seq)
    nchunk = tm // CMP_STRIDE
    xc_shape = jax.ShapeDtypeStruct((2, N // CMP_STRIDE, CMP_STRIDE * LANES), BF16)
    return pl.pallas_call(
        kern,
        grid=(N // tm,),
        in_specs=[
            pl.BlockSpec((tm, D), lambda i: (i, 0)),
            pl.BlockSpec((1, 6, D), lambda i: (i // tiles_per_seq, 0, 0)),
            pl.BlockSpec((1, D), lambda i: (0, 0)),
            pl.BlockSpec((1, D, N_CHUNKS * LANES), lambda i: (layer, 0, 0)),
            pl.BlockSpec((1, N_NORM_CHUNKS * LANES), lambda i: (0, 0)),
            pl.BlockSpec((16, LANES), lambda i: (0, 0)),
            pl.BlockSpec((2 * LANES, 2 * LANES), lambda i: (0, 0)),
        ],
        out_specs=[
            pl.BlockSpec((tm, N_SLOTS * LANES), lambda i: (i, 0)),
            pl.BlockSpec((2, nchunk, CMP_STRIDE * LANES), lambda i: (0, i, 0)),
        ],
        out_shape=[jax.ShapeDtypeStruct((N, N_SLOTS * LANES), BF16), xc_shape],
        scratch_shapes=[pltpu.VMEM((tm, LANES), F32)],
        compiler_params=_cparams(("arbitrary",)),
        name="inproj",
    )(x2, mod, norm_g, w_c, gain_c, qconst, bd)


def _gelu_tanh(x):
    return 0.5 * x * (1.0 + jnp.tanh(np.sqrt(2.0 / np.pi).astype(np.float32) * (x + 0.044715 * (x * x * x))))


def _cmpmlp_kernel(x_ref, w1_ref, w2_ref, pos_ref, gain_ref, o_ref, *, nc):
    kv = pl.program_id(0)
    half = CMP_STRIDE * HEAD_DIM
    bias = _dot(pos_ref[0], w1_ref[0])[0:1]
    lane = lax.broadcasted_iota(jnp.int32, (nc, LANES), 1)
    last = lax.broadcasted_iota(jnp.int32, (nc, LANES), 0) * CMP_STRIDE + (CMP_LEN - 1)
    poscols = jnp.where((lane == 64) | (lane == 65), (last >> 6).astype(F32),
                        jnp.where((lane == 66) | (lane == 67), (last & 63).astype(F32), 0.0))
    for hd in range(N_KV_NSA):
        xc = x_ref[0, :, hd * half:(hd + 1) * half]
        first = _dot(xc, w1_ref[0, :half, :])
        second = _dot(xc, w1_ref[0, half:, :])
        h = _gelu_tanh(first + pltpu.roll(second, nc - 1, axis=0) + bias)
        z = _dot(h.astype(BF16), w2_ref[0])
        zn = z * lax.rsqrt(jnp.sum(z * z, axis=-1, keepdims=True) * (1.0 / HEAD_DIM) + EPS) * gain_ref[...]
        o_ref[0, 0, hd] = jnp.where(kv == 0, zn + poscols, z + jnp.where(lane >= HEAD_DIM, 1.0, 0.0))


def _cmpmlp(xc, w1s, w2s, pos, gain, *, B, layer):
    _, nchunks, K = xc.shape
    nc = nchunks // B
    win = CMP_LEN * HEAD_DIM
    kern = functools.partial(_cmpmlp_kernel, nc=nc)
    return pl.pallas_call(
        kern,
        grid=(2, B),
        in_specs=[
            pl.BlockSpec((1, nc, K), lambda kv, b: (kv, b, 0)),
            pl.BlockSpec((1, win, CMP_HIDDEN), lambda kv, b: (2 * layer + kv, 0, 0)),
            pl.BlockSpec((1, CMP_HIDDEN, LANES), lambda kv, b: (2 * layer + kv, 0, 0)),
            pl.BlockSpec((1, 8, win), lambda kv, b: (2 * layer + kv, 0, 0)),
            pl.BlockSpec((1, LANES), lambda kv, b: (0, 0)),
        ],
        out_specs=pl.BlockSpec((1, 1, N_KV_NSA, nc, LANES), lambda kv, b: (kv, b, 0, 0, 0)),
        out_shape=jax.ShapeDtypeStruct((2, B, N_KV_NSA, nc, LANES), F32),
        compiler_params=_cparams(("arbitrary", "arbitrary")),
        name="cmpmlp",
    )(xc, w1s, w2s, pos, gain)


def _normalize_pack(pvs, extra=None):
    lane = lax.broadcasted_iota(jnp.int32, pvs[0].shape, 1)
    low = lane < HEAD_DIM
    pairs = []
    for p in range(len(pvs) // 2):
        even, odd = pvs[2 * p], pvs[2 * p + 1]
        den_e = even if extra is None else even + extra[2 * p]
        den_o = odd if extra is None else odd + extra[2 * p + 1]
        o_e = even * (1.0 / pltpu.roll(den_e, HEAD_DIM, axis=1))
        o_o = pltpu.roll(odd, HEAD_DIM, axis=1) * (1.0 / den_o)
        pairs.append(jnp.where(low, o_e, o_o))
    return jnp.concatenate(pairs, axis=-1)


def _cmpattn_kernel(q_ref, kc_ref, vc_ref, ov_ref, o_alias, ns_alias, o_ref, ns_ref, *, tq, n_sub, nc, n_cmp, n_sel,
                    top, tile0):
    del o_alias, ns_alias
    kc = kc_ref[0, 0, 0].astype(BF16)
    vc = vc_ref[0, 0, 0].astype(BF16)
    for sub in range(n_sub):
        q0 = ((tile0 + pl.program_id(2)) * n_sub + sub) * tq
        rows = pl.ds(sub * tq, tq)
        _cmpattn_subtile(q_ref.at[rows, :], kc, vc, ov_ref, o_ref.at[rows, :], ns_ref.at[0, 0, rows, :], q0,
                         tq=tq, nc=nc, n_cmp=n_cmp, n_sel=n_sel, top=top)


def _cmpattn_subtile(q_ref, kc, vc, ov_ref, o_ref, ns_ref, q0, *, tq, nc, n_cmp, n_sel, top):
    t_col = q0 + lax.broadcasted_iota(jnp.int32, (tq, 1), 0)
    row_valid = (t_col >= CMP_LEN - 1).astype(F32)
    t_row = q0 + lax.broadcasted_iota(jnp.int32, (1, tq), 1)
    n_col = lax.broadcasted_iota(jnp.int32, (nc, 1), 0)
    valid_t = (t_row - (n_col * CMP_STRIDE + (CMP_LEN - 1)) >= 0) & (n_col < n_cmp)
    bias_t = jnp.where(valid_t, 0.0, NEG_INF)
    row_valid_t = (t_row >= CMP_LEN - 1).astype(F32)

    outs = []
    psum_t = jnp.zeros((nc, tq), F32)
    for g in range(G_NSA):
        qg = q_ref[:, g * LANES:(g + 1) * LANES]
        st = _dot_nt(kc, qg) + bias_t
        et = jnp.exp(st - jnp.max(st, axis=0, keepdims=True))
        outs.append(lax.dot_general(et.astype(BF16), vc, (((0,), (0,)), ((), ())), preferred_element_type=F32))
        psum_t = psum_t + et * (1.0 / jnp.sum(et, axis=0, keepdims=True))
    o_ref[...] = (_normalize_pack(outs) * row_valid).astype(o_ref.dtype)

    psum_t = psum_t * row_valid_t
    imp_t = _dot(ov_ref[...], psum_t, precision=lax.Precision.HIGHEST)
    j = lax.broadcasted_iota(jnp.int32, (n_sel, 1), 0)
    cur = t_row >> 6
    forced = (j == 0) | (j == cur) | (j == cur - 1)
    v = jnp.where(forced, 1e9, jnp.where(j > cur, NEG_INF, imp_t))
    n_grp = n_sel // 8
    vg = [v[8 * r:8 * r + 8, :] for r in range(n_grp)]
    jg = lax.broadcasted_iota(jnp.int32, (8, 1), 0)
    cnt = [jnp.zeros((8, tq), F32) for _ in range(n_grp)]
    for jp in range(n_sel):
        vj = v[jp:jp + 1, :]
        for r in range(n_grp):
            ge = jnp.where(vj >= vg[r], 1.0, 0.0)
            gt = jnp.where(vj > vg[r], 1.0, 0.0)
            if 8 * r > jp:
                inc = ge
            elif 8 * r + 7 < jp:
                inc = gt
            else:
                inc = jnp.where(jg + 8 * r > jp, ge, gt)
            cnt[r] = cnt[r] + inc
    notsel_t = jnp.concatenate([jnp.where(cn < top, 0.0, 1.0) for cn in cnt], axis=0)
    if n_sel < LANES:
        notsel_t = jnp.concatenate([notsel_t, jnp.zeros((LANES - n_sel, tq), F32)], axis=0)
    ns_ref[...] = notsel_t.T.astype(ns_ref.dtype)


def _cmpattn(proj, kvc, ov, *, B, T):
    tq = 256
    nt = T // tq
    nc = T // CMP_STRIDE
    n_cmp = (T - CMP_LEN) // CMP_STRIDE + 1
    n_sel = T // SEL_LEN
    top = min(SEL_TOPK, n_sel)
    assert n_sel <= LANES and n_sel % 8 == 0
    n_parts = 4 if (nt % 4 == 0 and n_sel % 32 == 0) else 1
    n_sub = 4 if (nt // n_parts) % 4 == 0 else 1
    tb = n_sub * tq
    steps = nt // n_sub
    sp = steps // n_parts
    o_cmp = jnp.zeros((B * T, D_NSA), BF16)
    notsel = jnp.zeros((B, N_KV_NSA, T, LANES), BF16)
    for part in range(n_parts):
        nc_p = nc * (part + 1) // n_parts
        ns_p = n_sel * (part + 1) // n_parts
        kern = functools.partial(_cmpattn_kernel, tq=tq, n_sub=n_sub, nc=nc_p, n_cmp=n_cmp, n_sel=ns_p, top=top,
                                 tile0=part * sp)
        qrow = lambda b, h, i, part=part: (b * steps + part * sp + i, h)
        o_cmp, notsel = pl.pallas_call(
            kern,
            grid=(B, N_KV_NSA, sp),
            in_specs=[
                pl.BlockSpec((tb, G_NSA * LANES), qrow),
                pl.BlockSpec((1, 1, 1, nc_p, LANES), lambda b, h, i: (0, b, h, 0, 0)),
                pl.BlockSpec((1, 1, 1, nc_p, LANES), lambda b, h, i: (1, b, h, 0, 0)),
                pl.BlockSpec((ns_p, nc_p), lambda b, h, i: (0, 0)),
                pl.BlockSpec(memory_space=pl.ANY),
                pl.BlockSpec(memory_space=pl.ANY),
            ],
            out_specs=[
                pl.BlockSpec((tb, G_NSA * HEAD_DIM), qrow),
                pl.BlockSpec((1, 1, tb, LANES), lambda b, h, i, part=part: (b, h, part * sp + i, 0)),
            ],
            out_shape=[
                jax.ShapeDtypeStruct((B * T, D_NSA), BF16),
                jax.ShapeDtypeStruct((B, N_KV_NSA, T, LANES), BF16),
            ],
            input_output_aliases={4: 0, 5: 1},
            compiler_params=_cparams(("arbitrary", "arbitrary", "arbitrary")),
            name="cmpattn",
        )(proj, kvc, kvc, ov[:ns_p, :nc_p], o_cmp, notsel)
    return o_cmp, notsel


def _selattn_kernel(q_ref, ns_ref, k_ref, v_ref, bc_ref, o_ref, qx_ref, s0_ref, s1_ref, mx0_ref, mx1_ref, m_ref,
                    acc_ref, *, tq, tk):
    diag = pl.program_id(2)
    ns = ns_ref[0, 0]
    for g in range(G_NSA):
        qx_ref[g * tq:(g + 1) * tq, :] = jnp.concatenate([q_ref[:, g * LANES:(g + 1) * LANES], ns], axis=-1)
    m_ref[...] = jnp.full(m_ref.shape, NEG_INF, F32)
    acc_ref[...] = jnp.zeros(acc_ref.shape, F32)

    def produce(s_ref, mx_ref, kt, masked):
        start = pl.multiple_of(kt * tk, tk)
        kx = jnp.concatenate([k_ref[pl.ds(start, tk), :], bc_ref[pl.ds(start, tk), :]], axis=-1)
        s = _dot_nt(qx_ref[...], kx)
        if masked:
            r = lax.broadcasted_iota(jnp.int32, (tq, tk), 0)
            col = lax.broadcasted_iota(jnp.int32, (tq, tk), 1)
            s = s + jnp.concatenate([jnp.where(col <= r, 0.0, NEG_INF)] * G_NSA, axis=0)
        s_ref[...] = s
        mx_ref[...] = jnp.broadcast_to(jnp.max(s, axis=-1, keepdims=True), mx_ref.shape)

    def consume(s_ref, mx_ref, kt):
        m = m_ref[...]
        m_new = jnp.maximum(m, mx_ref[...])
        p = jnp.exp(s_ref[...] - jnp.concatenate([m_new] * (tk // LANES), axis=1)).astype(BF16)
        vx = v_ref[pl.ds(pl.multiple_of(kt * tk, tk), tk), :]
        acc_ref[...] = jnp.exp(m - m_new) * acc_ref[...] + _dot(p, vx)
        m_ref[...] = m_new

    def tile(k):
        return k - 1

    produce(s0_ref, mx0_ref, diag, True)

    def pair(jj, carry):
        k = 2 * jj
        produce(s1_ref, mx1_ref, tile(k + 1), False)
        consume(s0_ref, mx0_ref, jnp.where(k == 0, diag, tile(k)))
        produce(s0_ref, mx0_ref, tile(k + 2), False)
        consume(s1_ref, mx1_ref, tile(k + 1))
        return carry

    n_pairs = lax.shift_right_logical(diag, 1)
    lax.fori_loop(0, n_pairs, pair, 0)
    k_last = 2 * n_pairs

    @pl.when(k_last == diag)
    def _():
        consume(s0_ref, mx0_ref, jnp.where(diag == 0, diag, tile(diag)))

    @pl.when(k_last != diag)
    def _():
        produce(s1_ref, mx1_ref, tile(diag), False)
        consume(s0_ref, mx0_ref, jnp.where(k_last == 0, diag, tile(k_last)))
        consume(s1_ref, mx1_ref, tile(diag))

    acc = acc_ref[...]
    o_ref[...] = _normalize_pack([acc[g * tq:(g + 1) * tq] for g in range(G_NSA)]).astype(o_ref.dtype)


def _selattn(proj, notsel, blkcols, *, B, T):
    tq = 512
    tk = 512
    nt = T // tq
    m_rows = G_NSA * tq
    assert T % tk == 0 and tk == tq
    kern = functools.partial(_selattn_kernel, tq=tq, tk=tk)
    return pl.pallas_call(
        kern,
        grid=(B, N_KV_NSA, nt),
        in_specs=[
            pl.BlockSpec((tq, G_NSA * LANES), lambda b, h, i: (b * nt + i, h)),
            pl.BlockSpec((1, 1, tq, LANES), lambda b, h, i: (b, h, i, 0)),
            pl.BlockSpec((T, LANES), lambda b, h, i: (b, SLOT_KSL + h)),
            pl.BlockSpec((T, LANES), lambda b, h, i: (b, SLOT_VSL + h)),
            pl.BlockSpec((T, LANES), lambda b, h, i: (0, 0)),
        ],
        out_specs=pl.BlockSpec((tq, G_NSA * HEAD_DIM), lambda b, h, i: (b * nt + i, h)),
        out_shape=jax.ShapeDtypeStruct((B * T, D_NSA), BF16),
        scratch_shapes=[
            pltpu.VMEM((m_rows, 2 * LANES), BF16),
            pltpu.VMEM((m_rows, tk), F32),
            pltpu.VMEM((m_rows, tk), F32),
            pltpu.VMEM((m_rows, LANES), F32),
            pltpu.VMEM((m_rows, LANES), F32),
            pltpu.VMEM((m_rows, LANES), F32),
            pltpu.VMEM((m_rows, LANES), F32),
        ],
        compiler_params=_cparams(("arbitrary", "arbitrary", "arbitrary")),
        name="selattn",
    )(proj, notsel, proj, proj, blkcols)


def _band_kernel(sl_ref, sink_ref, q_ref, k_ref, v_ref, o_ref, *, tq, n_sub, window, seq, use_sinks):
    h = pl.program_id(1)
    span = window + tq
    groups = q_ref.shape[1] // LANES
    r = lax.broadcasted_iota(jnp.int32, (tq, span), 0)
    col = lax.broadcasted_iota(jnp.int32, (tq, span), 1)
    for sub in range(n_sub):
        q0 = (pl.program_id(2) * n_sub + sub) * tq
        start = pl.multiple_of(jnp.clip(q0 - window, 0, seq - span), tq)
        rows = pl.ds(sub * tq, tq)
        qx = jnp.concatenate([q_ref[rows, g * LANES:(g + 1) * LANES] for g in range(groups)], axis=0)
        d = (q0 - start) + r - col
        bias = jnp.where((d >= 0) & (d < window), 0.0, NEG_INF)
        s = _dot_nt(qx, k_ref[pl.ds(start, span), :]) + jnp.concatenate([bias] * groups, axis=0)
        m = jnp.broadcast_to(jnp.max(s, axis=-1, keepdims=True), (groups * tq, LANES))
        if use_sinks:
            t_rep = (q0 + lax.broadcasted_iota(jnp.int32, (tq, LANES), 0)).astype(F32)
            sink = jnp.concatenate(
                [sink_ref[h * groups + g] + sl_ref[h * groups + g] * t_rep for g in range(groups)], axis=0)
            m = jnp.maximum(m, sink)
        e = jnp.exp(s - jnp.concatenate([m] * (span // LANES), axis=1)).astype(BF16)
        pv = _dot(e, v_ref[pl.ds(start, span), :])
        extra = None
        if use_sinks:
            esink = jnp.exp(sink - m)
            extra = [esink[g * tq:(g + 1) * tq] for g in range(groups)]
        o_ref[rows, :] = _normalize_pack([pv[g * tq:(g + 1) * tq] for g in range(groups)], extra).astype(o_ref.dtype)


def _band(slopes, sinks, proj, *, B, T, window, q_slot, k_slot, v_slot, use_sinks, n_sub, name):
    tq = 128
    n_sub = min(n_sub, T // tq)
    tb = tq * n_sub
    nt = T // tb
    groups = 4
    assert window % tq == 0 and window + tq <= T and T % tb == 0
    qb = q_slot // groups
    kern = functools.partial(_band_kernel, tq=tq, n_sub=n_sub, window=window, seq=T, use_sinks=use_sinks)
    return pl.pallas_call(
        kern,
        grid=(B, 2, nt),
        in_specs=[
            pl.BlockSpec(memory_space=pltpu.SMEM),
            pl.BlockSpec(memory_space=pltpu.SMEM),
            pl.BlockSpec((tb, groups * LANES), lambda b, h, i: (b * nt + i, qb + h)),
            pl.BlockSpec((T, LANES), lambda b, h, i: (b, k_slot + h)),
            pl.BlockSpec((T, LANES), lambda b, h, i: (b, v_slot + h)),
        ],
        out_specs=pl.BlockSpec((tb, groups * HEAD_DIM), lambda b, h, i: (b * nt + i, h)),
        out_shape=jax.ShapeDtypeStruct((B * T, 2 * groups * HEAD_DIM), BF16),
        compiler_params=_cparams(("arbitrary", "arbitrary", "arbitrary")),
        name=name,
    )(slopes, sinks, proj, proj, proj)


def _outproj_kernel(ocmp_ref, oslc_ref, owin_ref, oswa_ref, gate_ref, x_ref, mod_ref, ex_ref, bn_ref, bs_ref,
                    w_ref, g2_ref, wr_ref, x1_ref, u2_ref, lt_ref):
    sg = _sigmoid(gate_ref[...].astype(F32)).astype(BF16)
    o_nsa = (_dot(sg, ex_ref[0]) * ocmp_ref[...].astype(F32)
             + _dot(sg, ex_ref[1]) * oslc_ref[...].astype(F32)
             + _dot(sg, ex_ref[2]) * owin_ref[...].astype(F32))
    o_swa = oswa_ref[...].astype(F32)
    n1 = o_nsa * lax.rsqrt(jnp.mean(o_nsa * o_nsa, axis=-1, keepdims=True) + EPS) * bn_ref[...]
    n2 = o_swa * lax.rsqrt(jnp.mean(o_swa * o_swa, axis=-1, keepdims=True) + EPS) * bs_ref[...]
    merged = jnp.concatenate([n1, n2], axis=-1).astype(BF16)
    y = _dot(merged, w_ref[0])
    x1 = x_ref[...] + mod_ref[0, 2:3, :] * y
    x1_ref[...] = x1
    u2 = x1 * lax.rsqrt(jnp.mean(x1 * x1, axis=-1, keepdims=True) + EPS) * g2_ref[...]
    u2 = u2 * (1.0 + mod_ref[0, 4:5, :]) + mod_ref[0, 3:4, :]
    u2_ref[...] = _pack_bf16_pairs(u2)
    u_hi = u2.astype(BF16)
    u_lo = (u2 - u_hi.astype(F32)).astype(BF16)
    wr = wr_ref[...]
    w_hi = wr.astype(BF16)
    w_lo = (wr - w_hi.astype(F32)).astype(BF16)
    lt_ref[...] = _dot_nt(w_hi, u_hi) + _dot_nt(w_hi, u_lo) + _dot_nt(w_lo, u_hi)


def _outproj(ocmp, oslc, owin, oswa, proj, x2, mod, expand, beta_n, beta_s, w_out, norm2_g, wr_t, *, seq, layer):
    N, D = x2.shape
    tm = 1024
    tiles_per_seq = seq // tm
    row = lambda i: (i, 0)
    const2 = lambda i: (0, 0)
    return pl.pallas_call(
        _outproj_kernel,
        grid=(N // tm,),
        in_specs=[
            pl.BlockSpec((tm, D_NSA), row),
            pl.BlockSpec((tm, D_NSA), row),
            pl.BlockSpec((tm, D_NSA), row),
            pl.BlockSpec((tm, D_SWA), row),
            pl.BlockSpec((tm, LANES), lambda i: (i, SLOT_GATE)),
            pl.BlockSpec((tm, D), row),
            pl.BlockSpec((1, 6, D), lambda i: (i // tiles_per_seq, 0, 0)),
            pl.BlockSpec((N_NSA_BRANCHES, LANES, D_NSA), lambda i: (0, 0, 0)),
            pl.BlockSpec((1, D_NSA), const2),
            pl.BlockSpec((1, D_SWA), const2),
            pl.BlockSpec((1, D_MIX, D), lambda i: (layer, 0, 0)),
            pl.BlockSpec((1, D), const2),
            pl.BlockSpec((N_EXPERTS, D), const2),
        ],
        out_specs=[
            pl.BlockSpec((tm, D), row),
            pl.BlockSpec((tm, D // 2), row),
            pl.BlockSpec((N_EXPERTS, tm), lambda i: (0, i)),
        ],
        out_shape=[
            jax.ShapeDtypeStruct((N, D), F32),
            jax.ShapeDtypeStruct((N, D // 2), jnp.uint32),
            jax.ShapeDtypeStruct((N_EXPERTS, N), F32),
        ],
        compiler_params=_cparams(("arbitrary",)),
        name="outproj",
    )(ocmp, oslc, owin, oswa, proj, x2, mod, expand, beta_n, beta_s, w_out, norm2_g, wr_t)


def _route_kernel(lt_ref, tri_ref, d0_ref, d1_ref, w0_ref, w1_ref, seg_ref, nch_ref, tail_ref, *, tm):
    lt = lt_ref[...]
    e = jnp.exp(lt - jnp.max(lt, axis=0, keepdims=True))
    aff = e / jnp.sum(e, axis=0, keepdims=True)
    rows = [aff[k:k + 1, :] for k in range(N_EXPERTS)]
    epg = EXPERTS_PER_GROUP
    scores = []
    for gr in range(N_GROUPS):
        xs = rows[gr * epg:(gr + 1) * epg]
        top1 = functools.reduce(jnp.maximum, xs)
        second = None
        for a in range(epg):
            for b in range(a + 1, epg):
                mn = jnp.minimum(xs[a], xs[b])
                second = mn if second is None else jnp.maximum(second, mn)
        scores.append(top1 + second)
    taken = None
    in_group = []
    for gr in range(N_GROUPS):
        best = None
        for o in range(gr + 1, N_GROUPS):
            c = scores[gr] >= scores[o]
            best = c if best is None else (best & c)
        if best is None:
            best = jnp.full(scores[gr].shape, True)
        sel = best if taken is None else (best & ~taken)
        taken = sel if taken is None else (taken | sel)
        in_group.append(sel)
    ys = []
    for k in range(epg):
        yk = rows[k]
        for gr in range(1, N_GROUPS):
            yk = jnp.where(in_group[gr], rows[gr * epg + k], yk)
        ys.append(yk)
    chosen = []
    for k in range(epg):
        rk = jnp.zeros_like(ys[k])
        for o in range(epg):
            if o == k:
                continue
            before = (ys[o] >= ys[k]) if o < k else (ys[o] > ys[k])
            rk = rk + jnp.where(before, 1.0, 0.0)
        chosen.append(rk < TOP_K)
    total = functools.reduce(lambda a, b: a + b, [jnp.where(chosen[k], ys[k], 0.0) for k in range(epg)])
    first, seen = [], None
    for k in range(epg):
        first.append(chosen[k] if seen is None else (chosen[k] & ~seen))
        seen = chosen[k] if seen is None else (seen | chosen[k])
    second = [chosen[k] & ~first[k] for k in range(epg)]
    inv_total = 1.0 / total
    w_first = functools.reduce(lambda a, b: a + b, [jnp.where(first[k], ys[k], 0.0) for k in range(epg)]) * inv_total
    w_second = functools.reduce(lambda a, b: a + b, [jnp.where(second[k], ys[k], 0.0) for k in range(epg)]) * inv_total
    f_rows, s_rows = [], []
    for ex in range(N_EXPERTS):
        gr, k = divmod(ex, epg)
        f_rows.append(jnp.where(in_group[gr] & first[k], 1.0, 0.0))
        s_rows.append(jnp.where(in_group[gr] & second[k], 1.0, 0.0))
    fmat = jnp.concatenate(f_rows, axis=0)
    smat = jnp.concatenate(s_rows, axis=0)
    cmat = fmat + smat
    carry = jnp.zeros((N_EXPERTS, 1), F32)
    ranks = []
    for blk in range(tm // TRI):
        cb = cmat[:, blk * TRI:(blk + 1) * TRI]
        ranks.append(_dot(cb.astype(BF16), tri_ref[...]) + carry)
        carry = carry + jnp.sum(cb, axis=1, keepdims=True)
    rank = jnp.concatenate(ranks, axis=1)
    padded = jnp.floor((carry + (SEG_ALIGN - 1)) * (1.0 / SEG_ALIGN)) * SEG_ALIGN
    seg_rows, run = [], jnp.zeros((1, 1), F32)
    for ex in range(N_EXPERTS):
        seg_rows.append(run)
        run = run + padded[ex:ex + 1]
    seg = jnp.concatenate(seg_rows, axis=0)
    dest = rank + seg
    d0_ref[0] = jnp.sum(fmat * dest, axis=0, keepdims=True).astype(jnp.int32)
    d1_ref[0] = jnp.sum(smat * dest, axis=0, keepdims=True).astype(jnp.int32)
    w0_ref[...] = jnp.broadcast_to(w_first, (LANES, tm)).T
    w1_ref[...] = jnp.broadcast_to(w_second, (LANES, tm)).T
    seg_ref[0] = jnp.broadcast_to(seg, (N_EXPERTS, LANES)).astype(jnp.int32)
    whole = jnp.floor(carry * (1.0 / MOE_CHUNK))
    rem = carry - whole * MOE_CHUNK
    n_full = whole + jnp.where(rem > MOE_CHUNK // 2, 1.0, 0.0)
    half_tail = jnp.where((rem > 0) & (rem <= MOE_CHUNK // 2), 1.0, 0.0)
    nch_ref[0] = jnp.broadcast_to(n_full, (N_EXPERTS, LANES)).astype(jnp.int32)
    tail_ref[0] = jnp.broadcast_to(half_tail, (N_EXPERTS, LANES)).astype(jnp.int32)


def _route(lt, tri):
    E, N = lt.shape
    tm = MOE_TILE
    nt = N // tm
    return pl.pallas_call(
        functools.partial(_route_kernel, tm=tm),
        grid=(nt,),
        in_specs=[pl.BlockSpec((E, tm), lambda i: (0, i)), pl.BlockSpec((TRI, TRI), lambda i: (0, 0))],
        out_specs=[
            pl.BlockSpec((1, 1, tm), lambda i: (i, 0, 0)),
            pl.BlockSpec((1, 1, tm), lambda i: (i, 0, 0)),
            pl.BlockSpec((tm, LANES), lambda i: (i, 0)),
            pl.BlockSpec((tm, LANES), lambda i: (i, 0)),
            pl.BlockSpec((1, E, LANES), lambda i: (i, 0, 0)),
            pl.BlockSpec((1, E, LANES), lambda i: (i, 0, 0)),
            pl.BlockSpec((1, E, LANES), lambda i: (i, 0, 0)),
        ],
        out_shape=[
            jax.ShapeDtypeStruct((nt, 1, tm), jnp.int32),
            jax.ShapeDtypeStruct((nt, 1, tm), jnp.int32),
            jax.ShapeDtypeStruct((N, LANES), F32),
            jax.ShapeDtypeStruct((N, LANES), F32),
            jax.ShapeDtypeStruct((nt, E, LANES), jnp.int32),
            jax.ShapeDtypeStruct((nt, E, LANES), jnp.int32),
            jax.ShapeDtypeStruct((nt, E, LANES), jnp.int32),
        ],
        compiler_params=_cparams(("arbitrary",)),
        name="route",
    )(lt, tri)


def _moe_kernel(seg_ref, nch_ref, tail_ref, d0_ref, d1_ref, u_ref, w1_ref, w3_ref, w2_ref, w0c_ref, w1c_ref, x1_ref,
                mod_ref, o_ref, xs_ref, ys_ref, g0_ref, g1_ref, *, tm, chunk, n_exp_steps, tsub):
    i = pl.program_id(0)
    step = pl.program_id(1)

    @pl.when(step == 0)
    def _():
        xs_ref[...] = jnp.zeros_like(xs_ref)
        ys_ref[...] = jnp.zeros_like(ys_ref)

        def move(grp, carry):
            t0 = pl.multiple_of(grp * 8, 8)
            blk = u_ref[pl.ds(t0, 8), :]
            for k in range(8):
                row = blk[k:k + 1, :]
                xs_ref[pl.ds(d0_ref[0, 0, t0 + k], 1), :] = row
                xs_ref[pl.ds(d1_ref[0, 0, t0 + k], 1), :] = row
            return carry

        lax.fori_loop(0, tm // 8, move, 0)

    @pl.when(step < n_exp_steps)
    def _():
        for sub in range(EXPERTS_PER_STEP):
            slot = i * N_EXPERTS + step * EXPERTS_PER_STEP + sub
            start = seg_ref[slot]
            n_full = nch_ref[slot]

            def do_rows(r0, rows, sub=sub):
                lo, hi = _unpack_bf16_pairs(xs_ref[pl.ds(r0, rows), :])
                xb = jnp.concatenate([lo, hi], axis=-1).astype(BF16)
                a = _dot(xb, w1_ref[0, sub])
                hmid = a * _sigmoid(a) * _dot(xb, w3_ref[0, sub])
                ys_ref[pl.ds(r0, rows), :] = _pack_bf16_pairs(_dot(hmid.astype(BF16), w2_ref[0, sub]))

            def do_chunk(j, carry, start=start, do_rows=do_rows):
                do_rows(pl.multiple_of(start + j * chunk, SEG_ALIGN), chunk)
                return carry

            lax.fori_loop(0, n_full, do_chunk, 0)

            @pl.when(tail_ref[slot] > 0)
            def _(start=start, n_full=n_full, do_rows=do_rows):
                do_rows(pl.multiple_of(start + n_full * chunk, SEG_ALIGN), chunk // 2)

    @pl.when(step >= n_exp_steps)
    def _():
        base = (step - n_exp_steps) * tsub

        def move(grp, carry):
            t0 = pl.multiple_of(grp * 8, 8)
            for d_ref, g_ref in ((d0_ref, g0_ref), (d1_ref, g1_ref)):
                rows = [ys_ref[pl.ds(d_ref[0, 0, base + t0 + k], 1), :] for k in range(8)]
                g_ref[pl.ds(t0, 8), :] = jnp.concatenate(rows, axis=0)
            return carry

        lax.fori_loop(0, tsub // 8, move, 0)
        lo0, hi0 = _unpack_bf16_pairs(g0_ref[...])
        lo1, hi1 = _unpack_bf16_pairs(g1_ref[...])
        half = g0_ref.shape[1]
        w0 = jnp.concatenate([w0c_ref[...]] * (half // LANES), axis=1)
        w1 = jnp.concatenate([w1c_ref[...]] * (half // LANES), axis=1)
        o_ref[:, :half] = x1_ref[:, :half] + mod_ref[0, 5:6, :half] * (w0 * lo0 + w1 * lo1)
        o_ref[:, half:] = x1_ref[:, half:] + mod_ref[0, 5:6, half:] * (w0 * hi0 + w1 * hi1)


def _moe(seg, nch, tail, d0, d1, u2p, w1, w3, w2, w0rep, w1rep, x1, mod, *, layer, seq):
    N, half = u2p.shape
    D = 2 * half
    E = w1.shape[1]
    tm = MOE_TILE
    tsub = 512
    nt = N // tm
    n_exp = E // EXPERTS_PER_STEP
    n_cmb = tm // tsub
    kern = functools.partial(_moe_kernel, tm=tm, chunk=MOE_CHUNK, n_exp_steps=n_exp, tsub=tsub)
    wexp = lambda i, s, *_: (layer, jnp.minimum(s, n_exp - 1), 0, 0)
    csub = lambda i, s, *_: (i * n_cmb + jnp.clip(s - n_exp, 0, n_cmb - 1), 0)
    grid_spec = pltpu.PrefetchScalarGridSpec(
        num_scalar_prefetch=3,
        grid=(nt, n_exp + n_cmb),
        in_specs=[
            pl.BlockSpec((1, 1, tm), lambda i, s, *_: (i, 0, 0), memory_space=pltpu.SMEM),
            pl.BlockSpec((1, 1, tm), lambda i, s, *_: (i, 0, 0), memory_space=pltpu.SMEM),
            pl.BlockSpec((tm, half), lambda i, s, *_: (i, 0)),
            pl.BlockSpec((1, EXPERTS_PER_STEP, D, D_EXPERT), wexp),
            pl.BlockSpec((1, EXPERTS_PER_STEP, D, D_EXPERT), wexp),
            pl.BlockSpec((1, EXPERTS_PER_STEP, D_EXPERT, D), wexp),
            pl.BlockSpec((tsub, LANES), csub),
            pl.BlockSpec((tsub, LANES), csub),
            pl.BlockSpec((tsub, D), csub),
            pl.BlockSpec((1, 6, D),
                         lambda i, s, *_: ((i * tm + jnp.clip(s - n_exp, 0, n_cmb - 1) * tsub) // seq, 0, 0)),
        ],
        out_specs=pl.BlockSpec((tsub, D), csub),
        scratch_shapes=[
            pltpu.VMEM((MOE_ROWS, half), jnp.uint32),
            pltpu.VMEM((MOE_ROWS, half), jnp.uint32),
            pltpu.VMEM((tsub, half), jnp.uint32),
            pltpu.VMEM((tsub, half), jnp.uint32),
        ],
    )
    return pl.pallas_call(
        kern,
        grid_spec=grid_spec,
        out_shape=jax.ShapeDtypeStruct((N, D), F32),
        compiler_params=pltpu.CompilerParams(dimension_semantics=("arbitrary", "arbitrary"),
                                             vmem_limit_bytes=MOE_VMEM_LIMIT),
        name="moe",
    )(seg, nch, tail, d0, d1, u2p, w1, w3, w2, w0rep, w1rep, x1, mod)


def _alibi_slopes(first, count):
    hh = jnp.arange(first + 1, first + count + 1, dtype=F32)
    return jnp.exp2(-8.0 * hh / N_HEADS_TOTAL)


def _bd_const():
    idx = np.arange(2 * LANES) // HEAD_DIM
    return jnp.asarray((idx[:, None] == idx[None, :]).astype(np.float32), BF16)


def _blkcols_const(T):
    blk = np.arange(T) // SEL_LEN
    m = np.where(np.arange(LANES)[None, :] == blk[:, None], NEG_INF, 0.0).astype(np.float32)
    return jnp.asarray(m, BF16)


def _overlap_const(T):
    nc = T // CMP_STRIDE
    n_sel = T // SEL_LEN
    cs = np.arange(nc) * CMP_STRIDE
    ss = np.arange(n_sel) * SEL_LEN
    ov = np.clip(np.minimum(cs[:, None] + CMP_LEN, ss[None, :] + SEL_LEN)
                 - np.maximum(cs[:, None], ss[None, :]), 0, None) / CMP_LEN
    return jnp.asarray(ov.T.astype(np.float32))


def _expand_const():
    ex = np.zeros((N_NSA_BRANCHES, LANES, D_NSA), np.float32)
    for br in range(N_NSA_BRANCHES):
        for hd in range(N_HEADS_NSA):
            ex[br, hd * N_NSA_BRANCHES + br, hd * HEAD_DIM:(hd + 1) * HEAD_DIM] = 1.0
    return jnp.asarray(ex, BF16)


def _compact_w_in(w):
    parts = [w[..., a:b] for a, b in (_C_QN, _C_QS, _C_KSL, _C_KW, _C_KS, _C_VSL, _C_VW, _C_VS, _C_KC, _C_VC, _C_GATE)]
    pad = jnp.zeros(w.shape[:-1] + (LANES - (_C_GATE[1] - _C_GATE[0]),), w.dtype)
    return jnp.concatenate(parts + [pad], axis=-1).astype(BF16)


def _qconst(slopes_q):
    hi = slopes_q.astype(BF16).astype(F32)
    lo = (slopes_q - hi).astype(BF16).astype(F32)
    cols = jnp.stack([64.0 * hi, 64.0 * lo, hi, lo], axis=1)
    return jnp.zeros((slopes_q.shape[0], LANES), F32).at[:, HEAD_DIM:HEAD_DIM + 4].set(cols)


def kernel(x, c, w_router, ada_w, ada_b, norm1_g, norm2_g, w_in, w_out, nsa_q_gain, nsa_k_gain, cmp_pos_k,
           cmp_pos_v, cmp_w1_k, cmp_w2_k, cmp_w1_v, cmp_w2_v, swa_q_gain, swa_k_gain, swa_sinks, beta_nsa,
           beta_swa, moe_w1, moe_w3, moe_w2):
    B, T, D = x.shape
    L = ada_w.shape[0]
    N = B * T
    nc = T // CMP_STRIDE
    assert D == D_MODEL and T % 1024 == 0 and T <= 64 * 128 and B <= 8

    slopes_nsa = _alibi_slopes(N_HEADS_SWA, N_HEADS_NSA)
    slopes_swa = _alibi_slopes(0, N_HEADS_SWA)
    qconst = _qconst(jnp.concatenate([slopes_nsa, slopes_swa]))
    bd = _bd_const()
    blkcols = _blkcols_const(T)
    ov = _overlap_const(T)
    expand = _expand_const()
    wr_t = w_router.T
    tri = jnp.asarray(np.triu(np.ones((TRI, TRI), np.float32), 1), BF16)
    assert N % MOE_TILE == 0
    moe_w1b, moe_w3b, moe_w2b = moe_w1.astype(BF16), moe_w3.astype(BF16), moe_w2.astype(BF16)
    w_outb = w_out.astype(BF16)
    w_inb = _compact_w_in(w_in)
    cmp_w1s = jnp.stack([cmp_w1_k, cmp_w1_v], axis=1).reshape(2 * L, CMP_LEN * HEAD_DIM, CMP_HIDDEN).astype(BF16)
    cmp_w2s = jnp.pad(jnp.stack([cmp_w2_k, cmp_w2_v], axis=1).reshape(2 * L, CMP_HIDDEN, HEAD_DIM),
                      ((0, 0), (0, 0), (0, LANES - HEAD_DIM))).astype(BF16)
    cmp_pos = jnp.stack([cmp_pos_k, cmp_pos_v], axis=1).reshape(2 * L, 1, CMP_LEN * HEAD_DIM)
    cmp_pos = jnp.broadcast_to(cmp_pos, (2 * L, 8, CMP_LEN * HEAD_DIM)).astype(BF16)
    zero_sinks = jnp.zeros((N_HEADS_NSA,), F32)

    c_pad = jnp.zeros((8, D), F32).at[:B].set(c)
    mod_all = _ada(c_pad, ada_w, ada_b)[:, :B].reshape(L, B, 6, D)

    x2 = x.reshape(N, D)
    for l in range(L):
        mod = mod_all[l]
        tile2 = lambda g: jnp.tile(g, 2)
        gain_c = jnp.concatenate([
            jnp.tile(nsa_q_gain[l], N_HEADS_NSA) * ATTN_SCALE,
            jnp.tile(swa_q_gain[l], N_HEADS_SWA) * ATTN_SCALE,
            tile2(nsa_k_gain[l, 1]), tile2(nsa_k_gain[l, 2]), tile2(swa_k_gain[l])]).reshape(1, -1)
        proj, xc = _inproj(x2, mod, norm1_g[l].reshape(1, D), w_inb, gain_c, qconst, bd, seq=T, layer=l)
        kgain = jnp.pad(nsa_k_gain[l, 0], (0, LANES - HEAD_DIM)).reshape(1, LANES)
        kvc = _cmpmlp(xc, cmp_w1s, cmp_w2s, cmp_pos, kgain, B=B, layer=l)

        o_cmp, notsel = _cmpattn(proj, kvc, ov, B=B, T=T)
        o_slc = _selattn(proj, notsel, blkcols, B=B, T=T)
        o_win = _band(slopes_nsa, zero_sinks, proj, B=B, T=T, window=NSA_WINDOW, q_slot=SLOT_QN,
                      k_slot=SLOT_KW, v_slot=SLOT_VW, use_sinks=False, n_sub=16, name="winattn")
        o_swa = _band(slopes_swa, swa_sinks[l], proj, B=B, T=T, window=SWA_WINDOW, q_slot=SLOT_QS,
                      k_slot=SLOT_KS, v_slot=SLOT_VS, use_sinks=True, n_sub=16, name="swaattn")

        x1, u2, lt = _outproj(o_cmp, o_slc, o_win, o_swa, proj, x2, mod, expand,
                              beta_nsa[l].reshape(1, -1), beta_swa[l].reshape(1, -1), w_outb,
                              norm2_g[l].reshape(1, D), wr_t, seq=T, layer=l)
        d0, d1, w0rep, w1rep, seg, nch, tail = _route(lt, tri)
        x2 = _moe(seg[:, :, 0].reshape(-1), nch[:, :, 0].reshape(-1), tail[:, :, 0].reshape(-1), d0, d1, u2,
                  moe_w1b, moe_w3b, moe_w2b, w0rep, w1rep, x1, mod, layer=l, seq=T)
    return x2.reshape(B, T, D)
```

```python
import functools

import numpy as np
import jax
import jax.numpy as jnp
from jax import lax
from jax.experimental import pallas as pl
from jax.experimental.pallas import tpu as pltpu

D_MODEL = 1024
HEAD_DIM = 64
N_HEADS_NSA = 8
N_KV_NSA = 2
G_NSA = N_HEADS_NSA // N_KV_NSA
N_HEADS_SWA = 8
D_NSA = N_HEADS_NSA * HEAD_DIM
D_SWA = N_HEADS_SWA * HEAD_DIM
D_MIX = D_NSA + D_SWA
N_HEADS_TOTAL = N_HEADS_NSA + N_HEADS_SWA
N_NSA_BRANCHES = 3
CMP_LEN = 32
CMP_STRIDE = 16
CMP_HIDDEN = 256
SEL_LEN = 64
SEL_TOPK = 16
NSA_WINDOW = 512
SWA_WINDOW = 128
N_EXPERTS = 16
N_GROUPS = 4
EXPERTS_PER_GROUP = N_EXPERTS // N_GROUPS
TOP_K = 2
D_EXPERT = 512
EPS = 1e-6
NEG_INF = -1e30
ATTN_SCALE = HEAD_DIM ** -0.5

F32 = jnp.float32
BF16 = jnp.bfloat16

LANES = 128
VMEM_LIMIT = 48 * 1024 * 1024

MOE_TILE = 2048
MOE_CHUNK = 256
SEG_ALIGN = 16
TRI = 256
MOE_ROWS = TOP_K * MOE_TILE + N_EXPERTS * SEG_ALIGN + MOE_CHUNK
EXPERTS_PER_STEP = 2
MOE_VMEM_LIMIT = 56 * 1024 * 1024

SLOT_QN = 0
SLOT_QS = 8
SLOT_KSL = 16
SLOT_KW = 18
SLOT_KS = 20
SLOT_VSL = 22
SLOT_VW = 24
SLOT_VS = 26
SLOT_GATE = 28
N_SLOTS = 29
N_CHUNKS = 17
N_NORM_CHUNKS = 11

_C_QN = (0, 512)
_C_KC = (512, 640)
_C_VC = (640, 768)
_C_KSL = (768, 896)
_C_VSL = (896, 1024)
_C_KW = (1024, 1152)
_C_VW = (1152, 1280)
_C_GATE = (1280, 1304)
_C_QS = (1304, 1816)
_C_KS = (1816, 1944)
_C_VS = (1944, 2072)


def _cparams(sem):
    return pltpu.CompilerParams(dimension_semantics=sem, vmem_limit_bytes=VMEM_LIMIT)


def _dot(a, b, **kw):
    return jnp.dot(a, b, preferred_element_type=F32, **kw)


def _dot_nt(a, b):
    return lax.dot_general(a, b, (((1,), (1,)), ((), ())), preferred_element_type=F32)


def _sigmoid(x):
    return 1.0 / (1.0 + jnp.exp(-x))


def _pack_bf16_pairs(x):
    n = x.shape[1] // 2
    return pltpu.pack_elementwise([x[:, :n], x[:, n:]], packed_dtype=BF16)


def _unpack_bf16_pairs(w):
    lo = pltpu.unpack_elementwise(w, index=0, packed_dtype=BF16, unpacked_dtype=F32)
    hi = pltpu.unpack_elementwise(w, index=1, packed_dtype=BF16, unpacked_dtype=F32)
    return lo, hi


def _ada_kernel(c_ref, w_ref, b_ref, o_ref):
    c = c_ref[...]
    cond = c * _sigmoid(c)
    o_ref[0] = _dot(cond.astype(BF16), w_ref[0].astype(BF16)) + b_ref[0]


def _ada(c_pad, ada_w, ada_b):
    L, D, N6 = ada_w.shape
    tn = 1536
    return pl.pallas_call(
        _ada_kernel,
        grid=(L, N6 // tn),
        in_specs=[
            pl.BlockSpec((8, D), lambda l, j: (0, 0)),
            pl.BlockSpec((1, D, tn), lambda l, j: (l, 0, j)),
            pl.BlockSpec((1, 1, tn), lambda l, j: (l, 0, j)),
        ],
        out_specs=pl.BlockSpec((1, 8, tn), lambda l, j: (l, 0, j)),
        out_shape=jax.ShapeDtypeStruct((L, 8, N6), F32),
        compiler_params=_cparams(("arbitrary", "arbitrary")),
        name="ada",
    )(c_pad, ada_w, ada_b.reshape(L, 1, N6))


def _inproj_kernel(x_ref, mod_ref, g_ref, w_ref, gain_ref, qc_ref, bd_ref, o_ref, xc_ref, scr_ref, *, tm, seq):
    i = pl.program_id(0)
    x = x_ref[...]
    ms = jnp.mean(x * x, axis=-1, keepdims=True)
    u = x * lax.rsqrt(ms + EPS) * g_ref[...]
    u = u * (1.0 + mod_ref[0, 1:2, :]) + mod_ref[0, 0:1, :]
    y = _dot(u.astype(BF16), w_ref[0])

    lane = lax.broadcasted_iota(jnp.int32, (tm, LANES), 1)
    low = lane < HEAD_DIM
    t = lax.broadcasted_iota(jnp.int32, (tm, LANES), 0) + lax.rem(i * tm, seq)
    hi_part = (t >> 6).astype(F32)
    lo_part = (t & 63).astype(F32)
    poscols = jnp.where((lane == 64) | (lane == 65), hi_part,
                        jnp.where((lane == 66) | (lane == 67), lo_part, 0.0))
    bd = bd_ref[...]

    def put(slot, val):
        o_ref[:, slot * LANES:(slot + 1) * LANES] = val.astype(o_ref.dtype)

    normed = {}
    for c0 in range(0, N_NORM_CHUNKS, 2):
        width = min(2, N_NORM_CHUNKS - c0) * LANES
        wide = y[:, c0 * LANES:c0 * LANES + width]
        msb = _dot((wide * wide).astype(BF16), bd[:width, :width]) * (1.0 / HEAD_DIM)
        wide = wide * lax.rsqrt(msb + EPS) * gain_ref[:, c0 * LANES:c0 * LANES + width]
        for k in range(width // LANES):
            normed[c0 + k] = wide[:, k * LANES:(k + 1) * LANES]

    for c in range(N_CHUNKS):
        blk = normed[c] if c < N_NORM_CHUNKS else y[:, c * LANES:(c + 1) * LANES]
        if c < 14:
            rolled = pltpu.roll(blk, HEAD_DIM, axis=1)
            if c < 8:
                f0 = qc_ref[2 * c:2 * c + 1, :]
                f1 = qc_ref[2 * c + 1:2 * c + 2, :]
                s0 = 2 * c
            elif c < N_NORM_CHUNKS:
                f0 = f1 = poscols
                s0 = SLOT_KSL + 2 * (c - 8)
            else:
                f0 = f1 = 1.0
                s0 = SLOT_VSL + 2 * (c - N_NORM_CHUNKS)
            put(s0, jnp.where(low, blk, f0))
            put(s0 + 1, jnp.where(low, rolled, f1))
        elif c == 16:
            put(SLOT_GATE, blk)
        else:
            scr_ref[...] = blk
            nck = tm // CMP_STRIDE
            low_c = lax.broadcasted_iota(jnp.int32, (nck, LANES), 1) < HEAD_DIM
            head_cols = CMP_STRIDE * HEAD_DIM
            for pair in range(CMP_STRIDE // 2):
                ta = scr_ref[pl.ds(2 * pair, nck, stride=CMP_STRIDE), :]
                tb = scr_ref[pl.ds(2 * pair + 1, nck, stride=CMP_STRIDE), :]
                h0 = jnp.where(low_c, ta, pltpu.roll(tb, HEAD_DIM, axis=1))
                h1 = jnp.where(low_c, pltpu.roll(ta, HEAD_DIM, axis=1), tb)
                xc_ref[c - 14, :, pair * LANES:(pair + 1) * LANES] = h0.astype(xc_ref.dtype)
                xc_ref[c - 14, :, head_cols + pair * LANES:head_cols + (pair + 1) * LANES] = h1.astype(xc_ref.dtype)


def _inproj(x2, mod, norm_g, w_c, gain_c, qconst, bd, *, seq, layer):
    N, D = x2.shape
    tm = 1024
    tiles_per_seq = seq // tm
    kern = functools.partial(_inproj_kernel, tm=tm, seq=seq)
    nchunk = tm // CMP_STRIDE
    xc_shape = jax.ShapeDtypeStruct((2, N // CMP_STRIDE, CMP_STRIDE * LANES), BF16)
    return pl.pallas_call(
        kern,
        grid=(N // tm,),
        in_specs=[
            pl.BlockSpec((tm, D), lambda i: (i, 0)),
            pl.BlockSpec((1, 6, D), lambda i: (i // tiles_per_seq, 0, 0)),
            pl.BlockSpec((1, D), lambda i: (0, 0)),
            pl.BlockSpec((1, D, N_CHUNKS * LANES), lambda i: (layer, 0, 0)),
            pl.BlockSpec((1, N_NORM_CHUNKS * LANES), lambda i: (0, 0)),
            pl.BlockSpec((16, LANES), lambda i: (0, 0)),
            pl.BlockSpec((2 * LANES, 2 * LANES), lambda i: (0, 0)),
        ],
        out_specs=[
            pl.BlockSpec((tm, N_SLOTS * LANES), lambda i: (i, 0)),
            pl.BlockSpec((2, nchunk, CMP_STRIDE * LANES), lambda i: (0, i, 0)),
        ],
        out_shape=[jax.ShapeDtypeStruct((N, N_SLOTS * LANES), BF16), xc_shape],
        scratch_shapes=[pltpu.VMEM((tm, LANES), F32)],
        compiler_params=_cparams(("arbitrary",)),
        name="inproj",
    )(x2, mod, norm_g, w_c, gain_c, qconst, bd)


def _gelu_tanh(x):
    return 0.5 * x * (1.0 + jnp.tanh(np.sqrt(2.0 / np.pi).astype(np.float32) * (x + 0.044715 * (x * x * x))))


def _cmpmlp_kernel(x_ref, w1_ref, w2_ref, pos_ref, gain_ref, o_ref, *, nc):
    kv = pl.program_id(0)
    half = CMP_STRIDE * HEAD_DIM
    bias = _dot(pos_ref[0], w1_ref[0])[0:1]
    lane = lax.broadcasted_iota(jnp.int32, (nc, LANES), 1)
    last = lax.broadcasted_iota(jnp.int32, (nc, LANES), 0) * CMP_STRIDE + (CMP_LEN - 1)
    poscols = jnp.where((lane == 64) | (lane == 65), (last >> 6).astype(F32),
                        jnp.where((lane == 66) | (lane == 67), (last & 63).astype(F32), 0.0))
    for hd in range(N_KV_NSA):
        xc = x_ref[0, :, hd * half:(hd + 1) * half]
        first = _dot(xc, w1_ref[0, :half, :])
        second = _dot(xc, w1_ref[0, half:, :])
        h = _gelu_tanh(first + pltpu.roll(second, nc - 1, axis=0) + bias)
        z = _dot(h.astype(BF16), w2_ref[0])
        zn = z * lax.rsqrt(jnp.sum(z * z, axis=-1, keepdims=True) * (1.0 / HEAD_DIM) + EPS) * gain_ref[...]
        o_ref[0, 0, hd] = jnp.where(kv == 0, zn + poscols, z + jnp.where(lane >= HEAD_DIM, 1.0, 0.0))


def _cmpmlp(xc, w1s, w2s, pos, gain, *, B, layer):
    _, nchunks, K = xc.shape
    nc = nchunks // B
    win = CMP_LEN * HEAD_DIM
    kern = functools.partial(_cmpmlp_kernel, nc=nc)
    return pl.pallas_call(
        kern,
        grid=(2, B),
        in_specs=[
            pl.BlockSpec((1, nc, K), lambda kv, b: (kv, b, 0)),
            pl.BlockSpec((1, win, CMP_HIDDEN), lambda kv, b: (2 * layer + kv, 0, 0)),
            pl.BlockSpec((1, CMP_HIDDEN, LANES), lambda kv, b: (2 * layer + kv, 0, 0)),
            pl.BlockSpec((1, 8, win), lambda kv, b: (2 * layer + kv, 0, 0)),
            pl.BlockSpec((1, LANES), lambda kv, b: (0, 0)),
        ],
        out_specs=pl.BlockSpec((1, 1, N_KV_NSA, nc, LANES), lambda kv, b: (kv, b, 0, 0, 0)),
        out_shape=jax.ShapeDtypeStruct((2, B, N_KV_NSA, nc, LANES), F32),
        compiler_params=_cparams(("arbitrary", "arbitrary")),
        name="cmpmlp",
    )(xc, w1s, w2s, pos, gain)


def _normalize_pack(pvs, extra=None):
    lane = lax.broadcasted_iota(jnp.int32, pvs[0].shape, 1)
    low = lane < HEAD_DIM
    pairs = []
    for p in range(len(pvs) // 2):
        even, odd = pvs[2 * p], pvs[2 * p + 1]
        den_e = even if extra is None else even + extra[2 * p]
        den_o = odd if extra is None else odd + extra[2 * p + 1]
        o_e = even * (1.0 / pltpu.roll(den_e, HEAD_DIM, axis=1))
        o_o = pltpu.roll(odd, HEAD_DIM, axis=1) * (1.0 / den_o)
        pairs.append(jnp.where(low, o_e, o_o))
    return jnp.concatenate(pairs, axis=-1)


def _cmpattn_kernel(q_ref, kc_ref, vc_ref, ov_ref, o_alias, ns_alias, o_ref, ns_ref, *, tq, n_sub, nc, n_cmp, n_sel,
                    top, tile0):
    del o_alias, ns_alias
    kc = kc_ref[0, 0, 0].astype(BF16)
    vc = vc_ref[0, 0, 0].astype(BF16)
    for sub in range(n_sub):
        q0 = ((tile0 + pl.program_id(2)) * n_sub + sub) * tq
        rows = pl.ds(sub * tq, tq)
        _cmpattn_subtile(q_ref.at[rows, :], kc, vc, ov_ref, o_ref.at[rows, :], ns_ref.at[0, 0, rows, :], q0,
                         tq=tq, nc=nc, n_cmp=n_cmp, n_sel=n_sel, top=top)


def _cmpattn_subtile(q_ref, kc, vc, ov_ref, o_ref, ns_ref, q0, *, tq, nc, n_cmp, n_sel, top):
    t_col = q0 + lax.broadcasted_iota(jnp.int32, (tq, 1), 0)
    row_valid = (t_col >= CMP_LEN - 1).astype(F32)
    t_row = q0 + lax.broadcasted_iota(jnp.int32, (1, tq), 1)
    n_col = lax.broadcasted_iota(jnp.int32, (nc, 1), 0)
    valid_t = (t_row - (n_col * CMP_STRIDE + (CMP_LEN - 1)) >= 0) & (n_col < n_cmp)
    bias_t = jnp.where(valid_t, 0.0, NEG_INF)
    row_valid_t = (t_row >= CMP_LEN - 1).astype(F32)

    outs = []
    psum_t = jnp.zeros((nc, tq), F32)
    for g in range(G_NSA):
        qg = q_ref[:, g * LANES:(g + 1) * LANES]
        st = _dot_nt(kc, qg) + bias_t
        et = jnp.exp(st - jnp.max(st, axis=0, keepdims=True))
        outs.append(lax.dot_general(et.astype(BF16), vc, (((0,), (0,)), ((), ())), preferred_element_type=F32))
        psum_t = psum_t + et * (1.0 / jnp.sum(et, axis=0, keepdims=True))
    o_ref[...] = (_normalize_pack(outs) * row_valid).astype(o_ref.dtype)

    psum_t = psum_t * row_valid_t
    ov = ov_ref[...].astype(BF16)
    p_hi = psum_t.astype(BF16)
    p_lo = (psum_t - p_hi.astype(F32)).astype(BF16)
    imp_t = _dot(ov, p_hi) + _dot(ov, p_lo)
    j = lax.broadcasted_iota(jnp.int32, (n_sel, 1), 0)
    cur = t_row >> 6
    forced = (j == 0) | (j == cur) | (j == cur - 1)
    v = jnp.where(forced, 1e9, jnp.where(j > cur, NEG_INF, imp_t))
    n_grp = n_sel // 8
    vg = [v[8 * r:8 * r + 8, :] for r in range(n_grp)]
    jg = lax.broadcasted_iota(jnp.int32, (8, 1), 0)
    cnt = [jnp.zeros((8, tq), F32) for _ in range(n_grp)]
    for jp in range(n_sel):
        vj = v[jp:jp + 1, :]
        for r in range(n_grp):
            ge = jnp.where(vj >= vg[r], 1.0, 0.0)
            gt = jnp.where(vj > vg[r], 1.0, 0.0)
            if 8 * r > jp:
                inc = ge
            elif 8 * r + 7 < jp:
                inc = gt
            else:
                inc = jnp.where(jg + 8 * r > jp, ge, gt)
            cnt[r] = cnt[r] + inc
    notsel_t = jnp.concatenate([jnp.where(cn < top, 0.0, 1.0) for cn in cnt], axis=0)
    if n_sel < LANES:
        notsel_t = jnp.concatenate([notsel_t, jnp.zeros((LANES - n_sel, tq), F32)], axis=0)
    ns_ref[...] = notsel_t.T.astype(ns_ref.dtype)


def _cmpattn(proj, kvc, ov, *, B, T):
    tq = 256
    nt = T // tq
    nc = T // CMP_STRIDE
    n_cmp = (T - CMP_LEN) // CMP_STRIDE + 1
    n_sel = T // SEL_LEN
    top = min(SEL_TOPK, n_sel)
    assert n_sel <= LANES and n_sel % 8 == 0
    n_parts = 4 if (nt % 4 == 0 and n_sel % 32 == 0) else 1
    n_sub = 4 if (nt // n_parts) % 4 == 0 else 1
    tb = n_sub * tq
    steps = nt // n_sub
    sp = steps // n_parts
    o_cmp = jnp.zeros((B * T, D_NSA), BF16)
    notsel = jnp.zeros((B, N_KV_NSA, T, LANES), BF16)
    for part in range(n_parts):
        nc_p = nc * (part + 1) // n_parts
        ns_p = n_sel * (part + 1) // n_parts
        kern = functools.partial(_cmpattn_kernel, tq=tq, n_sub=n_sub, nc=nc_p, n_cmp=n_cmp, n_sel=ns_p, top=top,
                                 tile0=part * sp)
        qrow = lambda b, h, i, part=part: (b * steps + part * sp + i, h)
        o_cmp, notsel = pl.pallas_call(
            kern,
            grid=(B, N_KV_NSA, sp),
            in_specs=[
                pl.BlockSpec((tb, G_NSA * LANES), qrow),
                pl.BlockSpec((1, 1, 1, nc_p, LANES), lambda b, h, i: (0, b, h, 0, 0)),
                pl.BlockSpec((1, 1, 1, nc_p, LANES), lambda b, h, i: (1, b, h, 0, 0)),
                pl.BlockSpec((ns_p, nc_p), lambda b, h, i: (0, 0)),
                pl.BlockSpec(memory_space=pl.ANY),
                pl.BlockSpec(memory_space=pl.ANY),
            ],
            out_specs=[
                pl.BlockSpec((tb, G_NSA * HEAD_DIM), qrow),
                pl.BlockSpec((1, 1, tb, LANES), lambda b, h, i, part=part: (b, h, part * sp + i, 0)),
            ],
            out_shape=[
                jax.ShapeDtypeStruct((B * T, D_NSA), BF16),
                jax.ShapeDtypeStruct((B, N_KV_NSA, T, LANES), BF16),
            ],
            input_output_aliases={4: 0, 5: 1},
            compiler_params=_cparams(("arbitrary", "arbitrary", "arbitrary")),
            name="cmpattn",
        )(proj, kvc, kvc, ov[:ns_p, :nc_p], o_cmp, notsel)
    return o_cmp, notsel


def _selattn_kernel(q_ref, ns_ref, k_ref, v_ref, bc_ref, o_ref, qx_ref, s0_ref, s1_ref, mx0_ref, mx1_ref, m_ref,
                    acc_ref, *, tq, tk):
    diag = pl.program_id(2)
    ns = ns_ref[0, 0]
    for g in range(G_NSA):
        qx_ref[g * tq:(g + 1) * tq, :] = jnp.concatenate([q_ref[:, g * LANES:(g + 1) * LANES], ns], axis=-1)
    m_ref[...] = jnp.full(m_ref.shape, NEG_INF, F32)
    acc_ref[...] = jnp.zeros(acc_ref.shape, F32)

    def produce(s_ref, mx_ref, kt, masked):
        start = pl.multiple_of(kt * tk, tk)
        kx = jnp.concatenate([k_ref[pl.ds(start, tk), :], bc_ref[pl.ds(start, tk), :]], axis=-1)
        s = _dot_nt(qx_ref[...], kx)
        if masked:
            r = lax.broadcasted_iota(jnp.int32, (tq, tk), 0)
            col = lax.broadcasted_iota(jnp.int32, (tq, tk), 1)
            s = s + jnp.concatenate([jnp.where(col <= r, 0.0, NEG_INF)] * G_NSA, axis=0)
        s_ref[...] = s
        mx_ref[...] = jnp.broadcast_to(jnp.max(s, axis=-1, keepdims=True), mx_ref.shape)

    def consume(s_ref, mx_ref, kt):
        m = m_ref[...]
        m_new = jnp.maximum(m, mx_ref[...])
        p = jnp.exp(s_ref[...] - jnp.concatenate([m_new] * (tk // LANES), axis=1)).astype(BF16)
        vx = v_ref[pl.ds(pl.multiple_of(kt * tk, tk), tk), :]
        acc_ref[...] = jnp.exp(m - m_new) * acc_ref[...] + _dot(p, vx)
        m_ref[...] = m_new

    def tile(k):
        return k - 1

    produce(s0_ref, mx0_ref, diag, True)

    def pair(jj, carry):
        k = 2 * jj
        produce(s1_ref, mx1_ref, tile(k + 1), False)
        consume(s0_ref, mx0_ref, jnp.where(k == 0, diag, tile(k)))
        produce(s0_ref, mx0_ref, tile(k + 2), False)
        consume(s1_ref, mx1_ref, tile(k + 1))
        return carry

    n_pairs = lax.shift_right_logical(diag, 1)
    lax.fori_loop(0, n_pairs, pair, 0)
    k_last = 2 * n_pairs

    @pl.when(k_last == diag)
    def _():
        consume(s0_ref, mx0_ref, jnp.where(diag == 0, diag, tile(diag)))

    @pl.when(k_last != diag)
    def _():
        produce(s1_ref, mx1_ref, tile(diag), False)
        consume(s0_ref, mx0_ref, jnp.where(k_last == 0, diag, tile(k_last)))
        consume(s1_ref, mx1_ref, tile(diag))

    acc = acc_ref[...]
    o_ref[...] = _normalize_pack([acc[g * tq:(g + 1) * tq] for g in range(G_NSA)]).astype(o_ref.dtype)


def _selattn(proj, notsel, blkcols, *, B, T):
    tq = 512
    tk = 512
    nt = T // tq
    m_rows = G_NSA * tq
    assert T % tk == 0 and tk == tq
    kern = functools.partial(_selattn_kernel, tq=tq, tk=tk)
    return pl.pallas_call(
        kern,
        grid=(B, N_KV_NSA, nt),
        in_specs=[
            pl.BlockSpec((tq, G_NSA * LANES), lambda b, h, i: (b * nt + i, h)),
            pl.BlockSpec((1, 1, tq, LANES), lambda b, h, i: (b, h, i, 0)),
            pl.BlockSpec((T, LANES), lambda b, h, i: (b, SLOT_KSL + h)),
            pl.BlockSpec((T, LANES), lambda b, h, i: (b, SLOT_VSL + h)),
            pl.BlockSpec((T, LANES), lambda b, h, i: (0, 0)),
        ],
        out_specs=pl.BlockSpec((tq, G_NSA * HEAD_DIM), lambda b, h, i: (b * nt + i, h)),
        out_shape=jax.ShapeDtypeStruct((B * T, D_NSA), BF16),
        scratch_shapes=[
            pltpu.VMEM((m_rows, 2 * LANES), BF16),
            pltpu.VMEM((m_rows, tk), F32),
            pltpu.VMEM((m_rows, tk), F32),
            pltpu.VMEM((m_rows, LANES), F32),
            pltpu.VMEM((m_rows, LANES), F32),
            pltpu.VMEM((m_rows, LANES), F32),
            pltpu.VMEM((m_rows, LANES), F32),
        ],
        compiler_params=_cparams(("arbitrary", "arbitrary", "arbitrary")),
        name="selattn",
    )(proj, notsel, proj, proj, blkcols)


def _band_kernel(sl_ref, sink_ref, q_ref, k_ref, v_ref, o_ref, *, tq, n_sub, window, seq, use_sinks):
    h = pl.program_id(1)
    span = window + tq
    groups = q_ref.shape[1] // LANES
    r = lax.broadcasted_iota(jnp.int32, (tq, span), 0)
    col = lax.broadcasted_iota(jnp.int32, (tq, span), 1)
    for sub in range(n_sub):
        q0 = (pl.program_id(2) * n_sub + sub) * tq
        start = pl.multiple_of(jnp.clip(q0 - window, 0, seq - span), tq)
        rows = pl.ds(sub * tq, tq)
        qx = jnp.concatenate([q_ref[rows, g * LANES:(g + 1) * LANES] for g in range(groups)], axis=0)
        d = (q0 - start) + r - col
        bias = jnp.where((d >= 0) & (d < window), 0.0, NEG_INF)
        s = _dot_nt(qx, k_ref[pl.ds(start, span), :]) + jnp.concatenate([bias] * groups, axis=0)
        m = jnp.broadcast_to(jnp.max(s, axis=-1, keepdims=True), (groups * tq, LANES))
        if use_sinks:
            t_rep = (q0 + lax.broadcasted_iota(jnp.int32, (tq, LANES), 0)).astype(F32)
            sink = jnp.concatenate(
                [sink_ref[h * groups + g] + sl_ref[h * groups + g] * t_rep for g in range(groups)], axis=0)
            m = jnp.maximum(m, sink)
        e = jnp.exp(s - jnp.concatenate([m] * (span // LANES), axis=1)).astype(BF16)
        pv = _dot(e, v_ref[pl.ds(start, span), :])
        extra = None
        if use_sinks:
            esink = jnp.exp(sink - m)
            extra = [esink[g * tq:(g + 1) * tq] for g in range(groups)]
        o_ref[rows, :] = _normalize_pack([pv[g * tq:(g + 1) * tq] for g in range(groups)], extra).astype(o_ref.dtype)


def _band(slopes, sinks, proj, *, B, T, window, q_slot, k_slot, v_slot, use_sinks, n_sub, name):
    tq = 128
    n_sub = min(n_sub, T // tq)
    tb = tq * n_sub
    nt = T // tb
    groups = 4
    assert window % tq == 0 and window + tq <= T and T % tb == 0
    qb = q_slot // groups
    kern = functools.partial(_band_kernel, tq=tq, n_sub=n_sub, window=window, seq=T, use_sinks=use_sinks)
    return pl.pallas_call(
        kern,
        grid=(B, 2, nt),
        in_specs=[
            pl.BlockSpec(memory_space=pltpu.SMEM),
            pl.BlockSpec(memory_space=pltpu.SMEM),
            pl.BlockSpec((tb, groups * LANES), lambda b, h, i: (b * nt + i, qb + h)),
            pl.BlockSpec((T, LANES), lambda b, h, i: (b, k_slot + h)),
            pl.BlockSpec((T, LANES), lambda b, h, i: (b, v_slot + h)),
        ],
        out_specs=pl.BlockSpec((tb, groups * HEAD_DIM), lambda b, h, i: (b * nt + i, h)),
        out_shape=jax.ShapeDtypeStruct((B * T, 2 * groups * HEAD_DIM), BF16),
        compiler_params=_cparams(("arbitrary", "arbitrary", "arbitrary")),
        name=name,
    )(slopes, sinks, proj, proj, proj)


def _outproj_kernel(ocmp_ref, oslc_ref, owin_ref, oswa_ref, gate_ref, x_ref, mod_ref, ex_ref, bn_ref, bs_ref,
                    w_ref, g2_ref, wr_ref, x1_ref, u2_ref, lt_ref):
    sg = _sigmoid(gate_ref[...].astype(F32)).astype(BF16)
    o_nsa = (_dot(sg, ex_ref[0]) * ocmp_ref[...].astype(F32)
             + _dot(sg, ex_ref[1]) * oslc_ref[...].astype(F32)
             + _dot(sg, ex_ref[2]) * owin_ref[...].astype(F32))
    o_swa = oswa_ref[...].astype(F32)
    n1 = o_nsa * lax.rsqrt(jnp.mean(o_nsa * o_nsa, axis=-1, keepdims=True) + EPS) * bn_ref[...]
    n2 = o_swa * lax.rsqrt(jnp.mean(o_swa * o_swa, axis=-1, keepdims=True) + EPS) * bs_ref[...]
    merged = jnp.concatenate([n1, n2], axis=-1).astype(BF16)
    y = _dot(merged, w_ref[0])
    x1 = x_ref[...] + mod_ref[0, 2:3, :] * y
    x1_ref[...] = x1
    u2 = x1 * lax.rsqrt(jnp.mean(x1 * x1, axis=-1, keepdims=True) + EPS) * g2_ref[...]
    u2 = u2 * (1.0 + mod_ref[0, 4:5, :]) + mod_ref[0, 3:4, :]
    u2_ref[...] = _pack_bf16_pairs(u2)
    u_hi = u2.astype(BF16)
    u_lo = (u2 - u_hi.astype(F32)).astype(BF16)
    wr = wr_ref[...]
    w_hi = wr.astype(BF16)
    w_lo = (wr - w_hi.astype(F32)).astype(BF16)
    lt_ref[...] = _dot_nt(w_hi, u_hi) + _dot_nt(w_hi, u_lo) + _dot_nt(w_lo, u_hi)


def _outproj(ocmp, oslc, owin, oswa, proj, x2, mod, expand, beta_n, beta_s, w_out, norm2_g, wr_t, *, seq, layer):
    N, D = x2.shape
    tm = 1024
    tiles_per_seq = seq // tm
    row = lambda i: (i, 0)
    const2 = lambda i: (0, 0)
    return pl.pallas_call(
        _outproj_kernel,
        grid=(N // tm,),
        in_specs=[
            pl.BlockSpec((tm, D_NSA), row),
            pl.BlockSpec((tm, D_NSA), row),
            pl.BlockSpec((tm, D_NSA), row),
            pl.BlockSpec((tm, D_SWA), row),
            pl.BlockSpec((tm, LANES), lambda i: (i, SLOT_GATE)),
            pl.BlockSpec((tm, D), row),
            pl.BlockSpec((1, 6, D), lambda i: (i // tiles_per_seq, 0, 0)),
            pl.BlockSpec((N_NSA_BRANCHES, LANES, D_NSA), lambda i: (0, 0, 0)),
            pl.BlockSpec((1, D_NSA), const2),
            pl.BlockSpec((1, D_SWA), const2),
            pl.BlockSpec((1, D_MIX, D), lambda i: (layer, 0, 0)),
            pl.BlockSpec((1, D), const2),
            pl.BlockSpec((N_EXPERTS, D), const2),
        ],
        out_specs=[
            pl.BlockSpec((tm, D), row),
            pl.BlockSpec((tm, D // 2), row),
            pl.BlockSpec((N_EXPERTS, tm), lambda i: (0, i)),
        ],
        out_shape=[
            jax.ShapeDtypeStruct((N, D), F32),
            jax.ShapeDtypeStruct((N, D // 2), jnp.uint32),
            jax.ShapeDtypeStruct((N_EXPERTS, N), F32),
        ],
        compiler_params=_cparams(("arbitrary",)),
        name="outproj",
    )(ocmp, oslc, owin, oswa, proj, x2, mod, expand, beta_n, beta_s, w_out, norm2_g, wr_t)


def _route_kernel(lt_ref, tri_ref, d0_ref, d1_ref, w0_ref, w1_ref, seg_ref, nch_ref, tail_ref, *, tm):
    lt = lt_ref[...]
    e = jnp.exp(lt - jnp.max(lt, axis=0, keepdims=True))
    aff = e / jnp.sum(e, axis=0, keepdims=True)
    rows = [aff[k:k + 1, :] for k in range(N_EXPERTS)]
    epg = EXPERTS_PER_GROUP
    scores = []
    for gr in range(N_GROUPS):
        xs = rows[gr * epg:(gr + 1) * epg]
        top1 = functools.reduce(jnp.maximum, xs)
        second = None
        for a in range(epg):
            for b in range(a + 1, epg):
                mn = jnp.minimum(xs[a], xs[b])
                second = mn if second is None else jnp.maximum(second, mn)
        scores.append(top1 + second)
    taken = None
    in_group = []
    for gr in range(N_GROUPS):
        best = None
        for o in range(gr + 1, N_GROUPS):
            c = scores[gr] >= scores[o]
            best = c if best is None else (best & c)
        if best is None:
            best = jnp.full(scores[gr].shape, True)
        sel = best if taken is None else (best & ~taken)
        taken = sel if taken is None else (taken | sel)
        in_group.append(sel)
    ys = []
    for k in range(epg):
        yk = rows[k]
        for gr in range(1, N_GROUPS):
            yk = jnp.where(in_group[gr], rows[gr * epg + k], yk)
        ys.append(yk)
    chosen = []
    for k in range(epg):
        rk = jnp.zeros_like(ys[k])
        for o in range(epg):
            if o == k:
                continue
            before = (ys[o] >= ys[k]) if o < k else (ys[o] > ys[k])
            rk = rk + jnp.where(before, 1.0, 0.0)
        chosen.append(rk < TOP_K)
    total = functools.reduce(lambda a, b: a + b, [jnp.where(chosen[k], ys[k], 0.0) for k in range(epg)])
    first, seen = [], None
    for k in range(epg):
        first.append(chosen[k] if seen is None else (chosen[k] & ~seen))
        seen = chosen[k] if seen is None else (seen | chosen[k])
    second = [chosen[k] & ~first[k] for k in range(epg)]
    inv_total = 1.0 / total
    w_first = functools.reduce(lambda a, b: a + b, [jnp.where(first[k], ys[k], 0.0) for k in range(epg)]) * inv_total
    w_second = functools.reduce(lambda a, b: a + b, [jnp.where(second[k], ys[k], 0.0) for k in range(epg)]) * inv_total
    f_rows, s_rows = [], []
    for ex in range(N_EXPERTS):
        gr, k = divmod(ex, epg)
        f_rows.append(jnp.where(in_group[gr] & first[k], 1.0, 0.0))
        s_rows.append(jnp.where(in_group[gr] & second[k], 1.0, 0.0))
    fmat = jnp.concatenate(f_rows, axis=0)
    smat = jnp.concatenate(s_rows, axis=0)
    cmat = fmat + smat
    carry = jnp.zeros((N_EXPERTS, 1), F32)
    ranks = []
    for blk in range(tm // TRI):
        cb = cmat[:, blk * TRI:(blk + 1) * TRI]
        ranks.append(_dot(cb.astype(BF16), tri_ref[...]) + carry)
        carry = carry + jnp.sum(cb, axis=1, keepdims=True)
    rank = jnp.concatenate(ranks, axis=1)
    padded = jnp.floor((carry + (SEG_ALIGN - 1)) * (1.0 / SEG_ALIGN)) * SEG_ALIGN
    seg_rows, run = [], jnp.zeros((1, 1), F32)
    for ex in range(N_EXPERTS):
        seg_rows.append(run)
        run = run + padded[ex:ex + 1]
    seg = jnp.concatenate(seg_rows, axis=0)
    dest = rank + seg
    d0_ref[0] = jnp.sum(fmat * dest, axis=0, keepdims=True).astype(jnp.int32)
    d1_ref[0] = jnp.sum(smat * dest, axis=0, keepdims=True).astype(jnp.int32)
    w0_ref[...] = jnp.broadcast_to(w_first, (LANES, tm)).T
    w1_ref[...] = jnp.broadcast_to(w_second, (LANES, tm)).T
    seg_ref[0] = jnp.broadcast_to(seg, (N_EXPERTS, LANES)).astype(jnp.int32)
    whole = jnp.floor(carry * (1.0 / MOE_CHUNK))
    rem = carry - whole * MOE_CHUNK
    n_full = whole + jnp.where(rem > MOE_CHUNK // 2, 1.0, 0.0)
    half_tail = jnp.where((rem > 0) & (rem <= MOE_CHUNK // 2), 1.0, 0.0)
    nch_ref[0] = jnp.broadcast_to(n_full, (N_EXPERTS, LANES)).astype(jnp.int32)
    tail_ref[0] = jnp.broadcast_to(half_tail, (N_EXPERTS, LANES)).astype(jnp.int32)


def _route(lt, tri):
    E, N = lt.shape
    tm = MOE_TILE
    nt = N // tm
    return pl.pallas_call(
        functools.partial(_route_kernel, tm=tm),
        grid=(nt,),
        in_specs=[pl.BlockSpec((E, tm), lambda i: (0, i)), pl.BlockSpec((TRI, TRI), lambda i: (0, 0))],
        out_specs=[
            pl.BlockSpec((1, 1, tm), lambda i: (i, 0, 0)),
            pl.BlockSpec((1, 1, tm), lambda i: (i, 0, 0)),
            pl.BlockSpec((tm, LANES), lambda i: (i, 0)),
            pl.BlockSpec((tm, LANES), lambda i: (i, 0)),
            pl.BlockSpec((1, E, LANES), lambda i: (i, 0, 0)),
            pl.BlockSpec((1, E, LANES), lambda i: (i, 0, 0)),
            pl.BlockSpec((1, E, LANES), lambda i: (i, 0, 0)),
        ],
        out_shape=[
            jax.ShapeDtypeStruct((nt, 1, tm), jnp.int32),
            jax.ShapeDtypeStruct((nt, 1, tm), jnp.int32),
            jax.ShapeDtypeStruct((N, LANES), F32),
            jax.ShapeDtypeStruct((N, LANES), F32),
            jax.ShapeDtypeStruct((nt, E, LANES), jnp.int32),
            jax.ShapeDtypeStruct((nt, E, LANES), jnp.int32),
            jax.ShapeDtypeStruct((nt, E, LANES), jnp.int32),
        ],
        compiler_params=_cparams(("arbitrary",)),
        name="route",
    )(lt, tri)


def _moe_kernel(seg_ref, nch_ref, tail_ref, d0_ref, d1_ref, u_ref, w1_ref, w3_ref, w2_ref, w0c_ref, w1c_ref, x1_ref,
                mod_ref, o_ref, xs_ref, ys_ref, g0_ref, g1_ref, *, tm, chunk, n_exp_steps, tsub):
    i = pl.program_id(0)
    step = pl.program_id(1)

    @pl.when(step == 0)
    def _():
        xs_ref[...] = jnp.zeros_like(xs_ref)
        ys_ref[...] = jnp.zeros_like(ys_ref)

        def move(grp, carry):
            t0 = pl.multiple_of(grp * 8, 8)
            blk = u_ref[pl.ds(t0, 8), :]
            for k in range(8):
                row = blk[k:k + 1, :]
                xs_ref[pl.ds(d0_ref[0, 0, t0 + k], 1), :] = row
                xs_ref[pl.ds(d1_ref[0, 0, t0 + k], 1), :] = row
            return carry

        lax.fori_loop(0, tm // 8, move, 0)

    @pl.when(step < n_exp_steps)
    def _():
        for sub in range(EXPERTS_PER_STEP):
            slot = i * N_EXPERTS + step * EXPERTS_PER_STEP + sub
            start = seg_ref[slot]
            n_full = nch_ref[slot]

            def do_rows(r0, rows, sub=sub):
                lo, hi = _unpack_bf16_pairs(xs_ref[pl.ds(r0, rows), :])
                xb = jnp.concatenate([lo, hi], axis=-1).astype(BF16)
                a = _dot(xb, w1_ref[0, sub])
                hmid = a * _sigmoid(a) * _dot(xb, w3_ref[0, sub])
                ys_ref[pl.ds(r0, rows), :] = _pack_bf16_pairs(_dot(hmid.astype(BF16), w2_ref[0, sub]))

            def do_chunk(j, carry, start=start, do_rows=do_rows):
                do_rows(pl.multiple_of(start + j * chunk, SEG_ALIGN), chunk)
                return carry

            lax.fori_loop(0, n_full, do_chunk, 0)

            @pl.when(tail_ref[slot] > 0)
            def _(start=start, n_full=n_full, do_rows=do_rows):
                do_rows(pl.multiple_of(start + n_full * chunk, SEG_ALIGN), chunk // 2)

    @pl.when(step >= n_exp_steps)
    def _():
        base = (step - n_exp_steps) * tsub

        def move(grp, carry):
            t0 = pl.multiple_of(grp * 8, 8)
            for d_ref, g_ref in ((d0_ref, g0_ref), (d1_ref, g1_ref)):
                rows = [ys_ref[pl.ds(d_ref[0, 0, base + t0 + k], 1), :] for k in range(8)]
                g_ref[pl.ds(t0, 8), :] = jnp.concatenate(rows, axis=0)
            return carry

        lax.fori_loop(0, tsub // 8, move, 0)
        lo0, hi0 = _unpack_bf16_pairs(g0_ref[...])
        lo1, hi1 = _unpack_bf16_pairs(g1_ref[...])
        half = g0_ref.shape[1]
        w0 = jnp.concatenate([w0c_ref[...]] * (half // LANES), axis=1)
        w1 = jnp.concatenate([w1c_ref[...]] * (half // LANES), axis=1)
        o_ref[:, :half] = x1_ref[:, :half] + mod_ref[0, 5:6, :half] * (w0 * lo0 + w1 * lo1)
        o_ref[:, half:] = x1_ref[:, half:] + mod_ref[0, 5:6, half:] * (w0 * hi0 + w1 * hi1)


def _moe(seg, nch, tail, d0, d1, u2p, w1, w3, w2, w0rep, w1rep, x1, mod, *, layer, seq):
    N, half = u2p.shape
    D = 2 * half
    E = w1.shape[1]
    tm = MOE_TILE
    tsub = 512
    nt = N // tm
    n_exp = E // EXPERTS_PER_STEP
    n_cmb = tm // tsub
    kern = functools.partial(_moe_kernel, tm=tm, chunk=MOE_CHUNK, n_exp_steps=n_exp, tsub=tsub)
    wexp = lambda i, s, *_: (layer, jnp.minimum(s, n_exp - 1), 0, 0)
    csub = lambda i, s, *_: (i * n_cmb + jnp.clip(s - n_exp, 0, n_cmb - 1), 0)
    grid_spec = pltpu.PrefetchScalarGridSpec(
        num_scalar_prefetch=3,
        grid=(nt, n_exp + n_cmb),
        in_specs=[
            pl.BlockSpec((1, 1, tm), lambda i, s, *_: (i, 0, 0), memory_space=pltpu.SMEM),
            pl.BlockSpec((1, 1, tm), lambda i, s, *_: (i, 0, 0), memory_space=pltpu.SMEM),
            pl.BlockSpec((tm, half), lambda i, s, *_: (i, 0)),
            pl.BlockSpec((1, EXPERTS_PER_STEP, D, D_EXPERT), wexp),
            pl.BlockSpec((1, EXPERTS_PER_STEP, D, D_EXPERT), wexp),
            pl.BlockSpec((1, EXPERTS_PER_STEP, D_EXPERT, D), wexp),
            pl.BlockSpec((tsub, LANES), csub),
            pl.BlockSpec((tsub, LANES), csub),
            pl.BlockSpec((tsub, D), csub),
            pl.BlockSpec((1, 6, D),
                         lambda i, s, *_: ((i * tm + jnp.clip(s - n_exp, 0, n_cmb - 1) * tsub) // seq, 0, 0)),
        ],
        out_specs=pl.BlockSpec((tsub, D), csub),
        scratch_shapes=[
            pltpu.VMEM((MOE_ROWS, half), jnp.uint32),
            pltpu.VMEM((MOE_ROWS, half), jnp.uint32),
            pltpu.VMEM((tsub, half), jnp.uint32),
            pltpu.VMEM((tsub, half), jnp.uint32),
        ],
    )
    return pl.pallas_call(
        kern,
        grid_spec=grid_spec,
        out_shape=jax.ShapeDtypeStruct((N, D), F32),
        compiler_params=pltpu.CompilerParams(dimension_semantics=("arbitrary", "arbitrary"),
                                             vmem_limit_bytes=MOE_VMEM_LIMIT),
        name="moe",
    )(seg, nch, tail, d0, d1, u2p, w1, w3, w2, w0rep, w1rep, x1, mod)


def _alibi_slopes(first, count):
    hh = jnp.arange(first + 1, first + count + 1, dtype=F32)
    return jnp.exp2(-8.0 * hh / N_HEADS_TOTAL)


def _bd_const():
    idx = np.arange(2 * LANES) // HEAD_DIM
    return jnp.asarray((idx[:, None] == idx[None, :]).astype(np.float32), BF16)


def _blkcols_const(T):
    blk = np.arange(T) // SEL_LEN
    m = np.where(np.arange(LANES)[None, :] == blk[:, None], NEG_INF, 0.0).astype(np.float32)
    return jnp.asarray(m, BF16)


def _overlap_const(T):
    nc = T // CMP_STRIDE
    n_sel = T // SEL_LEN
    cs = np.arange(nc) * CMP_STRIDE
    ss = np.arange(n_sel) * SEL_LEN
    ov = np.clip(np.minimum(cs[:, None] + CMP_LEN, ss[None, :] + SEL_LEN)
                 - np.maximum(cs[:, None], ss[None, :]), 0, None) / CMP_LEN
    return jnp.asarray(ov.T.astype(np.float32))


def _expand_const():
    ex = np.zeros((N_NSA_BRANCHES, LANES, D_NSA), np.float32)
    for br in range(N_NSA_BRANCHES):
        for hd in range(N_HEADS_NSA):
            ex[br, hd * N_NSA_BRANCHES + br, hd * HEAD_DIM:(hd + 1) * HEAD_DIM] = 1.0
    return jnp.asarray(ex, BF16)


def _compact_w_in(w):
    parts = [w[..., a:b] for a, b in (_C_QN, _C_QS, _C_KSL, _C_KW, _C_KS, _C_VSL, _C_VW, _C_VS, _C_KC, _C_VC, _C_GATE)]
    pad = jnp.zeros(w.shape[:-1] + (LANES - (_C_GATE[1] - _C_GATE[0]),), w.dtype)
    return jnp.concatenate(parts + [pad], axis=-1).astype(BF16)


def _qconst(slopes_q):
    hi = slopes_q.astype(BF16).astype(F32)
    lo = (slopes_q - hi).astype(BF16).astype(F32)
    cols = jnp.stack([64.0 * hi, 64.0 * lo, hi, lo], axis=1)
    return jnp.zeros((slopes_q.shape[0], LANES), F32).at[:, HEAD_DIM:HEAD_DIM + 4].set(cols)


def kernel(x, c, w_router, ada_w, ada_b, norm1_g, norm2_g, w_in, w_out, nsa_q_gain, nsa_k_gain, cmp_pos_k,
           cmp_pos_v, cmp_w1_k, cmp_w2_k, cmp_w1_v, cmp_w2_v, swa_q_gain, swa_k_gain, swa_sinks, beta_nsa,
           beta_swa, moe_w1, moe_w3, moe_w2):
    B, T, D = x.shape
    L = ada_w.shape[0]
    N = B * T
    nc = T // CMP_STRIDE
    assert D == D_MODEL and T % 1024 == 0 and T <= 64 * 128 and B <= 8

    slopes_nsa = _alibi_slopes(N_HEADS_SWA, N_HEADS_NSA)
    slopes_swa = _alibi_slopes(0, N_HEADS_SWA)
    qconst = _qconst(jnp.concatenate([slopes_nsa, slopes_swa]))
    bd = _bd_const()
    blkcols = _blkcols_const(T)
    ov = _overlap_const(T)
    expand = _expand_const()
    wr_t = w_router.T
    tri = jnp.asarray(np.triu(np.ones((TRI, TRI), np.float32), 1), BF16)
    assert N % MOE_TILE == 0
    moe_w1b, moe_w3b, moe_w2b = moe_w1.astype(BF16), moe_w3.astype(BF16), moe_w2.astype(BF16)
    w_outb = w_out.astype(BF16)
    w_inb = _compact_w_in(w_in)
    cmp_w1s = jnp.stack([cmp_w1_k, cmp_w1_v], axis=1).reshape(2 * L, CMP_LEN * HEAD_DIM, CMP_HIDDEN).astype(BF16)
    cmp_w2s = jnp.pad(jnp.stack([cmp_w2_k, cmp_w2_v], axis=1).reshape(2 * L, CMP_HIDDEN, HEAD_DIM),
                      ((0, 0), (0, 0), (0, LANES - HEAD_DIM))).astype(BF16)
    cmp_pos = jnp.stack([cmp_pos_k, cmp_pos_v], axis=1).reshape(2 * L, 1, CMP_LEN * HEAD_DIM)
    cmp_pos = jnp.broadcast_to(cmp_pos, (2 * L, 8, CMP_LEN * HEAD_DIM)).astype(BF16)
    zero_sinks = jnp.zeros((N_HEADS_NSA,), F32)

    c_pad = jnp.zeros((8, D), F32).at[:B].set(c)
    mod_all = _ada(c_pad, ada_w, ada_b)[:, :B].reshape(L, B, 6, D)

    x2 = x.reshape(N, D)
    for l in range(L):
        mod = mod_all[l]
        tile2 = lambda g: jnp.tile(g, 2)
        gain_c = jnp.concatenate([
            jnp.tile(nsa_q_gain[l], N_HEADS_NSA) * ATTN_SCALE,
            jnp.tile(swa_q_gain[l], N_HEADS_SWA) * ATTN_SCALE,
            tile2(nsa_k_gain[l, 1]), tile2(nsa_k_gain[l, 2]), tile2(swa_k_gain[l])]).reshape(1, -1)
        proj, xc = _inproj(x2, mod, norm1_g[l].reshape(1, D), w_inb, gain_c, qconst, bd, seq=T, layer=l)
        kgain = jnp.pad(nsa_k_gain[l, 0], (0, LANES - HEAD_DIM)).reshape(1, LANES)
        kvc = _cmpmlp(xc, cmp_w1s, cmp_w2s, cmp_pos, kgain, B=B, layer=l)

        o_cmp, notsel = _cmpattn(proj, kvc, ov, B=B, T=T)
        o_slc = _selattn(proj, notsel, blkcols, B=B, T=T)
        o_win = _band(slopes_nsa, zero_sinks, proj, B=B, T=T, window=NSA_WINDOW, q_slot=SLOT_QN,
                      k_slot=SLOT_KW, v_slot=SLOT_VW, use_sinks=False, n_sub=16, name="winattn")
        o_swa = _band(slopes_swa, swa_sinks[l], proj, B=B, T=T, window=SWA_WINDOW, q_slot=SLOT_QS,
                      k_slot=SLOT_KS, v_slot=SLOT_VS, use_sinks=True, n_sub=16, name="swaattn")

        x1, u2, lt = _outproj(o_cmp, o_slc, o_win, o_swa, proj, x2, mod, expand,
                              beta_nsa[l].reshape(1, -1), beta_swa[l].reshape(1, -1), w_outb,
                              norm2_g[l].reshape(1, D), wr_t, seq=T, layer=l)
        d0, d1, w0rep, w1rep, seg, nch, tail = _route(lt, tri)
        x2 = _moe(seg[:, :, 0].reshape(-1), nch[:, :, 0].reshape(-1), tail[:, :, 0].reshape(-1), d0, d1, u2,
                  moe_w1b, moe_w3b, moe_w2b, w0rep, w1rep, x1, mod, layer=l, seq=T)
    return x2.reshape(B, T, D)
```
